```python
import math
import jax, jax.numpy as jnp
from jax import lax
import numpy as np

D_MODEL = 1024
BATCH = 4
SEQ = 4096
DEPTH = 2

GRID_W = 64
CTX_LEN = 256
NORM_EPS = 1e-6
RET_HEADS = 4
RET_DK = 64
RET_DV = 128
RET_CHUNK = 128
ROPE_BASE = 10000.0
HYENA_CH = 512
HYENA_SHORT = 3
HYENA_EMB = 33
HYENA_FFN = 64
HYENA_FAST_DECAY = 0.3
HYENA_SLOW_DECAY = 1.5
HYENA_TARGET = 1e-2
EVEN_IN = 2 * RET_HEADS * RET_DK + 2 * RET_HEADS * RET_DV + 3 * HYENA_CH
EVEN_MIX = RET_HEADS * RET_DV + HYENA_CH
GLA_HEADS = 4
GLA_DK = 128
GLA_DV = 256
GLA_RANK = 16
GLA_TAU = 16.0
GLA_CHUNK = 64
ODD_IN = 2 * GLA_HEADS * GLA_DK + 2 * GLA_HEADS * GLA_DV
ODD_MIX = GLA_HEADS * GLA_DV
N_EXPERTS = 64
TOP_K = 8
N_GROUPS = 8
TOPK_GROUPS = 4
EXPERT_FF = 256
SHARED_FF = 256
ROUTED_SCALE = 2.5
N_EVEN = (DEPTH + 1) // 2
N_ODD = DEPTH // 2

kernel_name = "hybrid_retention_hyena_gla_moe_prefix"


def rms_norm(x, w):
    xf = x.astype(jnp.float32)
    y = xf * lax.rsqrt(jnp.mean(xf * xf, axis=-1, keepdims=True) + NORM_EPS)
    return (y * w.astype(jnp.float32)).astype(x.dtype)


def head_rms(o):
    return o * lax.rsqrt(jnp.mean(o * o, axis=-1, keepdims=True) + NORM_EPS)


def adaln(cond, w, b):
    m = jax.nn.silu(cond) @ w + b
    m = m.reshape(m.shape[:-1] + (1, m.shape[-1]))
    return jnp.split(m, 6, axis=-1)


def heads(t, n_heads):
    B, L, _ = t.shape
    return t.reshape(B, L, n_heads, -1).transpose(0, 2, 1, 3)


def to_tokens(o):
    B, H, L, d = o.shape
    return o.transpose(0, 2, 1, 3).reshape(B, L, H * d)


def rope_2d(t):
    L, dk = t.shape[2], t.shape[3]
    rows = L // GRID_W
    row = jnp.broadcast_to(jnp.arange(rows, dtype=jnp.float32)[:, None], (rows, GRID_W)).reshape(L)
    col = jnp.broadcast_to(jnp.arange(GRID_W, dtype=jnp.float32)[None, :], (rows, GRID_W)).reshape(L)
    half = dk // 2
    inv = ROPE_BASE ** (-jnp.arange(0, half, 2, dtype=jnp.float32) / half)

    def rot(xp, pos):
        ang = pos[:, None] * inv[None, :]
        cos, sin = jnp.cos(ang).astype(xp.dtype), jnp.sin(ang).astype(xp.dtype)
        x1, x2 = jnp.split(xp, 2, axis=-1)
        return jnp.concatenate([x1 * cos - x2 * sin, x1 * sin + x2 * cos], axis=-1)

    return jnp.concatenate([rot(t[..., :half], row), rot(t[..., half:], col)], axis=-1)


def chunked_gated_scan(q, k, v, log_a, s0, chunk):
    B, H, L, dk = q.shape
    dv = v.shape[-1]
    n = L // chunk

    def blocks(t):
        return jnp.moveaxis(t.astype(jnp.float32).reshape(B, H, n, chunk, t.shape[-1]), 2, 0)

    causal = jnp.tril(jnp.ones((chunk, chunk), dtype=bool))

    def step(state, inp):
        qc, kc, vc, ac = inp
        b = jnp.cumsum(ac, axis=2)
        b_end = b[:, :, -1:, :]
        q_dec = qc * jnp.exp(b)
        scores = jnp.einsum('bhid,bhjd->bhij', q_dec, kc * jnp.exp(-b))
        scores = jnp.where(causal, scores, 0.0)
        out = (jnp.einsum('bhij,bhjv->bhiv', scores, vc)
               + jnp.einsum('bhid,bhdv->bhiv', q_dec, state))
        state = (jnp.exp(b_end[:, :, 0, :])[..., None] * state
                 + jnp.einsum('bhjd,bhjv->bhdv', kc * jnp.exp(b_end - b), vc))
        return state, out

    state, out = lax.scan(step, s0.astype(jnp.float32),
                          (blocks(q), blocks(k), blocks(v), blocks(log_a)))
    return jnp.moveaxis(out, 0, 2).reshape(B, H, L, dv), state


def bidir_prefix_scan(ctx_in, lat_in, chunk):
    qc, kc, vc, fc, bc = ctx_in
    ql, kl, vl, fl, bl = lat_in
    B, H, _, dk = qc.shape
    s0 = jnp.zeros((B, H, dk, vc.shape[-1]), jnp.float32)
    flip = lambda t: jnp.flip(t, axis=2)
    oc_f, sc_f = chunked_gated_scan(qc, kc, vc, fc, s0, chunk)
    ol_f, _ = chunked_gated_scan(ql, kl, vl, fl, sc_f, chunk)
    oc_b, sc_b = chunked_gated_scan(flip(qc), flip(kc), flip(vc), flip(bc), s0, chunk)
    ol_b, _ = chunked_gated_scan(flip(ql), flip(kl), flip(vl), flip(bl), sc_b, chunk)
    return oc_f + flip(oc_b), ol_f + flip(ol_b)


def short_conv(u, w, b):
    K, C = w.shape
    y = lax.conv_general_dilated(u, w[:, None, :].astype(u.dtype), window_strides=(1,),
                                 padding=[(K // 2, K // 2)],
                                 dimension_numbers=('NWC', 'WIO', 'NWC'),
                                 feature_group_count=C)
    return y + b


def hyena_filters(L, w1, b1, w2, b2, w3, b3, w4, freq):
    f32 = jnp.float32
    t = jnp.linspace(0.0, 1.0, L, dtype=f32)[:, None]
    bands = (HYENA_EMB - 1) // 2
    ang = 2.0 * math.pi * jnp.arange(L, dtype=f32)[:, None] / L
    fr = jnp.linspace(1e-4, bands - 1, bands, dtype=f32)[None, :]
    feats = jnp.concatenate([t, jnp.cos(fr * ang), -jnp.sin(fr * ang)], axis=-1)
    fq = freq.astype(f32)
    h = jnp.sin(fq * (feats @ w1.astype(f32) + b1.astype(f32)))
    h = jnp.sin(fq * (h @ w2.astype(f32) + b2.astype(f32)))
    h = jnp.sin(fq * (h @ w3.astype(f32) + b3.astype(f32)))
    h = h @ w4.astype(f32)
    max_decay = math.log(HYENA_TARGET) / HYENA_FAST_DECAY
    min_decay = math.log(HYENA_TARGET) / HYENA_SLOW_DECAY
    deltas = jnp.linspace(min_decay, max_decay, HYENA_CH, dtype=f32)
    window = jnp.exp(-t * jnp.abs(deltas))
    h_f, h_b = jnp.split(h, 2, axis=-1)
    return h_f * window, h_b * window


def two_sided_fft_conv(u, h_f, h_b, bias):
    L = u.shape[1]
    kern = jnp.concatenate([h_f, jnp.zeros_like(h_f[:1]), h_b[:0:-1]], axis=0)
    U = jnp.fft.rfft(u, n=2 * L, axis=1)
    Kf = jnp.fft.rfft(kern, n=2 * L, axis=0)
    y = jnp.fft.irfft(U * Kf[None], n=2 * L, axis=1)[:, :L]
    return y + u * bias


def ret_hyena_mixer(h_ctx, h_lat, w_in, short_w, short_b, fw1, fb1, fw2, fb2, fw3, fb3, fw4,
                    ffreq, hy_bias, w_out, need_ctx):
    qk, vd = RET_HEADS * RET_DK, RET_HEADS * RET_DV
    cuts = [qk, 2 * qk, 2 * qk + vd, 2 * qk + 2 * vd]

    def project(h):
        q, k, v, g, z = jnp.split(h @ w_in, cuts, axis=-1)
        return (heads(q, RET_HEADS), heads(k, RET_HEADS) * (RET_DK ** -0.5),
                heads(v, RET_HEADS), g, z)

    qc, kc, vc, gc, zc = project(h_ctx)
    ql, kl, vl, gl, zl = project(h_lat)
    ql, kl = rope_2d(ql), rope_2d(kl)
    log_g_f = jnp.log1p(-jnp.exp2(-5.0 - jnp.arange(RET_HEADS, dtype=jnp.float32)))
    log_g_b = log_g_f[::-1]
    dec = lambda t, lg: jnp.broadcast_to(lg[None, :, None, None], t.shape[:3] + (RET_DK,))
    oc, ol = bidir_prefix_scan((qc, kc, vc, dec(qc, log_g_f), dec(qc, log_g_b)),
                               (ql, kl, vl, dec(ql, log_g_f), dec(ql, log_g_b)), RET_CHUNK)

    def retention_out(o, g):
        return to_tokens(head_rms(o)).astype(g.dtype) * jax.nn.silu(g)

    def hyena_out(z):
        L = z.shape[1]
        z = short_conv(z, short_w, short_b)
        x0, x1, v = jnp.split(z, 3, axis=-1)
        h_f, h_b = hyena_filters(L, fw1, fb1, fw2, fb2, fw3, fb3, fw4, ffreq)
        y = two_sided_fft_conv((x1 * v).astype(jnp.float32), h_f, h_b, hy_bias.astype(jnp.float32))
        return x0 * y.astype(z.dtype)

    y_lat = jnp.concatenate([retention_out(ol, gl), hyena_out(zl)], axis=-1) @ w_out
    y_ctx = (jnp.concatenate([retention_out(oc, gc), hyena_out(zc)], axis=-1) @ w_out
             if need_ctx else None)
    return y_ctx, y_lat


def gla_mixer(h_ctx, h_lat, w_in, w1f, w2f, bf, w1b, w2b, bb, norm_w, w_out, need_ctx):
    kd, vd = GLA_HEADS * GLA_DK, GLA_HEADS * GLA_DV
    cuts = [kd, 2 * kd, 2 * kd + vd]

    def log_gate(h, w1, w2, b):
        return heads(jax.nn.log_sigmoid(((h @ w1) @ w2 + b).astype(jnp.float32)) / GLA_TAU, GLA_HEADS)

    def project(h):
        q, k, v, r = jnp.split(h @ w_in, cuts, axis=-1)
        return ((heads(q, GLA_HEADS) * (GLA_DK ** -0.5), heads(k, GLA_HEADS), heads(v, GLA_HEADS),
                 log_gate(h, w1f, w2f, bf), log_gate(h, w1b, w2b, bb)), r)

    ctx_in, rc = project(h_ctx)
    lat_in, rl = project(h_lat)
    oc, ol = bidir_prefix_scan(ctx_in, lat_in, GLA_CHUNK)

    def out(o, r):
        o = head_rms(o) * norm_w.astype(jnp.float32)
        return (to_tokens(o).astype(r.dtype) * jax.nn.silu(r)) @ w_out

    return (out(oc, rc) if need_ctx else None), out(ol, rl)


def moe(h, w_router, router_bias, w_gate, w_up, w_down, ws_gate, ws_up, ws_down):
    T = h.shape[0]
    scores = jax.nn.sigmoid((h @ w_router).astype(jnp.float32))
    sel = scores + router_bias.astype(jnp.float32)
    grp = sel.reshape(T, N_GROUPS, N_EXPERTS // N_GROUPS)
    grp_score = jnp.sum(lax.top_k(grp, 2)[0], axis=-1)
    _, gidx = lax.top_k(grp_score, TOPK_GROUPS)
    gmask = jnp.any(gidx[..., None] == jnp.arange(N_GROUPS), axis=-2)
    emask = jnp.repeat(gmask, N_EXPERTS // N_GROUPS, axis=-1)
    _, eidx = lax.top_k(jnp.where(emask, sel, -jnp.inf), TOP_K)
    w = jnp.take_along_axis(scores, eidx, axis=-1)
    w = w / jnp.sum(w, axis=-1, keepdims=True) * ROUTED_SCALE
    gates = jnp.sum(jax.nn.one_hot(eidx, N_EXPERTS, dtype=jnp.float32) * w[..., None], axis=1)

    def expert_step(acc, p):
        wg, wu, wd, g = p
        y = (jax.nn.silu(h @ wg) * (h @ wu)) @ wd
        return acc + g[:, None] * y, None

    routed, _ = lax.scan(expert_step, jnp.zeros_like(h),
                         (w_gate, w_up, w_down, gates.T.astype(h.dtype)))
    shared = (jax.nn.silu(h @ ws_gate) * (h @ ws_up)) @ ws_down
    return routed + shared


def setup_inputs(seed: int = 0) -> dict:
    key = jax.random.key(seed)
    ks = iter(jax.random.split(key, 48))
    D = D_MODEL
    nrm = lambda shape, s: jax.random.normal(next(ks), shape, jnp.float32) * s
    gain = lambda shape: 1.0 + nrm(shape, 0.02)
    return {
        "x": nrm((BATCH, SEQ, D), 1.0),
        "c": nrm((BATCH, D), 1.0),
        "ctx": nrm((BATCH, CTX_LEN, D), 1.0),
        "c_ctx": nrm((D,), 1.0),
        "ada_w": nrm((DEPTH, D, 6 * D), 0.5 * D ** -0.5),
        "ada_b": nrm((DEPTH, 6 * D), 0.01),
        "norm1_w": gain((DEPTH, D)),
        "norm2_w": gain((DEPTH, D)),
        "ev_w_in": nrm((N_EVEN, D, EVEN_IN), D ** -0.5),
        "ev_short_w": nrm((N_EVEN, HYENA_SHORT, 3 * HYENA_CH), HYENA_SHORT ** -0.5),
        "ev_short_b": nrm((N_EVEN, 3 * HYENA_CH), 0.01),
        "ev_filt_w1": nrm((N_EVEN, HYENA_EMB, HYENA_FFN), HYENA_EMB ** -0.5),
        "ev_filt_b1": nrm((N_EVEN, HYENA_FFN), 0.1),
        "ev_filt_w2": nrm((N_EVEN, HYENA_FFN, HYENA_FFN), HYENA_FFN ** -0.5),
        "ev_filt_b2": nrm((N_EVEN, HYENA_FFN), 0.1),
        "ev_filt_w3": nrm((N_EVEN, HYENA_FFN, HYENA_FFN), HYENA_FFN ** -0.5),
        "ev_filt_b3": nrm((N_EVEN, HYENA_FFN), 0.1),
        "ev_filt_w4": nrm((N_EVEN, HYENA_FFN, 2 * HYENA_CH), 0.1 * HYENA_FFN ** -0.5),
        "ev_filt_freq": gain((N_EVEN, HYENA_FFN)),
        "ev_hyena_bias": nrm((N_EVEN, HYENA_CH), 1.0),
        "ev_w_out": nrm((N_EVEN, EVEN_MIX, D), EVEN_MIX ** -0.5),
        "od_w_in": nrm((N_ODD, D, ODD_IN), D ** -0.5),
        "od_gate_w1_f": nrm((N_ODD, D, GLA_RANK), D ** -0.5),
        "od_gate_w2_f": nrm((N_ODD, GLA_RANK, GLA_HEADS * GLA_DK), GLA_RANK ** -0.5),
        "od_gate_b_f": nrm((N_ODD, GLA_HEADS * GLA_DK), 0.01),
        "od_gate_w1_b": nrm((N_ODD, D, GLA_RANK), D ** -0.5),
        "od_gate_w2_b": nrm((N_ODD, GLA_RANK, GLA_HEADS * GLA_DK), GLA_RANK ** -0.5),
        "od_gate_b_b": nrm((N_ODD, GLA_HEADS * GLA_DK), 0.01),
        "od_norm_w": gain((N_ODD, GLA_DV)),
        "od_w_out": nrm((N_ODD, ODD_MIX, D), ODD_MIX ** -0.5),
        "router_w": nrm((DEPTH, D, N_EXPERTS), D ** -0.5),
        "router_bias": nrm((DEPTH, N_EXPERTS), 0.01),
        "exp_w_gate": nrm((DEPTH, N_EXPERTS, D, EXPERT_FF), D ** -0.5),
        "exp_w_up": nrm((DEPTH, N_EXPERTS, D, EXPERT_FF), D ** -0.5),
        "exp_w_down": nrm((DEPTH, N_EXPERTS, EXPERT_FF, D), EXPERT_FF ** -0.5),
        "sh_w_gate": nrm((DEPTH, D, SHARED_FF), D ** -0.5),
        "sh_w_up": nrm((DEPTH, D, SHARED_FF), D ** -0.5),
        "sh_w_down": nrm((DEPTH, SHARED_FF, D), SHARED_FF ** -0.5),
        "final_norm_w": gain((D,)),
    }


def reference(x, c, ctx, c_ctx, ada_w, ada_b, norm1_w, norm2_w,
              ev_w_in, ev_short_w, ev_short_b, ev_filt_w1, ev_filt_b1, ev_filt_w2, ev_filt_b2,
              ev_filt_w3, ev_filt_b3, ev_filt_w4, ev_filt_freq, ev_hyena_bias, ev_w_out,
              od_w_in, od_gate_w1_f, od_gate_w2_f, od_gate_b_f, od_gate_w1_b, od_gate_w2_b,
              od_gate_b_b, od_norm_w, od_w_out,
              router_w, router_bias, exp_w_gate, exp_w_up, exp_w_down,
              sh_w_gate, sh_w_up, sh_w_down, final_norm_w):
    D = x.shape[-1]
    for i in range(DEPTH):
        last = i == DEPTH - 1
        j = i // 2
        sh1, sc1, g1, sh2, sc2, g2 = adaln(c, ada_w[i], ada_b[i])
        csh1, csc1, cg1, csh2, csc2, cg2 = adaln(c_ctx, ada_w[i], ada_b[i])
        h_lat = rms_norm(x, norm1_w[i]) * (1.0 + sc1) + sh1
        h_ctx = rms_norm(ctx, norm1_w[i]) * (1.0 + csc1) + csh1
        if i % 2 == 0:
            y_ctx, y_lat = ret_hyena_mixer(
                h_ctx, h_lat, ev_w_in[j], ev_short_w[j], ev_short_b[j],
                ev_filt_w1[j], ev_filt_b1[j], ev_filt_w2[j], ev_filt_b2[j],
                ev_filt_w3[j], ev_filt_b3[j], ev_filt_w4[j], ev_filt_freq[j],
                ev_hyena_bias[j], ev_w_out[j], not last)
        else:
            y_ctx, y_lat = gla_mixer(
                h_ctx, h_lat, od_w_in[j], od_gate_w1_f[j], od_gate_w2_f[j], od_gate_b_f[j],
                od_gate_w1_b[j], od_gate_w2_b[j], od_gate_b_b[j], od_norm_w[j], od_w_out[j],
                not last)
        x = x + g1 * y_lat
        h_lat = rms_norm(x, norm2_w[i]) * (1.0 + sc2) + sh2
        moe_p = (router_w[i], router_bias[i], exp_w_gate[i], exp_w_up[i], exp_w_down[i],
                 sh_w_gate[i], sh_w_up[i], sh_w_down[i])
        if last:
            x = x + g2 * moe(h_lat.reshape(-1, D), *moe_p).reshape(x.shape)
        else:
            ctx = ctx + cg1 * y_ctx
            h_ctx = rms_norm(ctx, norm2_w[i]) * (1.0 + csc2) + csh2
            n_ctx = ctx.shape[0] * ctx.shape[1]
            y = moe(jnp.concatenate([h_ctx.reshape(-1, D), h_lat.reshape(-1, D)], axis=0), *moe_p)
            ctx = ctx + cg2 * y[:n_ctx].reshape(ctx.shape)
            x = x + g2 * y[n_ctx:].reshape(x.shape)
    return rms_norm(x, final_norm_w)
```

```python
import functools
import math

import jax
import jax.numpy as jnp
from jax import lax
from jax.experimental import pallas as pl
from jax.experimental.pallas import tpu as pltpu

F32 = jnp.float32
BF16 = jnp.bfloat16
HIGHEST = lax.Precision.HIGHEST

D_MODEL = 1024
GRID_W = 64
NORM_EPS = 1e-6
RET_HEADS, RET_DK, RET_DV, RET_CHUNK = 4, 64, 128, 128
ROPE_BASE = 10000.0
HYENA_CH, HYENA_EMB = 512, 33
HYENA_FAST_DECAY, HYENA_SLOW_DECAY, HYENA_TARGET = 0.3, 1.5, 1e-2
GLA_HEADS, GLA_DK, GLA_DV, GLA_RANK, GLA_TAU, GLA_CHUNK = 4, 128, 256, 16, 16.0, 64
N_EXPERTS, TOP_K, N_GROUPS, TOPK_GROUPS = 64, 8, 8, 4
GROUP_SIZE = N_EXPERTS // N_GROUPS
EXPERT_FF = 256
ROUTED_SCALE = 2.5

LANES = 128
SCAN_ROWS = 256
VMEM_LIMIT = 56 * 1024 * 1024


def _cparams(sem):
    return pltpu.CompilerParams(dimension_semantics=sem, vmem_limit_bytes=VMEM_LIMIT)


def _rms(x):
    return x * lax.rsqrt(jnp.mean(x * x, axis=-1, keepdims=True) + NORM_EPS)


def _silu(x):
    return x * jax.nn.sigmoid(x)


def _adaln_kernel(c_ref, w_ref, b_ref, o_ref):
    c = c_ref[...]
    o_ref[0] = jnp.dot(_silu(c), w_ref[0], preferred_element_type=F32,
                       precision=HIGHEST) + b_ref[0]


def adaln_rows(cond8, ada_w, ada_b):
    depth, d, n = ada_w.shape
    tn = 512
    return pl.pallas_call(
        _adaln_kernel,
        grid=(depth, n // tn),
        in_specs=[pl.BlockSpec((8, d), lambda l, j: (0, 0)),
                  pl.BlockSpec((1, d, tn), lambda l, j: (l, 0, j)),
                  pl.BlockSpec((1, 1, tn), lambda l, j: (l, 0, j))],
        out_specs=pl.BlockSpec((1, 8, tn), lambda l, j: (l, 0, j)),
        out_shape=jax.ShapeDtypeStruct((depth, 8, n), F32),
        compiler_params=_cparams(("parallel", "parallel")),
        name="adaln_rows",
    )(cond8, ada_w, ada_b.reshape(depth, 1, n))


def _modulated_norm(x, mod, nw, shift_row, scale_row):
    return _rms(x) * nw * (1.0 + mod[scale_row:scale_row + 1]) + mod[shift_row:shift_row + 1]


def _inproj_even_kernel(x_ref, mod_ref, nw_ref, w_ref, cos_ref, sin_ref, o_ref):
    h = _modulated_norm(x_ref[...], mod_ref[0], nw_ref[...], 0, 1).astype(BF16)
    n_main = o_ref.shape[1]
    qk = jnp.dot(h, w_ref[:, 0:512], preferred_element_type=F32)
    qk_sw = jnp.dot(h, w_ref[:, n_main:n_main + 512], preferred_element_type=F32)
    o_ref[:, 0:512] = (qk * cos_ref[...] + qk_sw * sin_ref[...]).astype(BF16)
    for c0 in range(512, n_main, 512):
        o_ref[:, c0:c0 + 512] = jnp.dot(h, w_ref[:, c0:c0 + 512],
                                        preferred_element_type=F32).astype(BF16)


def _inproj_odd_kernel(x_ref, mod_ref, nw_ref, w_ref, w2f_ref, w2b_ref, bf_ref, bb_ref,
                       o_ref, laf_ref, lab_ref):
    h = _modulated_norm(x_ref[...], mod_ref[0], nw_ref[...], 0, 1).astype(BF16)
    n_main = o_ref.shape[1]
    kd = GLA_HEADS * GLA_DK
    q = jnp.dot(h, w_ref[:, 0:kd], preferred_element_type=F32)
    o_ref[:, 0:kd] = (q * (GLA_DK ** -0.5)).astype(BF16)
    for c0 in range(kd, n_main, 512):
        o_ref[:, c0:c0 + 512] = jnp.dot(h, w_ref[:, c0:c0 + 512],
                                        preferred_element_type=F32).astype(BF16)
    low = jnp.dot(h, w_ref[:, n_main:n_main + LANES], preferred_element_type=F32)

    def log_gate(w2_ref, b_ref):
        z = jnp.dot(low, w2_ref[...], preferred_element_type=F32, precision=HIGHEST) + b_ref[...]
        return (jnp.minimum(z, 0.0) - jnp.log(1.0 + jnp.exp(-jnp.abs(z)))) * (1.0 / GLA_TAU)

    laf_ref[...] = log_gate(w2f_ref, bf_ref)
    lab_ref[...] = log_gate(w2b_ref, bb_ref)


def _seg_of_tile(i, tm, t_lat, seq):
    return jnp.where(i < t_lat // tm, 1 + (i * tm) // seq, 0)


def inproj_even(x_all, mods, nw, w_ext, cos_t, sin_t, *, t_lat, seq, tm=512):
    t_all, d = x_all.shape
    n_ext = w_ext.shape[1]
    n_main = n_ext - 512
    n_lat_tiles, pos_tiles = t_lat // tm, seq // tm

    def pos_map(i):
        return (jnp.where(i < n_lat_tiles, i % pos_tiles, pos_tiles), 0)

    return pl.pallas_call(
        _inproj_even_kernel,
        grid=(t_all // tm,),
        in_specs=[pl.BlockSpec((tm, d), lambda i: (i, 0)),
                  pl.BlockSpec((1, 6, d), lambda i: (_seg_of_tile(i, tm, t_lat, seq), 0, 0)),
                  pl.BlockSpec((1, d), lambda i: (0, 0)),
                  pl.BlockSpec((d, n_ext), lambda i: (0, 0)),
                  pl.BlockSpec((tm, 512), pos_map),
                  pl.BlockSpec((tm, 512), pos_map)],
        out_specs=pl.BlockSpec((tm, n_main), lambda i: (i, 0)),
        out_shape=jax.ShapeDtypeStruct((t_all, n_main), BF16),
        compiler_params=_cparams(("parallel",)),
        name="inproj_even",
    )(x_all, mods, nw, w_ext, cos_t, sin_t)


def inproj_odd(x_all, mods, nw, w_ext, w2f, w2b, bf, bb, *, t_lat, seq, tm=512):
    t_all, d = x_all.shape
    n_main = w_ext.shape[1] - LANES
    kd = GLA_HEADS * GLA_DK
    full = lambda shape: pl.BlockSpec(shape, lambda i: (0,) * len(shape))
    return pl.pallas_call(
        _inproj_odd_kernel,
        grid=(t_all // tm,),
        in_specs=[pl.BlockSpec((tm, d), lambda i: (i, 0)),
                  pl.BlockSpec((1, 6, d), lambda i: (_seg_of_tile(i, tm, t_lat, seq), 0, 0)),
                  full((1, d)), full(w_ext.shape), full(w2f.shape), full(w2b.shape),
                  full((1, kd)), full((1, kd))],
        out_specs=[pl.BlockSpec((tm, n_main), lambda i: (i, 0)),
                   pl.BlockSpec((tm, kd), lambda i: (i, 0)),
                   pl.BlockSpec((tm, kd), lambda i: (i, 0))],
        out_shape=[jax.ShapeDtypeStruct((t_all, n_main), BF16),
                   jax.ShapeDtypeStruct((t_all, kd), F32),
                   jax.ShapeDtypeStruct((t_all, kd), F32)],
        compiler_params=_cparams(("parallel",)),
        name="inproj_odd",
    )(x_all, mods, nw, w_ext, w2f, w2b, bf, bb)


def _scan_kernel(*refs, heads, dk, dv, chunk, gated, reverse, log_decay):
    it = iter(refs)
    q_ref, k_ref, v_ref = next(it), next(it), next(it)
    la_ref = next(it) if gated else None
    if reverse:
        oprev_ref, gate_ref, nw_ref = next(it), next(it), next(it)
    out_ref = next(it)
    state_ref = next(it)
    dec_ref = None if gated else next(it)

    hpg = LANES // dk
    groups = heads // hpg
    rows = q_ref.shape[0]
    n_chunks = rows // chunk
    j = pl.program_id(1)

    row_i = lax.broadcasted_iota(jnp.int32, (chunk, chunk), 0)
    col_i = lax.broadcasted_iota(jnp.int32, (chunk, chunk), 1)
    keep = (col_i >= row_i) if reverse else (row_i >= col_i)

    @pl.when(j == 0)
    def _init():
        state_ref[...] = jnp.zeros_like(state_ref)
        if not gated:
            pos = lax.broadcasted_iota(jnp.int32, (chunk, LANES), 0)
            steps = ((chunk - pos) if reverse else (pos + 1)).astype(F32)
            lane = lax.broadcasted_iota(jnp.int32, (chunk, LANES), 1)
            for g in range(groups):
                lg = jnp.zeros((chunk, LANES), F32)
                for a in range(hpg):
                    lg = jnp.where(lane // dk == a, log_decay[g * hpg + a], lg)
                logb = steps * lg
                b_end = float(chunk) * lg
                dec_ref[g, 0] = jnp.exp(logb)
                dec_ref[g, 1] = jnp.exp(-logb)
                dec_ref[g, 2] = jnp.exp(b_end - logb)
                dec_ref[g, 3] = jnp.exp(b_end)

    if gated:
        tri = jnp.where(keep, 1.0, 0.0).astype(F32)

    lane1 = lax.broadcasted_iota(jnp.int32, (1, LANES), 1)
    order = range(n_chunks - 1, -1, -1) if reverse else range(n_chunks)
    for c in order:
        r0 = c * chunk
        for g in range(groups):
            ksl = slice(g * LANES, (g + 1) * LANES)
            qg = q_ref[r0:r0 + chunk, ksl].astype(F32)
            kg = k_ref[r0:r0 + chunk, ksl].astype(F32)
            if gated:
                la = la_ref[r0:r0 + chunk, ksl]
                logb = jnp.dot(tri, la, preferred_element_type=F32, precision=HIGHEST)
                b_end = logb[0:1] if reverse else logb[chunk - 1:chunk]
                e_q, e_k = jnp.exp(logb), jnp.exp(-logb)
                e_s, e_e = jnp.exp(b_end - logb), jnp.exp(b_end)
            else:
                e_q, e_k, e_s = dec_ref[g, 0], dec_ref[g, 1], dec_ref[g, 2]
                e_e = dec_ref[g, 3][0:1]
            qd = qg * e_q
            kd_ = (kg * e_k).astype(BF16)
            ks = (kg * e_s).astype(BF16)
            for a in range(hpg):
                h = g * hpg + a
                qa = (jnp.where(lane1 // dk == a, qd, 0.0) if hpg > 1 else qd).astype(BF16)
                vh = v_ref[r0:r0 + chunk, h * dv:(h + 1) * dv]
                s = lax.dot_general(qa, kd_, (((1,), (1,)), ((), ())),
                                    preferred_element_type=F32)
                s = jnp.where(keep, s, 0.0).astype(BF16)
                st = state_ref[h]
                o = (jnp.dot(s, vh, preferred_element_type=F32)
                     + lax.dot_general(qa, st.astype(BF16), (((1,), (1,)), ((), ())),
                                       preferred_element_type=F32))
                state_ref[h] = e_e * st + lax.dot_general(
                    vh, ks, (((0,), (0,)), ((), ())), preferred_element_type=F32)
                osl = (slice(r0, r0 + chunk), slice(h * dv, (h + 1) * dv))
                if reverse:
                    o = _rms(o + oprev_ref[osl]) * nw_ref[...]
                    out_ref[osl] = (o * _silu(gate_ref[osl].astype(F32))).astype(BF16)
                else:
                    out_ref[osl] = o


def bidir_scan(proj, la_f, la_b, norm_w, *, batch, seq, ctx_len, heads, dk, dv, chunk,
               q_blk, k_blk, v_blk, g_blk, log_decay_f=None, log_decay_b=None):
    t_all = proj.shape[0]
    gated = la_f is not None
    hk, hv = heads * dk, heads * dv
    rb = SCAN_ROWS
    lat_blocks = seq // rb
    ctx_base = (batch * seq) // rb
    assert ctx_len == rb

    def rows_fwd(b, j):
        return jnp.where(j == 0, ctx_base + b, b * lat_blocks + j - 1)

    def rows_bwd(b, j):
        return jnp.where(j == 0, ctx_base + b, b * lat_blocks + lat_blocks - j)

    outs = None
    for reverse, rows_of, la, ld in ((False, rows_fwd, la_f, log_decay_f),
                                     (True, rows_bwd, la_b, log_decay_b)):
        spec = lambda w, cb: pl.BlockSpec((rb, w), lambda b, j, cb=cb: (rows_of(b, j), cb))
        in_specs = [spec(hk, q_blk), spec(hk, k_blk), spec(hv, v_blk)]
        args = [proj, proj, proj]
        if gated:
            in_specs.append(spec(hk, 0))
            args.append(la)
        if reverse:
            in_specs += [spec(hv, 0), spec(hv, g_blk), pl.BlockSpec((1, dv), lambda b, j: (0, 0))]
            args += [outs, proj, norm_w]
        scratch = [pltpu.VMEM((heads, dv, LANES), F32)]
        if not gated:
            scratch.append(pltpu.VMEM((hk // LANES, 4, chunk, LANES), F32))
        kern = functools.partial(_scan_kernel, heads=heads, dk=dk, dv=dv, chunk=chunk,
                                 gated=gated, reverse=reverse, log_decay=ld)
        outs = pl.pallas_call(
            kern,
            grid=(batch, lat_blocks + 1),
            in_specs=in_specs,
            out_specs=spec(hv, 0),
            out_shape=jax.ShapeDtypeStruct((t_all, hv), BF16 if reverse else F32),
            scratch_shapes=scratch,
            compiler_params=_cparams(("parallel", "arbitrary")),
            name="scan_bwd" if reverse else "scan_fwd",
        )(*args)
    return outs


def _shortconv_kernel(z_ref, zp_ref, zn_ref, w_ref, b_ref, u_ref, x0_ref, *, tiles_per_seq):
    i = pl.program_id(0)
    tm = z_ref.shape[0]
    z = z_ref[...].astype(F32)
    first = (i % tiles_per_seq) == 0
    last = (i % tiles_per_seq) == tiles_per_seq - 1
    halo = zp_ref.shape[0]
    prev_row = jnp.where(first, 0.0, zp_ref[halo - 1:halo, :].astype(F32))
    next_row = jnp.where(last, 0.0, zn_ref[0:1, :].astype(F32))
    row = lax.broadcasted_iota(jnp.int32, z.shape, 0)
    z_prev = jnp.where(row == 0, prev_row, pltpu.roll(z, 1, 0))
    z_next = jnp.where(row == tm - 1, next_row, pltpu.roll(z, tm - 1, 0))
    y = w_ref[0:1] * z_prev + w_ref[1:2] * z + w_ref[2:3] * z_next + b_ref[...]
    c = HYENA_CH
    x0_ref[...] = y[:, 0:c].astype(BF16)
    u_ref[...] = (y[:, c:2 * c] * y[:, 2 * c:3 * c]).astype(BF16)


def hyena_shortconv(proj, short_w, short_b, *, row0, batch, seq_len, z_blk, tm=256):
    halo = 16
    tiles_per_seq = seq_len // tm
    n_tiles = batch * tiles_per_seq
    t0, h_per_tile = row0 // tm, tm // halo
    nz = 3 * HYENA_CH
    n_halo_blocks = proj.shape[0] // halo
    out_map = lambda i: (i % tiles_per_seq, i // tiles_per_seq)
    kern = functools.partial(_shortconv_kernel, tiles_per_seq=tiles_per_seq)
    return pl.pallas_call(
        kern,
        grid=(n_tiles,),
        in_specs=[pl.BlockSpec((tm, nz), lambda i: (t0 + i, z_blk)),
                  pl.BlockSpec((halo, nz),
                               lambda i: (jnp.maximum((t0 + i) * h_per_tile - 1, 0), z_blk)),
                  pl.BlockSpec((halo, nz),
                               lambda i: (jnp.minimum((t0 + i + 1) * h_per_tile,
                                                      n_halo_blocks - 1), z_blk)),
                  pl.BlockSpec((3, nz), lambda i: (0, 0)),
                  pl.BlockSpec((1, nz), lambda i: (0, 0))],
        out_specs=[pl.BlockSpec((tm, HYENA_CH), out_map), pl.BlockSpec((tm, HYENA_CH), out_map)],
        out_shape=[jax.ShapeDtypeStruct((seq_len, batch * HYENA_CH), BF16)] * 2,
        compiler_params=_cparams(("parallel",)),
        name="hyena_shortconv",
    )(proj, proj, proj, short_w, short_b)


def _filter_kernel(feat_ref, w1_ref, b1_ref, w2_ref, b2_ref, w3_ref, b3_ref, w4_ref, fq_ref,
                   hp_ref, hm_ref, *, seq_len):
    i = pl.program_id(0)
    tl = feat_ref.shape[0]
    fq = fq_ref[...]
    dot = lambda a, b: jnp.dot(a, b, preferred_element_type=F32, precision=HIGHEST)
    h = jnp.sin(fq * (dot(feat_ref[...], w1_ref[...]) + b1_ref[...]))
    h = jnp.sin(fq * (dot(h, w2_ref[...]) + b2_ref[...]))
    h = jnp.sin(fq * (dot(h, w3_ref[...]) + b3_ref[...]))
    h = dot(h, w4_ref[...])
    c = HYENA_CH
    max_decay = math.log(HYENA_TARGET) / HYENA_FAST_DECAY
    min_decay = math.log(HYENA_TARGET) / HYENA_SLOW_DECAY
    ch = lax.broadcasted_iota(jnp.int32, (tl, c), 1).astype(F32)
    deltas = min_decay + ch * ((max_decay - min_decay) / (c - 1))
    row = lax.broadcasted_iota(jnp.int32, (tl, c), 0) + i * tl
    t = row.astype(F32) * (1.0 / (seq_len - 1))
    window = jnp.exp(-t * jnp.abs(deltas))
    h_f = h[:, 0:c] * window
    h_b = jnp.where(row == 0, 0.0, h[:, c:2 * c] * window)
    hp_ref[...] = (h_f + h_b).astype(BF16)
    hm_ref[...] = (h_f - h_b).astype(BF16)


def hyena_filter_pair(feats, w1p, b1, w2, b2, w3, b3, w4, fq):
    seq_len = feats.shape[0]
    tl = min(512, seq_len)
    full = lambda a: pl.BlockSpec(a.shape, lambda i: (0,) * a.ndim)
    kern = functools.partial(_filter_kernel, seq_len=seq_len)
    return pl.pallas_call(
        kern,
        grid=(seq_len // tl,),
        in_specs=[pl.BlockSpec((tl, feats.shape[1]), lambda i: (i, 0)),
                  full(w1p), full(b1), full(w2), full(b2), full(w3), full(b3), full(w4), full(fq)],
        out_specs=[pl.BlockSpec((tl, HYENA_CH), lambda i: (i, 0))] * 2,
        out_shape=[jax.ShapeDtypeStruct((seq_len, HYENA_CH), BF16)] * 2,
        compiler_params=_cparams(("parallel",)),
        name="hyena_filter",
    )(feats, w1p, b1, w2, b2, w3, b3, w4, fq)


def _dft_fwd_kernel(c_ref, s_ref, u_ref, a_ref, b_ref):
    u = u_ref[...]
    a_ref[...] = jnp.dot(c_ref[...], u, preferred_element_type=F32)
    b_ref[...] = jnp.dot(s_ref[...], u, preferred_element_type=F32)


def dft_forward(cs, ss, u_ext):
    n, cols = u_ext.shape
    tf, tn = min(512, n), 1024
    return pl.pallas_call(
        _dft_fwd_kernel,
        grid=(cols // tn, n // tf),
        in_specs=[pl.BlockSpec((tf, n), lambda c, f: (f, 0)),
                  pl.BlockSpec((tf, n), lambda c, f: (f, 0)),
                  pl.BlockSpec((n, tn), lambda c, f: (0, c))],
        out_specs=[pl.BlockSpec((tf, tn), lambda c, f: (f, c))] * 2,
        out_shape=[jax.ShapeDtypeStruct((n, cols), F32)] * 2,
        compiler_params=_cparams(("parallel", "parallel")),
        name="dft_forward",
    )(cs, ss, u_ext)


def _spectral_mul_kernel(au_ref, bu_ref, ap_ref, bp_ref, am_ref, bm_ref, hc_ref, hs_ref,
                         yre_ref, yim_ref, *, scale):
    hc, hs = hc_ref[...], hs_ref[...]
    k_re = hc * ap_ref[...] + hs * bp_ref[...]
    k_im = hs * am_ref[...] - hc * bm_ref[...]
    a, b = au_ref[...], bu_ref[...]
    yre_ref[...] = ((a * k_re + b * k_im) * scale).astype(BF16)
    yim_ref[...] = ((a * k_im - b * k_re) * scale).astype(BF16)


def spectral_mul(a, b, half_cos, half_sin, *, batch):
    n = a.shape[0]
    c = HYENA_CH
    tf = min(512, n)
    kern = functools.partial(_spectral_mul_kernel, scale=1.0 / n)
    ucol = lambda f, bi: (f, bi)
    return pl.pallas_call(
        kern,
        grid=(n // tf, batch),
        in_specs=[pl.BlockSpec((tf, c), ucol), pl.BlockSpec((tf, c), ucol),
                  pl.BlockSpec((tf, c), lambda f, bi: (f, batch)),
                  pl.BlockSpec((tf, c), lambda f, bi: (f, batch)),
                  pl.BlockSpec((tf, c), lambda f, bi: (f, batch + 1)),
                  pl.BlockSpec((tf, c), lambda f, bi: (f, batch + 1)),
                  pl.BlockSpec((tf, 1), lambda f, bi: (f, 0)),
                  pl.BlockSpec((tf, 1), lambda f, bi: (f, 0))],
        out_specs=[pl.BlockSpec((tf, c), ucol)] * 2,
        out_shape=[jax.ShapeDtypeStruct((n, batch * c), BF16)] * 2,
        compiler_params=_cparams(("parallel", "parallel")),
        name="hyena_spectral_mul",
    )(a, b, a, b, a, b, half_cos, half_sin)


def _dft_inv_kernel(c_ref, s_ref, yre_ref, yim_ref, u_ref, x0_ref, bias_ref, o_ref):
    y = (jnp.dot(c_ref[...], yre_ref[...], preferred_element_type=F32)
         - jnp.dot(s_ref[...], yim_ref[...], preferred_element_type=F32))
    u = u_ref[...].astype(F32)
    o_ref[...] = (x0_ref[...].astype(F32) * (y + u * bias_ref[...])).astype(BF16)


def dft_inverse(cs, ss, yre, yim, u, x0, bias, *, batch):
    n = cs.shape[0]
    c = HYENA_CH
    tt = min(512, n)
    col = lambda bi, t: (0, bi)
    tile = lambda bi, t: (t, bi)
    return pl.pallas_call(
        _dft_inv_kernel,
        grid=(batch, n // tt),
        in_specs=[pl.BlockSpec((tt, n), lambda bi, t: (t, 0)),
                  pl.BlockSpec((tt, n), lambda bi, t: (t, 0)),
                  pl.BlockSpec((n, c), col), pl.BlockSpec((n, c), col),
                  pl.BlockSpec((tt, c), tile), pl.BlockSpec((tt, c), tile),
                  pl.BlockSpec((1, c), lambda bi, t: (0, 0))],
        out_specs=pl.BlockSpec((tt, c), lambda bi, t: (bi * (n // tt) + t, 0)),
        out_shape=jax.ShapeDtypeStruct((batch * n, c), BF16),
        compiler_params=_cparams(("parallel", "parallel")),
        name="dft_inverse",
    )(cs, ss, yre, yim, u, x0, bias)


def _shifted_dft_tables(n):
    f = lax.broadcasted_iota(jnp.int32, (n, n), 0)
    s = lax.broadcasted_iota(jnp.int32, (n, n), 1)
    m = ((2 * f + 1) * (2 * s + 1)) % (8 * n)
    ang = m.astype(F32) * (2.0 * math.pi / (8 * n))
    half = (2 * jnp.arange(n, dtype=jnp.int32) + 1).astype(F32)[:, None] * (math.pi / (4 * n))
    return jnp.cos(ang).astype(BF16), jnp.sin(ang).astype(BF16), jnp.cos(half), jnp.sin(half)


def _filter_features(seq_len):
    t = jnp.linspace(0.0, 1.0, seq_len, dtype=F32)[:, None]
    bands = (HYENA_EMB - 1) // 2
    ang = 2.0 * math.pi * jnp.arange(seq_len, dtype=F32)[:, None] / seq_len
    fr = jnp.linspace(1e-4, bands - 1, bands, dtype=F32)[None, :]
    feats = jnp.concatenate([t, jnp.cos(fr * ang), -jnp.sin(fr * ang)], axis=-1)
    return jnp.pad(feats, ((0, 0), (0, LANES - HYENA_EMB)))


def hyena_long_conv(proj, hp, *, row0, batch, seq_len, z_blk):
    u, x0 = hyena_shortconv(proj, hp["short_w"], hp["short_b"], row0=row0, batch=batch,
                            seq_len=seq_len, z_blk=z_blk)
    f_p, f_m = hyena_filter_pair(_filter_features(seq_len), hp["w1"], hp["b1"], hp["w2"], hp["b2"],
                                 hp["w3"], hp["b3"], hp["w4"], hp["freq"])
    cs, ss, half_cos, half_sin = _shifted_dft_tables(seq_len)
    a, b = dft_forward(cs, ss, jnp.concatenate([u, f_p, f_m], axis=1))
    yre, yim = spectral_mul(a, b, half_cos, half_sin, batch=batch)
    return dft_inverse(cs, ss, yre, yim, u, x0, hp["bias"], batch=batch)


def _outproj_kernel(ma_ref, mb_ref, x_ref, mod_ref, nw_ref, w_ref, rw_ref,
                    xo_ref, h_ref, lt_ref):
    half = ma_ref.shape[1]
    y = (jnp.dot(ma_ref[...], w_ref[0:half], preferred_element_type=F32)
         + jnp.dot(mb_ref[...], w_ref[half:2 * half], preferred_element_type=F32))
    mod = mod_ref[0]
    x = x_ref[...] + mod[2:3] * y
    xo_ref[...] = x
    h = _modulated_norm(x, mod, nw_ref[...], 3, 4)
    h_ref[...] = h.astype(BF16)
    lt_ref[...] = lax.dot_general(rw_ref[...], h, (((1,), (1,)), ((), ())),
                                  preferred_element_type=F32, precision=HIGHEST)


def outproj(mix_a, a_blk, mix_b, b_blk, x_all, mods, nw, w_out, router_wt, *,
            n_rows, t_lat, seq, tm=512):
    d = x_all.shape[1]
    half = d // 2
    return pl.pallas_call(
        _outproj_kernel,
        grid=(n_rows // tm,),
        in_specs=[pl.BlockSpec((tm, half), lambda i: (i, a_blk)),
                  pl.BlockSpec((tm, half), lambda i: (i, b_blk)),
                  pl.BlockSpec((tm, d), lambda i: (i, 0)),
                  pl.BlockSpec((1, 6, d), lambda i: (_seg_of_tile(i, tm, t_lat, seq), 0, 0)),
                  pl.BlockSpec((1, d), lambda i: (0, 0)),
                  pl.BlockSpec((d, d), lambda i: (0, 0)),
                  pl.BlockSpec((N_EXPERTS, d), lambda i: (0, 0))],
        out_specs=[pl.BlockSpec((tm, d), lambda i: (i, 0)),
                   pl.BlockSpec((tm, d), lambda i: (i, 0)),
                   pl.BlockSpec((N_EXPERTS, tm), lambda i: (0, i))],
        out_shape=[jax.ShapeDtypeStruct((n_rows, d), F32),
                   jax.ShapeDtypeStruct((n_rows, d), BF16),
                   jax.ShapeDtypeStruct((N_EXPERTS, n_rows), F32)],
        compiler_params=_cparams(("parallel",)),
        name="outproj",
    )(mix_a, mix_b, x_all, mods, nw, w_out, router_wt)


def _first_max(x, idx, sentinel):
    m = jnp.max(x, axis=0, keepdims=True)
    first = jnp.min(jnp.where(x == m, idx, sentinel), axis=0, keepdims=True)
    return m, idx == first


def _router_kernel(lt_ref, bias_ref, eye_ref, g_ref):
    tn = lt_ref.shape[1]
    scores = jax.nn.sigmoid(lt_ref[...])
    sel = scores + bias_ref[...]
    neg = -jnp.inf
    in_grp = lax.broadcasted_iota(jnp.int32, (GROUP_SIZE, tn), 0)
    gscore = []
    for g in range(N_GROUPS):
        x = sel[g * GROUP_SIZE:(g + 1) * GROUP_SIZE]
        m1, hit = _first_max(x, in_grp, GROUP_SIZE)
        gscore.append(m1 + jnp.max(jnp.where(hit, neg, x), axis=0, keepdims=True))
    rows = []
    for g in range(N_GROUPS):
        beaten = jnp.zeros((1, tn), jnp.int32)
        for o in range(N_GROUPS):
            if o != g:
                wins = (gscore[o] >= gscore[g]) if o < g else (gscore[o] > gscore[g])
                beaten = beaten + wins.astype(jnp.int32)
        keep = jnp.broadcast_to(beaten < TOPK_GROUPS, (GROUP_SIZE, tn))
        rows.append(jnp.where(keep, sel[g * GROUP_SIZE:(g + 1) * GROUP_SIZE], neg))
    cand = jnp.concatenate(rows, axis=0)
    eidx = lax.broadcasted_iota(jnp.int32, cand.shape, 0)
    chosen = jnp.zeros(cand.shape, jnp.bool_)
    for _ in range(TOP_K):
        _, hit = _first_max(cand, eidx, N_EXPERTS)
        chosen = jnp.logical_or(chosen, hit)
        cand = jnp.where(hit, neg, cand)
    w = jnp.where(chosen, scores, 0.0)
    gates_t = w / jnp.sum(w, axis=0, keepdims=True) * ROUTED_SCALE
    gates_t = jnp.concatenate([gates_t, jnp.zeros((g_ref.shape[1] - N_EXPERTS, tn), F32)], axis=0)
    g_ref[...] = lax.dot_general(eye_ref[...], gates_t, (((1,), (1,)), ((), ())),
                                 preferred_element_type=F32, precision=HIGHEST)


def router(logits_t, router_bias, *, tn=256):
    e, t = logits_t.shape
    eye = jnp.eye(tn, dtype=F32)
    return pl.pallas_call(
        _router_kernel,
        grid=(t // tn,),
        in_specs=[pl.BlockSpec((e, tn), lambda i: (0, i)),
                  pl.BlockSpec((e, 1), lambda i: (0, 0)),
                  pl.BlockSpec((tn, tn), lambda i: (0, 0))],
        out_specs=pl.BlockSpec((tn, LANES), lambda i: (i, 0)),
        out_shape=jax.ShapeDtypeStruct((t, LANES), F32),
        compiler_params=_cparams(("parallel",)),
        name="router",
    )(logits_t, router_bias.reshape(e, 1), eye)


def _moe_kernel(h_ref, g_ref, wgu_ref, wd_ref, x_ref, mod_ref, fnw_ref, o_ref, acc_ref,
                *, n_routed, final_norm):
    e = pl.program_id(1)

    @pl.when(e == 0)
    def _():
        acc_ref[...] = jnp.zeros_like(acc_ref)

    ff = wd_ref.shape[1]
    gu = jnp.dot(h_ref[...], wgu_ref[0].astype(BF16), preferred_element_type=F32)
    act = _silu(gu[:, 0:ff]) * gu[:, ff:2 * ff]
    lane = lax.broadcasted_iota(jnp.int32, g_ref.shape, 1)
    gate = jnp.sum(jnp.where(lane == e, g_ref[...], 0.0), axis=1, keepdims=True)
    gate = jnp.where(e == n_routed, 1.0, gate)
    acc_ref[...] += jnp.dot((act * gate).astype(BF16), wd_ref[0].astype(BF16),
                            preferred_element_type=F32)

    @pl.when(e == n_routed)
    def _():
        x = x_ref[...] + mod_ref[0][5:6] * acc_ref[...]
        if final_norm:
            x = _rms(x) * fnw_ref[...]
        o_ref[...] = x


def moe_dense(h, gates, w_gu, w_d, x_all, mods, fnw, *, n_rows, t_lat, seq, final_norm, tm=1024):
    d = h.shape[1]
    n_e = w_gu.shape[0]
    ff = w_d.shape[1]
    kern = functools.partial(_moe_kernel, n_routed=n_e - 1, final_norm=final_norm)
    return pl.pallas_call(
        kern,
        grid=(n_rows // tm, n_e),
        in_specs=[pl.BlockSpec((tm, d), lambda i, e: (i, 0)),
                  pl.BlockSpec((tm, gates.shape[1]), lambda i, e: (i, 0)),
                  pl.BlockSpec((1, d, 2 * ff), lambda i, e: (e, 0, 0)),
                  pl.BlockSpec((1, ff, d), lambda i, e: (e, 0, 0)),
                  pl.BlockSpec((tm, d), lambda i, e: (i, 0)),
                  pl.BlockSpec((1, 6, d), lambda i, e: (_seg_of_tile(i, tm, t_lat, seq), 0, 0)),
                  pl.BlockSpec((1, d), lambda i, e: (0, 0))],
        out_specs=pl.BlockSpec((tm, d), lambda i, e: (i, 0)),
        out_shape=jax.ShapeDtypeStruct((n_rows, d), F32),
        scratch_shapes=[pltpu.VMEM((tm, d), F32)],
        compiler_params=_cparams(("parallel", "arbitrary")),
        name="moe_dense",
    )(h, gates, w_gu, w_d, x_all, mods, fnw)


def _rope_tables(seq, tm):
    half = RET_DK // 2
    pos = jnp.arange(seq, dtype=jnp.int32)
    row = (pos // GRID_W).astype(F32)
    col = (pos % GRID_W).astype(F32)
    inv = ROPE_BASE ** (-jnp.arange(0, half, 2, dtype=F32) / half)
    a_row, a_col = row[:, None] * inv[None, :], col[:, None] * inv[None, :]
    cos_h = jnp.concatenate([jnp.cos(a_row)] * 2 + [jnp.cos(a_col)] * 2, axis=-1)
    sin_h = jnp.concatenate([-jnp.sin(a_row), jnp.sin(a_row), -jnp.sin(a_col), jnp.sin(a_col)], -1)
    reps = 2 * RET_HEADS
    cos_t = jnp.concatenate([jnp.tile(cos_h, (1, reps)), jnp.ones((tm, reps * RET_DK), F32)], 0)
    sin_t = jnp.concatenate([jnp.tile(sin_h, (1, reps)), jnp.zeros((tm, reps * RET_DK), F32)], 0)
    return cos_t, sin_t


def _rope_partner_columns():
    quarter = RET_DK // 4
    idx = jnp.arange(2 * RET_HEADS * RET_DK, dtype=jnp.int32)
    within = idx % (2 * quarter)
    return jnp.where(within < quarter, idx + quarter, idx - quarter)


def kernel(x, c, ctx, c_ctx, ada_w, ada_b, norm1_w, norm2_w, ev_w_in, ev_short_w, ev_short_b, ev_filt_w1, ev_filt_b1, ev_filt_w2, ev_filt_b2, ev_filt_w3, ev_filt_b3, ev_filt_w4, ev_filt_freq, ev_hyena_bias, ev_w_out, od_w_in, od_gate_w1_f, od_gate_w2_f, od_gate_b_f, od_gate_w1_b, od_gate_w2_b, od_gate_b_b, od_norm_w, od_w_out, router_w, router_bias, exp_w_gate, exp_w_up, exp_w_down, sh_w_gate, sh_w_up, sh_w_down, final_norm_w):
    batch, seq, d = x.shape
    ctx_len = ctx.shape[1]
    depth = ada_w.shape[0]
    t_lat, t_ctx = batch * seq, batch * ctx_len
    t_all = t_lat + t_ctx
    tm = 512

    x_all = jnp.concatenate([x.reshape(t_lat, d), ctx.reshape(t_ctx, d)], axis=0)
    cond8 = jnp.concatenate([c_ctx[None, :], c, jnp.zeros((8 - 1 - batch, d), F32)], axis=0)
    mods_all = adaln_rows(cond8, ada_w, ada_b).reshape(depth, 8, 6, d)

    for i in range(depth):
        last = i == depth - 1
        j = i // 2
        mods = mods_all[i]
        nw1, nw2 = norm1_w[i][None, :], norm2_w[i][None, :]
        if i % 2 == 0:
            qk = 2 * RET_HEADS * RET_DK
            k_scale = jnp.concatenate([jnp.ones((qk // 2,), F32),
                                       jnp.full((qk // 2,), RET_DK ** -0.5, F32)])
            w_in = ev_w_in[j]
            w_qk = w_in[:, :qk] * k_scale
            w_ext = jnp.concatenate([w_qk, w_in[:, qk:], w_qk[:, _rope_partner_columns()]],
                                    axis=1).astype(BF16)
            cos_t, sin_t = _rope_tables(seq, tm)
            proj = inproj_even(x_all, mods, nw1, w_ext, cos_t, sin_t, t_lat=t_lat, seq=seq, tm=tm)
            log_g = [math.log1p(-2.0 ** (-5.0 - h)) for h in range(RET_HEADS)]
            mix_a = bidir_scan(proj, None, None, jnp.ones((1, RET_DV), F32),
                               batch=batch, seq=seq, ctx_len=ctx_len, heads=RET_HEADS, dk=RET_DK,
                               dv=RET_DV, chunk=RET_CHUNK, q_blk=0, k_blk=1, v_blk=1, g_blk=2,
                               log_decay_f=log_g, log_decay_b=log_g[::-1])
            hp = dict(short_w=ev_short_w[j], short_b=ev_short_b[j][None, :],
                      w1=jnp.pad(ev_filt_w1[j], ((0, LANES - HYENA_EMB), (0, 0))),
                      b1=ev_filt_b1[j][None, :], w2=ev_filt_w2[j], b2=ev_filt_b2[j][None, :],
                      w3=ev_filt_w3[j], b3=ev_filt_b3[j][None, :], w4=ev_filt_w4[j],
                      freq=ev_filt_freq[j][None, :], bias=ev_hyena_bias[j][None, :])
            hy_lat = hyena_long_conv(proj, hp, row0=0, batch=batch, seq_len=seq, z_blk=1)
            hy_ctx = hyena_long_conv(proj, hp, row0=t_lat, batch=batch, seq_len=ctx_len, z_blk=1)
            mix_b = jnp.concatenate([hy_lat, hy_ctx], axis=0)
            a_blk, b_blk = 0, 0
            w_out = ev_w_out[j].astype(BF16)
        else:
            kd = GLA_HEADS * GLA_DK
            pad_cols = LANES - 2 * GLA_RANK
            w_ext = jnp.concatenate([od_w_in[j], od_gate_w1_f[j], od_gate_w1_b[j],
                                     jnp.zeros((d, pad_cols), F32)], axis=1).astype(BF16)
            w2f = jnp.pad(od_gate_w2_f[j], ((0, LANES - GLA_RANK), (0, 0)))
            w2b = jnp.pad(od_gate_w2_b[j], ((GLA_RANK, LANES - 2 * GLA_RANK), (0, 0)))
            proj, la_f, la_b = inproj_odd(x_all, mods, nw1, w_ext, w2f, w2b,
                                          od_gate_b_f[j][None, :], od_gate_b_b[j][None, :],
                                          t_lat=t_lat, seq=seq, tm=tm)
            mix_a = bidir_scan(proj, la_f, la_b, od_norm_w[j][None, :],
                               batch=batch, seq=seq, ctx_len=ctx_len, heads=GLA_HEADS, dk=GLA_DK,
                               dv=GLA_DV, chunk=GLA_CHUNK, q_blk=0, k_blk=1, v_blk=1, g_blk=2)
            mix_b = mix_a
            a_blk, b_blk = 0, 1
            w_out = od_w_out[j].astype(BF16)

        n_rows = t_lat if last else t_all
        x_mid, h2, logits_t = outproj(mix_a, a_blk, mix_b, b_blk, x_all, mods, nw2, w_out,
                                      router_w[i].T, n_rows=n_rows, t_lat=t_lat, seq=seq, tm=tm)
        gates = router(logits_t, router_bias[i])
        w_gu = jnp.concatenate([jnp.concatenate([exp_w_gate[i], exp_w_up[i]], axis=2),
                                jnp.concatenate([sh_w_gate[i], sh_w_up[i]], axis=1)[None]], axis=0)
        w_d = jnp.concatenate([exp_w_down[i], sh_w_down[i][None]], axis=0)
        x_all = moe_dense(h2, gates, w_gu, w_d, x_mid, mods, final_norm_w[None, :],
                          n_rows=n_rows, t_lat=t_lat, seq=seq, final_norm=last)
    return x_all[:t_lat].reshape(batch, seq, d)
```

```python
import functools
import math

import jax
import jax.numpy as jnp
from jax import lax
from jax.experimental import pallas as pl
from jax.experimental.pallas import tpu as pltpu

F32 = jnp.float32
BF16 = jnp.bfloat16
HIGHEST = lax.Precision.HIGHEST

D_MODEL = 1024
GRID_W = 64
NORM_EPS = 1e-6
RET_HEADS, RET_DK, RET_DV, RET_CHUNK = 4, 64, 128, 128
ROPE_BASE = 10000.0
HYENA_CH, HYENA_EMB = 512, 33
HYENA_FAST_DECAY, HYENA_SLOW_DECAY, HYENA_TARGET = 0.3, 1.5, 1e-2
GLA_HEADS, GLA_DK, GLA_DV, GLA_RANK, GLA_TAU, GLA_CHUNK = 4, 128, 256, 16, 16.0, 64
N_EXPERTS, TOP_K, N_GROUPS, TOPK_GROUPS = 64, 8, 8, 4
GROUP_SIZE = N_EXPERTS // N_GROUPS
EXPERT_FF = 256
ROUTED_SCALE = 2.5

LANES = 128
SCAN_ROWS = 256
MOE_TILE = 256
ROW_UNIT = 16
EXPERT_ROWS = 256
SLOT_CHUNK = 512
SLOTS = -(-(MOE_TILE * TOP_K + N_EXPERTS * (ROW_UNIT - 1) + ROW_UNIT) // SLOT_CHUNK) * SLOT_CHUNK
N_UNITS = SLOTS // ROW_UNIT
PAD_UNITS = EXPERT_ROWS // ROW_UNIT
VMEM_LIMIT = 56 * 1024 * 1024


def _cparams(sem):
    return pltpu.CompilerParams(dimension_semantics=sem, vmem_limit_bytes=VMEM_LIMIT)


def _rms(x):
    return x * lax.rsqrt(jnp.mean(x * x, axis=-1, keepdims=True) + NORM_EPS)


def _silu(x):
    return x * jax.nn.sigmoid(x)


def _adaln_kernel(c_ref, w_ref, b_ref, o_ref):
    c = c_ref[...]
    o_ref[0] = jnp.dot(_silu(c), w_ref[0], preferred_element_type=F32,
                       precision=HIGHEST) + b_ref[0]


def adaln_rows(cond8, ada_w, ada_b):
    depth, d, n = ada_w.shape
    tn = 512
    return pl.pallas_call(
        _adaln_kernel,
        grid=(depth, n // tn),
        in_specs=[pl.BlockSpec((8, d), lambda l, j: (0, 0)),
                  pl.BlockSpec((1, d, tn), lambda l, j: (l, 0, j)),
                  pl.BlockSpec((1, 1, tn), lambda l, j: (l, 0, j))],
        out_specs=pl.BlockSpec((1, 8, tn), lambda l, j: (l, 0, j)),
        out_shape=jax.ShapeDtypeStruct((depth, 8, n), F32),
        compiler_params=_cparams(("parallel", "parallel")),
        name="adaln_rows",
    )(cond8, ada_w, ada_b.reshape(depth, 1, n))


def _modulated_norm(x, mod, nw, shift_row, scale_row):
    return _rms(x) * nw * (1.0 + mod[scale_row:scale_row + 1]) + mod[shift_row:shift_row + 1]


def _inproj_even_kernel(x_ref, mod_ref, nw_ref, w_ref, cos_ref, sin_ref, o_ref):
    h = _modulated_norm(x_ref[...], mod_ref[0], nw_ref[...], 0, 1).astype(BF16)
    n_main = o_ref.shape[1]
    qk = jnp.dot(h, w_ref[:, 0:512], preferred_element_type=F32)
    qk_sw = jnp.dot(h, w_ref[:, n_main:n_main + 512], preferred_element_type=F32)
    o_ref[:, 0:512] = (qk * cos_ref[...] + qk_sw * sin_ref[...]).astype(BF16)
    for c0 in range(512, n_main, 512):
        o_ref[:, c0:c0 + 512] = jnp.dot(h, w_ref[:, c0:c0 + 512],
                                        preferred_element_type=F32).astype(BF16)


def _inproj_odd_kernel(x_ref, mod_ref, nw_ref, w_ref, w2f_ref, w2b_ref, bf_ref, bb_ref,
                       o_ref, laf_ref, lab_ref):
    h = _modulated_norm(x_ref[...], mod_ref[0], nw_ref[...], 0, 1).astype(BF16)
    n_main = o_ref.shape[1]
    kd = GLA_HEADS * GLA_DK
    q = jnp.dot(h, w_ref[:, 0:kd], preferred_element_type=F32)
    o_ref[:, 0:kd] = (q * (GLA_DK ** -0.5)).astype(BF16)
    for c0 in range(kd, n_main, 512):
        o_ref[:, c0:c0 + 512] = jnp.dot(h, w_ref[:, c0:c0 + 512],
                                        preferred_element_type=F32).astype(BF16)
    low = jnp.dot(h, w_ref[:, n_main:n_main + LANES], preferred_element_type=F32)

    def log_gate(w2_ref, b_ref):
        z = jnp.dot(low, w2_ref[...], preferred_element_type=F32, precision=HIGHEST) + b_ref[...]
        return (jnp.minimum(z, 0.0) - jnp.log(1.0 + jnp.exp(-jnp.abs(z)))) * (1.0 / GLA_TAU)

    laf_ref[...] = log_gate(w2f_ref, bf_ref)
    lab_ref[...] = log_gate(w2b_ref, bb_ref)


def _seg_of_tile(i, tm, t_lat, seq):
    return jnp.where(i < t_lat // tm, 1 + (i * tm) // seq, 0)


def inproj_even(x_all, mods, nw, w_ext, cos_t, sin_t, *, t_lat, seq, tm=512):
    t_all, d = x_all.shape
    n_ext = w_ext.shape[1]
    n_main = n_ext - 512
    n_lat_tiles, pos_tiles = t_lat // tm, seq // tm

    def pos_map(i):
        return (jnp.where(i < n_lat_tiles, i % pos_tiles, pos_tiles), 0)

    return pl.pallas_call(
        _inproj_even_kernel,
        grid=(t_all // tm,),
        in_specs=[pl.BlockSpec((tm, d), lambda i: (i, 0)),
                  pl.BlockSpec((1, 6, d), lambda i: (_seg_of_tile(i, tm, t_lat, seq), 0, 0)),
                  pl.BlockSpec((1, d), lambda i: (0, 0)),
                  pl.BlockSpec((d, n_ext), lambda i: (0, 0)),
                  pl.BlockSpec((tm, 512), pos_map),
                  pl.BlockSpec((tm, 512), pos_map)],
        out_specs=pl.BlockSpec((tm, n_main), lambda i: (i, 0)),
        out_shape=jax.ShapeDtypeStruct((t_all, n_main), BF16),
        compiler_params=_cparams(("parallel",)),
        name="inproj_even",
    )(x_all, mods, nw, w_ext, cos_t, sin_t)


def inproj_odd(x_all, mods, nw, w_ext, w2f, w2b, bf, bb, *, t_lat, seq, tm=512):
    t_all, d = x_all.shape
    n_main = w_ext.shape[1] - LANES
    kd = GLA_HEADS * GLA_DK
    full = lambda shape: pl.BlockSpec(shape, lambda i: (0,) * len(shape))
    return pl.pallas_call(
        _inproj_odd_kernel,
        grid=(t_all // tm,),
        in_specs=[pl.BlockSpec((tm, d), lambda i: (i, 0)),
                  pl.BlockSpec((1, 6, d), lambda i: (_seg_of_tile(i, tm, t_lat, seq), 0, 0)),
                  full((1, d)), full(w_ext.shape), full(w2f.shape), full(w2b.shape),
                  full((1, kd)), full((1, kd))],
        out_specs=[pl.BlockSpec((tm, n_main), lambda i: (i, 0)),
                   pl.BlockSpec((tm, kd), lambda i: (i, 0)),
                   pl.BlockSpec((tm, kd), lambda i: (i, 0))],
        out_shape=[jax.ShapeDtypeStruct((t_all, n_main), BF16),
                   jax.ShapeDtypeStruct((t_all, kd), F32),
                   jax.ShapeDtypeStruct((t_all, kd), F32)],
        compiler_params=_cparams(("parallel",)),
        name="inproj_odd",
    )(x_all, mods, nw, w_ext, w2f, w2b, bf, bb)


def _scan_kernel(*refs, heads, dk, dv, chunk, gated, reverse, log_decay):
    it = iter(refs)
    q_ref, k_ref, v_ref = next(it), next(it), next(it)
    la_ref = next(it) if gated else None
    if reverse:
        oprev_ref, gate_ref, nw_ref = next(it), next(it), next(it)
    out_ref = next(it)
    state_ref = next(it)
    dec_ref = None if gated else next(it)

    hpg = LANES // dk
    groups = heads // hpg
    rows = q_ref.shape[0]
    n_chunks = rows // chunk
    j = pl.program_id(1)

    row_i = lax.broadcasted_iota(jnp.int32, (chunk, chunk), 0)
    col_i = lax.broadcasted_iota(jnp.int32, (chunk, chunk), 1)
    keep = (col_i >= row_i) if reverse else (row_i >= col_i)

    @pl.when(j == 0)
    def _init():
        state_ref[...] = jnp.zeros_like(state_ref)
        if not gated:
            pos = lax.broadcasted_iota(jnp.int32, (chunk, LANES), 0)
            steps = ((chunk - pos) if reverse else (pos + 1)).astype(F32)
            lane = lax.broadcasted_iota(jnp.int32, (chunk, LANES), 1)
            for g in range(groups):
                lg = jnp.zeros((chunk, LANES), F32)
                for a in range(hpg):
                    lg = jnp.where(lane // dk == a, log_decay[g * hpg + a], lg)
                logb = steps * lg
                b_end = float(chunk) * lg
                dec_ref[g, 0] = jnp.exp(logb)
                dec_ref[g, 1] = jnp.exp(-logb)
                dec_ref[g, 2] = jnp.exp(b_end - logb)
                dec_ref[g, 3] = jnp.exp(b_end)

    if gated:
        tri = jnp.where(keep, 1.0, 0.0).astype(F32)

    lane1 = lax.broadcasted_iota(jnp.int32, (1, LANES), 1)
    order = range(n_chunks - 1, -1, -1) if reverse else range(n_chunks)
    for c in order:
        r0 = c * chunk
        for g in range(groups):
            ksl = slice(g * LANES, (g + 1) * LANES)
            qg = q_ref[r0:r0 + chunk, ksl].astype(F32)
            kg = k_ref[r0:r0 + chunk, ksl].astype(F32)
            if gated:
                la = la_ref[r0:r0 + chunk, ksl]
                logb = jnp.dot(tri, la, preferred_element_type=F32, precision=HIGHEST)
                b_end = logb[0:1] if reverse else logb[chunk - 1:chunk]
                e_q, e_k = jnp.exp(logb), jnp.exp(-logb)
                e_s, e_e = jnp.exp(b_end - logb), jnp.exp(b_end)
            else:
                e_q, e_k, e_s = dec_ref[g, 0], dec_ref[g, 1], dec_ref[g, 2]
                e_e = dec_ref[g, 3][0:1]
            qd = qg * e_q
            kd_ = (kg * e_k).astype(BF16)
            ks = (kg * e_s).astype(BF16)
            for a in range(hpg):
                h = g * hpg + a
                qa = (jnp.where(lane1 // dk == a, qd, 0.0) if hpg > 1 else qd).astype(BF16)
                vh = v_ref[r0:r0 + chunk, h * dv:(h + 1) * dv]
                s = lax.dot_general(qa, kd_, (((1,), (1,)), ((), ())),
                                    preferred_element_type=F32)
                s = jnp.where(keep, s, 0.0).astype(BF16)
                st = state_ref[h]
                o = (jnp.dot(s, vh, preferred_element_type=F32)
                     + lax.dot_general(qa, st.astype(BF16), (((1,), (1,)), ((), ())),
                                       preferred_element_type=F32))
                state_ref[h] = e_e * st + lax.dot_general(
                    vh, ks, (((0,), (0,)), ((), ())), preferred_element_type=F32)
                osl = (slice(r0, r0 + chunk), slice(h * dv, (h + 1) * dv))
                if reverse:
                    o = _rms(o + oprev_ref[osl]) * nw_ref[...]
                    out_ref[osl] = (o * _silu(gate_ref[osl].astype(F32))).astype(BF16)
                else:
                    out_ref[osl] = o


def bidir_scan(proj, la_f, la_b, norm_w, *, batch, seq, ctx_len, heads, dk, dv, chunk,
               q_blk, k_blk, v_blk, g_blk, log_decay_f=None, log_decay_b=None):
    t_all = proj.shape[0]
    gated = la_f is not None
    hk, hv = heads * dk, heads * dv
    rb = SCAN_ROWS
    lat_blocks = seq // rb
    ctx_base = (batch * seq) // rb
    assert ctx_len == rb

    def rows_fwd(b, j):
        return jnp.where(j == 0, ctx_base + b, b * lat_blocks + j - 1)

    def rows_bwd(b, j):
        return jnp.where(j == 0, ctx_base + b, b * lat_blocks + lat_blocks - j)

    outs = None
    for reverse, rows_of, la, ld in ((False, rows_fwd, la_f, log_decay_f),
                                     (True, rows_bwd, la_b, log_decay_b)):
        spec = lambda w, cb: pl.BlockSpec((rb, w), lambda b, j, cb=cb: (rows_of(b, j), cb))
        in_specs = [spec(hk, q_blk), spec(hk, k_blk), spec(hv, v_blk)]
        args = [proj, proj, proj]
        if gated:
            in_specs.append(spec(hk, 0))
            args.append(la)
        if reverse:
            in_specs += [spec(hv, 0), spec(hv, g_blk), pl.BlockSpec((1, dv), lambda b, j: (0, 0))]
            args += [outs, proj, norm_w]
        scratch = [pltpu.VMEM((heads, dv, LANES), F32)]
        if not gated:
            scratch.append(pltpu.VMEM((hk // LANES, 4, chunk, LANES), F32))
        kern = functools.partial(_scan_kernel, heads=heads, dk=dk, dv=dv, chunk=chunk,
                                 gated=gated, reverse=reverse, log_decay=ld)
        outs = pl.pallas_call(
            kern,
            grid=(batch, lat_blocks + 1),
            in_specs=in_specs,
            out_specs=spec(hv, 0),
            out_shape=jax.ShapeDtypeStruct((t_all, hv), BF16 if reverse else F32),
            scratch_shapes=scratch,
            compiler_params=_cparams(("parallel", "arbitrary")),
            name="scan_bwd" if reverse else "scan_fwd",
        )(*args)
    return outs


def _shortconv_kernel(z_ref, zp_ref, zn_ref, w_ref, b_ref, u_ref, x0_ref, *, tiles_per_seq):
    i = pl.program_id(0)
    tm = z_ref.shape[0]
    z = z_ref[...].astype(F32)
    first = (i % tiles_per_seq) == 0
    last = (i % tiles_per_seq) == tiles_per_seq - 1
    halo = zp_ref.shape[0]
    prev_row = jnp.where(first, 0.0, zp_ref[halo - 1:halo, :].astype(F32))
    next_row = jnp.where(last, 0.0, zn_ref[0:1, :].astype(F32))
    row = lax.broadcasted_iota(jnp.int32, z.shape, 0)
    z_prev = jnp.where(row == 0, prev_row, pltpu.roll(z, 1, 0))
    z_next = jnp.where(row == tm - 1, next_row, pltpu.roll(z, tm - 1, 0))
    y = w_ref[0:1] * z_prev + w_ref[1:2] * z + w_ref[2:3] * z_next + b_ref[...]
    c = HYENA_CH
    x0_ref[...] = y[:, 0:c].astype(BF16)
    u_ref[...] = (y[:, c:2 * c] * y[:, 2 * c:3 * c]).astype(BF16)


def hyena_shortconv(proj, short_w, short_b, *, row0, batch, seq_len, z_blk, tm=256):
    halo = 16
    tiles_per_seq = seq_len // tm
    n_tiles = batch * tiles_per_seq
    t0, h_per_tile = row0 // tm, tm // halo
    nz = 3 * HYENA_CH
    n_halo_blocks = proj.shape[0] // halo
    out_map = lambda i: (i % tiles_per_seq, i // tiles_per_seq)
    kern = functools.partial(_shortconv_kernel, tiles_per_seq=tiles_per_seq)
    return pl.pallas_call(
        kern,
        grid=(n_tiles,),
        in_specs=[pl.BlockSpec((tm, nz), lambda i: (t0 + i, z_blk)),
                  pl.BlockSpec((halo, nz),
                               lambda i: (jnp.maximum((t0 + i) * h_per_tile - 1, 0), z_blk)),
                  pl.BlockSpec((halo, nz),
                               lambda i: (jnp.minimum((t0 + i + 1) * h_per_tile,
                                                      n_halo_blocks - 1), z_blk)),
                  pl.BlockSpec((3, nz), lambda i: (0, 0)),
                  pl.BlockSpec((1, nz), lambda i: (0, 0))],
        out_specs=[pl.BlockSpec((tm, HYENA_CH), out_map), pl.BlockSpec((tm, HYENA_CH), out_map)],
        out_shape=[jax.ShapeDtypeStruct((seq_len, batch * HYENA_CH), BF16)] * 2,
        compiler_params=_cparams(("parallel",)),
        name="hyena_shortconv",
    )(proj, proj, proj, short_w, short_b)


def _filter_kernel(feat_ref, w1_ref, b1_ref, w2_ref, b2_ref, w3_ref, b3_ref, w4_ref, fq_ref,
                   hp_ref, hm_ref, *, seq_len):
    i = pl.program_id(0)
    tl = feat_ref.shape[0]
    fq = fq_ref[...]
    dot = lambda a, b: jnp.dot(a, b, preferred_element_type=F32, precision=HIGHEST)
    h = jnp.sin(fq * (dot(feat_ref[...], w1_ref[...]) + b1_ref[...]))
    h = jnp.sin(fq * (dot(h, w2_ref[...]) + b2_ref[...]))
    h = jnp.sin(fq * (dot(h, w3_ref[...]) + b3_ref[...]))
    h = dot(h, w4_ref[...])
    c = HYENA_CH
    max_decay = math.log(HYENA_TARGET) / HYENA_FAST_DECAY
    min_decay = math.log(HYENA_TARGET) / HYENA_SLOW_DECAY
    ch = lax.broadcasted_iota(jnp.int32, (tl, c), 1).astype(F32)
    deltas = min_decay + ch * ((max_decay - min_decay) / (c - 1))
    row = lax.broadcasted_iota(jnp.int32, (tl, c), 0) + i * tl
    t = row.astype(F32) * (1.0 / (seq_len - 1))
    window = jnp.exp(-t * jnp.abs(deltas))
    h_f = h[:, 0:c] * window
    h_b = jnp.where(row == 0, 0.0, h[:, c:2 * c] * window)
    hp_ref[...] = (h_f + h_b).astype(BF16)
    hm_ref[...] = (h_f - h_b).astype(BF16)


def hyena_filter_pair(feats, w1p, b1, w2, b2, w3, b3, w4, fq):
    seq_len = feats.shape[0]
    tl = min(512, seq_len)
    full = lambda a: pl.BlockSpec(a.shape, lambda i: (0,) * a.ndim)
    kern = functools.partial(_filter_kernel, seq_len=seq_len)
    return pl.pallas_call(
        kern,
        grid=(seq_len // tl,),
        in_specs=[pl.BlockSpec((tl, feats.shape[1]), lambda i: (i, 0)),
                  full(w1p), full(b1), full(w2), full(b2), full(w3), full(b3), full(w4), full(fq)],
        out_specs=[pl.BlockSpec((tl, HYENA_CH), lambda i: (i, 0))] * 2,
        out_shape=[jax.ShapeDtypeStruct((seq_len, HYENA_CH), BF16)] * 2,
        compiler_params=_cparams(("parallel",)),
        name="hyena_filter",
    )(feats, w1p, b1, w2, b2, w3, b3, w4, fq)


def _dft_fwd_kernel(c_ref, s_ref, u_ref, a_ref, b_ref):
    u = u_ref[...]
    a_ref[...] = jnp.dot(c_ref[...], u, preferred_element_type=F32)
    b_ref[...] = jnp.dot(s_ref[...], u, preferred_element_type=F32)


def dft_forward(cs, ss, u_ext):
    n, cols = u_ext.shape
    tf, tn = min(512, n), 1024
    return pl.pallas_call(
        _dft_fwd_kernel,
        grid=(cols // tn, n // tf),
        in_specs=[pl.BlockSpec((tf, n), lambda c, f: (f, 0)),
                  pl.BlockSpec((tf, n), lambda c, f: (f, 0)),
                  pl.BlockSpec((n, tn), lambda c, f: (0, c))],
        out_specs=[pl.BlockSpec((tf, tn), lambda c, f: (f, c))] * 2,
        out_shape=[jax.ShapeDtypeStruct((n, cols), F32)] * 2,
        compiler_params=_cparams(("parallel", "parallel")),
        name="dft_forward",
    )(cs, ss, u_ext)


def _spectral_mul_kernel(au_ref, bu_ref, ap_ref, bp_ref, am_ref, bm_ref, hc_ref, hs_ref,
                         yre_ref, yim_ref, *, scale):
    hc, hs = hc_ref[...], hs_ref[...]
    k_re = hc * ap_ref[...] + hs * bp_ref[...]
    k_im = hs * am_ref[...] - hc * bm_ref[...]
    a, b = au_ref[...], bu_ref[...]
    yre_ref[...] = ((a * k_re + b * k_im) * scale).astype(BF16)
    yim_ref[...] = ((a * k_im - b * k_re) * scale).astype(BF16)


def spectral_mul(a, b, half_cos, half_sin, *, batch):
    n = a.shape[0]
    c = HYENA_CH
    tf = min(512, n)
    kern = functools.partial(_spectral_mul_kernel, scale=1.0 / n)
    ucol = lambda f, bi: (f, bi)
    return pl.pallas_call(
        kern,
        grid=(n // tf, batch),
        in_specs=[pl.BlockSpec((tf, c), ucol), pl.BlockSpec((tf, c), ucol),
                  pl.BlockSpec((tf, c), lambda f, bi: (f, batch)),
                  pl.BlockSpec((tf, c), lambda f, bi: (f, batch)),
                  pl.BlockSpec((tf, c), lambda f, bi: (f, batch + 1)),
                  pl.BlockSpec((tf, c), lambda f, bi: (f, batch + 1)),
                  pl.BlockSpec((tf, 1), lambda f, bi: (f, 0)),
                  pl.BlockSpec((tf, 1), lambda f, bi: (f, 0))],
        out_specs=[pl.BlockSpec((tf, c), ucol)] * 2,
        out_shape=[jax.ShapeDtypeStruct((n, batch * c), BF16)] * 2,
        compiler_params=_cparams(("parallel", "parallel")),
        name="hyena_spectral_mul",
    )(a, b, a, b, a, b, half_cos, half_sin)


def _dft_inv_kernel(c_ref, s_ref, yre_ref, yim_ref, u_ref, x0_ref, bias_ref, o_ref):
    y = (jnp.dot(c_ref[...], yre_ref[...], preferred_element_type=F32)
         - jnp.dot(s_ref[...], yim_ref[...], preferred_element_type=F32))
    u = u_ref[...].astype(F32)
    o_ref[...] = (x0_ref[...].astype(F32) * (y + u * bias_ref[...])).astype(BF16)


def dft_inverse(cs, ss, yre, yim, u, x0, bias, *, batch):
    n = cs.shape[0]
    c = HYENA_CH
    tt = min(512, n)
    col = lambda bi, t: (0, bi)
    tile = lambda bi, t: (t, bi)
    return pl.pallas_call(
        _dft_inv_kernel,
        grid=(batch, n // tt),
        in_specs=[pl.BlockSpec((tt, n), lambda bi, t: (t, 0)),
                  pl.BlockSpec((tt, n), lambda bi, t: (t, 0)),
                  pl.BlockSpec((n, c), col), pl.BlockSpec((n, c), col),
                  pl.BlockSpec((tt, c), tile), pl.BlockSpec((tt, c), tile),
                  pl.BlockSpec((1, c), lambda bi, t: (0, 0))],
        out_specs=pl.BlockSpec((tt, c), lambda bi, t: (bi * (n // tt) + t, 0)),
        out_shape=jax.ShapeDtypeStruct((batch * n, c), BF16),
        compiler_params=_cparams(("parallel", "parallel")),
        name="dft_inverse",
    )(cs, ss, yre, yim, u, x0, bias)


def _shifted_dft_tables(n):
    f = lax.broadcasted_iota(jnp.int32, (n, n), 0)
    s = lax.broadcasted_iota(jnp.int32, (n, n), 1)
    m = ((2 * f + 1) * (2 * s + 1)) % (8 * n)
    ang = m.astype(F32) * (2.0 * math.pi / (8 * n))
    half = (2 * jnp.arange(n, dtype=jnp.int32) + 1).astype(F32)[:, None] * (math.pi / (4 * n))
    return jnp.cos(ang).astype(BF16), jnp.sin(ang).astype(BF16), jnp.cos(half), jnp.sin(half)


def _filter_features(seq_len):
    t = jnp.linspace(0.0, 1.0, seq_len, dtype=F32)[:, None]
    bands = (HYENA_EMB - 1) // 2
    ang = 2.0 * math.pi * jnp.arange(seq_len, dtype=F32)[:, None] / seq_len
    fr = jnp.linspace(1e-4, bands - 1, bands, dtype=F32)[None, :]
    feats = jnp.concatenate([t, jnp.cos(fr * ang), -jnp.sin(fr * ang)], axis=-1)
    return jnp.pad(feats, ((0, 0), (0, LANES - HYENA_EMB)))


def hyena_long_conv(proj, hp, *, row0, batch, seq_len, z_blk):
    u, x0 = hyena_shortconv(proj, hp["short_w"], hp["short_b"], row0=row0, batch=batch,
                            seq_len=seq_len, z_blk=z_blk)
    f_p, f_m = hyena_filter_pair(_filter_features(seq_len), hp["w1"], hp["b1"], hp["w2"], hp["b2"],
                                 hp["w3"], hp["b3"], hp["w4"], hp["freq"])
    cs, ss, half_cos, half_sin = _shifted_dft_tables(seq_len)
    a, b = dft_forward(cs, ss, jnp.concatenate([u, f_p, f_m], axis=1))
    yre, yim = spectral_mul(a, b, half_cos, half_sin, batch=batch)
    return dft_inverse(cs, ss, yre, yim, u, x0, hp["bias"], batch=batch)


def _outproj_kernel(ma_ref, mb_ref, x_ref, mod_ref, nw_ref, w_ref, rw_ref,
                    xo_ref, h_ref, lt_ref):
    half = ma_ref.shape[1]
    y = (jnp.dot(ma_ref[...], w_ref[0:half], preferred_element_type=F32)
         + jnp.dot(mb_ref[...], w_ref[half:2 * half], preferred_element_type=F32))
    mod = mod_ref[0]
    x = x_ref[...] + mod[2:3] * y
    xo_ref[...] = x
    h = _modulated_norm(x, mod, nw_ref[...], 3, 4)
    h_ref[...] = h.astype(BF16)
    lt_ref[...] = lax.dot_general(rw_ref[...], h, (((1,), (1,)), ((), ())),
                                  preferred_element_type=F32, precision=HIGHEST)


def outproj(mix_a, a_blk, mix_b, b_blk, x_all, mods, nw, w_out, router_wt, *,
            n_rows, t_lat, seq, tm=512):
    d = x_all.shape[1]
    half = d // 2
    return pl.pallas_call(
        _outproj_kernel,
        grid=(n_rows // tm,),
        in_specs=[pl.BlockSpec((tm, half), lambda i: (i, a_blk)),
                  pl.BlockSpec((tm, half), lambda i: (i, b_blk)),
                  pl.BlockSpec((tm, d), lambda i: (i, 0)),
                  pl.BlockSpec((1, 6, d), lambda i: (_seg_of_tile(i, tm, t_lat, seq), 0, 0)),
                  pl.BlockSpec((1, d), lambda i: (0, 0)),
                  pl.BlockSpec((d, d), lambda i: (0, 0)),
                  pl.BlockSpec((N_EXPERTS, d), lambda i: (0, 0))],
        out_specs=[pl.BlockSpec((tm, d), lambda i: (i, 0)),
                   pl.BlockSpec((tm, d), lambda i: (i, 0)),
                   pl.BlockSpec((N_EXPERTS, tm), lambda i: (0, i))],
        out_shape=[jax.ShapeDtypeStruct((n_rows, d), F32),
                   jax.ShapeDtypeStruct((n_rows, d), BF16),
                   jax.ShapeDtypeStruct((N_EXPERTS, n_rows), F32)],
        compiler_params=_cparams(("parallel",)),
        name="outproj",
    )(mix_a, mix_b, x_all, mods, nw, w_out, router_wt)


def _first_max(x, idx, sentinel):
    m = jnp.max(x, axis=0, keepdims=True)
    first = jnp.min(jnp.where(x == m, idx, sentinel), axis=0, keepdims=True)
    return m, idx == first


def _router_kernel(lt_ref, bias_ref, eye_ref, before_ref, below_ref, lists_ref, tok_ref, pc_ref):
    tn = lt_ref.shape[1]
    scores = jax.nn.sigmoid(lt_ref[...])
    sel = scores + bias_ref[...]
    neg = -jnp.inf
    in_grp = lax.broadcasted_iota(jnp.int32, (GROUP_SIZE, tn), 0)
    gscore = []
    for g in range(N_GROUPS):
        x = sel[g * GROUP_SIZE:(g + 1) * GROUP_SIZE]
        m1, hit = _first_max(x, in_grp, GROUP_SIZE)
        gscore.append(m1 + jnp.max(jnp.where(hit, neg, x), axis=0, keepdims=True))
    rows = []
    for g in range(N_GROUPS):
        beaten = jnp.zeros((1, tn), jnp.int32)
        for o in range(N_GROUPS):
            if o != g:
                wins = (gscore[o] >= gscore[g]) if o < g else (gscore[o] > gscore[g])
                beaten = beaten + wins.astype(jnp.int32)
        keep = jnp.broadcast_to(beaten < TOPK_GROUPS, (GROUP_SIZE, tn))
        rows.append(jnp.where(keep, sel[g * GROUP_SIZE:(g + 1) * GROUP_SIZE], neg))
    cand = jnp.concatenate(rows, axis=0)
    eidx = lax.broadcasted_iota(jnp.int32, cand.shape, 0)
    chosen = jnp.zeros(cand.shape, jnp.bool_)
    hits = []
    for _ in range(TOP_K):
        _, hit = _first_max(cand, eidx, N_EXPERTS)
        hits.append(hit)
        chosen = jnp.logical_or(chosen, hit)
        cand = jnp.where(hit, neg, cand)
    w = jnp.where(chosen, scores, 0.0)
    gates_t = w / jnp.sum(w, axis=0, keepdims=True) * ROUTED_SCALE
    chosen_f = jnp.where(chosen, 1.0, 0.0)
    rank = jnp.dot(chosen_f.astype(BF16), before_ref[...], preferred_element_type=F32)
    count = jnp.sum(chosen_f, axis=1, keepdims=True)
    pc = jnp.floor((count + (ROW_UNIT - 1)) * (1.0 / ROW_UNIT)) * ROW_UNIT
    pc_lanes = jnp.broadcast_to(pc, (N_EXPERTS, LANES))
    start = jnp.dot(below_ref[...], pc_lanes, preferred_element_type=F32,
                    precision=HIGHEST)[:, 0:1]
    slot = start + rank
    pick = lambda hit, v: jnp.sum(jnp.where(hit, v, 0.0), axis=0, keepdims=True)
    lists = jnp.concatenate([pick(h, slot) for h in hits] + [pick(h, gates_t) for h in hits], axis=0)
    lists_ref[...] = lists
    pc_ref[0] = pc_lanes
    padded = jnp.concatenate([lists, jnp.zeros((LANES - 2 * TOP_K, tn), F32)], axis=0)
    tok_ref[...] = lax.dot_general(eye_ref[...], padded, (((1,), (1,)), ((), ())),
                                   preferred_element_type=F32, precision=HIGHEST)


def router(logits_t, router_bias):
    e, t = logits_t.shape
    tn = MOE_TILE
    n_tiles = t // tn
    eye = jnp.eye(tn, dtype=F32)
    tok_i = jnp.arange(tn, dtype=jnp.int32)
    before = (tok_i[:, None] < tok_i[None, :]).astype(BF16)
    exp_i = jnp.arange(e, dtype=jnp.int32)
    below = (exp_i[None, :] < exp_i[:, None]).astype(F32)
    const = lambda shape: pl.BlockSpec(shape, lambda i: (0,) * len(shape))
    return pl.pallas_call(
        _router_kernel,
        grid=(n_tiles,),
        in_specs=[pl.BlockSpec((e, tn), lambda i: (0, i)),
                  const((e, 1)), const((tn, tn)), const((tn, tn)), const((e, e))],
        out_specs=[pl.BlockSpec((2 * TOP_K, tn), lambda i: (0, i)),
                   pl.BlockSpec((tn, LANES), lambda i: (i, 0)),
                   pl.BlockSpec((1, e, LANES), lambda i: (i, 0, 0))],
        out_shape=[jax.ShapeDtypeStruct((2 * TOP_K, t), F32),
                   jax.ShapeDtypeStruct((t, LANES), F32),
                   jax.ShapeDtypeStruct((n_tiles, e, LANES), F32)],
        compiler_params=_cparams(("parallel",)),
        name="router",
    )(logits_t, router_bias.reshape(e, 1), eye, before, below)


def moe_plan(pc, n_tiles):
    e = pc.shape[1]
    assert n_tiles >= e
    run_end = jnp.cumsum(pc, axis=1)
    run_start = run_end - pc
    total = jnp.sum(pc, axis=0)
    total_al = ((total + EXPERT_ROWS - 1) // EXPERT_ROWS) * EXPERT_ROWS
    range_end = jnp.cumsum(total_al)
    range_start = range_end - total_al
    run_row = range_start[None, :] + jnp.cumsum(pc, axis=0) - pc
    unit_row = jnp.arange(N_UNITS, dtype=jnp.int32) * ROW_UNIT
    owner = jnp.sum((run_end[:, None, :] <= unit_row[None, :, None]).astype(jnp.int32), axis=2)
    owner_c = jnp.minimum(owner, e - 1)
    dst = (jnp.take_along_axis(run_row, owner_c, axis=1)
           + unit_row[None, :] - jnp.take_along_axis(run_start, owner_c, axis=1))
    unit_dst = jnp.where(owner < e, dst, -1).astype(jnp.int32)
    pad_row = (range_start + total)[:, None] + jnp.arange(PAD_UNITS, dtype=jnp.int32)[None, :] * ROW_UNIT
    pad_dst = jnp.where(pad_row < range_end[:, None], pad_row, -1)
    pad_dst = jnp.concatenate([pad_dst, jnp.full((n_tiles - e, PAD_UNITS), -1)], axis=0).astype(jnp.int32)
    max_row_tiles = _max_row_tiles(n_tiles)
    tile_end = range_end // EXPERT_ROWS
    tile_expert = jnp.sum((tile_end[None, :] <= jnp.arange(max_row_tiles, dtype=jnp.int32)[:, None])
                          .astype(jnp.int32), axis=1)
    tile_expert = jnp.minimum(tile_expert, e - 1).astype(jnp.int32)
    return unit_dst, pad_dst, tile_expert, tile_end[-1:].astype(jnp.int32)


def _max_row_tiles(n_tiles):
    worst_rows = n_tiles * (MOE_TILE * TOP_K + N_EXPERTS * (ROW_UNIT - 1)) + N_EXPERTS * (EXPERT_ROWS - 1)
    return -(-worst_rows // EXPERT_ROWS)


def _unit_copies(table_ref, i, n, copy_of):
    def each(action):
        def body(u, carry):
            row = table_ref[i, u]

            @pl.when(row >= 0)
            def _():
                action(copy_of(u, pl.multiple_of(row, ROW_UNIT)))
            return carry
        lax.fori_loop(0, n, body, 0)
    return (lambda: each(lambda cp: cp.start())), (lambda: each(lambda cp: cp.wait()))


def _dispatch_kernel(unit_ref, pad_ref, h_ref, lists_ref, xs_ref, loc_ref, sem):
    i = pl.program_id(0)
    tn = h_ref.shape[0]
    h = h_ref[...]
    for c0 in range(0, SLOTS, SLOT_CHUNK):
        srow = (lax.broadcasted_iota(jnp.int32, (SLOT_CHUNK, tn), 0) + c0).astype(F32)
        sel = srow == lists_ref[0:1, :]
        for k in range(1, TOP_K):
            sel = jnp.logical_or(sel, srow == lists_ref[k:k + 1, :])
        loc_ref[c0:c0 + SLOT_CHUNK, :] = jnp.dot(jnp.where(sel, 1.0, 0.0).astype(BF16), h,
                                                 preferred_element_type=F32).astype(BF16)
    run_copy = lambda u, row: pltpu.make_async_copy(
        loc_ref.at[pl.ds(u * ROW_UNIT, ROW_UNIT)], xs_ref.at[pl.ds(row, ROW_UNIT)], sem)
    zero_copy = lambda u, row: pltpu.make_async_copy(
        loc_ref.at[pl.ds(SLOTS - ROW_UNIT, ROW_UNIT)], xs_ref.at[pl.ds(row, ROW_UNIT)], sem)
    start_runs, wait_runs = _unit_copies(unit_ref, i, N_UNITS, run_copy)
    start_pads, wait_pads = _unit_copies(pad_ref, i, PAD_UNITS, zero_copy)
    start_runs()
    start_pads()
    wait_runs()
    wait_pads()


def moe_dispatch(h, lists, unit_dst, pad_dst, *, n_tiles):
    d = h.shape[1]
    cap_rows = _max_row_tiles(n_tiles) * EXPERT_ROWS
    return pl.pallas_call(
        _dispatch_kernel,
        grid_spec=pltpu.PrefetchScalarGridSpec(
            num_scalar_prefetch=2,
            grid=(n_tiles,),
            in_specs=[pl.BlockSpec((MOE_TILE, d), lambda i, *_: (i, 0)),
                      pl.BlockSpec((2 * TOP_K, MOE_TILE), lambda i, *_: (0, i))],
            out_specs=pl.BlockSpec(memory_space=pl.ANY),
            scratch_shapes=[pltpu.VMEM((SLOTS, d), BF16), pltpu.SemaphoreType.DMA(())]),
        out_shape=jax.ShapeDtypeStruct((cap_rows, d), BF16),
        compiler_params=_cparams(("arbitrary",)),
        name="moe_dispatch",
    )(unit_dst, pad_dst, h, lists)


def _expert_kernel(te_ref, nt_ref, x_ref, wg_ref, wu_ref, wd_ref, y_ref, wg_bf, wu_bf, wd_bf):
    t = pl.program_id(0)
    live = t < nt_ref[0]
    tc = jnp.minimum(t, nt_ref[0] - 1)
    new_expert = jnp.logical_or(t == 0, te_ref[tc] != te_ref[jnp.maximum(tc - 1, 0)])

    @pl.when(jnp.logical_and(live, new_expert))
    def _():
        wg_bf[...] = wg_ref[0, 0].astype(BF16)
        wu_bf[...] = wu_ref[0, 0].astype(BF16)
        wd_bf[...] = wd_ref[0, 0].astype(BF16)

    @pl.when(live)
    def _():
        x = x_ref[...]
        act = (_silu(jnp.dot(x, wg_bf[...], preferred_element_type=F32))
               * jnp.dot(x, wu_bf[...], preferred_element_type=F32))
        y_ref[...] = jnp.dot(act.astype(BF16), wd_bf[...], preferred_element_type=F32).astype(BF16)


def moe_experts(xs, w_gate, w_up, w_down, layer, tile_expert, n_row_tiles):
    cap_rows, d = xs.shape
    ff = w_gate.shape[3]
    row_map = lambda t, te, nt: (jnp.minimum(t, nt[0] - 1), 0)
    w_map = lambda t, te, nt: (layer, te[jnp.minimum(t, nt[0] - 1)], 0, 0)
    return pl.pallas_call(
        _expert_kernel,
        grid_spec=pltpu.PrefetchScalarGridSpec(
            num_scalar_prefetch=2,
            grid=(cap_rows // EXPERT_ROWS,),
            in_specs=[pl.BlockSpec((EXPERT_ROWS, d), row_map),
                      pl.BlockSpec((1, 1, d, ff), w_map), pl.BlockSpec((1, 1, d, ff), w_map),
                      pl.BlockSpec((1, 1, ff, d), w_map)],
            out_specs=pl.BlockSpec((EXPERT_ROWS, d), row_map),
            scratch_shapes=[pltpu.VMEM((d, ff), BF16), pltpu.VMEM((d, ff), BF16),
                            pltpu.VMEM((ff, d), BF16)]),
        out_shape=jax.ShapeDtypeStruct((cap_rows, d), BF16),
        compiler_params=_cparams(("arbitrary",)),
        name="moe_experts",
    )(tile_expert, n_row_tiles, xs, w_gate, w_up, w_down)


def _combine_kernel(unit_ref, ys_ref, tok_ref, h_ref, x_ref, mod_ref, fnw_ref, sg_ref, su_ref,
                    sd_ref, o_ref, loc_ref, sem, *, final_norm):
    i = pl.program_id(0)
    tn = h_ref.shape[0]

    @pl.when(i == 0)
    def _():
        loc_ref[...] = jnp.zeros_like(loc_ref)

    run_copy = lambda u, row: pltpu.make_async_copy(
        ys_ref.at[pl.ds(row, ROW_UNIT)], loc_ref.at[pl.ds(u * ROW_UNIT, ROW_UNIT)], sem)
    start_runs, wait_runs = _unit_copies(unit_ref, i, N_UNITS, run_copy)
    start_runs()
    h = h_ref[...]
    act = (_silu(jnp.dot(h, sg_ref[...], preferred_element_type=F32))
           * jnp.dot(h, su_ref[...], preferred_element_type=F32))
    acc = jnp.dot(act.astype(BF16), sd_ref[...], preferred_element_type=F32)
    wait_runs()
    tok = tok_ref[...]
    for c0 in range(0, SLOTS, SLOT_CHUNK):
        scol = (lax.broadcasted_iota(jnp.int32, (tn, SLOT_CHUNK), 1) + c0).astype(F32)
        q = jnp.zeros((tn, SLOT_CHUNK), F32)
        for k in range(TOP_K):
            q = q + jnp.where(scol == tok[:, k:k + 1], tok[:, TOP_K + k:TOP_K + k + 1], 0.0)
        acc = acc + jnp.dot(q.astype(BF16), loc_ref[c0:c0 + SLOT_CHUNK, :],
                            preferred_element_type=F32)
    x = x_ref[...] + mod_ref[0][5:6] * acc
    if final_norm:
        x = _rms(x) * fnw_ref[...]
    o_ref[...] = x


def moe_combine(ys, tok, h, x_all, mods, fnw, sh_gate, sh_up, sh_down, unit_dst, *,
                n_tiles, t_lat, seq, final_norm):
    d = h.shape[1]
    tn = MOE_TILE
    const = lambda a: pl.BlockSpec(a.shape, lambda i, *_: (0,) * a.ndim)
    kern = functools.partial(_combine_kernel, final_norm=final_norm)
    return pl.pallas_call(
        kern,
        grid_spec=pltpu.PrefetchScalarGridSpec(
            num_scalar_prefetch=1,
            grid=(n_tiles,),
            in_specs=[pl.BlockSpec(memory_space=pl.ANY),
                      pl.BlockSpec((tn, LANES), lambda i, *_: (i, 0)),
                      pl.BlockSpec((tn, d), lambda i, *_: (i, 0)),
                      pl.BlockSpec((tn, d), lambda i, *_: (i, 0)),
                      pl.BlockSpec((1, 6, d),
                                   lambda i, *_: (_seg_of_tile(i, tn, t_lat, seq), 0, 0)),
                      const(fnw), const(sh_gate), const(sh_up), const(sh_down)],
            out_specs=pl.BlockSpec((tn, d), lambda i, *_: (i, 0)),
            scratch_shapes=[pltpu.VMEM((SLOTS, d), BF16), pltpu.SemaphoreType.DMA(())]),
        out_shape=jax.ShapeDtypeStruct((n_tiles * tn, d), F32),
        compiler_params=_cparams(("arbitrary",)),
        name="moe_combine",
    )(unit_dst, ys, tok, h, x_all, mods, fnw, sh_gate, sh_up, sh_down)


def moe_block(h2, logits_t, router_bias, x_mid, mods, fnw, w_gate, w_up, w_down,
              sh_gate, sh_up, sh_down, *, layer, t_lat, seq, final_norm):
    n_tiles = h2.shape[0] // MOE_TILE
    lists, tok, pc = router(logits_t, router_bias)
    unit_dst, pad_dst, tile_expert, n_row_tiles = moe_plan(pc[:, :, 0].astype(jnp.int32), n_tiles)
    xs = moe_dispatch(h2, lists, unit_dst, pad_dst, n_tiles=n_tiles)
    ys = moe_experts(xs, w_gate, w_up, w_down, layer, tile_expert, n_row_tiles)
    return moe_combine(ys, tok, h2, x_mid, mods, fnw, sh_gate.astype(BF16), sh_up.astype(BF16),
                       sh_down.astype(BF16), unit_dst, n_tiles=n_tiles, t_lat=t_lat, seq=seq,
                       final_norm=final_norm)


def _rope_tables(seq, tm):
    half = RET_DK // 2
    pos = jnp.arange(seq, dtype=jnp.int32)
    row = (pos // GRID_W).astype(F32)
    col = (pos % GRID_W).astype(F32)
    inv = ROPE_BASE ** (-jnp.arange(0, half, 2, dtype=F32) / half)
    a_row, a_col = row[:, None] * inv[None, :], col[:, None] * inv[None, :]
    cos_h = jnp.concatenate([jnp.cos(a_row)] * 2 + [jnp.cos(a_col)] * 2, axis=-1)
    sin_h = jnp.concatenate([-jnp.sin(a_row), jnp.sin(a_row), -jnp.sin(a_col), jnp.sin(a_col)], -1)
    reps = 2 * RET_HEADS
    cos_t = jnp.concatenate([jnp.tile(cos_h, (1, reps)), jnp.ones((tm, reps * RET_DK), F32)], 0)
    sin_t = jnp.concatenate([jnp.tile(sin_h, (1, reps)), jnp.zeros((tm, reps * RET_DK), F32)], 0)
    return cos_t, sin_t


def _rope_partner_columns():
    quarter = RET_DK // 4
    idx = jnp.arange(2 * RET_HEADS * RET_DK, dtype=jnp.int32)
    within = idx % (2 * quarter)
    return jnp.where(within < quarter, idx + quarter, idx - quarter)


def kernel(x, c, ctx, c_ctx, ada_w, ada_b, norm1_w, norm2_w, ev_w_in, ev_short_w, ev_short_b, ev_filt_w1, ev_filt_b1, ev_filt_w2, ev_filt_b2, ev_filt_w3, ev_filt_b3, ev_filt_w4, ev_filt_freq, ev_hyena_bias, ev_w_out, od_w_in, od_gate_w1_f, od_gate_w2_f, od_gate_b_f, od_gate_w1_b, od_gate_w2_b, od_gate_b_b, od_norm_w, od_w_out, router_w, router_bias, exp_w_gate, exp_w_up, exp_w_down, sh_w_gate, sh_w_up, sh_w_down, final_norm_w):
    batch, seq, d = x.shape
    ctx_len = ctx.shape[1]
    depth = ada_w.shape[0]
    t_lat, t_ctx = batch * seq, batch * ctx_len
    t_all = t_lat + t_ctx
    tm = 512

    x_all = jnp.concatenate([x.reshape(t_lat, d), ctx.reshape(t_ctx, d)], axis=0)
    cond8 = jnp.concatenate([c_ctx[None, :], c, jnp.zeros((8 - 1 - batch, d), F32)], axis=0)
    mods_all = adaln_rows(cond8, ada_w, ada_b).reshape(depth, 8, 6, d)

    for i in range(depth):
        last = i == depth - 1
        j = i // 2
        mods = mods_all[i]
        nw1, nw2 = norm1_w[i][None, :], norm2_w[i][None, :]
        if i % 2 == 0:
            qk = 2 * RET_HEADS * RET_DK
            k_scale = jnp.concatenate([jnp.ones((qk // 2,), F32),
                                       jnp.full((qk // 2,), RET_DK ** -0.5, F32)])
            w_in = ev_w_in[j]
            w_qk = w_in[:, :qk] * k_scale
            w_ext = jnp.concatenate([w_qk, w_in[:, qk:], w_qk[:, _rope_partner_columns()]],
                                    axis=1).astype(BF16)
            cos_t, sin_t = _rope_tables(seq, tm)
            proj = inproj_even(x_all, mods, nw1, w_ext, cos_t, sin_t, t_lat=t_lat, seq=seq, tm=tm)
            log_g = [math.log1p(-2.0 ** (-5.0 - h)) for h in range(RET_HEADS)]
            mix_a = bidir_scan(proj, None, None, jnp.ones((1, RET_DV), F32),
                               batch=batch, seq=seq, ctx_len=ctx_len, heads=RET_HEADS, dk=RET_DK,
                               dv=RET_DV, chunk=RET_CHUNK, q_blk=0, k_blk=1, v_blk=1, g_blk=2,
                               log_decay_f=log_g, log_decay_b=log_g[::-1])
            hp = dict(short_w=ev_short_w[j], short_b=ev_short_b[j][None, :],
                      w1=jnp.pad(ev_filt_w1[j], ((0, LANES - HYENA_EMB), (0, 0))),
                      b1=ev_filt_b1[j][None, :], w2=ev_filt_w2[j], b2=ev_filt_b2[j][None, :],
                      w3=ev_filt_w3[j], b3=ev_filt_b3[j][None, :], w4=ev_filt_w4[j],
                      freq=ev_filt_freq[j][None, :], bias=ev_hyena_bias[j][None, :])
            hy_lat = hyena_long_conv(proj, hp, row0=0, batch=batch, seq_len=seq, z_blk=1)
            hy_ctx = hyena_long_conv(proj, hp, row0=t_lat, batch=batch, seq_len=ctx_len, z_blk=1)
            mix_b = jnp.concatenate([hy_lat, hy_ctx], axis=0)
            a_blk, b_blk = 0, 0
            w_out = ev_w_out[j].astype(BF16)
        else:
            kd = GLA_HEADS * GLA_DK
            pad_cols = LANES - 2 * GLA_RANK
            w_ext = jnp.concatenate([od_w_in[j], od_gate_w1_f[j], od_gate_w1_b[j],
                                     jnp.zeros((d, pad_cols), F32)], axis=1).astype(BF16)
            w2f = jnp.pad(od_gate_w2_f[j], ((0, LANES - GLA_RANK), (0, 0)))
            w2b = jnp.pad(od_gate_w2_b[j], ((GLA_RANK, LANES - 2 * GLA_RANK), (0, 0)))
            proj, la_f, la_b = inproj_odd(x_all, mods, nw1, w_ext, w2f, w2b,
                                          od_gate_b_f[j][None, :], od_gate_b_b[j][None, :],
                                          t_lat=t_lat, seq=seq, tm=tm)
            mix_a = bidir_scan(proj, la_f, la_b, od_norm_w[j][None, :],
                               batch=batch, seq=seq, ctx_len=ctx_len, heads=GLA_HEADS, dk=GLA_DK,
                               dv=GLA_DV, chunk=GLA_CHUNK, q_blk=0, k_blk=1, v_blk=1, g_blk=2)
            mix_b = mix_a
            a_blk, b_blk = 0, 1
            w_out = od_w_out[j].astype(BF16)

        n_rows = t_lat if last else t_all
        x_mid, h2, logits_t = outproj(mix_a, a_blk, mix_b, b_blk, x_all, mods, nw2, w_out,
                                      router_w[i].T, n_rows=n_rows, t_lat=t_lat, seq=seq, tm=tm)
        x_all = moe_block(h2, logits_t, router_bias[i], x_mid, mods, final_norm_w[None, :],
                          exp_w_gate, exp_w_up, exp_w_down,
                          sh_w_gate[i], sh_w_up[i], sh_w_down[i],
                          layer=i, t_lat=t_lat, seq=seq, final_norm=last)
    return x_all[:t_lat].reshape(batch, seq, d)
```

```python
import functools
import math

import jax
import jax.numpy as jnp
from jax import lax
from jax.experimental import pallas as pl
from jax.experimental.pallas import tpu as pltpu

F32 = jnp.float32
BF16 = jnp.bfloat16
HIGHEST = lax.Precision.HIGHEST

D_MODEL = 1024
GRID_W = 64
NORM_EPS = 1e-6
RET_HEADS, RET_DK, RET_DV, RET_CHUNK = 4, 64, 128, 128
ROPE_BASE = 10000.0
HYENA_CH, HYENA_EMB = 512, 33
HYENA_FAST_DECAY, HYENA_SLOW_DECAY, HYENA_TARGET = 0.3, 1.5, 1e-2
GLA_HEADS, GLA_DK, GLA_DV, GLA_RANK, GLA_TAU, GLA_CHUNK = 4, 128, 256, 16, 16.0, 64
N_EXPERTS, TOP_K, N_GROUPS, TOPK_GROUPS = 64, 8, 8, 4
GROUP_SIZE = N_EXPERTS // N_GROUPS
EXPERT_FF = 256
ROUTED_SCALE = 2.5

LANES = 128
SCAN_ROWS = 256
MOE_TILE = 256
ROW_UNIT = 16
EXPERT_ROWS = 512
SLOT_CHUNK = 512
SLOTS = -(-(MOE_TILE * TOP_K + N_EXPERTS * (ROW_UNIT - 1) + ROW_UNIT) // SLOT_CHUNK) * SLOT_CHUNK
N_UNITS = SLOTS // ROW_UNIT
PAD_UNITS = EXPERT_ROWS // ROW_UNIT
VMEM_LIMIT = 56 * 1024 * 1024


def _cparams(sem):
    return pltpu.CompilerParams(dimension_semantics=sem, vmem_limit_bytes=VMEM_LIMIT)


def _rms(x):
    return x * lax.rsqrt(jnp.mean(x * x, axis=-1, keepdims=True) + NORM_EPS)


def _silu(x):
    return x * jax.nn.sigmoid(x)


def _adaln_kernel(c_ref, w_ref, b_ref, o_ref):
    c = c_ref[...]
    o_ref[0] = jnp.dot(_silu(c), w_ref[0], preferred_element_type=F32,
                       precision=HIGHEST) + b_ref[0]


def adaln_rows(cond8, ada_w, ada_b):
    depth, d, n = ada_w.shape
    tn = 512
    return pl.pallas_call(
        _adaln_kernel,
        grid=(depth, n // tn),
        in_specs=[pl.BlockSpec((8, d), lambda l, j: (0, 0)),
                  pl.BlockSpec((1, d, tn), lambda l, j: (l, 0, j)),
                  pl.BlockSpec((1, 1, tn), lambda l, j: (l, 0, j))],
        out_specs=pl.BlockSpec((1, 8, tn), lambda l, j: (l, 0, j)),
        out_shape=jax.ShapeDtypeStruct((depth, 8, n), F32),
        compiler_params=_cparams(("parallel", "parallel")),
        name="adaln_rows",
    )(cond8, ada_w, ada_b.reshape(depth, 1, n))


def _modulated_norm(x, mod, nw, shift_row, scale_row):
    return _rms(x) * nw * (1.0 + mod[scale_row:scale_row + 1]) + mod[shift_row:shift_row + 1]


def _inproj_even_kernel(x_ref, mod_ref, nw_ref, w_ref, cos_ref, sin_ref, o_ref):
    h = _modulated_norm(x_ref[...], mod_ref[0], nw_ref[...], 0, 1).astype(BF16)
    n_main = o_ref.shape[1]
    qk = jnp.dot(h, w_ref[:, 0:512], preferred_element_type=F32)
    qk_sw = jnp.dot(h, w_ref[:, n_main:n_main + 512], preferred_element_type=F32)
    o_ref[:, 0:512] = (qk * cos_ref[...] + qk_sw * sin_ref[...]).astype(BF16)
    for c0 in range(512, n_main, 512):
        o_ref[:, c0:c0 + 512] = jnp.dot(h, w_ref[:, c0:c0 + 512],
                                        preferred_element_type=F32).astype(BF16)


def _inproj_odd_kernel(x_ref, mod_ref, nw_ref, w_ref, w2f_ref, w2b_ref, bf_ref, bb_ref,
                       o_ref, laf_ref, lab_ref):
    h = _modulated_norm(x_ref[...], mod_ref[0], nw_ref[...], 0, 1).astype(BF16)
    n_main = o_ref.shape[1]
    kd = GLA_HEADS * GLA_DK
    q = jnp.dot(h, w_ref[:, 0:kd], preferred_element_type=F32)
    o_ref[:, 0:kd] = (q * (GLA_DK ** -0.5)).astype(BF16)
    for c0 in range(kd, n_main, 512):
        o_ref[:, c0:c0 + 512] = jnp.dot(h, w_ref[:, c0:c0 + 512],
                                        preferred_element_type=F32).astype(BF16)
    low = jnp.dot(h, w_ref[:, n_main:n_main + LANES], preferred_element_type=F32)

    def log_gate(w2_ref, b_ref):
        z = jnp.dot(low, w2_ref[...], preferred_element_type=F32, precision=HIGHEST) + b_ref[...]
        return (jnp.minimum(z, 0.0) - jnp.log(1.0 + jnp.exp(-jnp.abs(z)))) * (1.0 / GLA_TAU)

    laf_ref[...] = log_gate(w2f_ref, bf_ref)
    lab_ref[...] = log_gate(w2b_ref, bb_ref)


def _seg_of_tile(i, tm, t_lat, seq):
    return jnp.where(i < t_lat // tm, 1 + (i * tm) // seq, 0)


def inproj_even(x_all, mods, nw, w_ext, cos_t, sin_t, *, t_lat, seq, tm=512):
    t_all, d = x_all.shape
    n_ext = w_ext.shape[1]
    n_main = n_ext - 512
    n_lat_tiles, pos_tiles = t_lat // tm, seq // tm

    def pos_map(i):
        return (jnp.where(i < n_lat_tiles, i % pos_tiles, pos_tiles), 0)

    return pl.pallas_call(
        _inproj_even_kernel,
        grid=(t_all // tm,),
        in_specs=[pl.BlockSpec((tm, d), lambda i: (i, 0)),
                  pl.BlockSpec((1, 6, d), lambda i: (_seg_of_tile(i, tm, t_lat, seq), 0, 0)),
                  pl.BlockSpec((1, d), lambda i: (0, 0)),
                  pl.BlockSpec((d, n_ext), lambda i: (0, 0)),
                  pl.BlockSpec((tm, 512), pos_map),
                  pl.BlockSpec((tm, 512), pos_map)],
        out_specs=pl.BlockSpec((tm, n_main), lambda i: (i, 0)),
        out_shape=jax.ShapeDtypeStruct((t_all, n_main), BF16),
        compiler_params=_cparams(("parallel",)),
        name="inproj_even",
    )(x_all, mods, nw, w_ext, cos_t, sin_t)


def inproj_odd(x_all, mods, nw, w_ext, w2f, w2b, bf, bb, *, t_lat, seq, tm=512):
    t_all, d = x_all.shape
    n_main = w_ext.shape[1] - LANES
    kd = GLA_HEADS * GLA_DK
    full = lambda shape: pl.BlockSpec(shape, lambda i: (0,) * len(shape))
    return pl.pallas_call(
        _inproj_odd_kernel,
        grid=(t_all // tm,),
        in_specs=[pl.BlockSpec((tm, d), lambda i: (i, 0)),
                  pl.BlockSpec((1, 6, d), lambda i: (_seg_of_tile(i, tm, t_lat, seq), 0, 0)),
                  full((1, d)), full(w_ext.shape), full(w2f.shape), full(w2b.shape),
                  full((1, kd)), full((1, kd))],
        out_specs=[pl.BlockSpec((tm, n_main), lambda i: (i, 0)),
                   pl.BlockSpec((tm, kd), lambda i: (i, 0)),
                   pl.BlockSpec((tm, kd), lambda i: (i, 0))],
        out_shape=[jax.ShapeDtypeStruct((t_all, n_main), BF16),
                   jax.ShapeDtypeStruct((t_all, kd), F32),
                   jax.ShapeDtypeStruct((t_all, kd), F32)],
        compiler_params=_cparams(("parallel",)),
        name="inproj_odd",
    )(x_all, mods, nw, w_ext, w2f, w2b, bf, bb)


def _scan_kernel(*refs, heads, dk, dv, chunk, gated, reverse, log_decay):
    it = iter(refs)
    q_ref, k_ref, v_ref = next(it), next(it), next(it)
    la_ref = next(it) if gated else None
    if reverse:
        oprev_ref, gate_ref, nw_ref = next(it), next(it), next(it)
    out_ref = next(it)
    state_ref = next(it)
    dec_ref = None if gated else next(it)

    hpg = LANES // dk
    groups = heads // hpg
    rows = q_ref.shape[0]
    n_chunks = rows // chunk
    j = pl.program_id(1)

    row_i = lax.broadcasted_iota(jnp.int32, (chunk, chunk), 0)
    col_i = lax.broadcasted_iota(jnp.int32, (chunk, chunk), 1)
    keep = (col_i >= row_i) if reverse else (row_i >= col_i)

    @pl.when(j == 0)
    def _init():
        state_ref[...] = jnp.zeros_like(state_ref)
        if not gated:
            pos = lax.broadcasted_iota(jnp.int32, (chunk, LANES), 0)
            steps = ((chunk - pos) if reverse else (pos + 1)).astype(F32)
            lane = lax.broadcasted_iota(jnp.int32, (chunk, LANES), 1)
            for g in range(groups):
                lg = jnp.zeros((chunk, LANES), F32)
                for a in range(hpg):
                    lg = jnp.where(lane // dk == a, log_decay[g * hpg + a], lg)
                logb = steps * lg
                b_end = float(chunk) * lg
                dec_ref[g, 0] = jnp.exp(logb)
                dec_ref[g, 1] = jnp.exp(-logb)
                dec_ref[g, 2] = jnp.exp(b_end - logb)
                dec_ref[g, 3] = jnp.exp(b_end)

    if gated:
        tri = jnp.where(keep, 1.0, 0.0).astype(F32)

    lane1 = lax.broadcasted_iota(jnp.int32, (1, LANES), 1)
    order = range(n_chunks - 1, -1, -1) if reverse else range(n_chunks)
    for c in order:
        r0 = c * chunk
        for g in range(groups):
            ksl = slice(g * LANES, (g + 1) * LANES)
            qg = q_ref[r0:r0 + chunk, ksl].astype(F32)
            kg = k_ref[r0:r0 + chunk, ksl].astype(F32)
            if gated:
                la = la_ref[r0:r0 + chunk, ksl]
                logb = jnp.dot(tri, la, preferred_element_type=F32, precision=HIGHEST)
                b_end = logb[0:1] if reverse else logb[chunk - 1:chunk]
                e_q, e_k = jnp.exp(logb), jnp.exp(-logb)
                e_s, e_e = jnp.exp(b_end - logb), jnp.exp(b_end)
            else:
                e_q, e_k, e_s = dec_ref[g, 0], dec_ref[g, 1], dec_ref[g, 2]
                e_e = dec_ref[g, 3][0:1]
            qd = qg * e_q
            kd_ = (kg * e_k).astype(BF16)
            ks = (kg * e_s).astype(BF16)
            for a in range(hpg):
                h = g * hpg + a
                qa = (jnp.where(lane1 // dk == a, qd, 0.0) if hpg > 1 else qd).astype(BF16)
                vh = v_ref[r0:r0 + chunk, h * dv:(h + 1) * dv]
                s = lax.dot_general(qa, kd_, (((1,), (1,)), ((), ())),
                                    preferred_element_type=F32)
                s = jnp.where(keep, s, 0.0).astype(BF16)
                st = state_ref[h]
                o = (jnp.dot(s, vh, preferred_element_type=F32)
                     + lax.dot_general(qa, st.astype(BF16), (((1,), (1,)), ((), ())),
                                       preferred_element_type=F32))
                state_ref[h] = e_e * st + lax.dot_general(
                    vh, ks, (((0,), (0,)), ((), ())), preferred_element_type=F32)
                osl = (slice(r0, r0 + chunk), slice(h * dv, (h + 1) * dv))
                if reverse:
                    o = _rms(o + oprev_ref[osl]) * nw_ref[...]
                    out_ref[osl] = (o * _silu(gate_ref[osl].astype(F32))).astype(BF16)
                else:
                    out_ref[osl] = o


def bidir_scan(proj, la_f, la_b, norm_w, *, batch, seq, ctx_len, heads, dk, dv, chunk,
               q_blk, k_blk, v_blk, g_blk, log_decay_f=None, log_decay_b=None):
    t_all = proj.shape[0]
    gated = la_f is not None
    hk, hv = heads * dk, heads * dv
    rb = SCAN_ROWS
    lat_blocks = seq // rb
    ctx_base = (batch * seq) // rb
    assert ctx_len == rb

    def rows_fwd(b, j):
        return jnp.where(j == 0, ctx_base + b, b * lat_blocks + j - 1)

    def rows_bwd(b, j):
        return jnp.where(j == 0, ctx_base + b, b * lat_blocks + lat_blocks - j)

    outs = None
    for reverse, rows_of, la, ld in ((False, rows_fwd, la_f, log_decay_f),
                                     (True, rows_bwd, la_b, log_decay_b)):
        spec = lambda w, cb: pl.BlockSpec((rb, w), lambda b, j, cb=cb: (rows_of(b, j), cb))
        in_specs = [spec(hk, q_blk), spec(hk, k_blk), spec(hv, v_blk)]
        args = [proj, proj, proj]
        if gated:
            in_specs.append(spec(hk, 0))
            args.append(la)
        if reverse:
            in_specs += [spec(hv, 0), spec(hv, g_blk), pl.BlockSpec((1, dv), lambda b, j: (0, 0))]
            args += [outs, proj, norm_w]
        scratch = [pltpu.VMEM((heads, dv, LANES), F32)]
        if not gated:
            scratch.append(pltpu.VMEM((hk // LANES, 4, chunk, LANES), F32))
        kern = functools.partial(_scan_kernel, heads=heads, dk=dk, dv=dv, chunk=chunk,
                                 gated=gated, reverse=reverse, log_decay=ld)
        outs = pl.pallas_call(
            kern,
            grid=(batch, lat_blocks + 1),
            in_specs=in_specs,
            out_specs=spec(hv, 0),
            out_shape=jax.ShapeDtypeStruct((t_all, hv), BF16 if reverse else F32),
            scratch_shapes=scratch,
            compiler_params=_cparams(("parallel", "arbitrary")),
            name="scan_bwd" if reverse else "scan_fwd",
        )(*args)
    return outs


def _shortconv_kernel(z_ref, zp_ref, zn_ref, w_ref, b_ref, u_ref, x0_ref, *, tiles_per_seq):
    i = pl.program_id(0)
    tm = z_ref.shape[0]
    z = z_ref[...].astype(F32)
    first = (i % tiles_per_seq) == 0
    last = (i % tiles_per_seq) == tiles_per_seq - 1
    halo = zp_ref.shape[0]
    prev_row = jnp.where(first, 0.0, zp_ref[halo - 1:halo, :].astype(F32))
    next_row = jnp.where(last, 0.0, zn_ref[0:1, :].astype(F32))
    row = lax.broadcasted_iota(jnp.int32, z.shape, 0)
    z_prev = jnp.where(row == 0, prev_row, pltpu.roll(z, 1, 0))
    z_next = jnp.where(row == tm - 1, next_row, pltpu.roll(z, tm - 1, 0))
    y = w_ref[0:1] * z_prev + w_ref[1:2] * z + w_ref[2:3] * z_next + b_ref[...]
    c = HYENA_CH
    x0_ref[...] = y[:, 0:c].astype(BF16)
    u_ref[...] = (y[:, c:2 * c] * y[:, 2 * c:3 * c]).astype(BF16)


def hyena_shortconv(proj, short_w, short_b, *, row0, batch, seq_len, z_blk, tm=256):
    halo = 16
    tiles_per_seq = seq_len // tm
    n_tiles = batch * tiles_per_seq
    t0, h_per_tile = row0 // tm, tm // halo
    nz = 3 * HYENA_CH
    n_halo_blocks = proj.shape[0] // halo
    out_map = lambda i: (i % tiles_per_seq, i // tiles_per_seq)
    kern = functools.partial(_shortconv_kernel, tiles_per_seq=tiles_per_seq)
    return pl.pallas_call(
        kern,
        grid=(n_tiles,),
        in_specs=[pl.BlockSpec((tm, nz), lambda i: (t0 + i, z_blk)),
                  pl.BlockSpec((halo, nz),
                               lambda i: (jnp.maximum((t0 + i) * h_per_tile - 1, 0), z_blk)),
                  pl.BlockSpec((halo, nz),
                               lambda i: (jnp.minimum((t0 + i + 1) * h_per_tile,
                                                      n_halo_blocks - 1), z_blk)),
                  pl.BlockSpec((3, nz), lambda i: (0, 0)),
                  pl.BlockSpec((1, nz), lambda i: (0, 0))],
        out_specs=[pl.BlockSpec((tm, HYENA_CH), out_map), pl.BlockSpec((tm, HYENA_CH), out_map)],
        out_shape=[jax.ShapeDtypeStruct((seq_len, batch * HYENA_CH), BF16)] * 2,
        compiler_params=_cparams(("parallel",)),
        name="hyena_shortconv",
    )(proj, proj, proj, short_w, short_b)


def _filter_kernel(feat_ref, w1_ref, b1_ref, w2_ref, b2_ref, w3_ref, b3_ref, w4_ref, fq_ref,
                   hp_ref, hm_ref, *, seq_len):
    i = pl.program_id(0)
    tl = feat_ref.shape[0]
    fq = fq_ref[...]
    dot = lambda a, b: jnp.dot(a, b, preferred_element_type=F32, precision=HIGHEST)
    h = jnp.sin(fq * (dot(feat_ref[...], w1_ref[...]) + b1_ref[...]))
    h = jnp.sin(fq * (dot(h, w2_ref[...]) + b2_ref[...]))
    h = jnp.sin(fq * (dot(h, w3_ref[...]) + b3_ref[...]))
    h = dot(h, w4_ref[...])
    c = HYENA_CH
    max_decay = math.log(HYENA_TARGET) / HYENA_FAST_DECAY
    min_decay = math.log(HYENA_TARGET) / HYENA_SLOW_DECAY
    ch = lax.broadcasted_iota(jnp.int32, (tl, c), 1).astype(F32)
    deltas = min_decay + ch * ((max_decay - min_decay) / (c - 1))
    row = lax.broadcasted_iota(jnp.int32, (tl, c), 0) + i * tl
    t = row.astype(F32) * (1.0 / (seq_len - 1))
    window = jnp.exp(-t * jnp.abs(deltas))
    h_f = h[:, 0:c] * window
    h_b = jnp.where(row == 0, 0.0, h[:, c:2 * c] * window)
    hp_ref[...] = (h_f + h_b).astype(BF16)
    hm_ref[...] = (h_f - h_b).astype(BF16)


def hyena_filter_pair(feats, w1p, b1, w2, b2, w3, b3, w4, fq):
    seq_len = feats.shape[0]
    tl = min(512, seq_len)
    full = lambda a: pl.BlockSpec(a.shape, lambda i: (0,) * a.ndim)
    kern = functools.partial(_filter_kernel, seq_len=seq_len)
    return pl.pallas_call(
        kern,
        grid=(seq_len // tl,),
        in_specs=[pl.BlockSpec((tl, feats.shape[1]), lambda i: (i, 0)),
                  full(w1p), full(b1), full(w2), full(b2), full(w3), full(b3), full(w4), full(fq)],
        out_specs=[pl.BlockSpec((tl, HYENA_CH), lambda i: (i, 0))] * 2,
        out_shape=[jax.ShapeDtypeStruct((seq_len, HYENA_CH), BF16)] * 2,
        compiler_params=_cparams(("parallel",)),
        name="hyena_filter",
    )(feats, w1p, b1, w2, b2, w3, b3, w4, fq)


def _dft_fwd_kernel(c_ref, s_ref, u_ref, a_ref, b_ref):
    u = u_ref[...]
    a_ref[...] = jnp.dot(c_ref[...], u, preferred_element_type=F32)
    b_ref[...] = jnp.dot(s_ref[...], u, preferred_element_type=F32)


def dft_forward(cs, ss, u_ext):
    n, cols = u_ext.shape
    tf, tn = min(512, n), 1024
    return pl.pallas_call(
        _dft_fwd_kernel,
        grid=(cols // tn, n // tf),
        in_specs=[pl.BlockSpec((tf, n), lambda c, f: (f, 0)),
                  pl.BlockSpec((tf, n), lambda c, f: (f, 0)),
                  pl.BlockSpec((n, tn), lambda c, f: (0, c))],
        out_specs=[pl.BlockSpec((tf, tn), lambda c, f: (f, c))] * 2,
        out_shape=[jax.ShapeDtypeStruct((n, cols), F32)] * 2,
        compiler_params=_cparams(("parallel", "parallel")),
        name="dft_forward",
    )(cs, ss, u_ext)


def _spectral_mul_kernel(au_ref, bu_ref, ap_ref, bp_ref, am_ref, bm_ref, hc_ref, hs_ref,
                         yre_ref, yim_ref, *, scale):
    hc, hs = hc_ref[...], hs_ref[...]
    k_re = hc * ap_ref[...] + hs * bp_ref[...]
    k_im = hs * am_ref[...] - hc * bm_ref[...]
    a, b = au_ref[...], bu_ref[...]
    yre_ref[...] = ((a * k_re + b * k_im) * scale).astype(BF16)
    yim_ref[...] = ((a * k_im - b * k_re) * scale).astype(BF16)


def spectral_mul(a, b, half_cos, half_sin, *, batch):
    n = a.shape[0]
    c = HYENA_CH
    tf = min(512, n)
    kern = functools.partial(_spectral_mul_kernel, scale=1.0 / n)
    ucol = lambda f, bi: (f, bi)
    return pl.pallas_call(
        kern,
        grid=(n // tf, batch),
        in_specs=[pl.BlockSpec((tf, c), ucol), pl.BlockSpec((tf, c), ucol),
                  pl.BlockSpec((tf, c), lambda f, bi: (f, batch)),
                  pl.BlockSpec((tf, c), lambda f, bi: (f, batch)),
                  pl.BlockSpec((tf, c), lambda f, bi: (f, batch + 1)),
                  pl.BlockSpec((tf, c), lambda f, bi: (f, batch + 1)),
                  pl.BlockSpec((tf, 1), lambda f, bi: (f, 0)),
                  pl.BlockSpec((tf, 1), lambda f, bi: (f, 0))],
        out_specs=[pl.BlockSpec((tf, c), ucol)] * 2,
        out_shape=[jax.ShapeDtypeStruct((n, batch * c), BF16)] * 2,
        compiler_params=_cparams(("parallel", "parallel")),
        name="hyena_spectral_mul",
    )(a, b, a, b, a, b, half_cos, half_sin)


def _dft_inv_kernel(c_ref, s_ref, yre_ref, yim_ref, u_ref, x0_ref, bias_ref, o_ref):
    y = (jnp.dot(c_ref[...], yre_ref[...], preferred_element_type=F32)
         - jnp.dot(s_ref[...], yim_ref[...], preferred_element_type=F32))
    u = u_ref[...].astype(F32)
    o_ref[...] = (x0_ref[...].astype(F32) * (y + u * bias_ref[...])).astype(BF16)


def dft_inverse(cs, ss, yre, yim, u, x0, bias, *, batch):
    n = cs.shape[0]
    c = HYENA_CH
    tt = min(512, n)
    col = lambda bi, t: (0, bi)
    tile = lambda bi, t: (t, bi)
    return pl.pallas_call(
        _dft_inv_kernel,
        grid=(batch, n // tt),
        in_specs=[pl.BlockSpec((tt, n), lambda bi, t: (t, 0)),
                  pl.BlockSpec((tt, n), lambda bi, t: (t, 0)),
                  pl.BlockSpec((n, c), col), pl.BlockSpec((n, c), col),
                  pl.BlockSpec((tt, c), tile), pl.BlockSpec((tt, c), tile),
                  pl.BlockSpec((1, c), lambda bi, t: (0, 0))],
        out_specs=pl.BlockSpec((tt, c), lambda bi, t: (bi * (n // tt) + t, 0)),
        out_shape=jax.ShapeDtypeStruct((batch * n, c), BF16),
        compiler_params=_cparams(("parallel", "parallel")),
        name="dft_inverse",
    )(cs, ss, yre, yim, u, x0, bias)


def _shifted_dft_tables(n):
    f = lax.broadcasted_iota(jnp.int32, (n, n), 0)
    s = lax.broadcasted_iota(jnp.int32, (n, n), 1)
    m = ((2 * f + 1) * (2 * s + 1)) % (8 * n)
    ang = m.astype(F32) * (2.0 * math.pi / (8 * n))
    half = (2 * jnp.arange(n, dtype=jnp.int32) + 1).astype(F32)[:, None] * (math.pi / (4 * n))
    return jnp.cos(ang).astype(BF16), jnp.sin(ang).astype(BF16), jnp.cos(half), jnp.sin(half)


def _filter_features(seq_len):
    t = jnp.linspace(0.0, 1.0, seq_len, dtype=F32)[:, None]
    bands = (HYENA_EMB - 1) // 2
    ang = 2.0 * math.pi * jnp.arange(seq_len, dtype=F32)[:, None] / seq_len
    fr = jnp.linspace(1e-4, bands - 1, bands, dtype=F32)[None, :]
    feats = jnp.concatenate([t, jnp.cos(fr * ang), -jnp.sin(fr * ang)], axis=-1)
    return jnp.pad(feats, ((0, 0), (0, LANES - HYENA_EMB)))


def hyena_long_conv(proj, hp, *, row0, batch, seq_len, z_blk):
    u, x0 = hyena_shortconv(proj, hp["short_w"], hp["short_b"], row0=row0, batch=batch,
                            seq_len=seq_len, z_blk=z_blk)
    f_p, f_m = hyena_filter_pair(_filter_features(seq_len), hp["w1"], hp["b1"], hp["w2"], hp["b2"],
                                 hp["w3"], hp["b3"], hp["w4"], hp["freq"])
    cs, ss, half_cos, half_sin = _shifted_dft_tables(seq_len)
    a, b = dft_forward(cs, ss, jnp.concatenate([u, f_p, f_m], axis=1))
    yre, yim = spectral_mul(a, b, half_cos, half_sin, batch=batch)
    return dft_inverse(cs, ss, yre, yim, u, x0, hp["bias"], batch=batch)


def _outproj_kernel(ma_ref, mb_ref, x_ref, mod_ref, nw_ref, w_ref, rw_ref,
                    xo_ref, h_ref, lt_ref):
    half = ma_ref.shape[1]
    y = (jnp.dot(ma_ref[...], w_ref[0:half], preferred_element_type=F32)
         + jnp.dot(mb_ref[...], w_ref[half:2 * half], preferred_element_type=F32))
    mod = mod_ref[0]
    x = x_ref[...] + mod[2:3] * y
    xo_ref[...] = x
    h = _modulated_norm(x, mod, nw_ref[...], 3, 4)
    h_ref[...] = h.astype(BF16)
    lt_ref[...] = lax.dot_general(rw_ref[...], h, (((1,), (1,)), ((), ())),
                                  preferred_element_type=F32, precision=HIGHEST)


def outproj(mix_a, a_blk, mix_b, b_blk, x_all, mods, nw, w_out, router_wt, *,
            n_rows, t_lat, seq, tm=512):
    d = x_all.shape[1]
    half = d // 2
    return pl.pallas_call(
        _outproj_kernel,
        grid=(n_rows // tm,),
        in_specs=[pl.BlockSpec((tm, half), lambda i: (i, a_blk)),
                  pl.BlockSpec((tm, half), lambda i: (i, b_blk)),
                  pl.BlockSpec((tm, d), lambda i: (i, 0)),
                  pl.BlockSpec((1, 6, d), lambda i: (_seg_of_tile(i, tm, t_lat, seq), 0, 0)),
                  pl.BlockSpec((1, d), lambda i: (0, 0)),
                  pl.BlockSpec((d, d), lambda i: (0, 0)),
                  pl.BlockSpec((N_EXPERTS, d), lambda i: (0, 0))],
        out_specs=[pl.BlockSpec((tm, d), lambda i: (i, 0)),
                   pl.BlockSpec((tm, d), lambda i: (i, 0)),
                   pl.BlockSpec((N_EXPERTS, tm), lambda i: (0, i))],
        out_shape=[jax.ShapeDtypeStruct((n_rows, d), F32),
                   jax.ShapeDtypeStruct((n_rows, d), BF16),
                   jax.ShapeDtypeStruct((N_EXPERTS, n_rows), F32)],
        compiler_params=_cparams(("parallel",)),
        name="outproj",
    )(mix_a, mix_b, x_all, mods, nw, w_out, router_wt)


def _first_max(x, idx, sentinel):
    m = jnp.max(x, axis=0, keepdims=True)
    first = jnp.min(jnp.where(x == m, idx, sentinel), axis=0, keepdims=True)
    return m, idx == first


def _router_kernel(lt_ref, bias_ref, eye_ref, before_ref, below_ref, lists_ref, tok_ref, pc_ref):
    tn = lt_ref.shape[1]
    scores = jax.nn.sigmoid(lt_ref[...])
    sel = scores + bias_ref[...]
    neg = -jnp.inf
    in_grp = lax.broadcasted_iota(jnp.int32, (GROUP_SIZE, tn), 0)
    gscore = []
    for g in range(N_GROUPS):
        x = sel[g * GROUP_SIZE:(g + 1) * GROUP_SIZE]
        m1, hit = _first_max(x, in_grp, GROUP_SIZE)
        gscore.append(m1 + jnp.max(jnp.where(hit, neg, x), axis=0, keepdims=True))
    rows = []
    for g in range(N_GROUPS):
        beaten = jnp.zeros((1, tn), jnp.int32)
        for o in range(N_GROUPS):
            if o != g:
                wins = (gscore[o] >= gscore[g]) if o < g else (gscore[o] > gscore[g])
                beaten = beaten + wins.astype(jnp.int32)
        keep = jnp.broadcast_to(beaten < TOPK_GROUPS, (GROUP_SIZE, tn))
        rows.append(jnp.where(keep, sel[g * GROUP_SIZE:(g + 1) * GROUP_SIZE], neg))
    cand = jnp.concatenate(rows, axis=0)
    eidx = lax.broadcasted_iota(jnp.int32, cand.shape, 0)
    chosen = jnp.zeros(cand.shape, jnp.bool_)
    hits = []
    for _ in range(TOP_K):
        _, hit = _first_max(cand, eidx, N_EXPERTS)
        hits.append(hit)
        chosen = jnp.logical_or(chosen, hit)
        cand = jnp.where(hit, neg, cand)
    w = jnp.where(chosen, scores, 0.0)
    gates_t = w / jnp.sum(w, axis=0, keepdims=True) * ROUTED_SCALE
    chosen_f = jnp.where(chosen, 1.0, 0.0)
    rank = jnp.dot(chosen_f.astype(BF16), before_ref[...], preferred_element_type=F32)
    count = jnp.sum(chosen_f, axis=1, keepdims=True)
    pc = jnp.floor((count + (ROW_UNIT - 1)) * (1.0 / ROW_UNIT)) * ROW_UNIT
    pc_lanes = jnp.broadcast_to(pc, (N_EXPERTS, LANES))
    start = jnp.dot(below_ref[...], pc_lanes, preferred_element_type=F32,
                    precision=HIGHEST)[:, 0:1]
    slot = start + rank
    pick = lambda hit, v: jnp.sum(jnp.where(hit, v, 0.0), axis=0, keepdims=True)
    lists = jnp.concatenate([pick(h, slot) for h in hits] + [pick(h, gates_t) for h in hits], axis=0)
    lists_ref[...] = lists
    pc_ref[0] = pc_lanes
    padded = jnp.concatenate([lists, jnp.zeros((LANES - 2 * TOP_K, tn), F32)], axis=0)
    tok_ref[...] = lax.dot_general(eye_ref[...], padded, (((1,), (1,)), ((), ())),
                                   preferred_element_type=F32, precision=HIGHEST)


def router(logits_t, router_bias):
    e, t = logits_t.shape
    tn = MOE_TILE
    n_tiles = t // tn
    eye = jnp.eye(tn, dtype=F32)
    tok_i = jnp.arange(tn, dtype=jnp.int32)
    before = (tok_i[:, None] < tok_i[None, :]).astype(BF16)
    exp_i = jnp.arange(e, dtype=jnp.int32)
    below = (exp_i[None, :] < exp_i[:, None]).astype(F32)
    const = lambda shape: pl.BlockSpec(shape, lambda i: (0,) * len(shape))
    return pl.pallas_call(
        _router_kernel,
        grid=(n_tiles,),
        in_specs=[pl.BlockSpec((e, tn), lambda i: (0, i)),
                  const((e, 1)), const((tn, tn)), const((tn, tn)), const((e, e))],
        out_specs=[pl.BlockSpec((2 * TOP_K, tn), lambda i: (0, i)),
                   pl.BlockSpec((tn, LANES), lambda i: (i, 0)),
                   pl.BlockSpec((1, e, LANES), lambda i: (i, 0, 0))],
        out_shape=[jax.ShapeDtypeStruct((2 * TOP_K, t), F32),
                   jax.ShapeDtypeStruct((t, LANES), F32),
                   jax.ShapeDtypeStruct((n_tiles, e, LANES), F32)],
        compiler_params=_cparams(("parallel",)),
        name="router",
    )(logits_t, router_bias.reshape(e, 1), eye, before, below)


def moe_plan(pc, n_tiles):
    e = pc.shape[1]
    assert n_tiles >= e
    cap_rows = _sorted_capacity(n_tiles)
    run_end = jnp.cumsum(pc, axis=1)
    run_start = run_end - pc
    total = jnp.sum(pc, axis=0)
    total_al = ((total + EXPERT_ROWS - 1) // EXPERT_ROWS) * EXPERT_ROWS
    range_end = jnp.cumsum(total_al)
    range_start = range_end - total_al
    run_row = range_start[None, :] + jnp.cumsum(pc, axis=0) - pc
    unit_row = jnp.arange(N_UNITS, dtype=jnp.int32) * ROW_UNIT
    owner = jnp.sum((run_end[:, None, :] <= unit_row[None, :, None]).astype(jnp.int32), axis=2)
    owner_hot = (owner[:, :, None] == jnp.arange(e, dtype=jnp.int32)).astype(jnp.int32)
    dst = jnp.sum(owner_hot * (run_row - run_start)[:, None, :], axis=2) + unit_row[None, :]
    used = owner < e
    unit_dst = jnp.where(used, dst, cap_rows + unit_row[None, :]).astype(jnp.int32)
    unit_src = jnp.where(used, dst, 0).astype(jnp.int32)
    pad_off = jnp.arange(PAD_UNITS, dtype=jnp.int32)[None, :] * ROW_UNIT
    pad_row = (range_start + total)[:, None] + pad_off
    pad_dump = cap_rows + SLOTS + pad_off
    pad_dst = jnp.where(pad_row < range_end[:, None], pad_row, pad_dump)
    pad_dst = jnp.concatenate([pad_dst, jnp.broadcast_to(pad_dump, (n_tiles - e, PAD_UNITS))],
                              axis=0).astype(jnp.int32)
    tile_end = range_end // EXPERT_ROWS
    tile_expert = jnp.sum((tile_end[None, :] <= jnp.arange(cap_rows // EXPERT_ROWS,
                                                           dtype=jnp.int32)[:, None])
                          .astype(jnp.int32), axis=1)
    tile_expert = jnp.minimum(tile_expert, e - 1).astype(jnp.int32)
    return unit_dst, unit_src, pad_dst, tile_expert, tile_end[-1:].astype(jnp.int32)


def _sorted_capacity(n_tiles):
    worst_rows = n_tiles * (MOE_TILE * TOP_K + N_EXPERTS * (ROW_UNIT - 1)) + N_EXPERTS * (EXPERT_ROWS - 1)
    return -(-worst_rows // EXPERT_ROWS) * EXPERT_ROWS


DUMP_ROWS = SLOTS + PAD_UNITS * ROW_UNIT


def _dispatch_kernel(unit_ref, pad_ref, h_ref, lists_ref, xs_ref, loc_ref, sem):
    i = pl.program_id(0)
    tn = h_ref.shape[0]
    h = h_ref[...]
    for c0 in range(0, SLOTS, SLOT_CHUNK):
        srow = (lax.broadcasted_iota(jnp.int32, (SLOT_CHUNK, tn), 0) + c0).astype(F32)
        sel = srow == lists_ref[0:1, :]
        for k in range(1, TOP_K):
            sel = jnp.logical_or(sel, srow == lists_ref[k:k + 1, :])
        loc_ref[c0:c0 + SLOT_CHUNK, :] = jnp.dot(jnp.where(sel, 1.0, 0.0).astype(BF16), h,
                                                 preferred_element_type=F32).astype(BF16)
    for u in range(N_UNITS):
        row = pl.multiple_of(unit_ref[i, u], ROW_UNIT)
        pltpu.make_async_copy(loc_ref.at[pl.ds(u * ROW_UNIT, ROW_UNIT)],
                              xs_ref.at[pl.ds(row, ROW_UNIT)], sem).start()
    for p in range(PAD_UNITS):
        row = pl.multiple_of(pad_ref[i, p], ROW_UNIT)
        pltpu.make_async_copy(loc_ref.at[pl.ds(SLOTS - ROW_UNIT, ROW_UNIT)],
                              xs_ref.at[pl.ds(row, ROW_UNIT)], sem).start()
    pltpu.make_async_copy(loc_ref, xs_ref.at[pl.ds(0, SLOTS)], sem).wait()
    pltpu.make_async_copy(loc_ref.at[pl.ds(0, PAD_UNITS * ROW_UNIT)],
                          xs_ref.at[pl.ds(0, PAD_UNITS * ROW_UNIT)], sem).wait()


def moe_dispatch(h, lists, unit_dst, pad_dst, *, n_tiles):
    d = h.shape[1]
    cap_rows = _sorted_capacity(n_tiles) + DUMP_ROWS
    return pl.pallas_call(
        _dispatch_kernel,
        grid_spec=pltpu.PrefetchScalarGridSpec(
            num_scalar_prefetch=2,
            grid=(n_tiles,),
            in_specs=[pl.BlockSpec((MOE_TILE, d), lambda i, *_: (i, 0)),
                      pl.BlockSpec((2 * TOP_K, MOE_TILE), lambda i, *_: (0, i))],
            out_specs=pl.BlockSpec(memory_space=pl.ANY),
            scratch_shapes=[pltpu.VMEM((SLOTS, d), BF16), pltpu.SemaphoreType.DMA(())]),
        out_shape=jax.ShapeDtypeStruct((cap_rows, d), BF16),
        compiler_params=_cparams(("arbitrary",)),
        name="moe_dispatch",
    )(unit_dst, pad_dst, h, lists)


def _expert_kernel(te_ref, nt_ref, x_ref, wg_ref, wu_ref, wd_ref, y_ref, wg_bf, wu_bf, wd_bf):
    t = pl.program_id(0)
    live = t < nt_ref[0]
    tc = jnp.minimum(t, jnp.maximum(nt_ref[0] - 1, 0))
    new_expert = jnp.logical_or(t == 0, te_ref[tc] != te_ref[jnp.maximum(tc - 1, 0)])

    @pl.when(jnp.logical_and(live, new_expert))
    def _():
        wg_bf[...] = wg_ref[0, 0].astype(BF16)
        wu_bf[...] = wu_ref[0, 0].astype(BF16)
        wd_bf[...] = wd_ref[0, 0].astype(BF16)

    @pl.when(live)
    def _():
        x = x_ref[...]
        act = (_silu(jnp.dot(x, wg_bf[...], preferred_element_type=F32))
               * jnp.dot(x, wu_bf[...], preferred_element_type=F32))
        y_ref[...] = jnp.dot(act.astype(BF16), wd_bf[...], preferred_element_type=F32).astype(BF16)


def moe_experts(xs, w_gate, w_up, w_down, layer, tile_expert, n_row_tiles):
    d = xs.shape[1]
    cap_rows = xs.shape[0] - DUMP_ROWS
    ff = w_gate.shape[3]
    last_live = lambda nt: jnp.maximum(nt[0] - 1, 0)
    row_map = lambda t, te, nt: (jnp.minimum(t, last_live(nt)), 0)
    w_map = lambda t, te, nt: (layer, te[jnp.minimum(t, last_live(nt))], 0, 0)
    return pl.pallas_call(
        _expert_kernel,
        grid_spec=pltpu.PrefetchScalarGridSpec(
            num_scalar_prefetch=2,
            grid=(cap_rows // EXPERT_ROWS,),
            in_specs=[pl.BlockSpec((EXPERT_ROWS, d), row_map),
                      pl.BlockSpec((1, 1, d, ff), w_map), pl.BlockSpec((1, 1, d, ff), w_map),
                      pl.BlockSpec((1, 1, ff, d), w_map)],
            out_specs=pl.BlockSpec((EXPERT_ROWS, d), row_map),
            scratch_shapes=[pltpu.VMEM((d, ff), BF16), pltpu.VMEM((d, ff), BF16),
                            pltpu.VMEM((ff, d), BF16)]),
        out_shape=jax.ShapeDtypeStruct((cap_rows, d), BF16),
        compiler_params=_cparams(("arbitrary",)),
        name="moe_experts",
    )(tile_expert, n_row_tiles, xs, w_gate, w_up, w_down)


def _combine_kernel(unit_ref, ys_ref, tok_ref, h_ref, x_ref, mod_ref, fnw_ref, sg_ref, su_ref,
                    sd_ref, o_ref, loc_ref, sem, *, final_norm):
    i = pl.program_id(0)
    tn = h_ref.shape[0]
    for u in range(N_UNITS):
        row = pl.multiple_of(unit_ref[i, u], ROW_UNIT)
        pltpu.make_async_copy(ys_ref.at[pl.ds(row, ROW_UNIT)],
                              loc_ref.at[pl.ds(u * ROW_UNIT, ROW_UNIT)], sem).start()
    h = h_ref[...]
    act = (_silu(jnp.dot(h, sg_ref[...], preferred_element_type=F32))
           * jnp.dot(h, su_ref[...], preferred_element_type=F32))
    acc = jnp.dot(act.astype(BF16), sd_ref[...], preferred_element_type=F32)
    pltpu.make_async_copy(ys_ref.at[pl.ds(0, SLOTS)], loc_ref, sem).wait()
    tok = tok_ref[...]
    for c0 in range(0, SLOTS, SLOT_CHUNK):
        scol = (lax.broadcasted_iota(jnp.int32, (tn, SLOT_CHUNK), 1) + c0).astype(F32)
        q = jnp.zeros((tn, SLOT_CHUNK), F32)
        for k in range(TOP_K):
            q = q + jnp.where(scol == tok[:, k:k + 1], tok[:, TOP_K + k:TOP_K + k + 1], 0.0)
        acc = acc + jnp.dot(q.astype(BF16), loc_ref[c0:c0 + SLOT_CHUNK, :],
                            preferred_element_type=F32)
    x = x_ref[...] + mod_ref[0][5:6] * acc
    if final_norm:
        x = _rms(x) * fnw_ref[...]
    o_ref[...] = x


def moe_combine(ys, tok, h, x_all, mods, fnw, sh_gate, sh_up, sh_down, unit_dst, *,
                n_tiles, t_lat, seq, final_norm):
    d = h.shape[1]
    tn = MOE_TILE
    const = lambda a: pl.BlockSpec(a.shape, lambda i, *_: (0,) * a.ndim)
    kern = functools.partial(_combine_kernel, final_norm=final_norm)
    return pl.pallas_call(
        kern,
        grid_spec=pltpu.PrefetchScalarGridSpec(
            num_scalar_prefetch=1,
            grid=(n_tiles,),
            in_specs=[pl.BlockSpec(memory_space=pl.ANY),
                      pl.BlockSpec((tn, LANES), lambda i, *_: (i, 0)),
                      pl.BlockSpec((tn, d), lambda i, *_: (i, 0)),
                      pl.BlockSpec((tn, d), lambda i, *_: (i, 0)),
                      pl.BlockSpec((1, 6, d),
                                   lambda i, *_: (_seg_of_tile(i, tn, t_lat, seq), 0, 0)),
                      const(fnw), const(sh_gate), const(sh_up), const(sh_down)],
            out_specs=pl.BlockSpec((tn, d), lambda i, *_: (i, 0)),
            scratch_shapes=[pltpu.VMEM((SLOTS, d), BF16), pltpu.SemaphoreType.DMA(())]),
        out_shape=jax.ShapeDtypeStruct((n_tiles * tn, d), F32),
        compiler_params=_cparams(("arbitrary",)),
        name="moe_combine",
    )(unit_dst, ys, tok, h, x_all, mods, fnw, sh_gate, sh_up, sh_down)


def moe_block(h2, logits_t, router_bias, x_mid, mods, fnw, w_gate, w_up, w_down,
              sh_gate, sh_up, sh_down, *, layer, t_lat, seq, final_norm):
    n_tiles = h2.shape[0] // MOE_TILE
    lists, tok, pc = router(logits_t, router_bias)
    unit_dst, unit_src, pad_dst, tile_expert, n_row_tiles = moe_plan(
        pc[:, :, 0].astype(jnp.int32), n_tiles)
    xs = moe_dispatch(h2, lists, unit_dst, pad_dst, n_tiles=n_tiles)
    ys = moe_experts(xs, w_gate, w_up, w_down, layer, tile_expert, n_row_tiles)
    return moe_combine(ys, tok, h2, x_mid, mods, fnw, sh_gate.astype(BF16), sh_up.astype(BF16),
                       sh_down.astype(BF16), unit_src, n_tiles=n_tiles, t_lat=t_lat, seq=seq,
                       final_norm=final_norm)


def _rope_tables(seq, tm):
    half = RET_DK // 2
    pos = jnp.arange(seq, dtype=jnp.int32)
    row = (pos // GRID_W).astype(F32)
    col = (pos % GRID_W).astype(F32)
    inv = ROPE_BASE ** (-jnp.arange(0, half, 2, dtype=F32) / half)
    a_row, a_col = row[:, None] * inv[None, :], col[:, None] * inv[None, :]
    cos_h = jnp.concatenate([jnp.cos(a_row)] * 2 + [jnp.cos(a_col)] * 2, axis=-1)
    sin_h = jnp.concatenate([-jnp.sin(a_row), jnp.sin(a_row), -jnp.sin(a_col), jnp.sin(a_col)], -1)
    reps = 2 * RET_HEADS
    cos_t = jnp.concatenate([jnp.tile(cos_h, (1, reps)), jnp.ones((tm, reps * RET_DK), F32)], 0)
    sin_t = jnp.concatenate([jnp.tile(sin_h, (1, reps)), jnp.zeros((tm, reps * RET_DK), F32)], 0)
    return cos_t, sin_t


def _rope_partner_columns():
    quarter = RET_DK // 4
    idx = jnp.arange(2 * RET_HEADS * RET_DK, dtype=jnp.int32)
    within = idx % (2 * quarter)
    return jnp.where(within < quarter, idx + quarter, idx - quarter)


def kernel(x, c, ctx, c_ctx, ada_w, ada_b, norm1_w, norm2_w, ev_w_in, ev_short_w, ev_short_b, ev_filt_w1, ev_filt_b1, ev_filt_w2, ev_filt_b2, ev_filt_w3, ev_filt_b3, ev_filt_w4, ev_filt_freq, ev_hyena_bias, ev_w_out, od_w_in, od_gate_w1_f, od_gate_w2_f, od_gate_b_f, od_gate_w1_b, od_gate_w2_b, od_gate_b_b, od_norm_w, od_w_out, router_w, router_bias, exp_w_gate, exp_w_up, exp_w_down, sh_w_gate, sh_w_up, sh_w_down, final_norm_w):
    batch, seq, d = x.shape
    ctx_len = ctx.shape[1]
    depth = ada_w.shape[0]
    t_lat, t_ctx = batch * seq, batch * ctx_len
    t_all = t_lat + t_ctx
    tm = 512

    x_all = jnp.concatenate([x.reshape(t_lat, d), ctx.reshape(t_ctx, d)], axis=0)
    cond8 = jnp.concatenate([c_ctx[None, :], c, jnp.zeros((8 - 1 - batch, d), F32)], axis=0)
    mods_all = adaln_rows(cond8, ada_w, ada_b).reshape(depth, 8, 6, d)

    for i in range(depth):
        last = i == depth - 1
        j = i // 2
        mods = mods_all[i]
        nw1, nw2 = norm1_w[i][None, :], norm2_w[i][None, :]
        if i % 2 == 0:
            qk = 2 * RET_HEADS * RET_DK
            k_scale = jnp.concatenate([jnp.ones((qk // 2,), F32),
                                       jnp.full((qk // 2,), RET_DK ** -0.5, F32)])
            w_in = ev_w_in[j]
            w_qk = w_in[:, :qk] * k_scale
            w_ext = jnp.concatenate([w_qk, w_in[:, qk:], w_qk[:, _rope_partner_columns()]],
                                    axis=1).astype(BF16)
            cos_t, sin_t = _rope_tables(seq, tm)
            proj = inproj_even(x_all, mods, nw1, w_ext, cos_t, sin_t, t_lat=t_lat, seq=seq, tm=tm)
            log_g = [math.log1p(-2.0 ** (-5.0 - h)) for h in range(RET_HEADS)]
            mix_a = bidir_scan(proj, None, None, jnp.ones((1, RET_DV), F32),
                               batch=batch, seq=seq, ctx_len=ctx_len, heads=RET_HEADS, dk=RET_DK,
                               dv=RET_DV, chunk=RET_CHUNK, q_blk=0, k_blk=1, v_blk=1, g_blk=2,
                               log_decay_f=log_g, log_decay_b=log_g[::-1])
            hp = dict(short_w=ev_short_w[j], short_b=ev_short_b[j][None, :],
                      w1=jnp.pad(ev_filt_w1[j], ((0, LANES - HYENA_EMB), (0, 0))),
                      b1=ev_filt_b1[j][None, :], w2=ev_filt_w2[j], b2=ev_filt_b2[j][None, :],
                      w3=ev_filt_w3[j], b3=ev_filt_b3[j][None, :], w4=ev_filt_w4[j],
                      freq=ev_filt_freq[j][None, :], bias=ev_hyena_bias[j][None, :])
            hy_lat = hyena_long_conv(proj, hp, row0=0, batch=batch, seq_len=seq, z_blk=1)
            hy_ctx = hyena_long_conv(proj, hp, row0=t_lat, batch=batch, seq_len=ctx_len, z_blk=1)
            mix_b = jnp.concatenate([hy_lat, hy_ctx], axis=0)
            a_blk, b_blk = 0, 0
            w_out = ev_w_out[j].astype(BF16)
        else:
            kd = GLA_HEADS * GLA_DK
            pad_cols = LANES - 2 * GLA_RANK
            w_ext = jnp.concatenate([od_w_in[j], od_gate_w1_f[j], od_gate_w1_b[j],
                                     jnp.zeros((d, pad_cols), F32)], axis=1).astype(BF16)
            w2f = jnp.pad(od_gate_w2_f[j], ((0, LANES - GLA_RANK), (0, 0)))
            w2b = jnp.pad(od_gate_w2_b[j], ((GLA_RANK, LANES - 2 * GLA_RANK), (0, 0)))
            proj, la_f, la_b = inproj_odd(x_all, mods, nw1, w_ext, w2f, w2b,
                                          od_gate_b_f[j][None, :], od_gate_b_b[j][None, :],
                                          t_lat=t_lat, seq=seq, tm=tm)
            mix_a = bidir_scan(proj, la_f, la_b, od_norm_w[j][None, :],
                               batch=batch, seq=seq, ctx_len=ctx_len, heads=GLA_HEADS, dk=GLA_DK,
                               dv=GLA_DV, chunk=GLA_CHUNK, q_blk=0, k_blk=1, v_blk=1, g_blk=2)
            mix_b = mix_a
            a_blk, b_blk = 0, 1
            w_out = od_w_out[j].astype(BF16)

        n_rows = t_lat if last else t_all
        x_mid, h2, logits_t = outproj(mix_a, a_blk, mix_b, b_blk, x_all, mods, nw2, w_out,
                                      router_w[i].T, n_rows=n_rows, t_lat=t_lat, seq=seq, tm=tm)
        x_all = moe_block(h2, logits_t, router_bias[i], x_mid, mods, final_norm_w[None, :],
                          exp_w_gate, exp_w_up, exp_w_down,
                          sh_w_gate[i], sh_w_up[i], sh_w_down[i],
                          layer=i, t_lat=t_lat, seq=seq, final_norm=last)
    return x_all[:t_lat].reshape(batch, seq, d)
```

```python
import functools
import math

import jax
import jax.numpy as jnp
from jax import lax
from jax.experimental import pallas as pl
from jax.experimental.pallas import tpu as pltpu

F32 = jnp.float32
BF16 = jnp.bfloat16
HIGHEST = lax.Precision.HIGHEST

D_MODEL = 1024
GRID_W = 64
NORM_EPS = 1e-6
RET_HEADS, RET_DK, RET_DV, RET_CHUNK = 4, 64, 128, 128
ROPE_BASE = 10000.0
HYENA_CH, HYENA_EMB = 512, 33
HYENA_FAST_DECAY, HYENA_SLOW_DECAY, HYENA_TARGET = 0.3, 1.5, 1e-2
GLA_HEADS, GLA_DK, GLA_DV, GLA_RANK, GLA_TAU, GLA_CHUNK = 4, 128, 256, 16, 16.0, 64
N_EXPERTS, TOP_K, N_GROUPS, TOPK_GROUPS = 64, 8, 8, 4
GROUP_SIZE = N_EXPERTS // N_GROUPS
EXPERT_FF = 256
ROUTED_SCALE = 2.5

LANES = 128
SCAN_ROWS = 256
MOE_TILE = 256
ROW_UNIT = 16
EXPERT_ROWS = 512
SLOT_CHUNK = 512
SLOTS = -(-(MOE_TILE * TOP_K + N_EXPERTS * (ROW_UNIT - 1) + ROW_UNIT) // SLOT_CHUNK) * SLOT_CHUNK
N_UNITS = SLOTS // ROW_UNIT
PAD_UNITS = EXPERT_ROWS // ROW_UNIT
VMEM_LIMIT = 56 * 1024 * 1024


def _cparams(sem):
    return pltpu.CompilerParams(dimension_semantics=sem, vmem_limit_bytes=VMEM_LIMIT)


def _rms(x):
    return x * lax.rsqrt(jnp.mean(x * x, axis=-1, keepdims=True) + NORM_EPS)


def _silu(x):
    return x * jax.nn.sigmoid(x)


def _adaln_kernel(c_ref, w_ref, b_ref, o_ref):
    c = c_ref[...]
    o_ref[0] = jnp.dot(_silu(c), w_ref[0], preferred_element_type=F32,
                       precision=HIGHEST) + b_ref[0]


def adaln_rows(cond8, ada_w, ada_b):
    depth, d, n = ada_w.shape
    tn = 512
    return pl.pallas_call(
        _adaln_kernel,
        grid=(depth, n // tn),
        in_specs=[pl.BlockSpec((8, d), lambda l, j: (0, 0)),
                  pl.BlockSpec((1, d, tn), lambda l, j: (l, 0, j)),
                  pl.BlockSpec((1, 1, tn), lambda l, j: (l, 0, j))],
        out_specs=pl.BlockSpec((1, 8, tn), lambda l, j: (l, 0, j)),
        out_shape=jax.ShapeDtypeStruct((depth, 8, n), F32),
        compiler_params=_cparams(("parallel", "parallel")),
        name="adaln_rows",
    )(cond8, ada_w, ada_b.reshape(depth, 1, n))


def _modulated_norm(x, mod, nw, shift_row, scale_row):
    return _rms(x) * nw * (1.0 + mod[scale_row:scale_row + 1]) + mod[shift_row:shift_row + 1]


def _inproj_even_kernel(x_ref, mod_ref, nw_ref, w_ref, cos_ref, sin_ref, o_ref):
    h = _modulated_norm(x_ref[...], mod_ref[0], nw_ref[...], 0, 1).astype(BF16)
    n_main = o_ref.shape[1]
    qk = jnp.dot(h, w_ref[:, 0:512], preferred_element_type=F32)
    qk_sw = jnp.dot(h, w_ref[:, n_main:n_main + 512], preferred_element_type=F32)
    o_ref[:, 0:512] = (qk * cos_ref[...] + qk_sw * sin_ref[...]).astype(BF16)
    for c0 in range(512, n_main, 512):
        o_ref[:, c0:c0 + 512] = jnp.dot(h, w_ref[:, c0:c0 + 512],
                                        preferred_element_type=F32).astype(BF16)


def _inproj_odd_kernel(x_ref, mod_ref, nw_ref, w_ref, w2f_ref, w2b_ref, bf_ref, bb_ref,
                       o_ref, laf_ref, lab_ref):
    h = _modulated_norm(x_ref[...], mod_ref[0], nw_ref[...], 0, 1).astype(BF16)
    n_main = o_ref.shape[1]
    kd = GLA_HEADS * GLA_DK
    q = jnp.dot(h, w_ref[:, 0:kd], preferred_element_type=F32)
    o_ref[:, 0:kd] = (q * (GLA_DK ** -0.5)).astype(BF16)
    for c0 in range(kd, n_main, 512):
        o_ref[:, c0:c0 + 512] = jnp.dot(h, w_ref[:, c0:c0 + 512],
                                        preferred_element_type=F32).astype(BF16)
    low = jnp.dot(h, w_ref[:, n_main:n_main + LANES], preferred_element_type=F32)

    def log_gate(w2_ref, b_ref):
        z = jnp.dot(low, w2_ref[...], preferred_element_type=F32, precision=HIGHEST) + b_ref[...]
        return (jnp.minimum(z, 0.0) - jnp.log(1.0 + jnp.exp(-jnp.abs(z)))) * (1.0 / GLA_TAU)

    laf_ref[...] = log_gate(w2f_ref, bf_ref)
    lab_ref[...] = log_gate(w2b_ref, bb_ref)


def _seg_of_tile(i, tm, t_lat, seq):
    return jnp.where(i < t_lat // tm, 1 + (i * tm) // seq, 0)


def inproj_even(x_all, mods, nw, w_ext, cos_t, sin_t, *, t_lat, seq, tm=512):
    t_all, d = x_all.shape
    n_ext = w_ext.shape[1]
    n_main = n_ext - 512
    n_lat_tiles, pos_tiles = t_lat // tm, seq // tm

    def pos_map(i):
        return (jnp.where(i < n_lat_tiles, i % pos_tiles, pos_tiles), 0)

    return pl.pallas_call(
        _inproj_even_kernel,
        grid=(t_all // tm,),
        in_specs=[pl.BlockSpec((tm, d), lambda i: (i, 0)),
                  pl.BlockSpec((1, 6, d), lambda i: (_seg_of_tile(i, tm, t_lat, seq), 0, 0)),
                  pl.BlockSpec((1, d), lambda i: (0, 0)),
                  pl.BlockSpec((d, n_ext), lambda i: (0, 0)),
                  pl.BlockSpec((tm, 512), pos_map),
                  pl.BlockSpec((tm, 512), pos_map)],
        out_specs=pl.BlockSpec((tm, n_main), lambda i: (i, 0)),
        out_shape=jax.ShapeDtypeStruct((t_all, n_main), BF16),
        compiler_params=_cparams(("parallel",)),
        name="inproj_even",
    )(x_all, mods, nw, w_ext, cos_t, sin_t)


def inproj_odd(x_all, mods, nw, w_ext, w2f, w2b, bf, bb, *, t_lat, seq, tm=512):
    t_all, d = x_all.shape
    n_main = w_ext.shape[1] - LANES
    kd = GLA_HEADS * GLA_DK
    full = lambda shape: pl.BlockSpec(shape, lambda i: (0,) * len(shape))
    return pl.pallas_call(
        _inproj_odd_kernel,
        grid=(t_all // tm,),
        in_specs=[pl.BlockSpec((tm, d), lambda i: (i, 0)),
                  pl.BlockSpec((1, 6, d), lambda i: (_seg_of_tile(i, tm, t_lat, seq), 0, 0)),
                  full((1, d)), full(w_ext.shape), full(w2f.shape), full(w2b.shape),
                  full((1, kd)), full((1, kd))],
        out_specs=[pl.BlockSpec((tm, n_main), lambda i: (i, 0)),
                   pl.BlockSpec((tm, kd), lambda i: (i, 0)),
                   pl.BlockSpec((tm, kd), lambda i: (i, 0))],
        out_shape=[jax.ShapeDtypeStruct((t_all, n_main), BF16),
                   jax.ShapeDtypeStruct((t_all, kd), F32),
                   jax.ShapeDtypeStruct((t_all, kd), F32)],
        compiler_params=_cparams(("parallel",)),
        name="inproj_odd",
    )(x_all, mods, nw, w_ext, w2f, w2b, bf, bb)


def _scan_kernel(*refs, heads, dk, dv, chunk, gated, reverse, log_decay):
    it = iter(refs)
    q_ref, k_ref, v_ref = next(it), next(it), next(it)
    la_ref = next(it) if gated else None
    if reverse:
        oprev_ref, gate_ref, nw_ref = next(it), next(it), next(it)
    out_ref = next(it)
    state_ref = next(it)
    dec_ref = None if gated else next(it)

    hpg = LANES // dk
    groups = heads // hpg
    rows = q_ref.shape[0]
    n_chunks = rows // chunk
    j = pl.program_id(1)

    row_i = lax.broadcasted_iota(jnp.int32, (rows, rows), 0)
    col_i = lax.broadcasted_iota(jnp.int32, (rows, rows), 1)
    same_chunk = (row_i // chunk) == (col_i // chunk)
    keep = jnp.logical_and(same_chunk, (col_i >= row_i) if reverse else (row_i >= col_i))

    @pl.when(j == 0)
    def _init():
        state_ref[...] = jnp.zeros_like(state_ref)
        if not gated:
            pos = lax.broadcasted_iota(jnp.int32, (rows, LANES), 0) % chunk
            steps = ((chunk - pos) if reverse else (pos + 1)).astype(F32)
            lane = lax.broadcasted_iota(jnp.int32, (rows, LANES), 1)
            for g in range(groups):
                lg = jnp.zeros((rows, LANES), F32)
                for a in range(hpg):
                    lg = jnp.where(lane // dk == a, log_decay[g * hpg + a], lg)
                logb = steps * lg
                b_end = float(chunk) * lg
                dec_ref[g, 0] = jnp.exp(logb)
                dec_ref[g, 1] = jnp.exp(-logb)
                dec_ref[g, 2] = jnp.exp(b_end - logb)
                dec_ref[g, 3] = jnp.exp(b_end)

    if gated:
        tri = jnp.where(keep, 1.0, 0.0).astype(BF16)
        la = la_ref[...]
        la_hi = la.astype(BF16)
        la_lo = (la - la_hi.astype(F32)).astype(BF16)
        logb_all = (jnp.dot(tri, la_hi, preferred_element_type=F32)
                    + jnp.dot(tri, la_lo, preferred_element_type=F32))

    lane1 = lax.broadcasted_iota(jnp.int32, (1, LANES), 1)
    order = range(n_chunks - 1, -1, -1) if reverse else range(n_chunks)
    end_row = lambda c: c * chunk if reverse else (c + 1) * chunk - 1
    for g in range(groups):
        ksl = slice(g * LANES, (g + 1) * LANES)
        qg = q_ref[:, ksl].astype(F32)
        kg = k_ref[:, ksl].astype(F32)
        if gated:
            logb = logb_all[:, ksl]
            ends = [logb[end_row(c):end_row(c) + 1] for c in range(n_chunks)]
            b_end = jnp.concatenate([jnp.broadcast_to(e, (chunk, LANES)) for e in ends], axis=0)
            e_q, e_k, e_s = jnp.exp(logb), jnp.exp(-logb), jnp.exp(b_end - logb)
            e_e = [jnp.exp(e) for e in ends]
        else:
            e_q, e_k, e_s = dec_ref[g, 0], dec_ref[g, 1], dec_ref[g, 2]
            e_e = [dec_ref[g, 3][0:1]] * n_chunks
        qd = qg * e_q
        kd_ = (kg * e_k).astype(BF16)
        ks = (kg * e_s).astype(BF16)
        for a in range(hpg):
            h = g * hpg + a
            qa = (jnp.where(lane1 // dk == a, qd, 0.0) if hpg > 1 else qd).astype(BF16)
            vh = v_ref[:, h * dv:(h + 1) * dv]
            s = lax.dot_general(qa, kd_, (((1,), (1,)), ((), ())), preferred_element_type=F32)
            o_intra = jnp.dot(jnp.where(keep, s, 0.0).astype(BF16), vh,
                              preferred_element_type=F32)
            st = state_ref[h]
            for c in order:
                rs = slice(c * chunk, (c + 1) * chunk)
                o = o_intra[rs] + lax.dot_general(qa[rs], st.astype(BF16), (((1,), (1,)), ((), ())),
                                                  preferred_element_type=F32)
                st = e_e[c] * st + lax.dot_general(vh[rs], ks[rs], (((0,), (0,)), ((), ())),
                                                   preferred_element_type=F32)
                osl = (rs, slice(h * dv, (h + 1) * dv))
                if reverse:
                    o = _rms(o + oprev_ref[osl]) * nw_ref[...]
                    out_ref[osl] = (o * _silu(gate_ref[osl].astype(F32))).astype(BF16)
                else:
                    out_ref[osl] = o
            state_ref[h] = st


def bidir_scan(proj, la_f, la_b, norm_w, *, batch, seq, ctx_len, heads, dk, dv, chunk,
               q_blk, k_blk, v_blk, g_blk, log_decay_f=None, log_decay_b=None):
    t_all = proj.shape[0]
    gated = la_f is not None
    hk, hv = heads * dk, heads * dv
    rb = SCAN_ROWS
    lat_blocks = seq // rb
    ctx_base = (batch * seq) // rb
    assert ctx_len == rb

    def rows_fwd(b, j):
        return jnp.where(j == 0, ctx_base + b, b * lat_blocks + j - 1)

    def rows_bwd(b, j):
        return jnp.where(j == 0, ctx_base + b, b * lat_blocks + lat_blocks - j)

    outs = None
    for reverse, rows_of, la, ld in ((False, rows_fwd, la_f, log_decay_f),
                                     (True, rows_bwd, la_b, log_decay_b)):
        spec = lambda w, cb: pl.BlockSpec((rb, w), lambda b, j, cb=cb: (rows_of(b, j), cb))
        in_specs = [spec(hk, q_blk), spec(hk, k_blk), spec(hv, v_blk)]
        args = [proj, proj, proj]
        if gated:
            in_specs.append(spec(hk, 0))
            args.append(la)
        if reverse:
            in_specs += [spec(hv, 0), spec(hv, g_blk), pl.BlockSpec((1, dv), lambda b, j: (0, 0))]
            args += [outs, proj, norm_w]
        scratch = [pltpu.VMEM((heads, dv, LANES), F32)]
        if not gated:
            scratch.append(pltpu.VMEM((hk // LANES, 4, rb, LANES), F32))
        kern = functools.partial(_scan_kernel, heads=heads, dk=dk, dv=dv, chunk=chunk,
                                 gated=gated, reverse=reverse, log_decay=ld)
        outs = pl.pallas_call(
            kern,
            grid=(batch, lat_blocks + 1),
            in_specs=in_specs,
            out_specs=spec(hv, 0),
            out_shape=jax.ShapeDtypeStruct((t_all, hv), BF16 if reverse else F32),
            scratch_shapes=scratch,
            compiler_params=_cparams(("parallel", "arbitrary")),
            name="scan_bwd" if reverse else "scan_fwd",
        )(*args)
    return outs


def _shortconv_kernel(z_ref, zp_ref, zn_ref, w_ref, b_ref, u_ref, x0_ref, *, tiles_per_seq):
    i = pl.program_id(0)
    tm = z_ref.shape[0]
    z = z_ref[...].astype(F32)
    first = (i % tiles_per_seq) == 0
    last = (i % tiles_per_seq) == tiles_per_seq - 1
    halo = zp_ref.shape[0]
    prev_row = jnp.where(first, 0.0, zp_ref[halo - 1:halo, :].astype(F32))
    next_row = jnp.where(last, 0.0, zn_ref[0:1, :].astype(F32))
    row = lax.broadcasted_iota(jnp.int32, z.shape, 0)
    z_prev = jnp.where(row == 0, prev_row, pltpu.roll(z, 1, 0))
    z_next = jnp.where(row == tm - 1, next_row, pltpu.roll(z, tm - 1, 0))
    y = w_ref[0:1] * z_prev + w_ref[1:2] * z + w_ref[2:3] * z_next + b_ref[...]
    c = HYENA_CH
    x0_ref[...] = y[:, 0:c].astype(BF16)
    u_ref[...] = (y[:, c:2 * c] * y[:, 2 * c:3 * c]).astype(BF16)


def hyena_shortconv(proj, short_w, short_b, *, row0, batch, seq_len, z_blk, tm=256):
    halo = 16
    tiles_per_seq = seq_len // tm
    n_tiles = batch * tiles_per_seq
    t0, h_per_tile = row0 // tm, tm // halo
    nz = 3 * HYENA_CH
    n_halo_blocks = proj.shape[0] // halo
    out_map = lambda i: (i % tiles_per_seq, i // tiles_per_seq)
    kern = functools.partial(_shortconv_kernel, tiles_per_seq=tiles_per_seq)
    return pl.pallas_call(
        kern,
        grid=(n_tiles,),
        in_specs=[pl.BlockSpec((tm, nz), lambda i: (t0 + i, z_blk)),
                  pl.BlockSpec((halo, nz),
                               lambda i: (jnp.maximum((t0 + i) * h_per_tile - 1, 0), z_blk)),
                  pl.BlockSpec((halo, nz),
                               lambda i: (jnp.minimum((t0 + i + 1) * h_per_tile,
                                                      n_halo_blocks - 1), z_blk)),
                  pl.BlockSpec((3, nz), lambda i: (0, 0)),
                  pl.BlockSpec((1, nz), lambda i: (0, 0))],
        out_specs=[pl.BlockSpec((tm, HYENA_CH), out_map), pl.BlockSpec((tm, HYENA_CH), out_map)],
        out_shape=[jax.ShapeDtypeStruct((seq_len, batch * HYENA_CH), BF16)] * 2,
        compiler_params=_cparams(("parallel",)),
        name="hyena_shortconv",
    )(proj, proj, proj, short_w, short_b)


def _filter_kernel(feat_ref, w1_ref, b1_ref, w2_ref, b2_ref, w3_ref, b3_ref, w4_ref, fq_ref,
                   hp_ref, hm_ref, *, seq_len):
    i = pl.program_id(0)
    tl = feat_ref.shape[0]
    fq = fq_ref[...]
    dot = lambda a, b: jnp.dot(a, b, preferred_element_type=F32, precision=HIGHEST)
    h = jnp.sin(fq * (dot(feat_ref[...], w1_ref[...]) + b1_ref[...]))
    h = jnp.sin(fq * (dot(h, w2_ref[...]) + b2_ref[...]))
    h = jnp.sin(fq * (dot(h, w3_ref[...]) + b3_ref[...]))
    h = dot(h, w4_ref[...])
    c = HYENA_CH
    max_decay = math.log(HYENA_TARGET) / HYENA_FAST_DECAY
    min_decay = math.log(HYENA_TARGET) / HYENA_SLOW_DECAY
    ch = lax.broadcasted_iota(jnp.int32, (tl, c), 1).astype(F32)
    deltas = min_decay + ch * ((max_decay - min_decay) / (c - 1))
    row = lax.broadcasted_iota(jnp.int32, (tl, c), 0) + i * tl
    t = row.astype(F32) * (1.0 / (seq_len - 1))
    window = jnp.exp(-t * jnp.abs(deltas))
    h_f = h[:, 0:c] * window
    h_b = jnp.where(row == 0, 0.0, h[:, c:2 * c] * window)
    hp_ref[...] = (h_f + h_b).astype(BF16)
    hm_ref[...] = (h_f - h_b).astype(BF16)


def hyena_filter_pair(feats, w1p, b1, w2, b2, w3, b3, w4, fq):
    seq_len = feats.shape[0]
    tl = min(512, seq_len)
    full = lambda a: pl.BlockSpec(a.shape, lambda i: (0,) * a.ndim)
    kern = functools.partial(_filter_kernel, seq_len=seq_len)
    return pl.pallas_call(
        kern,
        grid=(seq_len // tl,),
        in_specs=[pl.BlockSpec((tl, feats.shape[1]), lambda i: (i, 0)),
                  full(w1p), full(b1), full(w2), full(b2), full(w3), full(b3), full(w4), full(fq)],
        out_specs=[pl.BlockSpec((tl, HYENA_CH), lambda i: (i, 0))] * 2,
        out_shape=[jax.ShapeDtypeStruct((seq_len, HYENA_CH), BF16)] * 2,
        compiler_params=_cparams(("parallel",)),
        name="hyena_filter",
    )(feats, w1p, b1, w2, b2, w3, b3, w4, fq)


def _dft_fwd_kernel(c_ref, s_ref, u_ref, a_ref, b_ref):
    u = u_ref[...]
    a_ref[...] = jnp.dot(c_ref[...], u, preferred_element_type=F32)
    b_ref[...] = jnp.dot(s_ref[...], u, preferred_element_type=F32)


def dft_forward(cs, ss, u_ext):
    n, cols = u_ext.shape
    tf, tn = min(512, n), 1024
    return pl.pallas_call(
        _dft_fwd_kernel,
        grid=(cols // tn, n // tf),
        in_specs=[pl.BlockSpec((tf, n), lambda c, f: (f, 0)),
                  pl.BlockSpec((tf, n), lambda c, f: (f, 0)),
                  pl.BlockSpec((n, tn), lambda c, f: (0, c))],
        out_specs=[pl.BlockSpec((tf, tn), lambda c, f: (f, c))] * 2,
        out_shape=[jax.ShapeDtypeStruct((n, cols), F32)] * 2,
        compiler_params=_cparams(("parallel", "parallel")),
        name="dft_forward",
    )(cs, ss, u_ext)


def _spectral_mul_kernel(au_ref, bu_ref, ap_ref, bp_ref, am_ref, bm_ref, hc_ref, hs_ref,
                         yre_ref, yim_ref, *, scale):
    hc, hs = hc_ref[...], hs_ref[...]
    k_re = hc * ap_ref[...] + hs * bp_ref[...]
    k_im = hs * am_ref[...] - hc * bm_ref[...]
    a, b = au_ref[...], bu_ref[...]
    yre_ref[...] = ((a * k_re + b * k_im) * scale).astype(BF16)
    yim_ref[...] = ((a * k_im - b * k_re) * scale).astype(BF16)


def spectral_mul(a, b, half_cos, half_sin, *, batch):
    n = a.shape[0]
    c = HYENA_CH
    tf = min(512, n)
    kern = functools.partial(_spectral_mul_kernel, scale=1.0 / n)
    ucol = lambda f, bi: (f, bi)
    return pl.pallas_call(
        kern,
        grid=(n // tf, batch),
        in_specs=[pl.BlockSpec((tf, c), ucol), pl.BlockSpec((tf, c), ucol),
                  pl.BlockSpec((tf, c), lambda f, bi: (f, batch)),
                  pl.BlockSpec((tf, c), lambda f, bi: (f, batch)),
                  pl.BlockSpec((tf, c), lambda f, bi: (f, batch + 1)),
                  pl.BlockSpec((tf, c), lambda f, bi: (f, batch + 1)),
                  pl.BlockSpec((tf, 1), lambda f, bi: (f, 0)),
                  pl.BlockSpec((tf, 1), lambda f, bi: (f, 0))],
        out_specs=[pl.BlockSpec((tf, c), ucol)] * 2,
        out_shape=[jax.ShapeDtypeStruct((n, batch * c), BF16)] * 2,
        compiler_params=_cparams(("parallel", "parallel")),
        name="hyena_spectral_mul",
    )(a, b, a, b, a, b, half_cos, half_sin)


def _dft_inv_kernel(c_ref, s_ref, yre_ref, yim_ref, u_ref, x0_ref, bias_ref, o_ref):
    y = (jnp.dot(c_ref[...], yre_ref[...], preferred_element_type=F32)
         - jnp.dot(s_ref[...], yim_ref[...], preferred_element_type=F32))
    u = u_ref[...].astype(F32)
    o_ref[...] = (x0_ref[...].astype(F32) * (y + u * bias_ref[...])).astype(BF16)


def dft_inverse(cs, ss, yre, yim, u, x0, bias, *, batch):
    n = cs.shape[0]
    c = HYENA_CH
    tt = min(512, n)
    col = lambda bi, t: (0, bi)
    tile = lambda bi, t: (t, bi)
    return pl.pallas_call(
        _dft_inv_kernel,
        grid=(batch, n // tt),
        in_specs=[pl.BlockSpec((tt, n), lambda bi, t: (t, 0)),
                  pl.BlockSpec((tt, n), lambda bi, t: (t, 0)),
                  pl.BlockSpec((n, c), col), pl.BlockSpec((n, c), col),
                  pl.BlockSpec((tt, c), tile), pl.BlockSpec((tt, c), tile),
                  pl.BlockSpec((1, c), lambda bi, t: (0, 0))],
        out_specs=pl.BlockSpec((tt, c), lambda bi, t: (bi * (n // tt) + t, 0)),
        out_shape=jax.ShapeDtypeStruct((batch * n, c), BF16),
        compiler_params=_cparams(("parallel", "parallel")),
        name="dft_inverse",
    )(cs, ss, yre, yim, u, x0, bias)


def _shifted_dft_tables(n):
    f = lax.broadcasted_iota(jnp.int32, (n, n), 0)
    s = lax.broadcasted_iota(jnp.int32, (n, n), 1)
    m = ((2 * f + 1) * (2 * s + 1)) % (8 * n)
    ang = m.astype(F32) * (2.0 * math.pi / (8 * n))
    half = (2 * jnp.arange(n, dtype=jnp.int32) + 1).astype(F32)[:, None] * (math.pi / (4 * n))
    return jnp.cos(ang).astype(BF16), jnp.sin(ang).astype(BF16), jnp.cos(half), jnp.sin(half)


def _filter_features(seq_len):
    t = jnp.linspace(0.0, 1.0, seq_len, dtype=F32)[:, None]
    bands = (HYENA_EMB - 1) // 2
    ang = 2.0 * math.pi * jnp.arange(seq_len, dtype=F32)[:, None] / seq_len
    fr = jnp.linspace(1e-4, bands - 1, bands, dtype=F32)[None, :]
    feats = jnp.concatenate([t, jnp.cos(fr * ang), -jnp.sin(fr * ang)], axis=-1)
    return jnp.pad(feats, ((0, 0), (0, LANES - HYENA_EMB)))


def hyena_long_conv(proj, hp, *, row0, batch, seq_len, z_blk):
    u, x0 = hyena_shortconv(proj, hp["short_w"], hp["short_b"], row0=row0, batch=batch,
                            seq_len=seq_len, z_blk=z_blk)
    f_p, f_m = hyena_filter_pair(_filter_features(seq_len), hp["w1"], hp["b1"], hp["w2"], hp["b2"],
                                 hp["w3"], hp["b3"], hp["w4"], hp["freq"])
    cs, ss, half_cos, half_sin = _shifted_dft_tables(seq_len)
    a, b = dft_forward(cs, ss, jnp.concatenate([u, f_p, f_m], axis=1))
    yre, yim = spectral_mul(a, b, half_cos, half_sin, batch=batch)
    return dft_inverse(cs, ss, yre, yim, u, x0, hp["bias"], batch=batch)


def _outproj_kernel(ma_ref, mb_ref, x_ref, mod_ref, nw_ref, w_ref, rw_ref,
                    xo_ref, h_ref, lt_ref):
    half = ma_ref.shape[1]
    y = (jnp.dot(ma_ref[...], w_ref[0:half], preferred_element_type=F32)
         + jnp.dot(mb_ref[...], w_ref[half:2 * half], preferred_element_type=F32))
    mod = mod_ref[0]
    x = x_ref[...] + mod[2:3] * y
    xo_ref[...] = x
    h = _modulated_norm(x, mod, nw_ref[...], 3, 4)
    h_ref[...] = h.astype(BF16)
    lt_ref[...] = lax.dot_general(rw_ref[...], h, (((1,), (1,)), ((), ())),
                                  preferred_element_type=F32, precision=HIGHEST)


def outproj(mix_a, a_blk, mix_b, b_blk, x_all, mods, nw, w_out, router_wt, *,
            n_rows, t_lat, seq, tm=512):
    d = x_all.shape[1]
    half = d // 2
    return pl.pallas_call(
        _outproj_kernel,
        grid=(n_rows // tm,),
        in_specs=[pl.BlockSpec((tm, half), lambda i: (i, a_blk)),
                  pl.BlockSpec((tm, half), lambda i: (i, b_blk)),
                  pl.BlockSpec((tm, d), lambda i: (i, 0)),
                  pl.BlockSpec((1, 6, d), lambda i: (_seg_of_tile(i, tm, t_lat, seq), 0, 0)),
                  pl.BlockSpec((1, d), lambda i: (0, 0)),
                  pl.BlockSpec((d, d), lambda i: (0, 0)),
                  pl.BlockSpec((N_EXPERTS, d), lambda i: (0, 0))],
        out_specs=[pl.BlockSpec((tm, d), lambda i: (i, 0)),
                   pl.BlockSpec((tm, d), lambda i: (i, 0)),
                   pl.BlockSpec((N_EXPERTS, tm), lambda i: (0, i))],
        out_shape=[jax.ShapeDtypeStruct((n_rows, d), F32),
                   jax.ShapeDtypeStruct((n_rows, d), BF16),
                   jax.ShapeDtypeStruct((N_EXPERTS, n_rows), F32)],
        compiler_params=_cparams(("parallel",)),
        name="outproj",
    )(mix_a, mix_b, x_all, mods, nw, w_out, router_wt)


def _first_max(x, idx, sentinel):
    m = jnp.max(x, axis=0, keepdims=True)
    first = jnp.min(jnp.where(x == m, idx, sentinel), axis=0, keepdims=True)
    return m, idx == first


def _router_kernel(lt_ref, bias_ref, eye_ref, before_ref, below_ref, lists_ref, tok_ref, pc_ref):
    tn = lt_ref.shape[1]
    scores = jax.nn.sigmoid(lt_ref[...])
    sel = scores + bias_ref[...]
    neg = -jnp.inf
    in_grp = lax.broadcasted_iota(jnp.int32, (GROUP_SIZE, tn), 0)
    gscore = []
    for g in range(N_GROUPS):
        x = sel[g * GROUP_SIZE:(g + 1) * GROUP_SIZE]
        m1, hit = _first_max(x, in_grp, GROUP_SIZE)
        gscore.append(m1 + jnp.max(jnp.where(hit, neg, x), axis=0, keepdims=True))
    rows = []
    for g in range(N_GROUPS):
        beaten = jnp.zeros((1, tn), jnp.int32)
        for o in range(N_GROUPS):
            if o != g:
                wins = (gscore[o] >= gscore[g]) if o < g else (gscore[o] > gscore[g])
                beaten = beaten + wins.astype(jnp.int32)
        keep = jnp.broadcast_to(beaten < TOPK_GROUPS, (GROUP_SIZE, tn))
        rows.append(jnp.where(keep, sel[g * GROUP_SIZE:(g + 1) * GROUP_SIZE], neg))
    cand = jnp.concatenate(rows, axis=0)
    eidx = lax.broadcasted_iota(jnp.int32, cand.shape, 0)
    chosen = jnp.zeros(cand.shape, jnp.bool_)
    hits = []
    for _ in range(TOP_K):
        _, hit = _first_max(cand, eidx, N_EXPERTS)
        hits.append(hit)
        chosen = jnp.logical_or(chosen, hit)
        cand = jnp.where(hit, neg, cand)
    w = jnp.where(chosen, scores, 0.0)
    gates_t = w / jnp.sum(w, axis=0, keepdims=True) * ROUTED_SCALE
    chosen_f = jnp.where(chosen, 1.0, 0.0)
    rank = jnp.dot(chosen_f.astype(BF16), before_ref[...], preferred_element_type=F32)
    count = jnp.sum(chosen_f, axis=1, keepdims=True)
    pc = jnp.floor((count + (ROW_UNIT - 1)) * (1.0 / ROW_UNIT)) * ROW_UNIT
    pc_lanes = jnp.broadcast_to(pc, (N_EXPERTS, LANES))
    start = jnp.dot(below_ref[...], pc_lanes, preferred_element_type=F32,
                    precision=HIGHEST)[:, 0:1]
    slot = start + rank
    pick = lambda hit, v: jnp.sum(jnp.where(hit, v, 0.0), axis=0, keepdims=True)
    lists = jnp.concatenate([pick(h, slot) for h in hits] + [pick(h, gates_t) for h in hits], axis=0)
    lists_ref[...] = lists
    pc_ref[0] = pc_lanes
    padded = jnp.concatenate([lists, jnp.zeros((LANES - 2 * TOP_K, tn), F32)], axis=0)
    tok_ref[...] = lax.dot_general(eye_ref[...], padded, (((1,), (1,)), ((), ())),
                                   preferred_element_type=F32, precision=HIGHEST)


def router(logits_t, router_bias):
    e, t = logits_t.shape
    tn = MOE_TILE
    n_tiles = t // tn
    eye = jnp.eye(tn, dtype=F32)
    tok_i = jnp.arange(tn, dtype=jnp.int32)
    before = (tok_i[:, None] < tok_i[None, :]).astype(BF16)
    exp_i = jnp.arange(e, dtype=jnp.int32)
    below = (exp_i[None, :] < exp_i[:, None]).astype(F32)
    const = lambda shape: pl.BlockSpec(shape, lambda i: (0,) * len(shape))
    return pl.pallas_call(
        _router_kernel,
        grid=(n_tiles,),
        in_specs=[pl.BlockSpec((e, tn), lambda i: (0, i)),
                  const((e, 1)), const((tn, tn)), const((tn, tn)), const((e, e))],
        out_specs=[pl.BlockSpec((2 * TOP_K, tn), lambda i: (0, i)),
                   pl.BlockSpec((tn, LANES), lambda i: (i, 0)),
                   pl.BlockSpec((1, e, LANES), lambda i: (i, 0, 0))],
        out_shape=[jax.ShapeDtypeStruct((2 * TOP_K, t), F32),
                   jax.ShapeDtypeStruct((t, LANES), F32),
                   jax.ShapeDtypeStruct((n_tiles, e, LANES), F32)],
        compiler_params=_cparams(("parallel",)),
        name="router",
    )(logits_t, router_bias.reshape(e, 1), eye, before, below)


def moe_plan(pc, n_tiles):
    e = pc.shape[1]
    assert n_tiles >= e
    cap_rows = _sorted_capacity(n_tiles)
    run_end = jnp.cumsum(pc, axis=1)
    run_start = run_end - pc
    total = jnp.sum(pc, axis=0)
    total_al = ((total + EXPERT_ROWS - 1) // EXPERT_ROWS) * EXPERT_ROWS
    range_end = jnp.cumsum(total_al)
    range_start = range_end - total_al
    run_row = range_start[None, :] + jnp.cumsum(pc, axis=0) - pc
    unit_row = jnp.arange(N_UNITS, dtype=jnp.int32) * ROW_UNIT
    owner = jnp.sum((run_end[:, None, :] <= unit_row[None, :, None]).astype(jnp.int32), axis=2)
    owner_hot = (owner[:, :, None] == jnp.arange(e, dtype=jnp.int32)).astype(jnp.int32)
    dst = jnp.sum(owner_hot * (run_row - run_start)[:, None, :], axis=2) + unit_row[None, :]
    used = owner < e
    dump = cap_rows + (jnp.arange(n_tiles, dtype=jnp.int32) % 2)[:, None] * DUMP_ROWS
    unit_dst = jnp.where(used, dst, dump + unit_row[None, :]).astype(jnp.int32)
    unit_src = jnp.where(used, dst, 0).astype(jnp.int32)
    pad_off = jnp.arange(PAD_UNITS, dtype=jnp.int32)[None, :] * ROW_UNIT
    pad_row = (range_start + total)[:, None] + pad_off
    pad_row = jnp.concatenate([jnp.where(pad_row < range_end[:, None], pad_row, -1),
                               jnp.full((n_tiles - e, PAD_UNITS), -1, jnp.int32)], axis=0)
    pad_dst = jnp.where(pad_row >= 0, pad_row, dump + SLOTS + pad_off).astype(jnp.int32)
    tile_end = range_end // EXPERT_ROWS
    tile_expert = jnp.sum((tile_end[None, :] <= jnp.arange(cap_rows // EXPERT_ROWS,
                                                           dtype=jnp.int32)[:, None])
                          .astype(jnp.int32), axis=1)
    tile_expert = jnp.minimum(tile_expert, e - 1).astype(jnp.int32)
    return unit_dst, unit_src, pad_dst, tile_expert, tile_end[-1:].astype(jnp.int32)


def _sorted_capacity(n_tiles):
    worst_rows = n_tiles * (MOE_TILE * TOP_K + N_EXPERTS * (ROW_UNIT - 1)) + N_EXPERTS * (EXPERT_ROWS - 1)
    return -(-worst_rows // EXPERT_ROWS) * EXPERT_ROWS


DUMP_ROWS = SLOTS + PAD_UNITS * ROW_UNIT


def _dispatch_kernel(unit_ref, pad_ref, h_ref, lists_ref, xs_ref, loc_ref, sems):
    i = pl.program_id(0)
    last = pl.num_programs(0) - 1
    slot = i % 2
    tn = h_ref.shape[0]
    h = h_ref[...]
    for c0 in range(0, SLOTS, SLOT_CHUNK):
        srow = (lax.broadcasted_iota(jnp.int32, (SLOT_CHUNK, tn), 0) + c0).astype(F32)
        sel = srow == lists_ref[0:1, :]
        for k in range(1, TOP_K):
            sel = jnp.logical_or(sel, srow == lists_ref[k:k + 1, :])
        loc_ref[slot, c0:c0 + SLOT_CHUNK, :] = jnp.dot(jnp.where(sel, 1.0, 0.0).astype(BF16), h,
                                                       preferred_element_type=F32).astype(BF16)
    for u in range(N_UNITS):
        row = pl.multiple_of(unit_ref[i, u], ROW_UNIT)
        pltpu.make_async_copy(loc_ref.at[slot, pl.ds(u * ROW_UNIT, ROW_UNIT)],
                              xs_ref.at[pl.ds(row, ROW_UNIT)], sems.at[slot]).start()
    for p in range(PAD_UNITS):
        row = pl.multiple_of(pad_ref[i, p], ROW_UNIT)
        pltpu.make_async_copy(loc_ref.at[slot, pl.ds(SLOTS - ROW_UNIT, ROW_UNIT)],
                              xs_ref.at[pl.ds(row, ROW_UNIT)], sems.at[slot]).start()

    def wait_all(s):
        pltpu.make_async_copy(loc_ref.at[s], xs_ref.at[pl.ds(0, SLOTS)], sems.at[s]).wait()
        pltpu.make_async_copy(loc_ref.at[s, pl.ds(0, PAD_UNITS * ROW_UNIT)],
                              xs_ref.at[pl.ds(0, PAD_UNITS * ROW_UNIT)], sems.at[s]).wait()

    @pl.when(i > 0)
    def _():
        wait_all(1 - slot)

    @pl.when(i == last)
    def _():
        wait_all(slot)


def moe_dispatch(h, lists, unit_dst, pad_dst, *, n_tiles):
    d = h.shape[1]
    cap_rows = _sorted_capacity(n_tiles) + 2 * DUMP_ROWS
    return pl.pallas_call(
        _dispatch_kernel,
        grid_spec=pltpu.PrefetchScalarGridSpec(
            num_scalar_prefetch=2,
            grid=(n_tiles,),
            in_specs=[pl.BlockSpec((MOE_TILE, d), lambda i, *_: (i, 0)),
                      pl.BlockSpec((2 * TOP_K, MOE_TILE), lambda i, *_: (0, i))],
            out_specs=pl.BlockSpec(memory_space=pl.ANY),
            scratch_shapes=[pltpu.VMEM((2, SLOTS, d), BF16), pltpu.SemaphoreType.DMA((2,))]),
        out_shape=jax.ShapeDtypeStruct((cap_rows, d), BF16),
        compiler_params=_cparams(("arbitrary",)),
        name="moe_dispatch",
    )(unit_dst, pad_dst, h, lists)


def _expert_kernel(te_ref, nt_ref, x_ref, wg_ref, wu_ref, wd_ref, y_ref, wg_bf, wu_bf, wd_bf):
    t = pl.program_id(0)
    live = t < nt_ref[0]
    tc = jnp.minimum(t, jnp.maximum(nt_ref[0] - 1, 0))
    new_expert = jnp.logical_or(t == 0, te_ref[tc] != te_ref[jnp.maximum(tc - 1, 0)])

    @pl.when(jnp.logical_and(live, new_expert))
    def _():
        wg_bf[...] = wg_ref[0, 0].astype(BF16)
        wu_bf[...] = wu_ref[0, 0].astype(BF16)
        wd_bf[...] = wd_ref[0, 0].astype(BF16)

    @pl.when(live)
    def _():
        x = x_ref[...]
        act = (_silu(jnp.dot(x, wg_bf[...], preferred_element_type=F32))
               * jnp.dot(x, wu_bf[...], preferred_element_type=F32))
        y_ref[...] = jnp.dot(act.astype(BF16), wd_bf[...], preferred_element_type=F32).astype(BF16)


def moe_experts(xs, w_gate, w_up, w_down, layer, tile_expert, n_row_tiles):
    d = xs.shape[1]
    cap_rows = xs.shape[0] - 2 * DUMP_ROWS
    ff = w_gate.shape[3]
    last_live = lambda nt: jnp.maximum(nt[0] - 1, 0)
    row_map = lambda t, te, nt: (jnp.minimum(t, last_live(nt)), 0)
    w_map = lambda t, te, nt: (layer, te[jnp.minimum(t, last_live(nt))], 0, 0)
    return pl.pallas_call(
        _expert_kernel,
        grid_spec=pltpu.PrefetchScalarGridSpec(
            num_scalar_prefetch=2,
            grid=(cap_rows // EXPERT_ROWS,),
            in_specs=[pl.BlockSpec((EXPERT_ROWS, d), row_map),
                      pl.BlockSpec((1, 1, d, ff), w_map), pl.BlockSpec((1, 1, d, ff), w_map),
                      pl.BlockSpec((1, 1, ff, d), w_map)],
            out_specs=pl.BlockSpec((EXPERT_ROWS, d), row_map),
            scratch_shapes=[pltpu.VMEM((d, ff), BF16), pltpu.VMEM((d, ff), BF16),
                            pltpu.VMEM((ff, d), BF16)]),
        out_shape=jax.ShapeDtypeStruct((cap_rows, d), BF16),
        compiler_params=_cparams(("arbitrary",)),
        name="moe_experts",
    )(tile_expert, n_row_tiles, xs, w_gate, w_up, w_down)


def _combine_kernel(unit_ref, ys_ref, tok_ref, h_ref, x_ref, mod_ref, fnw_ref, sg_ref, su_ref,
                    sd_ref, o_ref, loc_ref, sems, *, final_norm):
    i = pl.program_id(0)
    last = pl.num_programs(0) - 1
    slot = i % 2
    tn = h_ref.shape[0]

    def fetch(tile, s):
        for u in range(N_UNITS):
            row = pl.multiple_of(unit_ref[tile, u], ROW_UNIT)
            pltpu.make_async_copy(ys_ref.at[pl.ds(row, ROW_UNIT)],
                                  loc_ref.at[s, pl.ds(u * ROW_UNIT, ROW_UNIT)], sems.at[s]).start()

    @pl.when(i == 0)
    def _():
        fetch(0, 0)

    @pl.when(i < last)
    def _():
        fetch(i + 1, 1 - slot)

    h = h_ref[...]
    act = (_silu(jnp.dot(h, sg_ref[...], preferred_element_type=F32))
           * jnp.dot(h, su_ref[...], preferred_element_type=F32))
    acc = jnp.dot(act.astype(BF16), sd_ref[...], preferred_element_type=F32)
    pltpu.make_async_copy(ys_ref.at[pl.ds(0, SLOTS)], loc_ref.at[slot], sems.at[slot]).wait()
    tok = tok_ref[...]
    for c0 in range(0, SLOTS, SLOT_CHUNK):
        scol = (lax.broadcasted_iota(jnp.int32, (tn, SLOT_CHUNK), 1) + c0).astype(F32)
        q = jnp.zeros((tn, SLOT_CHUNK), F32)
        for k in range(TOP_K):
            q = jnp.where(scol == tok[:, k:k + 1], tok[:, TOP_K + k:TOP_K + k + 1], q)
        acc = acc + jnp.dot(q.astype(BF16), loc_ref[slot, c0:c0 + SLOT_CHUNK, :],
                            preferred_element_type=F32)
    x = x_ref[...] + mod_ref[0][5:6] * acc
    if final_norm:
        x = _rms(x) * fnw_ref[...]
    o_ref[...] = x


def moe_combine(ys, tok, h, x_all, mods, fnw, sh_gate, sh_up, sh_down, unit_dst, *,
                n_tiles, t_lat, seq, final_norm):
    d = h.shape[1]
    tn = MOE_TILE
    const = lambda a: pl.BlockSpec(a.shape, lambda i, *_: (0,) * a.ndim)
    kern = functools.partial(_combine_kernel, final_norm=final_norm)
    return pl.pallas_call(
        kern,
        grid_spec=pltpu.PrefetchScalarGridSpec(
            num_scalar_prefetch=1,
            grid=(n_tiles,),
            in_specs=[pl.BlockSpec(memory_space=pl.ANY),
                      pl.BlockSpec((tn, LANES), lambda i, *_: (i, 0)),
                      pl.BlockSpec((tn, d), lambda i, *_: (i, 0)),
                      pl.BlockSpec((tn, d), lambda i, *_: (i, 0)),
                      pl.BlockSpec((1, 6, d),
                                   lambda i, *_: (_seg_of_tile(i, tn, t_lat, seq), 0, 0)),
                      const(fnw), const(sh_gate), const(sh_up), const(sh_down)],
            out_specs=pl.BlockSpec((tn, d), lambda i, *_: (i, 0)),
            scratch_shapes=[pltpu.VMEM((2, SLOTS, d), BF16), pltpu.SemaphoreType.DMA((2,))]),
        out_shape=jax.ShapeDtypeStruct((n_tiles * tn, d), F32),
        compiler_params=_cparams(("arbitrary",)),
        name="moe_combine",
    )(unit_dst, ys, tok, h, x_all, mods, fnw, sh_gate, sh_up, sh_down)


def moe_block(h2, logits_t, router_bias, x_mid, mods, fnw, w_gate, w_up, w_down,
              sh_gate, sh_up, sh_down, *, layer, t_lat, seq, final_norm):
    n_tiles = h2.shape[0] // MOE_TILE
    lists, tok, pc = router(logits_t, router_bias)
    unit_dst, unit_src, pad_dst, tile_expert, n_row_tiles = moe_plan(
        pc[:, :, 0].astype(jnp.int32), n_tiles)
    xs = moe_dispatch(h2, lists, unit_dst, pad_dst, n_tiles=n_tiles)
    ys = moe_experts(xs, w_gate, w_up, w_down, layer, tile_expert, n_row_tiles)
    return moe_combine(ys, tok, h2, x_mid, mods, fnw, sh_gate.astype(BF16), sh_up.astype(BF16),
                       sh_down.astype(BF16), unit_src, n_tiles=n_tiles, t_lat=t_lat, seq=seq,
                       final_norm=final_norm)


def _rope_tables(seq, tm):
    half = RET_DK // 2
    pos = jnp.arange(seq, dtype=jnp.int32)
    row = (pos // GRID_W).astype(F32)
    col = (pos % GRID_W).astype(F32)
    inv = ROPE_BASE ** (-jnp.arange(0, half, 2, dtype=F32) / half)
    a_row, a_col = row[:, None] * inv[None, :], col[:, None] * inv[None, :]
    cos_h = jnp.concatenate([jnp.cos(a_row)] * 2 + [jnp.cos(a_col)] * 2, axis=-1)
    sin_h = jnp.concatenate([-jnp.sin(a_row), jnp.sin(a_row), -jnp.sin(a_col), jnp.sin(a_col)], -1)
    reps = 2 * RET_HEADS
    cos_t = jnp.concatenate([jnp.tile(cos_h, (1, reps)), jnp.ones((tm, reps * RET_DK), F32)], 0)
    sin_t = jnp.concatenate([jnp.tile(sin_h, (1, reps)), jnp.zeros((tm, reps * RET_DK), F32)], 0)
    return cos_t, sin_t


def _rope_partner_columns():
    quarter = RET_DK // 4
    idx = jnp.arange(2 * RET_HEADS * RET_DK, dtype=jnp.int32)
    within = idx % (2 * quarter)
    return jnp.where(within < quarter, idx + quarter, idx - quarter)


def kernel(x, c, ctx, c_ctx, ada_w, ada_b, norm1_w, norm2_w, ev_w_in, ev_short_w, ev_short_b, ev_filt_w1, ev_filt_b1, ev_filt_w2, ev_filt_b2, ev_filt_w3, ev_filt_b3, ev_filt_w4, ev_filt_freq, ev_hyena_bias, ev_w_out, od_w_in, od_gate_w1_f, od_gate_w2_f, od_gate_b_f, od_gate_w1_b, od_gate_w2_b, od_gate_b_b, od_norm_w, od_w_out, router_w, router_bias, exp_w_gate, exp_w_up, exp_w_down, sh_w_gate, sh_w_up, sh_w_down, final_norm_w):
    batch, seq, d = x.shape
    ctx_len = ctx.shape[1]
    depth = ada_w.shape[0]
    t_lat, t_ctx = batch * seq, batch * ctx_len
    t_all = t_lat + t_ctx
    tm = 512

    x_all = jnp.concatenate([x.reshape(t_lat, d), ctx.reshape(t_ctx, d)], axis=0)
    cond8 = jnp.concatenate([c_ctx[None, :], c, jnp.zeros((8 - 1 - batch, d), F32)], axis=0)
    mods_all = adaln_rows(cond8, ada_w, ada_b).reshape(depth, 8, 6, d)

    for i in range(depth):
        last = i == depth - 1
        j = i // 2
        mods = mods_all[i]
        nw1, nw2 = norm1_w[i][None, :], norm2_w[i][None, :]
        if i % 2 == 0:
            qk = 2 * RET_HEADS * RET_DK
            k_scale = jnp.concatenate([jnp.ones((qk // 2,), F32),
                                       jnp.full((qk // 2,), RET_DK ** -0.5, F32)])
            w_in = ev_w_in[j]
            w_qk = w_in[:, :qk] * k_scale
            w_ext = jnp.concatenate([w_qk, w_in[:, qk:], w_qk[:, _rope_partner_columns()]],
                                    axis=1).astype(BF16)
            cos_t, sin_t = _rope_tables(seq, tm)
            proj = inproj_even(x_all, mods, nw1, w_ext, cos_t, sin_t, t_lat=t_lat, seq=seq, tm=tm)
            log_g = [math.log1p(-2.0 ** (-5.0 - h)) for h in range(RET_HEADS)]
            mix_a = bidir_scan(proj, None, None, jnp.ones((1, RET_DV), F32),
                               batch=batch, seq=seq, ctx_len=ctx_len, heads=RET_HEADS, dk=RET_DK,
                               dv=RET_DV, chunk=RET_CHUNK, q_blk=0, k_blk=1, v_blk=1, g_blk=2,
                               log_decay_f=log_g, log_decay_b=log_g[::-1])
            hp = dict(short_w=ev_short_w[j], short_b=ev_short_b[j][None, :],
                      w1=jnp.pad(ev_filt_w1[j], ((0, LANES - HYENA_EMB), (0, 0))),
                      b1=ev_filt_b1[j][None, :], w2=ev_filt_w2[j], b2=ev_filt_b2[j][None, :],
                      w3=ev_filt_w3[j], b3=ev_filt_b3[j][None, :], w4=ev_filt_w4[j],
                      freq=ev_filt_freq[j][None, :], bias=ev_hyena_bias[j][None, :])
            hy_lat = hyena_long_conv(proj, hp, row0=0, batch=batch, seq_len=seq, z_blk=1)
            hy_ctx = hyena_long_conv(proj, hp, row0=t_lat, batch=batch, seq_len=ctx_len, z_blk=1)
            mix_b = jnp.concatenate([hy_lat, hy_ctx], axis=0)
            a_blk, b_blk = 0, 0
            w_out = ev_w_out[j].astype(BF16)
        else:
            kd = GLA_HEADS * GLA_DK
            pad_cols = LANES - 2 * GLA_RANK
            w_ext = jnp.concatenate([od_w_in[j], od_gate_w1_f[j], od_gate_w1_b[j],
                                     jnp.zeros((d, pad_cols), F32)], axis=1).astype(BF16)
            w2f = jnp.pad(od_gate_w2_f[j], ((0, LANES - GLA_RANK), (0, 0)))
            w2b = jnp.pad(od_gate_w2_b[j], ((GLA_RANK, LANES - 2 * GLA_RANK), (0, 0)))
            proj, la_f, la_b = inproj_odd(x_all, mods, nw1, w_ext, w2f, w2b,
                                          od_gate_b_f[j][None, :], od_gate_b_b[j][None, :],
                                          t_lat=t_lat, seq=seq, tm=tm)
            mix_a = bidir_scan(proj, la_f, la_b, od_norm_w[j][None, :],
                               batch=batch, seq=seq, ctx_len=ctx_len, heads=GLA_HEADS, dk=GLA_DK,
                               dv=GLA_DV, chunk=GLA_CHUNK, q_blk=0, k_blk=1, v_blk=1, g_blk=2)
            mix_b = mix_a
            a_blk, b_blk = 0, 1
            w_out = od_w_out[j].astype(BF16)

        n_rows = t_lat if last else t_all
        x_mid, h2, logits_t = outproj(mix_a, a_blk, mix_b, b_blk, x_all, mods, nw2, w_out,
                                      router_w[i].T, n_rows=n_rows, t_lat=t_lat, seq=seq, tm=tm)
        x_all = moe_block(h2, logits_t, router_bias[i], x_mid, mods, final_norm_w[None, :],
                          exp_w_gate, exp_w_up, exp_w_down,
                          sh_w_gate[i], sh_w_up[i], sh_w_down[i],
                          layer=i, t_lat=t_lat, seq=seq, final_norm=last)
    return x_all[:t_lat].reshape(batch, seq, d)
```

```python
import functools
import math

import jax
import jax.numpy as jnp
from jax import lax
from jax.experimental import pallas as pl
from jax.experimental.pallas import tpu as pltpu

F32 = jnp.float32
BF16 = jnp.bfloat16
HIGHEST = lax.Precision.HIGHEST

D_MODEL = 1024
GRID_W = 64
NORM_EPS = 1e-6
RET_HEADS, RET_DK, RET_DV, RET_CHUNK = 4, 64, 128, 128
ROPE_BASE = 10000.0
HYENA_CH, HYENA_EMB = 512, 33
HYENA_FAST_DECAY, HYENA_SLOW_DECAY, HYENA_TARGET = 0.3, 1.5, 1e-2
GLA_HEADS, GLA_DK, GLA_DV, GLA_RANK, GLA_TAU, GLA_CHUNK = 4, 128, 256, 16, 16.0, 64
N_EXPERTS, TOP_K, N_GROUPS, TOPK_GROUPS = 64, 8, 8, 4
GROUP_SIZE = N_EXPERTS // N_GROUPS
EXPERT_FF = 256
ROUTED_SCALE = 2.5

LANES = 128
SCAN_ROWS = 256
MOE_TILE = 256
ROW_UNIT = 16
EXPERT_ROWS = 512
SLOT_CHUNK = 512
SLOTS = -(-(MOE_TILE * TOP_K + N_EXPERTS * (ROW_UNIT - 1) + ROW_UNIT) // SLOT_CHUNK) * SLOT_CHUNK
N_UNITS = SLOTS // ROW_UNIT
PAD_UNITS = EXPERT_ROWS // ROW_UNIT
VMEM_LIMIT = 56 * 1024 * 1024


def _cparams(sem):
    return pltpu.CompilerParams(dimension_semantics=sem, vmem_limit_bytes=VMEM_LIMIT)


def _rms(x):
    return x * lax.rsqrt(jnp.mean(x * x, axis=-1, keepdims=True) + NORM_EPS)


def _silu(x):
    return x * jax.nn.sigmoid(x)


def _adaln_kernel(c_ref, w_ref, b_ref, o_ref):
    c = c_ref[...]
    o_ref[0] = jnp.dot(_silu(c), w_ref[0], preferred_element_type=F32,
                       precision=HIGHEST) + b_ref[0]


def adaln_rows(cond8, ada_w, ada_b):
    depth, d, n = ada_w.shape
    tn = 512
    return pl.pallas_call(
        _adaln_kernel,
        grid=(depth, n // tn),
        in_specs=[pl.BlockSpec((8, d), lambda l, j: (0, 0)),
                  pl.BlockSpec((1, d, tn), lambda l, j: (l, 0, j)),
                  pl.BlockSpec((1, 1, tn), lambda l, j: (l, 0, j))],
        out_specs=pl.BlockSpec((1, 8, tn), lambda l, j: (l, 0, j)),
        out_shape=jax.ShapeDtypeStruct((depth, 8, n), F32),
        compiler_params=_cparams(("parallel", "parallel")),
        name="adaln_rows",
    )(cond8, ada_w, ada_b.reshape(depth, 1, n))


def _modulated_norm(x, mod, nw, shift_row, scale_row):
    return _rms(x) * nw * (1.0 + mod[scale_row:scale_row + 1]) + mod[shift_row:shift_row + 1]


def _inproj_even_kernel(x_ref, mod_ref, nw_ref, w_ref, cos_ref, sin_ref, o_ref):
    h = _modulated_norm(x_ref[...], mod_ref[0], nw_ref[...], 0, 1).astype(BF16)
    n_main = o_ref.shape[1]
    qk = jnp.dot(h, w_ref[:, 0:512], preferred_element_type=F32)
    qk_sw = jnp.dot(h, w_ref[:, n_main:n_main + 512], preferred_element_type=F32)
    o_ref[:, 0:512] = (qk * cos_ref[...] + qk_sw * sin_ref[...]).astype(BF16)
    for c0 in range(512, n_main, 512):
        o_ref[:, c0:c0 + 512] = jnp.dot(h, w_ref[:, c0:c0 + 512],
                                        preferred_element_type=F32).astype(BF16)


def _inproj_odd_kernel(x_ref, mod_ref, nw_ref, w_ref, w2f_ref, w2b_ref, bf_ref, bb_ref,
                       o_ref, laf_ref, lab_ref):
    h = _modulated_norm(x_ref[...], mod_ref[0], nw_ref[...], 0, 1).astype(BF16)
    n_main = o_ref.shape[1]
    kd = GLA_HEADS * GLA_DK
    q = jnp.dot(h, w_ref[:, 0:kd], preferred_element_type=F32)
    o_ref[:, 0:kd] = (q * (GLA_DK ** -0.5)).astype(BF16)
    for c0 in range(kd, n_main, 512):
        o_ref[:, c0:c0 + 512] = jnp.dot(h, w_ref[:, c0:c0 + 512],
                                        preferred_element_type=F32).astype(BF16)
    low = jnp.dot(h, w_ref[:, n_main:n_main + LANES], preferred_element_type=F32)

    def log_gate(w2_ref, b_ref):
        z = jnp.dot(low, w2_ref[...], preferred_element_type=F32, precision=HIGHEST) + b_ref[...]
        return (jnp.minimum(z, 0.0) - jnp.log(1.0 + jnp.exp(-jnp.abs(z)))) * (1.0 / GLA_TAU)

    laf_ref[...] = log_gate(w2f_ref, bf_ref)
    lab_ref[...] = log_gate(w2b_ref, bb_ref)


def _seg_of_tile(i, tm, t_lat, seq):
    return jnp.where(i < t_lat // tm, 1 + (i * tm) // seq, 0)


def inproj_even(x_all, mods, nw, w_ext, cos_t, sin_t, *, t_lat, seq, tm=512):
    t_all, d = x_all.shape
    n_ext = w_ext.shape[1]
    n_main = n_ext - 512
    n_lat_tiles, pos_tiles = t_lat // tm, seq // tm

    def pos_map(i):
        return (jnp.where(i < n_lat_tiles, i % pos_tiles, pos_tiles), 0)

    return pl.pallas_call(
        _inproj_even_kernel,
        grid=(t_all // tm,),
        in_specs=[pl.BlockSpec((tm, d), lambda i: (i, 0)),
                  pl.BlockSpec((1, 6, d), lambda i: (_seg_of_tile(i, tm, t_lat, seq), 0, 0)),
                  pl.BlockSpec((1, d), lambda i: (0, 0)),
                  pl.BlockSpec((d, n_ext), lambda i: (0, 0)),
                  pl.BlockSpec((tm, 512), pos_map),
                  pl.BlockSpec((tm, 512), pos_map)],
        out_specs=pl.BlockSpec((tm, n_main), lambda i: (i, 0)),
        out_shape=jax.ShapeDtypeStruct((t_all, n_main), BF16),
        compiler_params=_cparams(("parallel",)),
        name="inproj_even",
    )(x_all, mods, nw, w_ext, cos_t, sin_t)


def inproj_odd(x_all, mods, nw, w_ext, w2f, w2b, bf, bb, *, t_lat, seq, tm=512):
    t_all, d = x_all.shape
    n_main = w_ext.shape[1] - LANES
    kd = GLA_HEADS * GLA_DK
    full = lambda shape: pl.BlockSpec(shape, lambda i: (0,) * len(shape))
    return pl.pallas_call(
        _inproj_odd_kernel,
        grid=(t_all // tm,),
        in_specs=[pl.BlockSpec((tm, d), lambda i: (i, 0)),
                  pl.BlockSpec((1, 6, d), lambda i: (_seg_of_tile(i, tm, t_lat, seq), 0, 0)),
                  full((1, d)), full(w_ext.shape), full(w2f.shape), full(w2b.shape),
                  full((1, kd)), full((1, kd))],
        out_specs=[pl.BlockSpec((tm, n_main), lambda i: (i, 0)),
                   pl.BlockSpec((tm, kd), lambda i: (i, 0)),
                   pl.BlockSpec((tm, kd), lambda i: (i, 0))],
        out_shape=[jax.ShapeDtypeStruct((t_all, n_main), BF16),
                   jax.ShapeDtypeStruct((t_all, kd), F32),
                   jax.ShapeDtypeStruct((t_all, kd), F32)],
        compiler_params=_cparams(("parallel",)),
        name="inproj_odd",
    )(x_all, mods, nw, w_ext, w2f, w2b, bf, bb)


def _scan_kernel(*refs, heads, dk, dv, chunk, gated, reverse, log_decay):
    it = iter(refs)
    q_ref, k_ref, v_ref = next(it), next(it), next(it)
    la_ref = next(it) if gated else None
    if reverse:
        oprev_ref, gate_ref, nw_ref = next(it), next(it), next(it)
    out_ref = next(it)
    state_ref = next(it)
    dec_ref = None if gated else next(it)

    hpg = LANES // dk
    groups = heads // hpg
    rows = q_ref.shape[0]
    n_chunks = rows // chunk
    j = pl.program_id(1)

    row_i = lax.broadcasted_iota(jnp.int32, (rows, rows), 0)
    col_i = lax.broadcasted_iota(jnp.int32, (rows, rows), 1)
    same_chunk = (row_i // chunk) == (col_i // chunk)
    keep = jnp.logical_and(same_chunk, (col_i >= row_i) if reverse else (row_i >= col_i))

    @pl.when(j == 0)
    def _init():
        state_ref[...] = jnp.zeros_like(state_ref)
        if not gated:
            pos = lax.broadcasted_iota(jnp.int32, (rows, LANES), 0) % chunk
            steps = ((chunk - pos) if reverse else (pos + 1)).astype(F32)
            lane = lax.broadcasted_iota(jnp.int32, (rows, LANES), 1)
            for g in range(groups):
                lg = jnp.zeros((rows, LANES), F32)
                for a in range(hpg):
                    lg = jnp.where(lane // dk == a, log_decay[g * hpg + a], lg)
                logb = steps * lg
                b_end = float(chunk) * lg
                dec_ref[g, 0] = jnp.exp(logb)
                dec_ref[g, 1] = jnp.exp(-logb)
                dec_ref[g, 2] = jnp.exp(b_end - logb)
                dec_ref[g, 3] = jnp.exp(b_end)

    if gated:
        tri = jnp.where(keep, 1.0, 0.0).astype(BF16)
        la = la_ref[...]
        la_hi = la.astype(BF16)
        la_lo = (la - la_hi.astype(F32)).astype(BF16)
        logb_all = (jnp.dot(tri, la_hi, preferred_element_type=F32)
                    + jnp.dot(tri, la_lo, preferred_element_type=F32))

    lane1 = lax.broadcasted_iota(jnp.int32, (1, LANES), 1)
    order = range(n_chunks - 1, -1, -1) if reverse else range(n_chunks)
    end_row = lambda c: c * chunk if reverse else (c + 1) * chunk - 1
    for g in range(groups):
        ksl = slice(g * LANES, (g + 1) * LANES)
        qg = q_ref[:, ksl].astype(F32)
        kg = k_ref[:, ksl].astype(F32)
        if gated:
            logb = logb_all[:, ksl]
            ends = [logb[end_row(c):end_row(c) + 1] for c in range(n_chunks)]
            b_end = jnp.concatenate([jnp.broadcast_to(e, (chunk, LANES)) for e in ends], axis=0)
            e_q, e_k, e_s = jnp.exp(logb), jnp.exp(-logb), jnp.exp(b_end - logb)
            e_e = [jnp.exp(e) for e in ends]
        else:
            e_q, e_k, e_s = dec_ref[g, 0], dec_ref[g, 1], dec_ref[g, 2]
            e_e = [dec_ref[g, 3][0:1]] * n_chunks
        qd = qg * e_q
        kd_ = (kg * e_k).astype(BF16)
        ks = (kg * e_s).astype(BF16)
        for a in range(hpg):
            h = g * hpg + a
            qa = (jnp.where(lane1 // dk == a, qd, 0.0) if hpg > 1 else qd).astype(BF16)
            vh = v_ref[:, h * dv:(h + 1) * dv]
            s = lax.dot_general(qa, kd_, (((1,), (1,)), ((), ())), preferred_element_type=F32)
            o_intra = jnp.dot(jnp.where(keep, s, 0.0).astype(BF16), vh,
                              preferred_element_type=F32)
            st = state_ref[h]
            for c in order:
                rs = slice(c * chunk, (c + 1) * chunk)
                o = o_intra[rs] + lax.dot_general(qa[rs], st.astype(BF16), (((1,), (1,)), ((), ())),
                                                  preferred_element_type=F32)
                st = e_e[c] * st + lax.dot_general(vh[rs], ks[rs], (((0,), (0,)), ((), ())),
                                                   preferred_element_type=F32)
                osl = (rs, slice(h * dv, (h + 1) * dv))
                if reverse:
                    o = _rms(o + oprev_ref[osl]) * nw_ref[...]
                    out_ref[osl] = (o * _silu(gate_ref[osl].astype(F32))).astype(BF16)
                else:
                    out_ref[osl] = o
            state_ref[h] = st


def bidir_scan(proj, la_f, la_b, norm_w, *, batch, seq, ctx_len, heads, dk, dv, chunk,
               q_blk, k_blk, v_blk, g_blk, log_decay_f=None, log_decay_b=None):
    t_all = proj.shape[0]
    gated = la_f is not None
    hk, hv = heads * dk, heads * dv
    rb = SCAN_ROWS
    lat_blocks = seq // rb
    ctx_base = (batch * seq) // rb
    assert ctx_len == rb

    def rows_fwd(b, j):
        return jnp.where(j == 0, ctx_base + b, b * lat_blocks + j - 1)

    def rows_bwd(b, j):
        return jnp.where(j == 0, ctx_base + b, b * lat_blocks + lat_blocks - j)

    outs = None
    for reverse, rows_of, la, ld in ((False, rows_fwd, la_f, log_decay_f),
                                     (True, rows_bwd, la_b, log_decay_b)):
        spec = lambda w, cb: pl.BlockSpec((rb, w), lambda b, j, cb=cb: (rows_of(b, j), cb))
        in_specs = [spec(hk, q_blk), spec(hk, k_blk), spec(hv, v_blk)]
        args = [proj, proj, proj]
        if gated:
            in_specs.append(spec(hk, 0))
            args.append(la)
        if reverse:
            in_specs += [spec(hv, 0), spec(hv, g_blk), pl.BlockSpec((1, dv), lambda b, j: (0, 0))]
            args += [outs, proj, norm_w]
        scratch = [pltpu.VMEM((heads, dv, LANES), F32)]
        if not gated:
            scratch.append(pltpu.VMEM((hk // LANES, 4, rb, LANES), F32))
        kern = functools.partial(_scan_kernel, heads=heads, dk=dk, dv=dv, chunk=chunk,
                                 gated=gated, reverse=reverse, log_decay=ld)
        outs = pl.pallas_call(
            kern,
            grid=(batch, lat_blocks + 1),
            in_specs=in_specs,
            out_specs=spec(hv, 0),
            out_shape=jax.ShapeDtypeStruct((t_all, hv), BF16 if reverse else F32),
            scratch_shapes=scratch,
            compiler_params=_cparams(("parallel", "arbitrary")),
            name="scan_bwd" if reverse else "scan_fwd",
        )(*args)
    return outs


def _shortconv_kernel(z_ref, zp_ref, zn_ref, w_ref, b_ref, u_ref, x0_ref, *, tiles_per_seq):
    i = pl.program_id(0)
    tm = z_ref.shape[0]
    z = z_ref[...].astype(F32)
    first = (i % tiles_per_seq) == 0
    last = (i % tiles_per_seq) == tiles_per_seq - 1
    halo = zp_ref.shape[0]
    prev_row = jnp.where(first, 0.0, zp_ref[halo - 1:halo, :].astype(F32))
    next_row = jnp.where(last, 0.0, zn_ref[0:1, :].astype(F32))
    row = lax.broadcasted_iota(jnp.int32, z.shape, 0)
    z_prev = jnp.where(row == 0, prev_row, pltpu.roll(z, 1, 0))
    z_next = jnp.where(row == tm - 1, next_row, pltpu.roll(z, tm - 1, 0))
    y = w_ref[0:1] * z_prev + w_ref[1:2] * z + w_ref[2:3] * z_next + b_ref[...]
    c = HYENA_CH
    x0_ref[...] = y[:, 0:c].astype(BF16)
    u_ref[...] = (y[:, c:2 * c] * y[:, 2 * c:3 * c]).astype(BF16)


def hyena_shortconv(proj, short_w, short_b, *, row0, batch, seq_len, z_blk, tm=256):
    halo = 16
    tiles_per_seq = seq_len // tm
    n_tiles = batch * tiles_per_seq
    t0, h_per_tile = row0 // tm, tm // halo
    nz = 3 * HYENA_CH
    n_halo_blocks = proj.shape[0] // halo
    out_map = lambda i: (i % tiles_per_seq, i // tiles_per_seq)
    kern = functools.partial(_shortconv_kernel, tiles_per_seq=tiles_per_seq)
    return pl.pallas_call(
        kern,
        grid=(n_tiles,),
        in_specs=[pl.BlockSpec((tm, nz), lambda i: (t0 + i, z_blk)),
                  pl.BlockSpec((halo, nz),
                               lambda i: (jnp.maximum((t0 + i) * h_per_tile - 1, 0), z_blk)),
                  pl.BlockSpec((halo, nz),
                               lambda i: (jnp.minimum((t0 + i + 1) * h_per_tile,
                                                      n_halo_blocks - 1), z_blk)),
                  pl.BlockSpec((3, nz), lambda i: (0, 0)),
                  pl.BlockSpec((1, nz), lambda i: (0, 0))],
        out_specs=[pl.BlockSpec((tm, HYENA_CH), out_map), pl.BlockSpec((tm, HYENA_CH), out_map)],
        out_shape=[jax.ShapeDtypeStruct((seq_len, batch * HYENA_CH), BF16)] * 2,
        compiler_params=_cparams(("parallel",)),
        name="hyena_shortconv",
    )(proj, proj, proj, short_w, short_b)


def _filter_kernel(feat_ref, w1_ref, b1_ref, w2_ref, b2_ref, w3_ref, b3_ref, w4_ref, fq_ref,
                   hp_ref, hm_ref, *, seq_len):
    i = pl.program_id(0)
    tl = feat_ref.shape[0]
    fq = fq_ref[...]
    dot = lambda a, b: jnp.dot(a, b, preferred_element_type=F32, precision=HIGHEST)
    h = jnp.sin(fq * (dot(feat_ref[...], w1_ref[...]) + b1_ref[...]))
    h = jnp.sin(fq * (dot(h, w2_ref[...]) + b2_ref[...]))
    h = jnp.sin(fq * (dot(h, w3_ref[...]) + b3_ref[...]))
    h = dot(h, w4_ref[...])
    c = HYENA_CH
    max_decay = math.log(HYENA_TARGET) / HYENA_FAST_DECAY
    min_decay = math.log(HYENA_TARGET) / HYENA_SLOW_DECAY
    ch = lax.broadcasted_iota(jnp.int32, (tl, c), 1).astype(F32)
    deltas = min_decay + ch * ((max_decay - min_decay) / (c - 1))
    row = lax.broadcasted_iota(jnp.int32, (tl, c), 0) + i * tl
    t = row.astype(F32) * (1.0 / (seq_len - 1))
    window = jnp.exp(-t * jnp.abs(deltas))
    h_f = h[:, 0:c] * window
    h_b = jnp.where(row == 0, 0.0, h[:, c:2 * c] * window)
    hp_ref[...] = (h_f + h_b).astype(BF16)
    hm_ref[...] = (h_f - h_b).astype(BF16)


def hyena_filter_pair(feats, w1p, b1, w2, b2, w3, b3, w4, fq):
    seq_len = feats.shape[0]
    tl = min(512, seq_len)
    full = lambda a: pl.BlockSpec(a.shape, lambda i: (0,) * a.ndim)
    kern = functools.partial(_filter_kernel, seq_len=seq_len)
    return pl.pallas_call(
        kern,
        grid=(seq_len // tl,),
        in_specs=[pl.BlockSpec((tl, feats.shape[1]), lambda i: (i, 0)),
                  full(w1p), full(b1), full(w2), full(b2), full(w3), full(b3), full(w4), full(fq)],
        out_specs=[pl.BlockSpec((tl, HYENA_CH), lambda i: (i, 0))] * 2,
        out_shape=[jax.ShapeDtypeStruct((seq_len, HYENA_CH), BF16)] * 2,
        compiler_params=_cparams(("parallel",)),
        name="hyena_filter",
    )(feats, w1p, b1, w2, b2, w3, b3, w4, fq)


def _dft_fwd_kernel(c_ref, s_ref, u_ref, a_ref, b_ref):
    u = u_ref[...]
    a_ref[...] = jnp.dot(c_ref[...], u, preferred_element_type=F32)
    b_ref[...] = jnp.dot(s_ref[...], u, preferred_element_type=F32)


def dft_forward(cs, ss, u_ext):
    n, cols = u_ext.shape
    tf, tn = min(512, n), 1024
    return pl.pallas_call(
        _dft_fwd_kernel,
        grid=(cols // tn, n // tf),
        in_specs=[pl.BlockSpec((tf, n), lambda c, f: (f, 0)),
                  pl.BlockSpec((tf, n), lambda c, f: (f, 0)),
                  pl.BlockSpec((n, tn), lambda c, f: (0, c))],
        out_specs=[pl.BlockSpec((tf, tn), lambda c, f: (f, c))] * 2,
        out_shape=[jax.ShapeDtypeStruct((n, cols), F32)] * 2,
        compiler_params=_cparams(("parallel", "parallel")),
        name="dft_forward",
    )(cs, ss, u_ext)


def _spectral_mul_kernel(au_ref, bu_ref, ap_ref, bp_ref, am_ref, bm_ref, hc_ref, hs_ref,
                         yre_ref, yim_ref, *, scale):
    hc, hs = hc_ref[...], hs_ref[...]
    k_re = hc * ap_ref[...] + hs * bp_ref[...]
    k_im = hs * am_ref[...] - hc * bm_ref[...]
    a, b = au_ref[...], bu_ref[...]
    yre_ref[...] = ((a * k_re + b * k_im) * scale).astype(BF16)
    yim_ref[...] = ((a * k_im - b * k_re) * scale).astype(BF16)


def spectral_mul(a, b, half_cos, half_sin, *, batch):
    n = a.shape[0]
    c = HYENA_CH
    tf = min(512, n)
    kern = functools.partial(_spectral_mul_kernel, scale=1.0 / n)
    ucol = lambda f, bi: (f, bi)
    return pl.pallas_call(
        kern,
        grid=(n // tf, batch),
        in_specs=[pl.BlockSpec((tf, c), ucol), pl.BlockSpec((tf, c), ucol),
                  pl.BlockSpec((tf, c), lambda f, bi: (f, batch)),
                  pl.BlockSpec((tf, c), lambda f, bi: (f, batch)),
                  pl.BlockSpec((tf, c), lambda f, bi: (f, batch + 1)),
                  pl.BlockSpec((tf, c), lambda f, bi: (f, batch + 1)),
                  pl.BlockSpec((tf, 1), lambda f, bi: (f, 0)),
                  pl.BlockSpec((tf, 1), lambda f, bi: (f, 0))],
        out_specs=[pl.BlockSpec((tf, c), ucol)] * 2,
        out_shape=[jax.ShapeDtypeStruct((n, batch * c), BF16)] * 2,
        compiler_params=_cparams(("parallel", "parallel")),
        name="hyena_spectral_mul",
    )(a, b, a, b, a, b, half_cos, half_sin)


def _dft_inv_kernel(c_ref, s_ref, yre_ref, yim_ref, u_ref, x0_ref, bias_ref, o_ref):
    y = (jnp.dot(c_ref[...], yre_ref[...], preferred_element_type=F32)
         - jnp.dot(s_ref[...], yim_ref[...], preferred_element_type=F32))
    u = u_ref[...].astype(F32)
    o_ref[...] = (x0_ref[...].astype(F32) * (y + u * bias_ref[...])).astype(BF16)


def dft_inverse(cs, ss, yre, yim, u, x0, bias, *, batch):
    n = cs.shape[0]
    c = HYENA_CH
    tt = min(512, n)
    col = lambda bi, t: (0, bi)
    tile = lambda bi, t: (t, bi)
    return pl.pallas_call(
        _dft_inv_kernel,
        grid=(batch, n // tt),
        in_specs=[pl.BlockSpec((tt, n), lambda bi, t: (t, 0)),
                  pl.BlockSpec((tt, n), lambda bi, t: (t, 0)),
                  pl.BlockSpec((n, c), col), pl.BlockSpec((n, c), col),
                  pl.BlockSpec((tt, c), tile), pl.BlockSpec((tt, c), tile),
                  pl.BlockSpec((1, c), lambda bi, t: (0, 0))],
        out_specs=pl.BlockSpec((tt, c), lambda bi, t: (bi * (n // tt) + t, 0)),
        out_shape=jax.ShapeDtypeStruct((batch * n, c), BF16),
        compiler_params=_cparams(("parallel", "parallel")),
        name="dft_inverse",
    )(cs, ss, yre, yim, u, x0, bias)


def _dft_table_kernel(cg_ref, sg_ref, cd_ref, sd_ref, c_ref, s_ref):
    cg, sg, cd, sd = cg_ref[...], sg_ref[...], cd_ref[0], sd_ref[0]
    c_ref[...] = (cg * cd - sg * sd).astype(BF16)
    s_ref[...] = (sg * cd + cg * sd).astype(BF16)


def _shifted_dft_tables(n):
    tf = min(256, n)
    theta = 2.0 * math.pi / (8 * n)
    odd_s = 2 * jnp.arange(n, dtype=jnp.int32)[None, :] + 1
    odd_i = 2 * jnp.arange(tf, dtype=jnp.int32)[:, None] + 1
    gamma = ((odd_i * odd_s) % (8 * n)).astype(F32) * theta
    f0 = jnp.arange(n // tf, dtype=jnp.int32)[:, None] * tf
    delta = ((2 * f0 * odd_s) % (8 * n)).astype(F32)[:, None, :] * theta
    whole = pl.BlockSpec((tf, n), lambda i: (0, 0))
    row = pl.BlockSpec((1, 1, n), lambda i: (i, 0, 0))
    cs, ss = pl.pallas_call(
        _dft_table_kernel,
        grid=(n // tf,),
        in_specs=[whole, whole, row, row],
        out_specs=[pl.BlockSpec((tf, n), lambda i: (i, 0))] * 2,
        out_shape=[jax.ShapeDtypeStruct((n, n), BF16)] * 2,
        compiler_params=_cparams(("parallel",)),
        name="dft_tables",
    )(jnp.cos(gamma), jnp.sin(gamma), jnp.cos(delta), jnp.sin(delta))
    half = (2 * jnp.arange(n, dtype=jnp.int32) + 1).astype(F32)[:, None] * (math.pi / (4 * n))
    return cs, ss, jnp.cos(half), jnp.sin(half)


def _filter_features(seq_len):
    t = jnp.linspace(0.0, 1.0, seq_len, dtype=F32)[:, None]
    bands = (HYENA_EMB - 1) // 2
    ang = 2.0 * math.pi * jnp.arange(seq_len, dtype=F32)[:, None] / seq_len
    fr = jnp.linspace(1e-4, bands - 1, bands, dtype=F32)[None, :]
    feats = jnp.concatenate([t, jnp.cos(fr * ang), -jnp.sin(fr * ang)], axis=-1)
    return jnp.pad(feats, ((0, 0), (0, LANES - HYENA_EMB)))


def hyena_long_conv(proj, hp, *, row0, batch, seq_len, z_blk):
    u, x0 = hyena_shortconv(proj, hp["short_w"], hp["short_b"], row0=row0, batch=batch,
                            seq_len=seq_len, z_blk=z_blk)
    f_p, f_m = hyena_filter_pair(_filter_features(seq_len), hp["w1"], hp["b1"], hp["w2"], hp["b2"],
                                 hp["w3"], hp["b3"], hp["w4"], hp["freq"])
    cs, ss, half_cos, half_sin = _shifted_dft_tables(seq_len)
    a, b = dft_forward(cs, ss, jnp.concatenate([u, f_p, f_m], axis=1))
    yre, yim = spectral_mul(a, b, half_cos, half_sin, batch=batch)
    return dft_inverse(cs, ss, yre, yim, u, x0, hp["bias"], batch=batch)


def _outproj_kernel(ma_ref, mb_ref, x_ref, mod_ref, nw_ref, w_ref, rw_ref,
                    xo_ref, h_ref, lt_ref):
    half = ma_ref.shape[1]
    y = (jnp.dot(ma_ref[...], w_ref[0:half], preferred_element_type=F32)
         + jnp.dot(mb_ref[...], w_ref[half:2 * half], preferred_element_type=F32))
    mod = mod_ref[0]
    x = x_ref[...] + mod[2:3] * y
    xo_ref[...] = x
    h = _modulated_norm(x, mod, nw_ref[...], 3, 4)
    h_ref[...] = h.astype(BF16)
    h_hi = h.astype(BF16)
    h_lo = (h - h_hi.astype(F32)).astype(BF16)
    nt = lambda a, b: lax.dot_general(a, b, (((1,), (1,)), ((), ())), preferred_element_type=F32)
    lt_ref[...] = nt(rw_ref[0], h_hi) + (nt(rw_ref[0], h_lo) + nt(rw_ref[1], h_hi))


def outproj(mix_a, a_blk, mix_b, b_blk, x_all, mods, nw, w_out, router_wt, *,
            n_rows, t_lat, seq, tm=512):
    d = x_all.shape[1]
    half = d // 2
    return pl.pallas_call(
        _outproj_kernel,
        grid=(n_rows // tm,),
        in_specs=[pl.BlockSpec((tm, half), lambda i: (i, a_blk)),
                  pl.BlockSpec((tm, half), lambda i: (i, b_blk)),
                  pl.BlockSpec((tm, d), lambda i: (i, 0)),
                  pl.BlockSpec((1, 6, d), lambda i: (_seg_of_tile(i, tm, t_lat, seq), 0, 0)),
                  pl.BlockSpec((1, d), lambda i: (0, 0)),
                  pl.BlockSpec((d, d), lambda i: (0, 0)),
                  pl.BlockSpec((2, N_EXPERTS, d), lambda i: (0, 0, 0))],
        out_specs=[pl.BlockSpec((tm, d), lambda i: (i, 0)),
                   pl.BlockSpec((tm, d), lambda i: (i, 0)),
                   pl.BlockSpec((N_EXPERTS, tm), lambda i: (0, i))],
        out_shape=[jax.ShapeDtypeStruct((n_rows, d), F32),
                   jax.ShapeDtypeStruct((n_rows, d), BF16),
                   jax.ShapeDtypeStruct((N_EXPERTS, n_rows), F32)],
        compiler_params=_cparams(("parallel",)),
        name="outproj",
    )(mix_a, mix_b, x_all, mods, nw, w_out, router_wt)


def _first_max(x, idx, sentinel):
    m = jnp.max(x, axis=0, keepdims=True)
    first = jnp.min(jnp.where(x == m, idx, sentinel), axis=0, keepdims=True)
    return m, idx == first


def _router_kernel(lt_ref, bias_ref, eye_ref, before_ref, below_ref, lists_ref, tok_ref, pc_ref):
    tn = lt_ref.shape[1]
    scores = jax.nn.sigmoid(lt_ref[...])
    sel = scores + bias_ref[...]
    neg = -jnp.inf
    in_grp = lax.broadcasted_iota(jnp.int32, (GROUP_SIZE, tn), 0)
    gscore = []
    for g in range(N_GROUPS):
        x = sel[g * GROUP_SIZE:(g + 1) * GROUP_SIZE]
        m1, hit = _first_max(x, in_grp, GROUP_SIZE)
        gscore.append(m1 + jnp.max(jnp.where(hit, neg, x), axis=0, keepdims=True))
    rows = []
    for g in range(N_GROUPS):
        beaten = jnp.zeros((1, tn), jnp.int32)
        for o in range(N_GROUPS):
            if o != g:
                wins = (gscore[o] >= gscore[g]) if o < g else (gscore[o] > gscore[g])
                beaten = beaten + wins.astype(jnp.int32)
        keep = jnp.broadcast_to(beaten < TOPK_GROUPS, (GROUP_SIZE, tn))
        rows.append(jnp.where(keep, sel[g * GROUP_SIZE:(g + 1) * GROUP_SIZE], neg))
    cand = jnp.concatenate(rows, axis=0)
    eidx = lax.broadcasted_iota(jnp.int32, cand.shape, 0)
    chosen = jnp.zeros(cand.shape, jnp.bool_)
    hits = []
    for _ in range(TOP_K):
        _, hit = _first_max(cand, eidx, N_EXPERTS)
        hits.append(hit)
        chosen = jnp.logical_or(chosen, hit)
        cand = jnp.where(hit, neg, cand)
    w = jnp.where(chosen, scores, 0.0)
    gates_t = w / jnp.sum(w, axis=0, keepdims=True) * ROUTED_SCALE
    chosen_f = jnp.where(chosen, 1.0, 0.0)
    rank = jnp.dot(chosen_f.astype(BF16), before_ref[...], preferred_element_type=F32)
    count = jnp.sum(chosen_f, axis=1, keepdims=True)
    pc = jnp.floor((count + (ROW_UNIT - 1)) * (1.0 / ROW_UNIT)) * ROW_UNIT
    pc_lanes = jnp.broadcast_to(pc, (N_EXPERTS, LANES))
    start = jnp.dot(below_ref[...], pc_lanes, preferred_element_type=F32,
                    precision=HIGHEST)[:, 0:1]
    slot = start + rank
    pick = lambda hit, v: jnp.sum(jnp.where(hit, v, 0.0), axis=0, keepdims=True)
    lists = jnp.concatenate([pick(h, slot) for h in hits] + [pick(h, gates_t) for h in hits], axis=0)
    lists_ref[...] = lists
    pc_ref[0] = pc_lanes
    padded = jnp.concatenate([lists, jnp.zeros((LANES - 2 * TOP_K, tn), F32)], axis=0)
    tok_ref[...] = lax.dot_general(eye_ref[...], padded, (((1,), (1,)), ((), ())),
                                   preferred_element_type=F32, precision=HIGHEST)


def router(logits, router_bias):
    e, t = logits.shape
    tn = MOE_TILE
    n_tiles = t // tn
    eye = jnp.eye(tn, dtype=F32)
    tok_i = jnp.arange(tn, dtype=jnp.int32)
    before = (tok_i[:, None] < tok_i[None, :]).astype(BF16)
    exp_i = jnp.arange(e, dtype=jnp.int32)
    below = (exp_i[None, :] < exp_i[:, None]).astype(F32)
    const = lambda shape: pl.BlockSpec(shape, lambda i: (0,) * len(shape))
    return pl.pallas_call(
        _router_kernel,
        grid=(n_tiles,),
        in_specs=[pl.BlockSpec((e, tn), lambda i: (0, i)),
                  const((e, 1)), const((tn, tn)), const((tn, tn)), const((e, e))],
        out_specs=[pl.BlockSpec((2 * TOP_K, tn), lambda i: (0, i)),
                   pl.BlockSpec((tn, LANES), lambda i: (i, 0)),
                   pl.BlockSpec((1, e, LANES), lambda i: (i, 0, 0))],
        out_shape=[jax.ShapeDtypeStruct((2 * TOP_K, t), F32),
                   jax.ShapeDtypeStruct((t, LANES), F32),
                   jax.ShapeDtypeStruct((n_tiles, e, LANES), F32)],
        compiler_params=_cparams(("parallel",)),
        name="router",
    )(logits, router_bias.reshape(e, 1), eye, before, below)


def moe_plan(pc, n_tiles):
    e = pc.shape[1]
    assert n_tiles >= e
    cap_rows = _sorted_capacity(n_tiles)
    run_end = jnp.cumsum(pc, axis=1)
    run_start = run_end - pc
    total = jnp.sum(pc, axis=0)
    total_al = ((total + EXPERT_ROWS - 1) // EXPERT_ROWS) * EXPERT_ROWS
    range_end = jnp.cumsum(total_al)
    range_start = range_end - total_al
    run_row = range_start[None, :] + jnp.cumsum(pc, axis=0) - pc
    unit_row = jnp.arange(N_UNITS, dtype=jnp.int32) * ROW_UNIT
    owner = jnp.sum((run_end[:, None, :] <= unit_row[None, :, None]).astype(jnp.int32), axis=2)
    owner_hot = (owner[:, :, None] == jnp.arange(e, dtype=jnp.int32)).astype(jnp.int32)
    dst = jnp.sum(owner_hot * (run_row - run_start)[:, None, :], axis=2) + unit_row[None, :]
    used = owner < e
    dump = cap_rows + (jnp.arange(n_tiles, dtype=jnp.int32) % 2)[:, None] * DUMP_ROWS
    unit_dst = jnp.where(used, dst, dump + unit_row[None, :]).astype(jnp.int32)
    unit_src = jnp.where(used, dst, 0).astype(jnp.int32)
    pad_off = jnp.arange(PAD_UNITS, dtype=jnp.int32)[None, :] * ROW_UNIT
    pad_row = (range_start + total)[:, None] + pad_off
    pad_row = jnp.concatenate([jnp.where(pad_row < range_end[:, None], pad_row, -1),
                               jnp.full((n_tiles - e, PAD_UNITS), -1, jnp.int32)], axis=0)
    pad_dst = jnp.where(pad_row >= 0, pad_row, dump + SLOTS + pad_off).astype(jnp.int32)
    tile_end = range_end // EXPERT_ROWS
    tile_expert = jnp.sum((tile_end[None, :] <= jnp.arange(cap_rows // EXPERT_ROWS,
                                                           dtype=jnp.int32)[:, None])
                          .astype(jnp.int32), axis=1)
    tile_expert = jnp.minimum(tile_expert, e - 1).astype(jnp.int32)
    used_rows = run_end[:, -1].astype(jnp.int32)
    return unit_dst, unit_src, pad_dst, used_rows, tile_expert, tile_end[-1:].astype(jnp.int32)


def _sorted_capacity(n_tiles):
    worst_rows = n_tiles * (MOE_TILE * TOP_K + N_EXPERTS * (ROW_UNIT - 1)) + N_EXPERTS * (EXPERT_ROWS - 1)
    return -(-worst_rows // EXPERT_ROWS) * EXPERT_ROWS


DUMP_ROWS = SLOTS + PAD_UNITS * ROW_UNIT


UNITS_PER_CHUNK = SLOT_CHUNK // ROW_UNIT
MAIN_SLOTS = SLOTS - SLOT_CHUNK
assert MAIN_SLOTS >= MOE_TILE * TOP_K


def _dispatch_kernel(unit_ref, pad_ref, used_ref, h_ref, lists_ref, xs_ref, loc_ref, zero_ref, sems):
    i = pl.program_id(0)
    last = pl.num_programs(0) - 1
    slot = i % 2
    tn = h_ref.shape[0]

    @pl.when(i == 0)
    def _():
        zero_ref[...] = jnp.zeros_like(zero_ref)

    def chunk(c0):
        srow = (lax.broadcasted_iota(jnp.int32, (SLOT_CHUNK, tn), 0) + c0).astype(F32)
        sel = srow == lists_ref[0:1, :]
        for k in range(1, TOP_K):
            sel = jnp.logical_or(sel, srow == lists_ref[k:k + 1, :])
        loc_ref[slot, c0:c0 + SLOT_CHUNK, :] = jnp.dot(
            jnp.where(sel, 1.0, 0.0).astype(BF16), h_ref[...],
            preferred_element_type=F32).astype(BF16)
        for u in range(c0 // ROW_UNIT, c0 // ROW_UNIT + UNITS_PER_CHUNK):
            row = pl.multiple_of(unit_ref[i, u], ROW_UNIT)
            pltpu.make_async_copy(loc_ref.at[slot, pl.ds(u * ROW_UNIT, ROW_UNIT)],
                                  xs_ref.at[pl.ds(row, ROW_UNIT)], sems.at[slot]).start()

    for c0 in range(0, MAIN_SLOTS, SLOT_CHUNK):
        chunk(c0)

    @pl.when(used_ref[i] > MAIN_SLOTS)
    def _():
        chunk(MAIN_SLOTS)

    for p in range(PAD_UNITS):
        row = pl.multiple_of(pad_ref[i, p], ROW_UNIT)
        pltpu.make_async_copy(zero_ref, xs_ref.at[pl.ds(row, ROW_UNIT)], sems.at[slot]).start()

    def wait_all(s, tile):
        pltpu.make_async_copy(loc_ref.at[s, pl.ds(0, MAIN_SLOTS)],
                              xs_ref.at[pl.ds(0, MAIN_SLOTS)], sems.at[s]).wait()

        @pl.when(used_ref[tile] > MAIN_SLOTS)
        def _():
            pltpu.make_async_copy(loc_ref.at[s, pl.ds(MAIN_SLOTS, SLOT_CHUNK)],
                                  xs_ref.at[pl.ds(0, SLOT_CHUNK)], sems.at[s]).wait()
        pltpu.make_async_copy(loc_ref.at[s, pl.ds(0, PAD_UNITS * ROW_UNIT)],
                              xs_ref.at[pl.ds(0, PAD_UNITS * ROW_UNIT)], sems.at[s]).wait()

    @pl.when(i > 0)
    def _():
        wait_all(1 - slot, i - 1)

    @pl.when(i == last)
    def _():
        wait_all(slot, i)


def moe_dispatch(h, lists, unit_dst, pad_dst, used_rows, *, n_tiles):
    d = h.shape[1]
    cap_rows = _sorted_capacity(n_tiles) + 2 * DUMP_ROWS
    return pl.pallas_call(
        _dispatch_kernel,
        grid_spec=pltpu.PrefetchScalarGridSpec(
            num_scalar_prefetch=3,
            grid=(n_tiles,),
            in_specs=[pl.BlockSpec((MOE_TILE, d), lambda i, *_: (i, 0)),
                      pl.BlockSpec((2 * TOP_K, MOE_TILE), lambda i, *_: (0, i))],
            out_specs=pl.BlockSpec(memory_space=pl.ANY),
            scratch_shapes=[pltpu.VMEM((2, SLOTS, d), BF16), pltpu.VMEM((ROW_UNIT, d), BF16),
                            pltpu.SemaphoreType.DMA((2,))]),
        out_shape=jax.ShapeDtypeStruct((cap_rows, d), BF16),
        compiler_params=_cparams(("arbitrary",)),
        name="moe_dispatch",
    )(unit_dst, pad_dst, used_rows, h, lists)


def _expert_kernel(te_ref, nt_ref, x_ref, wg_ref, wu_ref, wd_ref, y_ref, wg_bf, wu_bf, wd_bf):
    t = pl.program_id(0)
    live = t < nt_ref[0]
    tc = jnp.minimum(t, jnp.maximum(nt_ref[0] - 1, 0))
    new_expert = jnp.logical_or(t == 0, te_ref[tc] != te_ref[jnp.maximum(tc - 1, 0)])

    @pl.when(jnp.logical_and(live, new_expert))
    def _():
        wg_bf[...] = wg_ref[0, 0].astype(BF16)
        wu_bf[...] = wu_ref[0, 0].astype(BF16)
        wd_bf[...] = wd_ref[0, 0].astype(BF16)

    @pl.when(live)
    def _():
        x = x_ref[...]
        act = (_silu(jnp.dot(x, wg_bf[...], preferred_element_type=F32))
               * jnp.dot(x, wu_bf[...], preferred_element_type=F32))
        y_ref[...] = jnp.dot(act.astype(BF16), wd_bf[...], preferred_element_type=F32).astype(BF16)


def moe_experts(xs, w_gate, w_up, w_down, layer, tile_expert, n_row_tiles):
    d = xs.shape[1]
    cap_rows = xs.shape[0] - 2 * DUMP_ROWS
    ff = w_gate.shape[3]
    last_live = lambda nt: jnp.maximum(nt[0] - 1, 0)
    row_map = lambda t, te, nt: (jnp.minimum(t, last_live(nt)), 0)
    w_map = lambda t, te, nt: (layer, te[jnp.minimum(t, last_live(nt))], 0, 0)
    return pl.pallas_call(
        _expert_kernel,
        grid_spec=pltpu.PrefetchScalarGridSpec(
            num_scalar_prefetch=2,
            grid=(cap_rows // EXPERT_ROWS,),
            in_specs=[pl.BlockSpec((EXPERT_ROWS, d), row_map),
                      pl.BlockSpec((1, 1, d, ff), w_map), pl.BlockSpec((1, 1, d, ff), w_map),
                      pl.BlockSpec((1, 1, ff, d), w_map)],
            out_specs=pl.BlockSpec((EXPERT_ROWS, d), row_map),
            scratch_shapes=[pltpu.VMEM((d, ff), BF16), pltpu.VMEM((d, ff), BF16),
                            pltpu.VMEM((ff, d), BF16)]),
        out_shape=jax.ShapeDtypeStruct((cap_rows, d), BF16),
        compiler_params=_cparams(("arbitrary",)),
        name="moe_experts",
    )(tile_expert, n_row_tiles, xs, w_gate, w_up, w_down)


def _combine_kernel(unit_ref, used_ref, ys_ref, tok_ref, h_ref, x_ref, mod_ref, fnw_ref, sg_ref,
                    su_ref, sd_ref, o_ref, loc_ref, acc_ref, sems, *, final_norm):
    i = pl.program_id(0)
    last = pl.num_programs(0) - 1
    slot = i % 2
    tn = h_ref.shape[0]
    def fetch(tile, s):
        def units(u0, u1):
            for u in range(u0, u1):
                row = pl.multiple_of(unit_ref[tile, u], ROW_UNIT)
                pltpu.make_async_copy(ys_ref.at[pl.ds(row, ROW_UNIT)],
                                      loc_ref.at[s, pl.ds(u * ROW_UNIT, ROW_UNIT)], sems.at[s]).start()
        units(0, MAIN_SLOTS // ROW_UNIT)

        @pl.when(used_ref[tile] > MAIN_SLOTS)
        def _():
            units(MAIN_SLOTS // ROW_UNIT, N_UNITS)

    @pl.when(i == 0)
    def _():
        fetch(0, 0)

    @pl.when(i < last)
    def _():
        fetch(i + 1, 1 - slot)

    h = h_ref[...]
    act = (_silu(jnp.dot(h, sg_ref[...], preferred_element_type=F32))
           * jnp.dot(h, su_ref[...], preferred_element_type=F32))
    acc = jnp.dot(act.astype(BF16), sd_ref[...], preferred_element_type=F32)
    tail = used_ref[i] > MAIN_SLOTS
    pltpu.make_async_copy(ys_ref.at[pl.ds(0, MAIN_SLOTS)], loc_ref.at[slot, pl.ds(0, MAIN_SLOTS)],
                          sems.at[slot]).wait()

    @pl.when(tail)
    def _():
        pltpu.make_async_copy(ys_ref.at[pl.ds(0, SLOT_CHUNK)],
                              loc_ref.at[slot, pl.ds(MAIN_SLOTS, SLOT_CHUNK)], sems.at[slot]).wait()

    tok = tok_ref[...]

    def chunk(c0):
        scol = (lax.broadcasted_iota(jnp.int32, (tn, SLOT_CHUNK), 1) + c0).astype(F32)
        q = jnp.zeros((tn, SLOT_CHUNK), F32)
        for k in range(TOP_K):
            q = jnp.where(scol == tok[:, k:k + 1], tok[:, TOP_K + k:TOP_K + k + 1], q)
        return jnp.dot(q.astype(BF16), loc_ref[slot, c0:c0 + SLOT_CHUNK, :],
                       preferred_element_type=F32)

    for c0 in range(0, MAIN_SLOTS, SLOT_CHUNK):
        acc = acc + chunk(c0)
    acc_ref[...] = acc

    @pl.when(tail)
    def _():
        acc_ref[...] += chunk(MAIN_SLOTS)

    x = x_ref[...] + mod_ref[0][5:6] * acc_ref[...]
    if final_norm:
        x = _rms(x) * fnw_ref[...]
    o_ref[...] = x


def moe_combine(ys, tok, h, x_all, mods, fnw, sh_gate, sh_up, sh_down, unit_dst, used_rows, *,
                n_tiles, t_lat, seq, final_norm):
    d = h.shape[1]
    tn = MOE_TILE
    const = lambda a: pl.BlockSpec(a.shape, lambda i, *_: (0,) * a.ndim)
    kern = functools.partial(_combine_kernel, final_norm=final_norm)
    return pl.pallas_call(
        kern,
        grid_spec=pltpu.PrefetchScalarGridSpec(
            num_scalar_prefetch=2,
            grid=(n_tiles,),
            in_specs=[pl.BlockSpec(memory_space=pl.ANY),
                      pl.BlockSpec((tn, LANES), lambda i, *_: (i, 0)),
                      pl.BlockSpec((tn, d), lambda i, *_: (i, 0)),
                      pl.BlockSpec((tn, d), lambda i, *_: (i, 0)),
                      pl.BlockSpec((1, 6, d),
                                   lambda i, *_: (_seg_of_tile(i, tn, t_lat, seq), 0, 0)),
                      const(fnw), const(sh_gate), const(sh_up), const(sh_down)],
            out_specs=pl.BlockSpec((tn, d), lambda i, *_: (i, 0)),
            scratch_shapes=[pltpu.VMEM((2, SLOTS, d), BF16), pltpu.VMEM((tn, d), F32),
                            pltpu.SemaphoreType.DMA((2,))]),
        out_shape=jax.ShapeDtypeStruct((n_tiles * tn, d), F32),
        compiler_params=_cparams(("arbitrary",)),
        name="moe_combine",
    )(unit_dst, used_rows, ys, tok, h, x_all, mods, fnw, sh_gate, sh_up, sh_down)


def moe_block(h2, logits, router_bias, x_mid, mods, fnw, w_gate, w_up, w_down,
              sh_gate, sh_up, sh_down, *, layer, t_lat, seq, final_norm):
    n_tiles = h2.shape[0] // MOE_TILE
    lists, tok, pc = router(logits, router_bias)
    unit_dst, unit_src, pad_dst, used_rows, tile_expert, n_row_tiles = moe_plan(
        pc[:, :, 0].astype(jnp.int32), n_tiles)
    xs = moe_dispatch(h2, lists, unit_dst, pad_dst, used_rows, n_tiles=n_tiles)
    ys = moe_experts(xs, w_gate, w_up, w_down, layer, tile_expert, n_row_tiles)
    return moe_combine(ys, tok, h2, x_mid, mods, fnw, sh_gate.astype(BF16), sh_up.astype(BF16),
                       sh_down.astype(BF16), unit_src, used_rows, n_tiles=n_tiles, t_lat=t_lat,
                       seq=seq, final_norm=final_norm)


def _rope_tables(seq, tm):
    half = RET_DK // 2
    pos = jnp.arange(seq, dtype=jnp.int32)
    row = (pos // GRID_W).astype(F32)
    col = (pos % GRID_W).astype(F32)
    inv = ROPE_BASE ** (-jnp.arange(0, half, 2, dtype=F32) / half)
    a_row, a_col = row[:, None] * inv[None, :], col[:, None] * inv[None, :]
    cos_h = jnp.concatenate([jnp.cos(a_row)] * 2 + [jnp.cos(a_col)] * 2, axis=-1)
    sin_h = jnp.concatenate([-jnp.sin(a_row), jnp.sin(a_row), -jnp.sin(a_col), jnp.sin(a_col)], -1)
    reps = 2 * RET_HEADS
    cos_t = jnp.concatenate([jnp.tile(cos_h, (1, reps)), jnp.ones((tm, reps * RET_DK), F32)], 0)
    sin_t = jnp.concatenate([jnp.tile(sin_h, (1, reps)), jnp.zeros((tm, reps * RET_DK), F32)], 0)
    return cos_t, sin_t


def _rope_partner_columns():
    quarter = RET_DK // 4
    idx = jnp.arange(2 * RET_HEADS * RET_DK, dtype=jnp.int32)
    within = idx % (2 * quarter)
    return jnp.where(within < quarter, idx + quarter, idx - quarter)


def kernel(x, c, ctx, c_ctx, ada_w, ada_b, norm1_w, norm2_w, ev_w_in, ev_short_w, ev_short_b, ev_filt_w1, ev_filt_b1, ev_filt_w2, ev_filt_b2, ev_filt_w3, ev_filt_b3, ev_filt_w4, ev_filt_freq, ev_hyena_bias, ev_w_out, od_w_in, od_gate_w1_f, od_gate_w2_f, od_gate_b_f, od_gate_w1_b, od_gate_w2_b, od_gate_b_b, od_norm_w, od_w_out, router_w, router_bias, exp_w_gate, exp_w_up, exp_w_down, sh_w_gate, sh_w_up, sh_w_down, final_norm_w):
    batch, seq, d = x.shape
    ctx_len = ctx.shape[1]
    depth = ada_w.shape[0]
    t_lat, t_ctx = batch * seq, batch * ctx_len
    t_all = t_lat + t_ctx
    tm = 512

    x_all = jnp.concatenate([x.reshape(t_lat, d), ctx.reshape(t_ctx, d)], axis=0)
    cond8 = jnp.concatenate([c_ctx[None, :], c, jnp.zeros((8 - 1 - batch, d), F32)], axis=0)
    mods_all = adaln_rows(cond8, ada_w, ada_b).reshape(depth, 8, 6, d)

    for i in range(depth):
        last = i == depth - 1
        j = i // 2
        mods = mods_all[i]
        nw1, nw2 = norm1_w[i][None, :], norm2_w[i][None, :]
        if i % 2 == 0:
            qk = 2 * RET_HEADS * RET_DK
            k_scale = jnp.concatenate([jnp.ones((qk // 2,), F32),
                                       jnp.full((qk // 2,), RET_DK ** -0.5, F32)])
            w_in = ev_w_in[j]
            w_qk = w_in[:, :qk] * k_scale
            w_ext = jnp.concatenate([w_qk, w_in[:, qk:], w_qk[:, _rope_partner_columns()]],
                                    axis=1).astype(BF16)
            cos_t, sin_t = _rope_tables(seq, tm)
            proj = inproj_even(x_all, mods, nw1, w_ext, cos_t, sin_t, t_lat=t_lat, seq=seq, tm=tm)
            log_g = [math.log1p(-2.0 ** (-5.0 - h)) for h in range(RET_HEADS)]
            mix_a = bidir_scan(proj, None, None, jnp.ones((1, RET_DV), F32),
                               batch=batch, seq=seq, ctx_len=ctx_len, heads=RET_HEADS, dk=RET_DK,
                               dv=RET_DV, chunk=RET_CHUNK, q_blk=0, k_blk=1, v_blk=1, g_blk=2,
                               log_decay_f=log_g, log_decay_b=log_g[::-1])
            hp = dict(short_w=ev_short_w[j], short_b=ev_short_b[j][None, :],
                      w1=jnp.pad(ev_filt_w1[j], ((0, LANES - HYENA_EMB), (0, 0))),
                      b1=ev_filt_b1[j][None, :], w2=ev_filt_w2[j], b2=ev_filt_b2[j][None, :],
                      w3=ev_filt_w3[j], b3=ev_filt_b3[j][None, :], w4=ev_filt_w4[j],
                      freq=ev_filt_freq[j][None, :], bias=ev_hyena_bias[j][None, :])
            hy_lat = hyena_long_conv(proj, hp, row0=0, batch=batch, seq_len=seq, z_blk=1)
            hy_ctx = hyena_long_conv(proj, hp, row0=t_lat, batch=batch, seq_len=ctx_len, z_blk=1)
            mix_b = jnp.concatenate([hy_lat, hy_ctx], axis=0)
            a_blk, b_blk = 0, 0
            w_out = ev_w_out[j].astype(BF16)
        else:
            kd = GLA_HEADS * GLA_DK
            pad_cols = LANES - 2 * GLA_RANK
            w_ext = jnp.concatenate([od_w_in[j], od_gate_w1_f[j], od_gate_w1_b[j],
                                     jnp.zeros((d, pad_cols), F32)], axis=1).astype(BF16)
            w2f = jnp.pad(od_gate_w2_f[j], ((0, LANES - GLA_RANK), (0, 0)))
            w2b = jnp.pad(od_gate_w2_b[j], ((GLA_RANK, LANES - 2 * GLA_RANK), (0, 0)))
            proj, la_f, la_b = inproj_odd(x_all, mods, nw1, w_ext, w2f, w2b,
                                          od_gate_b_f[j][None, :], od_gate_b_b[j][None, :],
                                          t_lat=t_lat, seq=seq, tm=tm)
            mix_a = bidir_scan(proj, la_f, la_b, od_norm_w[j][None, :],
                               batch=batch, seq=seq, ctx_len=ctx_len, heads=GLA_HEADS, dk=GLA_DK,
                               dv=GLA_DV, chunk=GLA_CHUNK, q_blk=0, k_blk=1, v_blk=1, g_blk=2)
            mix_b = mix_a
            a_blk, b_blk = 0, 1
            w_out = od_w_out[j].astype(BF16)

        n_rows = t_lat if last else t_all
        rw_t = router_w[i].T
        rw_hi = rw_t.astype(BF16)
        rw_split = jnp.stack([rw_hi, (rw_t - rw_hi.astype(F32)).astype(BF16)])
        x_mid, h2, logits = outproj(mix_a, a_blk, mix_b, b_blk, x_all, mods, nw2, w_out,
                                    rw_split, n_rows=n_rows, t_lat=t_lat, seq=seq, tm=tm)
        x_all = moe_block(h2, logits, router_bias[i], x_mid, mods, final_norm_w[None, :],
                          exp_w_gate, exp_w_up, exp_w_down,
                          sh_w_gate[i], sh_w_up[i], sh_w_down[i],
                          layer=i, t_lat=t_lat, seq=seq, final_norm=last)
    return x_all[:t_lat].reshape(batch, seq, d)
```

```python
import functools
import math

import jax
import jax.numpy as jnp
from jax import lax
from jax.experimental import pallas as pl
from jax.experimental.pallas import tpu as pltpu

F32 = jnp.float32
BF16 = jnp.bfloat16
HIGHEST = lax.Precision.HIGHEST

D_MODEL = 1024
GRID_W = 64
NORM_EPS = 1e-6
RET_HEADS, RET_DK, RET_DV, RET_CHUNK = 4, 64, 128, 128
ROPE_BASE = 10000.0
HYENA_CH, HYENA_EMB = 512, 33
HYENA_FAST_DECAY, HYENA_SLOW_DECAY, HYENA_TARGET = 0.3, 1.5, 1e-2
GLA_HEADS, GLA_DK, GLA_DV, GLA_RANK, GLA_TAU, GLA_CHUNK = 4, 128, 256, 16, 16.0, 64
N_EXPERTS, TOP_K, N_GROUPS, TOPK_GROUPS = 64, 8, 8, 4
GROUP_SIZE = N_EXPERTS // N_GROUPS
EXPERT_FF = 256
ROUTED_SCALE = 2.5

LANES = 128
SCAN_ROWS = 256
MOE_TILE = 256
ROW_UNIT = 16
EXPERT_ROWS = 512
SLOT_CHUNK = 512
SLOTS = -(-(MOE_TILE * TOP_K + N_EXPERTS * (ROW_UNIT - 1) + ROW_UNIT) // SLOT_CHUNK) * SLOT_CHUNK
N_UNITS = SLOTS // ROW_UNIT
PAD_UNITS = EXPERT_ROWS // ROW_UNIT
VMEM_LIMIT = 56 * 1024 * 1024


def _cparams(sem):
    return pltpu.CompilerParams(dimension_semantics=sem, vmem_limit_bytes=VMEM_LIMIT)


def _rms(x):
    return x * lax.rsqrt(jnp.mean(x * x, axis=-1, keepdims=True) + NORM_EPS)


def _silu(x):
    return x * jax.nn.sigmoid(x)


def _adaln_kernel(c_ref, w_ref, b_ref, o_ref):
    c = c_ref[...]
    o_ref[0] = jnp.dot(_silu(c), w_ref[0], preferred_element_type=F32,
                       precision=HIGHEST) + b_ref[0]


def adaln_rows(cond8, ada_w, ada_b):
    depth, d, n = ada_w.shape
    tn = 512
    return pl.pallas_call(
        _adaln_kernel,
        grid=(depth, n // tn),
        in_specs=[pl.BlockSpec((8, d), lambda l, j: (0, 0)),
                  pl.BlockSpec((1, d, tn), lambda l, j: (l, 0, j)),
                  pl.BlockSpec((1, 1, tn), lambda l, j: (l, 0, j))],
        out_specs=pl.BlockSpec((1, 8, tn), lambda l, j: (l, 0, j)),
        out_shape=jax.ShapeDtypeStruct((depth, 8, n), F32),
        compiler_params=_cparams(("parallel", "parallel")),
        name="adaln_rows",
    )(cond8, ada_w, ada_b.reshape(depth, 1, n))


def _modulated_norm(x, mod, nw, shift_row, scale_row):
    return _rms(x) * nw * (1.0 + mod[scale_row:scale_row + 1]) + mod[shift_row:shift_row + 1]


def _inproj_even_kernel(x_ref, mod_ref, nw_ref, w_ref, cos_ref, sin_ref, o_ref):
    h = _modulated_norm(x_ref[...], mod_ref[0], nw_ref[...], 0, 1).astype(BF16)
    n_main = o_ref.shape[1]
    qk = jnp.dot(h, w_ref[:, 0:512], preferred_element_type=F32)
    qk_sw = jnp.dot(h, w_ref[:, n_main:n_main + 512], preferred_element_type=F32)
    o_ref[:, 0:512] = (qk * cos_ref[...] + qk_sw * sin_ref[...]).astype(BF16)
    for c0 in range(512, n_main, 512):
        o_ref[:, c0:c0 + 512] = jnp.dot(h, w_ref[:, c0:c0 + 512],
                                        preferred_element_type=F32).astype(BF16)


def _inproj_odd_kernel(x_ref, mod_ref, nw_ref, w_ref, w2_ref, b2_ref, o_ref, laf_ref, lab_ref):
    h = _modulated_norm(x_ref[...], mod_ref[0], nw_ref[...], 0, 1).astype(BF16)
    n_main = o_ref.shape[1]
    kd = GLA_HEADS * GLA_DK
    q = jnp.dot(h, w_ref[:, 0:kd], preferred_element_type=F32)
    o_ref[:, 0:kd] = (q * (GLA_DK ** -0.5)).astype(BF16)
    for c0 in range(kd, n_main, 512):
        o_ref[:, c0:c0 + 512] = jnp.dot(h, w_ref[:, c0:c0 + 512],
                                        preferred_element_type=F32).astype(BF16)
    low = jnp.dot(h, w_ref[:, n_main:n_main + LANES], preferred_element_type=F32)

    low_hi = low.astype(BF16)
    low_lo = (low - low_hi.astype(F32)).astype(BF16)
    dot = lambda a, b: jnp.dot(a, b, preferred_element_type=F32)
    z = dot(low_hi, w2_ref[0]) + (dot(low_lo, w2_ref[0]) + dot(low_hi, w2_ref[1])) + b2_ref[...]
    la = (jnp.minimum(z, 0.0) - jnp.log(1.0 + jnp.exp(-jnp.abs(z)))) * (1.0 / GLA_TAU)
    laf_ref[...] = la[:, 0:kd]
    lab_ref[...] = la[:, kd:2 * kd]


def _seg_of_tile(i, tm, t_lat, seq):
    return jnp.where(i < t_lat // tm, 1 + (i * tm) // seq, 0)


def inproj_even(x_all, mods, nw, w_ext, cos_t, sin_t, *, t_lat, seq, tm=512):
    t_all, d = x_all.shape
    n_ext = w_ext.shape[1]
    n_main = n_ext - 512
    n_lat_tiles, pos_tiles = t_lat // tm, seq // tm

    def pos_map(i):
        return (jnp.where(i < n_lat_tiles, i % pos_tiles, pos_tiles), 0)

    return pl.pallas_call(
        _inproj_even_kernel,
        grid=(t_all // tm,),
        in_specs=[pl.BlockSpec((tm, d), lambda i: (i, 0)),
                  pl.BlockSpec((1, 6, d), lambda i: (_seg_of_tile(i, tm, t_lat, seq), 0, 0)),
                  pl.BlockSpec((1, d), lambda i: (0, 0)),
                  pl.BlockSpec((d, n_ext), lambda i: (0, 0)),
                  pl.BlockSpec((tm, 512), pos_map),
                  pl.BlockSpec((tm, 512), pos_map)],
        out_specs=pl.BlockSpec((tm, n_main), lambda i: (i, 0)),
        out_shape=jax.ShapeDtypeStruct((t_all, n_main), BF16),
        compiler_params=_cparams(("parallel",)),
        name="inproj_even",
    )(x_all, mods, nw, w_ext, cos_t, sin_t)


def inproj_odd(x_all, mods, nw, w_ext, w2, b2, *, t_lat, seq, tm=512):
    t_all, d = x_all.shape
    n_main = w_ext.shape[1] - LANES
    kd = GLA_HEADS * GLA_DK
    full = lambda shape: pl.BlockSpec(shape, lambda i: (0,) * len(shape))
    return pl.pallas_call(
        _inproj_odd_kernel,
        grid=(t_all // tm,),
        in_specs=[pl.BlockSpec((tm, d), lambda i: (i, 0)),
                  pl.BlockSpec((1, 6, d), lambda i: (_seg_of_tile(i, tm, t_lat, seq), 0, 0)),
                  full((1, d)), full(w_ext.shape), full(w2.shape), full(b2.shape)],
        out_specs=[pl.BlockSpec((tm, n_main), lambda i: (i, 0)),
                   pl.BlockSpec((tm, kd), lambda i: (i, 0)),
                   pl.BlockSpec((tm, kd), lambda i: (i, 0))],
        out_shape=[jax.ShapeDtypeStruct((t_all, n_main), BF16),
                   jax.ShapeDtypeStruct((t_all, kd), F32),
                   jax.ShapeDtypeStruct((t_all, kd), F32)],
        compiler_params=_cparams(("parallel",)),
        name="inproj_odd",
    )(x_all, mods, nw, w_ext, w2, b2)


def _scan_kernel(*refs, heads, dk, dv, chunk, gated, reverse, log_decay):
    it = iter(refs)
    q_ref, k_ref, v_ref = next(it), next(it), next(it)
    la_ref = next(it) if gated else None
    if reverse:
        oprev_ref, gate_ref, nw_ref = next(it), next(it), next(it)
    out_ref = next(it)
    state_ref = next(it)
    dec_ref = None if gated else next(it)

    hpg = LANES // dk
    groups = heads // hpg
    rows = q_ref.shape[0]
    n_chunks = rows // chunk
    j = pl.program_id(1)

    row_i = lax.broadcasted_iota(jnp.int32, (rows, rows), 0)
    col_i = lax.broadcasted_iota(jnp.int32, (rows, rows), 1)
    same_chunk = (row_i // chunk) == (col_i // chunk)
    keep = jnp.logical_and(same_chunk, (col_i >= row_i) if reverse else (row_i >= col_i))

    @pl.when(j == 0)
    def _init():
        state_ref[...] = jnp.zeros_like(state_ref)
        if not gated:
            pos = lax.broadcasted_iota(jnp.int32, (rows, LANES), 0) % chunk
            steps = ((chunk - pos) if reverse else (pos + 1)).astype(F32)
            lane = lax.broadcasted_iota(jnp.int32, (rows, LANES), 1)
            for g in range(groups):
                lg = jnp.zeros((rows, LANES), F32)
                for a in range(hpg):
                    lg = jnp.where(lane // dk == a, log_decay[g * hpg + a], lg)
                logb = steps * lg
                b_end = float(chunk) * lg
                dec_ref[g, 0] = jnp.exp(logb)
                dec_ref[g, 1] = jnp.exp(-logb)
                dec_ref[g, 2] = jnp.exp(b_end - logb)
                dec_ref[g, 3] = jnp.exp(b_end)

    if gated:
        tri = jnp.where(keep, 1.0, 0.0).astype(BF16)
        la = la_ref[...]
        la_hi = la.astype(BF16)
        la_lo = (la - la_hi.astype(F32)).astype(BF16)
        logb_all = (jnp.dot(tri, la_hi, preferred_element_type=F32)
                    + jnp.dot(tri, la_lo, preferred_element_type=F32))

    lane1 = lax.broadcasted_iota(jnp.int32, (1, LANES), 1)
    order = range(n_chunks - 1, -1, -1) if reverse else range(n_chunks)
    end_row = lambda c: c * chunk if reverse else (c + 1) * chunk - 1
    for g in range(groups):
        ksl = slice(g * LANES, (g + 1) * LANES)
        qg = q_ref[:, ksl].astype(F32)
        kg = k_ref[:, ksl].astype(F32)
        if gated:
            logb = logb_all[:, ksl]
            ends = [logb[end_row(c):end_row(c) + 1] for c in range(n_chunks)]
            b_end = jnp.concatenate([jnp.broadcast_to(e, (chunk, LANES)) for e in ends], axis=0)
            e_q, e_k, e_s = jnp.exp(logb), jnp.exp(-logb), jnp.exp(b_end - logb)
            e_e = [jnp.exp(e) for e in ends]
        else:
            e_q, e_k, e_s = dec_ref[g, 0], dec_ref[g, 1], dec_ref[g, 2]
            e_e = [dec_ref[g, 3][0:1]] * n_chunks
        qd = qg * e_q
        kd_ = (kg * e_k).astype(BF16)
        ks = (kg * e_s).astype(BF16)
        for a in range(hpg):
            h = g * hpg + a
            qa = (jnp.where(lane1 // dk == a, qd, 0.0) if hpg > 1 else qd).astype(BF16)
            vh = v_ref[:, h * dv:(h + 1) * dv]
            s = lax.dot_general(qa, kd_, (((1,), (1,)), ((), ())), preferred_element_type=F32)
            o_intra = jnp.dot(jnp.where(keep, s, 0.0).astype(BF16), vh,
                              preferred_element_type=F32)
            st = state_ref[h]
            for c in order:
                rs = slice(c * chunk, (c + 1) * chunk)
                o = o_intra[rs] + lax.dot_general(qa[rs], st.astype(BF16), (((1,), (1,)), ((), ())),
                                                  preferred_element_type=F32)
                st = e_e[c] * st + lax.dot_general(vh[rs], ks[rs], (((0,), (0,)), ((), ())),
                                                   preferred_element_type=F32)
                osl = (rs, slice(h * dv, (h + 1) * dv))
                if reverse:
                    o = _rms(o + oprev_ref[osl]) * nw_ref[...]
                    out_ref[osl] = (o * _silu(gate_ref[osl].astype(F32))).astype(BF16)
                else:
                    out_ref[osl] = o
            state_ref[h] = st


def bidir_scan(proj, la_f, la_b, norm_w, *, batch, seq, ctx_len, heads, dk, dv, chunk,
               q_blk, k_blk, v_blk, g_blk, log_decay_f=None, log_decay_b=None):
    t_all = proj.shape[0]
    gated = la_f is not None
    hk, hv = heads * dk, heads * dv
    rb = SCAN_ROWS
    lat_blocks = seq // rb
    ctx_base = (batch * seq) // rb
    assert ctx_len == rb

    def rows_fwd(b, j):
        return jnp.where(j == 0, ctx_base + b, b * lat_blocks + j - 1)

    def rows_bwd(b, j):
        return jnp.where(j == 0, ctx_base + b, b * lat_blocks + lat_blocks - j)

    outs = None
    for reverse, rows_of, la, ld in ((False, rows_fwd, la_f, log_decay_f),
                                     (True, rows_bwd, la_b, log_decay_b)):
        spec = lambda w, cb: pl.BlockSpec((rb, w), lambda b, j, cb=cb: (rows_of(b, j), cb))
        in_specs = [spec(hk, q_blk), spec(hk, k_blk), spec(hv, v_blk)]
        args = [proj, proj, proj]
        if gated:
            in_specs.append(spec(hk, 0))
            args.append(la)
        if reverse:
            in_specs += [spec(hv, 0), spec(hv, g_blk), pl.BlockSpec((1, dv), lambda b, j: (0, 0))]
            args += [outs, proj, norm_w]
        scratch = [pltpu.VMEM((heads, dv, LANES), F32)]
        if not gated:
            scratch.append(pltpu.VMEM((hk // LANES, 4, rb, LANES), F32))
        kern = functools.partial(_scan_kernel, heads=heads, dk=dk, dv=dv, chunk=chunk,
                                 gated=gated, reverse=reverse, log_decay=ld)
        outs = pl.pallas_call(
            kern,
            grid=(batch, lat_blocks + 1),
            in_specs=in_specs,
            out_specs=spec(hv, 0),
            out_shape=jax.ShapeDtypeStruct((t_all, hv), BF16 if reverse else F32),
            scratch_shapes=scratch,
            compiler_params=_cparams(("parallel", "arbitrary")),
            name="scan_bwd" if reverse else "scan_fwd",
        )(*args)
    return outs


def _shortconv_kernel(z_ref, zp_ref, zn_ref, w_ref, b_ref, u_ref, x0_ref, *, tiles_per_seq):
    i = pl.program_id(0)
    tm = z_ref.shape[0]
    z = z_ref[...].astype(F32)
    first = (i % tiles_per_seq) == 0
    last = (i % tiles_per_seq) == tiles_per_seq - 1
    halo = zp_ref.shape[0]
    prev_row = jnp.where(first, 0.0, zp_ref[halo - 1:halo, :].astype(F32))
    next_row = jnp.where(last, 0.0, zn_ref[0:1, :].astype(F32))
    row = lax.broadcasted_iota(jnp.int32, z.shape, 0)
    z_prev = jnp.where(row == 0, prev_row, pltpu.roll(z, 1, 0))
    z_next = jnp.where(row == tm - 1, next_row, pltpu.roll(z, tm - 1, 0))
    y = w_ref[0:1] * z_prev + w_ref[1:2] * z + w_ref[2:3] * z_next + b_ref[...]
    c = HYENA_CH
    x0_ref[...] = y[:, 0:c].astype(BF16)
    u_ref[...] = (y[:, c:2 * c] * y[:, 2 * c:3 * c]).astype(BF16)


def hyena_shortconv(proj, short_w, short_b, *, row0, batch, seq_len, z_blk, tm=256):
    halo = 16
    tiles_per_seq = seq_len // tm
    n_tiles = batch * tiles_per_seq
    t0, h_per_tile = row0 // tm, tm // halo
    nz = 3 * HYENA_CH
    n_halo_blocks = proj.shape[0] // halo
    out_map = lambda i: (i % tiles_per_seq, i // tiles_per_seq)
    kern = functools.partial(_shortconv_kernel, tiles_per_seq=tiles_per_seq)
    return pl.pallas_call(
        kern,
        grid=(n_tiles,),
        in_specs=[pl.BlockSpec((tm, nz), lambda i: (t0 + i, z_blk)),
                  pl.BlockSpec((halo, nz),
                               lambda i: (jnp.maximum((t0 + i) * h_per_tile - 1, 0), z_blk)),
                  pl.BlockSpec((halo, nz),
                               lambda i: (jnp.minimum((t0 + i + 1) * h_per_tile,
                                                      n_halo_blocks - 1), z_blk)),
                  pl.BlockSpec((3, nz), lambda i: (0, 0)),
                  pl.BlockSpec((1, nz), lambda i: (0, 0))],
        out_specs=[pl.BlockSpec((tm, HYENA_CH), out_map), pl.BlockSpec((tm, HYENA_CH), out_map)],
        out_shape=[jax.ShapeDtypeStruct((seq_len, batch * HYENA_CH), BF16)] * 2,
        compiler_params=_cparams(("parallel",)),
        name="hyena_shortconv",
    )(proj, proj, proj, short_w, short_b)


def _filter_kernel(feat_ref, w1_ref, b1_ref, w2_ref, b2_ref, w3_ref, b3_ref, w4_ref, fq_ref,
                   hp_ref, hm_ref, *, seq_len):
    i = pl.program_id(0)
    tl = feat_ref.shape[0]
    fq = fq_ref[...]
    dot = lambda a, b: jnp.dot(a, b, preferred_element_type=F32, precision=HIGHEST)
    h = jnp.sin(fq * (dot(feat_ref[...], w1_ref[...]) + b1_ref[...]))
    h = jnp.sin(fq * (dot(h, w2_ref[...]) + b2_ref[...]))
    h = jnp.sin(fq * (dot(h, w3_ref[...]) + b3_ref[...]))
    h = dot(h, w4_ref[...])
    c = HYENA_CH
    max_decay = math.log(HYENA_TARGET) / HYENA_FAST_DECAY
    min_decay = math.log(HYENA_TARGET) / HYENA_SLOW_DECAY
    ch = lax.broadcasted_iota(jnp.int32, (tl, c), 1).astype(F32)
    deltas = min_decay + ch * ((max_decay - min_decay) / (c - 1))
    row = lax.broadcasted_iota(jnp.int32, (tl, c), 0) + i * tl
    t = row.astype(F32) * (1.0 / (seq_len - 1))
    window = jnp.exp(-t * jnp.abs(deltas))
    h_f = h[:, 0:c] * window
    h_b = jnp.where(row == 0, 0.0, h[:, c:2 * c] * window)
    hp_ref[...] = (h_f + h_b).astype(BF16)
    hm_ref[...] = (h_f - h_b).astype(BF16)


def hyena_filter_pair(feats, w1p, b1, w2, b2, w3, b3, w4, fq):
    seq_len = feats.shape[0]
    tl = min(512, seq_len)
    full = lambda a: pl.BlockSpec(a.shape, lambda i: (0,) * a.ndim)
    kern = functools.partial(_filter_kernel, seq_len=seq_len)
    return pl.pallas_call(
        kern,
        grid=(seq_len // tl,),
        in_specs=[pl.BlockSpec((tl, feats.shape[1]), lambda i: (i, 0)),
                  full(w1p), full(b1), full(w2), full(b2), full(w3), full(b3), full(w4), full(fq)],
        out_specs=[pl.BlockSpec((tl, HYENA_CH), lambda i: (i, 0))] * 2,
        out_shape=[jax.ShapeDtypeStruct((seq_len, HYENA_CH), BF16)] * 2,
        compiler_params=_cparams(("parallel",)),
        name="hyena_filter",
    )(feats, w1p, b1, w2, b2, w3, b3, w4, fq)


def _dft_fwd_kernel(c_ref, s_ref, u_ref, a_ref, b_ref):
    u = u_ref[...]
    a_ref[...] = jnp.dot(c_ref[...], u, preferred_element_type=F32)
    b_ref[...] = jnp.dot(s_ref[...], u, preferred_element_type=F32)


def dft_forward(cs, ss, u_ext):
    n, cols = u_ext.shape
    tf, tn = min(512, n), 1024
    return pl.pallas_call(
        _dft_fwd_kernel,
        grid=(cols // tn, n // tf),
        in_specs=[pl.BlockSpec((tf, n), lambda c, f: (f, 0)),
                  pl.BlockSpec((tf, n), lambda c, f: (f, 0)),
                  pl.BlockSpec((n, tn), lambda c, f: (0, c))],
        out_specs=[pl.BlockSpec((tf, tn), lambda c, f: (f, c))] * 2,
        out_shape=[jax.ShapeDtypeStruct((n, cols), F32)] * 2,
        compiler_params=_cparams(("parallel", "parallel")),
        name="dft_forward",
    )(cs, ss, u_ext)


def _spectral_mul_kernel(au_ref, bu_ref, ap_ref, bp_ref, am_ref, bm_ref, hc_ref, hs_ref,
                         yre_ref, yim_ref, *, scale):
    hc, hs = hc_ref[...], hs_ref[...]
    k_re = hc * ap_ref[...] + hs * bp_ref[...]
    k_im = hs * am_ref[...] - hc * bm_ref[...]
    a, b = au_ref[...], bu_ref[...]
    yre_ref[...] = ((a * k_re + b * k_im) * scale).astype(BF16)
    yim_ref[...] = ((a * k_im - b * k_re) * scale).astype(BF16)


def spectral_mul(a, b, half_cos, half_sin, *, batch):
    n = a.shape[0]
    c = HYENA_CH
    tf = min(512, n)
    kern = functools.partial(_spectral_mul_kernel, scale=1.0 / n)
    ucol = lambda f, bi: (f, bi)
    return pl.pallas_call(
        kern,
        grid=(n // tf, batch),
        in_specs=[pl.BlockSpec((tf, c), ucol), pl.BlockSpec((tf, c), ucol),
                  pl.BlockSpec((tf, c), lambda f, bi: (f, batch)),
                  pl.BlockSpec((tf, c), lambda f, bi: (f, batch)),
                  pl.BlockSpec((tf, c), lambda f, bi: (f, batch + 1)),
                  pl.BlockSpec((tf, c), lambda f, bi: (f, batch + 1)),
                  pl.BlockSpec((tf, 1), lambda f, bi: (f, 0)),
                  pl.BlockSpec((tf, 1), lambda f, bi: (f, 0))],
        out_specs=[pl.BlockSpec((tf, c), ucol)] * 2,
        out_shape=[jax.ShapeDtypeStruct((n, batch * c), BF16)] * 2,
        compiler_params=_cparams(("parallel", "parallel")),
        name="hyena_spectral_mul",
    )(a, b, a, b, a, b, half_cos, half_sin)


def _dft_inv_kernel(c_ref, s_ref, yre_ref, yim_ref, u_ref, x0_ref, bias_ref, o_ref):
    y = (jnp.dot(c_ref[...], yre_ref[...], preferred_element_type=F32)
         - jnp.dot(s_ref[...], yim_ref[...], preferred_element_type=F32))
    u = u_ref[...].astype(F32)
    o_ref[...] = (x0_ref[...].astype(F32) * (y + u * bias_ref[...])).astype(BF16)


def dft_inverse(cs, ss, yre, yim, u, x0, bias, *, batch):
    n = cs.shape[0]
    c = HYENA_CH
    tt = min(512, n)
    col = lambda bi, t: (0, bi)
    tile = lambda bi, t: (t, bi)
    return pl.pallas_call(
        _dft_inv_kernel,
        grid=(batch, n // tt),
        in_specs=[pl.BlockSpec((tt, n), lambda bi, t: (t, 0)),
                  pl.BlockSpec((tt, n), lambda bi, t: (t, 0)),
                  pl.BlockSpec((n, c), col), pl.BlockSpec((n, c), col),
                  pl.BlockSpec((tt, c), tile), pl.BlockSpec((tt, c), tile),
                  pl.BlockSpec((1, c), lambda bi, t: (0, 0))],
        out_specs=pl.BlockSpec((tt, c), lambda bi, t: (bi * (n // tt) + t, 0)),
        out_shape=jax.ShapeDtypeStruct((batch * n, c), BF16),
        compiler_params=_cparams(("parallel", "parallel")),
        name="dft_inverse",
    )(cs, ss, yre, yim, u, x0, bias)


def _dft_table_kernel(cg_ref, sg_ref, cd_ref, sd_ref, c_ref, s_ref):
    cg, sg, cd, sd = cg_ref[...], sg_ref[...], cd_ref[0], sd_ref[0]
    c_ref[...] = (cg * cd - sg * sd).astype(BF16)
    s_ref[...] = (sg * cd + cg * sd).astype(BF16)


def _shifted_dft_tables(n):
    tf = min(256, n)
    theta = 2.0 * math.pi / (8 * n)
    odd_s = 2 * jnp.arange(n, dtype=jnp.int32)[None, :] + 1
    odd_i = 2 * jnp.arange(tf, dtype=jnp.int32)[:, None] + 1
    gamma = ((odd_i * odd_s) % (8 * n)).astype(F32) * theta
    f0 = jnp.arange(n // tf, dtype=jnp.int32)[:, None] * tf
    delta = ((2 * f0 * odd_s) % (8 * n)).astype(F32)[:, None, :] * theta
    whole = pl.BlockSpec((tf, n), lambda i: (0, 0))
    row = pl.BlockSpec((1, 1, n), lambda i: (i, 0, 0))
    cs, ss = pl.pallas_call(
        _dft_table_kernel,
        grid=(n // tf,),
        in_specs=[whole, whole, row, row],
        out_specs=[pl.BlockSpec((tf, n), lambda i: (i, 0))] * 2,
        out_shape=[jax.ShapeDtypeStruct((n, n), BF16)] * 2,
        compiler_params=_cparams(("parallel",)),
        name="dft_tables",
    )(jnp.cos(gamma), jnp.sin(gamma), jnp.cos(delta), jnp.sin(delta))
    half = (2 * jnp.arange(n, dtype=jnp.int32) + 1).astype(F32)[:, None] * (math.pi / (4 * n))
    return cs, ss, jnp.cos(half), jnp.sin(half)


def _filter_features(seq_len):
    t = jnp.linspace(0.0, 1.0, seq_len, dtype=F32)[:, None]
    bands = (HYENA_EMB - 1) // 2
    ang = 2.0 * math.pi * jnp.arange(seq_len, dtype=F32)[:, None] / seq_len
    fr = jnp.linspace(1e-4, bands - 1, bands, dtype=F32)[None, :]
    feats = jnp.concatenate([t, jnp.cos(fr * ang), -jnp.sin(fr * ang)], axis=-1)
    return jnp.pad(feats, ((0, 0), (0, LANES - HYENA_EMB)))


def hyena_long_conv(proj, hp, *, row0, batch, seq_len, z_blk):
    u, x0 = hyena_shortconv(proj, hp["short_w"], hp["short_b"], row0=row0, batch=batch,
                            seq_len=seq_len, z_blk=z_blk)
    f_p, f_m = hyena_filter_pair(_filter_features(seq_len), hp["w1"], hp["b1"], hp["w2"], hp["b2"],
                                 hp["w3"], hp["b3"], hp["w4"], hp["freq"])
    cs, ss, half_cos, half_sin = _shifted_dft_tables(seq_len)
    a, b = dft_forward(cs, ss, jnp.concatenate([u, f_p, f_m], axis=1))
    yre, yim = spectral_mul(a, b, half_cos, half_sin, batch=batch)
    return dft_inverse(cs, ss, yre, yim, u, x0, hp["bias"], batch=batch)


def _outproj_kernel(ma_ref, mb_ref, x_ref, mod_ref, nw_ref, w_ref, rw_ref,
                    xo_ref, h_ref, lt_ref):
    half = ma_ref.shape[1]
    y = (jnp.dot(ma_ref[...], w_ref[0:half], preferred_element_type=F32)
         + jnp.dot(mb_ref[...], w_ref[half:2 * half], preferred_element_type=F32))
    mod = mod_ref[0]
    x = x_ref[...] + mod[2:3] * y
    xo_ref[...] = x
    h = _modulated_norm(x, mod, nw_ref[...], 3, 4)
    h_ref[...] = h.astype(BF16)
    h_hi = h.astype(BF16)
    h_lo = (h - h_hi.astype(F32)).astype(BF16)
    nt = lambda a, b: lax.dot_general(a, b, (((1,), (1,)), ((), ())), preferred_element_type=F32)
    lt_ref[...] = nt(rw_ref[0], h_hi) + (nt(rw_ref[0], h_lo) + nt(rw_ref[1], h_hi))


def outproj(mix_a, a_blk, mix_b, b_blk, x_all, mods, nw, w_out, router_wt, *,
            n_rows, t_lat, seq, tm=512):
    d = x_all.shape[1]
    half = d // 2
    return pl.pallas_call(
        _outproj_kernel,
        grid=(n_rows // tm,),
        in_specs=[pl.BlockSpec((tm, half), lambda i: (i, a_blk)),
                  pl.BlockSpec((tm, half), lambda i: (i, b_blk)),
                  pl.BlockSpec((tm, d), lambda i: (i, 0)),
                  pl.BlockSpec((1, 6, d), lambda i: (_seg_of_tile(i, tm, t_lat, seq), 0, 0)),
                  pl.BlockSpec((1, d), lambda i: (0, 0)),
                  pl.BlockSpec((d, d), lambda i: (0, 0)),
                  pl.BlockSpec((2, N_EXPERTS, d), lambda i: (0, 0, 0))],
        out_specs=[pl.BlockSpec((tm, d), lambda i: (i, 0)),
                   pl.BlockSpec((tm, d), lambda i: (i, 0)),
                   pl.BlockSpec((N_EXPERTS, tm), lambda i: (0, i))],
        out_shape=[jax.ShapeDtypeStruct((n_rows, d), F32),
                   jax.ShapeDtypeStruct((n_rows, d), BF16),
                   jax.ShapeDtypeStruct((N_EXPERTS, n_rows), F32)],
        compiler_params=_cparams(("parallel",)),
        name="outproj",
    )(mix_a, mix_b, x_all, mods, nw, w_out, router_wt)


def _first_max(x, idx, sentinel):
    m = jnp.max(x, axis=0, keepdims=True)
    first = jnp.min(jnp.where(x == m, idx, sentinel), axis=0, keepdims=True)
    return m, idx == first


def _router_kernel(lt_ref, bias_ref, eye_ref, before_ref, below_ref, lists_ref, tok_ref, pc_ref):
    tn = lt_ref.shape[1]
    scores = jax.nn.sigmoid(lt_ref[...])
    sel = scores + bias_ref[...]
    neg = -jnp.inf
    in_grp = lax.broadcasted_iota(jnp.int32, (GROUP_SIZE, tn), 0)
    gscore = []
    for g in range(N_GROUPS):
        x = sel[g * GROUP_SIZE:(g + 1) * GROUP_SIZE]
        m1, hit = _first_max(x, in_grp, GROUP_SIZE)
        gscore.append(m1 + jnp.max(jnp.where(hit, neg, x), axis=0, keepdims=True))
    rows = []
    for g in range(N_GROUPS):
        beaten = jnp.zeros((1, tn), jnp.int32)
        for o in range(N_GROUPS):
            if o != g:
                wins = (gscore[o] >= gscore[g]) if o < g else (gscore[o] > gscore[g])
                beaten = beaten + wins.astype(jnp.int32)
        keep = jnp.broadcast_to(beaten < TOPK_GROUPS, (GROUP_SIZE, tn))
        rows.append(jnp.where(keep, sel[g * GROUP_SIZE:(g + 1) * GROUP_SIZE], neg))
    cand = jnp.concatenate(rows, axis=0)
    eidx = lax.broadcasted_iota(jnp.int32, cand.shape, 0)
    chosen = jnp.zeros(cand.shape, jnp.bool_)
    hits = []
    for _ in range(TOP_K):
        _, hit = _first_max(cand, eidx, N_EXPERTS)
        hits.append(hit)
        chosen = jnp.logical_or(chosen, hit)
        cand = jnp.where(hit, neg, cand)
    w = jnp.where(chosen, scores, 0.0)
    gates_t = w / jnp.sum(w, axis=0, keepdims=True) * ROUTED_SCALE
    chosen_f = jnp.where(chosen, 1.0, 0.0)
    rank = jnp.dot(chosen_f.astype(BF16), before_ref[...], preferred_element_type=F32)
    count = jnp.sum(chosen_f, axis=1, keepdims=True)
    pc = jnp.floor((count + (ROW_UNIT - 1)) * (1.0 / ROW_UNIT)) * ROW_UNIT
    pc_lanes = jnp.broadcast_to(pc, (N_EXPERTS, LANES))
    start = jnp.dot(below_ref[...], pc_lanes, preferred_element_type=F32,
                    precision=HIGHEST)[:, 0:1]
    slot = start + rank
    pick = lambda hit, v: jnp.sum(jnp.where(hit, v, 0.0), axis=0, keepdims=True)
    lists = jnp.concatenate([pick(h, slot) for h in hits] + [pick(h, gates_t) for h in hits], axis=0)
    lists_ref[...] = lists
    pc_ref[0] = pc_lanes
    padded = jnp.concatenate([lists, jnp.zeros((LANES - 2 * TOP_K, tn), F32)], axis=0)
    tok_ref[...] = lax.dot_general(eye_ref[...], padded, (((1,), (1,)), ((), ())),
                                   preferred_element_type=F32, precision=HIGHEST)


def router(logits, router_bias):
    e, t = logits.shape
    tn = MOE_TILE
    n_tiles = t // tn
    eye = jnp.eye(tn, dtype=F32)
    tok_i = jnp.arange(tn, dtype=jnp.int32)
    before = (tok_i[:, None] < tok_i[None, :]).astype(BF16)
    exp_i = jnp.arange(e, dtype=jnp.int32)
    below = (exp_i[None, :] < exp_i[:, None]).astype(F32)
    const = lambda shape: pl.BlockSpec(shape, lambda i: (0,) * len(shape))
    return pl.pallas_call(
        _router_kernel,
        grid=(n_tiles,),
        in_specs=[pl.BlockSpec((e, tn), lambda i: (0, i)),
                  const((e, 1)), const((tn, tn)), const((tn, tn)), const((e, e))],
        out_specs=[pl.BlockSpec((2 * TOP_K, tn), lambda i: (0, i)),
                   pl.BlockSpec((tn, LANES), lambda i: (i, 0)),
                   pl.BlockSpec((1, e, LANES), lambda i: (i, 0, 0))],
        out_shape=[jax.ShapeDtypeStruct((2 * TOP_K, t), F32),
                   jax.ShapeDtypeStruct((t, LANES), F32),
                   jax.ShapeDtypeStruct((n_tiles, e, LANES), F32)],
        compiler_params=_cparams(("parallel",)),
        name="router",
    )(logits, router_bias.reshape(e, 1), eye, before, below)


def moe_plan(pc, n_tiles):
    e = pc.shape[1]
    assert n_tiles >= e
    cap_rows = _sorted_capacity(n_tiles)
    run_end = jnp.cumsum(pc, axis=1)
    run_start = run_end - pc
    total = jnp.sum(pc, axis=0)
    total_al = ((total + EXPERT_ROWS - 1) // EXPERT_ROWS) * EXPERT_ROWS
    range_end = jnp.cumsum(total_al)
    range_start = range_end - total_al
    run_row = range_start[None, :] + jnp.cumsum(pc, axis=0) - pc
    unit_row = jnp.arange(N_UNITS, dtype=jnp.int32) * ROW_UNIT
    owner = jnp.sum((run_end[:, None, :] <= unit_row[None, :, None]).astype(jnp.int32), axis=2)
    owner_hot = (owner[:, :, None] == jnp.arange(e, dtype=jnp.int32)).astype(jnp.int32)
    dst = jnp.sum(owner_hot * (run_row - run_start)[:, None, :], axis=2) + unit_row[None, :]
    used = owner < e
    dump = cap_rows + (jnp.arange(n_tiles, dtype=jnp.int32) % 2)[:, None] * DUMP_ROWS
    unit_dst = jnp.where(used, dst, dump + unit_row[None, :]).astype(jnp.int32)
    unit_src = jnp.where(used, dst, 0).astype(jnp.int32)
    pad_off = jnp.arange(PAD_UNITS, dtype=jnp.int32)[None, :] * ROW_UNIT
    pad_row = (range_start + total)[:, None] + pad_off
    pad_row = jnp.concatenate([jnp.where(pad_row < range_end[:, None], pad_row, -1),
                               jnp.full((n_tiles - e, PAD_UNITS), -1, jnp.int32)], axis=0)
    pad_dst = jnp.where(pad_row >= 0, pad_row, dump + SLOTS + pad_off).astype(jnp.int32)
    tile_end = range_end // EXPERT_ROWS
    tile_expert = jnp.sum((tile_end[None, :] <= jnp.arange(cap_rows // EXPERT_ROWS,
                                                           dtype=jnp.int32)[:, None])
                          .astype(jnp.int32), axis=1)
    tile_expert = jnp.minimum(tile_expert, e - 1).astype(jnp.int32)
    used_rows = run_end[:, -1].astype(jnp.int32)
    return unit_dst, unit_src, pad_dst, used_rows, tile_expert, tile_end[-1:].astype(jnp.int32)


def _sorted_capacity(n_tiles):
    worst_rows = n_tiles * (MOE_TILE * TOP_K + N_EXPERTS * (ROW_UNIT - 1)) + N_EXPERTS * (EXPERT_ROWS - 1)
    return -(-worst_rows // EXPERT_ROWS) * EXPERT_ROWS


DUMP_ROWS = SLOTS + PAD_UNITS * ROW_UNIT


UNITS_PER_CHUNK = SLOT_CHUNK // ROW_UNIT
MAIN_SLOTS = SLOTS - SLOT_CHUNK
assert MAIN_SLOTS >= MOE_TILE * TOP_K


def _dispatch_kernel(unit_ref, pad_ref, used_ref, h_ref, lists_ref, xs_ref, loc_ref, zero_ref, sems):
    i = pl.program_id(0)
    last = pl.num_programs(0) - 1
    slot = i % 2
    tn = h_ref.shape[0]

    @pl.when(i == 0)
    def _():
        zero_ref[...] = jnp.zeros_like(zero_ref)

    def chunk(c0):
        srow = (lax.broadcasted_iota(jnp.int32, (SLOT_CHUNK, tn), 0) + c0).astype(F32)
        sel = srow == lists_ref[0:1, :]
        for k in range(1, TOP_K):
            sel = jnp.logical_or(sel, srow == lists_ref[k:k + 1, :])
        loc_ref[slot, c0:c0 + SLOT_CHUNK, :] = jnp.dot(
            jnp.where(sel, 1.0, 0.0).astype(BF16), h_ref[...],
            preferred_element_type=F32).astype(BF16)
        for u in range(c0 // ROW_UNIT, c0 // ROW_UNIT + UNITS_PER_CHUNK):
            row = pl.multiple_of(unit_ref[i, u], ROW_UNIT)
            pltpu.make_async_copy(loc_ref.at[slot, pl.ds(u * ROW_UNIT, ROW_UNIT)],
                                  xs_ref.at[pl.ds(row, ROW_UNIT)], sems.at[slot]).start()

    for c0 in range(0, MAIN_SLOTS, SLOT_CHUNK):
        chunk(c0)

    @pl.when(used_ref[i] > MAIN_SLOTS)
    def _():
        chunk(MAIN_SLOTS)

    for p in range(PAD_UNITS):
        row = pl.multiple_of(pad_ref[i, p], ROW_UNIT)
        pltpu.make_async_copy(zero_ref, xs_ref.at[pl.ds(row, ROW_UNIT)], sems.at[slot]).start()

    def wait_all(s, tile):
        pltpu.make_async_copy(loc_ref.at[s, pl.ds(0, MAIN_SLOTS)],
                              xs_ref.at[pl.ds(0, MAIN_SLOTS)], sems.at[s]).wait()

        @pl.when(used_ref[tile] > MAIN_SLOTS)
        def _():
            pltpu.make_async_copy(loc_ref.at[s, pl.ds(MAIN_SLOTS, SLOT_CHUNK)],
                                  xs_ref.at[pl.ds(0, SLOT_CHUNK)], sems.at[s]).wait()
        pltpu.make_async_copy(loc_ref.at[s, pl.ds(0, PAD_UNITS * ROW_UNIT)],
                              xs_ref.at[pl.ds(0, PAD_UNITS * ROW_UNIT)], sems.at[s]).wait()

    @pl.when(i > 0)
    def _():
        wait_all(1 - slot, i - 1)

    @pl.when(i == last)
    def _():
        wait_all(slot, i)


def moe_dispatch(h, lists, unit_dst, pad_dst, used_rows, *, n_tiles):
    d = h.shape[1]
    cap_rows = _sorted_capacity(n_tiles) + 2 * DUMP_ROWS
    return pl.pallas_call(
        _dispatch_kernel,
        grid_spec=pltpu.PrefetchScalarGridSpec(
            num_scalar_prefetch=3,
            grid=(n_tiles,),
            in_specs=[pl.BlockSpec((MOE_TILE, d), lambda i, *_: (i, 0)),
                      pl.BlockSpec((2 * TOP_K, MOE_TILE), lambda i, *_: (0, i))],
            out_specs=pl.BlockSpec(memory_space=pl.ANY),
            scratch_shapes=[pltpu.VMEM((2, SLOTS, d), BF16), pltpu.VMEM((ROW_UNIT, d), BF16),
                            pltpu.SemaphoreType.DMA((2,))]),
        out_shape=jax.ShapeDtypeStruct((cap_rows, d), BF16),
        compiler_params=_cparams(("arbitrary",)),
        name="moe_dispatch",
    )(unit_dst, pad_dst, used_rows, h, lists)


X_RING = 3


def _expert_kernel(te_ref, nt_ref, xs_ref, wg_ref, wu_ref, wd_ref, y_ref, x_buf, wg_bf, wu_bf,
                   wd_bf, sems):
    t = pl.program_id(0)
    n_live = nt_ref[0]
    live = t < n_live
    tc = jnp.minimum(t, jnp.maximum(n_live - 1, 0))
    new_expert = jnp.logical_or(t == 0, te_ref[tc] != te_ref[jnp.maximum(tc - 1, 0)])
    rows = x_buf.shape[1]

    def x_copy(tile):
        s = tile % X_RING
        return pltpu.make_async_copy(xs_ref.at[pl.ds(pl.multiple_of(tile * rows, rows), rows)],
                                     x_buf.at[s], sems.at[s])

    for k in range(X_RING - 1):
        @pl.when(jnp.logical_and(t == 0, k < n_live))
        def _():
            x_copy(k).start()

    @pl.when(t + (X_RING - 1) < n_live)
    def _():
        x_copy(t + (X_RING - 1)).start()

    @pl.when(jnp.logical_and(live, new_expert))
    def _():
        wg_bf[...] = wg_ref[0, 0].astype(BF16)
        wu_bf[...] = wu_ref[0, 0].astype(BF16)
        wd_bf[...] = wd_ref[0, 0].astype(BF16)

    @pl.when(live)
    def _():
        x_copy(t).wait()
        x = x_buf[t % X_RING]
        act = (_silu(jnp.dot(x, wg_bf[...], preferred_element_type=F32))
               * jnp.dot(x, wu_bf[...], preferred_element_type=F32))
        y_ref[...] = jnp.dot(act.astype(BF16), wd_bf[...], preferred_element_type=F32).astype(BF16)


def moe_experts(xs, w_gate, w_up, w_down, layer, tile_expert, n_row_tiles):
    d = xs.shape[1]
    cap_rows = xs.shape[0] - 2 * DUMP_ROWS
    ff = w_gate.shape[3]
    last_live = lambda nt: jnp.maximum(nt[0] - 1, 0)
    row_map = lambda t, te, nt: (jnp.minimum(t, last_live(nt)), 0)
    w_map = lambda t, te, nt: (layer, te[jnp.minimum(t, last_live(nt))], 0, 0)
    return pl.pallas_call(
        _expert_kernel,
        grid_spec=pltpu.PrefetchScalarGridSpec(
            num_scalar_prefetch=2,
            grid=(cap_rows // EXPERT_ROWS,),
            in_specs=[pl.BlockSpec(memory_space=pl.ANY),
                      pl.BlockSpec((1, 1, d, ff), w_map), pl.BlockSpec((1, 1, d, ff), w_map),
                      pl.BlockSpec((1, 1, ff, d), w_map)],
            out_specs=pl.BlockSpec((EXPERT_ROWS, d), row_map),
            scratch_shapes=[pltpu.VMEM((X_RING, EXPERT_ROWS, d), BF16),
                            pltpu.VMEM((d, ff), BF16), pltpu.VMEM((d, ff), BF16),
                            pltpu.VMEM((ff, d), BF16), pltpu.SemaphoreType.DMA((X_RING,))]),
        out_shape=jax.ShapeDtypeStruct((cap_rows, d), BF16),
        compiler_params=_cparams(("arbitrary",)),
        name="moe_experts",
    )(tile_expert, n_row_tiles, xs, w_gate, w_up, w_down)


def _combine_kernel(unit_ref, used_ref, ys_ref, tok_ref, h_ref, x_ref, mod_ref, fnw_ref, sg_ref,
                    su_ref, sd_ref, o_ref, loc_ref, acc_ref, sems, *, final_norm):
    i = pl.program_id(0)
    last = pl.num_programs(0) - 1
    slot = i % 2
    tn = h_ref.shape[0]
    def fetch(tile, s):
        def units(u0, u1):
            for u in range(u0, u1):
                row = pl.multiple_of(unit_ref[tile, u], ROW_UNIT)
                pltpu.make_async_copy(ys_ref.at[pl.ds(row, ROW_UNIT)],
                                      loc_ref.at[s, pl.ds(u * ROW_UNIT, ROW_UNIT)], sems.at[s]).start()
        units(0, MAIN_SLOTS // ROW_UNIT)

        @pl.when(used_ref[tile] > MAIN_SLOTS)
        def _():
            units(MAIN_SLOTS // ROW_UNIT, N_UNITS)

    @pl.when(i == 0)
    def _():
        fetch(0, 0)

    @pl.when(i < last)
    def _():
        fetch(i + 1, 1 - slot)

    h = h_ref[...]
    act = (_silu(jnp.dot(h, sg_ref[...], preferred_element_type=F32))
           * jnp.dot(h, su_ref[...], preferred_element_type=F32))
    acc = jnp.dot(act.astype(BF16), sd_ref[...], preferred_element_type=F32)
    tail = used_ref[i] > MAIN_SLOTS
    pltpu.make_async_copy(ys_ref.at[pl.ds(0, MAIN_SLOTS)], loc_ref.at[slot, pl.ds(0, MAIN_SLOTS)],
                          sems.at[slot]).wait()

    @pl.when(tail)
    def _():
        pltpu.make_async_copy(ys_ref.at[pl.ds(0, SLOT_CHUNK)],
                              loc_ref.at[slot, pl.ds(MAIN_SLOTS, SLOT_CHUNK)], sems.at[slot]).wait()

    tok = tok_ref[...]

    def chunk(c0):
        scol = (lax.broadcasted_iota(jnp.int32, (tn, SLOT_CHUNK), 1) + c0).astype(F32)
        q = jnp.zeros((tn, SLOT_CHUNK), F32)
        for k in range(TOP_K):
            q = jnp.where(scol == tok[:, k:k + 1], tok[:, TOP_K + k:TOP_K + k + 1], q)
        return jnp.dot(q.astype(BF16), loc_ref[slot, c0:c0 + SLOT_CHUNK, :],
                       preferred_element_type=F32)

    for c0 in range(0, MAIN_SLOTS, SLOT_CHUNK):
        acc = acc + chunk(c0)
    acc_ref[...] = acc

    @pl.when(tail)
    def _():
        acc_ref[...] += chunk(MAIN_SLOTS)

    x = x_ref[...] + mod_ref[0][5:6] * acc_ref[...]
    if final_norm:
        x = _rms(x) * fnw_ref[...]
    o_ref[...] = x


def moe_combine(ys, tok, h, x_all, mods, fnw, sh_gate, sh_up, sh_down, unit_dst, used_rows, *,
                n_tiles, t_lat, seq, final_norm):
    d = h.shape[1]
    tn = MOE_TILE
    const = lambda a: pl.BlockSpec(a.shape, lambda i, *_: (0,) * a.ndim)
    kern = functools.partial(_combine_kernel, final_norm=final_norm)
    return pl.pallas_call(
        kern,
        grid_spec=pltpu.PrefetchScalarGridSpec(
            num_scalar_prefetch=2,
            grid=(n_tiles,),
            in_specs=[pl.BlockSpec(memory_space=pl.ANY),
                      pl.BlockSpec((tn, LANES), lambda i, *_: (i, 0)),
                      pl.BlockSpec((tn, d), lambda i, *_: (i, 0)),
                      pl.BlockSpec((tn, d), lambda i, *_: (i, 0)),
                      pl.BlockSpec((1, 6, d),
                                   lambda i, *_: (_seg_of_tile(i, tn, t_lat, seq), 0, 0)),
                      const(fnw), const(sh_gate), const(sh_up), const(sh_down)],
            out_specs=pl.BlockSpec((tn, d), lambda i, *_: (i, 0)),
            scratch_shapes=[pltpu.VMEM((2, SLOTS, d), BF16), pltpu.VMEM((tn, d), F32),
                            pltpu.SemaphoreType.DMA((2,))]),
        out_shape=jax.ShapeDtypeStruct((n_tiles * tn, d), F32),
        compiler_params=_cparams(("arbitrary",)),
        name="moe_combine",
    )(unit_dst, used_rows, ys, tok, h, x_all, mods, fnw, sh_gate, sh_up, sh_down)


def moe_block(h2, logits, router_bias, x_mid, mods, fnw, w_gate, w_up, w_down,
              sh_gate, sh_up, sh_down, *, layer, t_lat, seq, final_norm):
    n_tiles = h2.shape[0] // MOE_TILE
    lists, tok, pc = router(logits, router_bias)
    unit_dst, unit_src, pad_dst, used_rows, tile_expert, n_row_tiles = moe_plan(
        pc[:, :, 0].astype(jnp.int32), n_tiles)
    xs = moe_dispatch(h2, lists, unit_dst, pad_dst, used_rows, n_tiles=n_tiles)
    ys = moe_experts(xs, w_gate, w_up, w_down, layer, tile_expert, n_row_tiles)
    return moe_combine(ys, tok, h2, x_mid, mods, fnw, sh_gate.astype(BF16), sh_up.astype(BF16),
                       sh_down.astype(BF16), unit_src, used_rows, n_tiles=n_tiles, t_lat=t_lat,
                       seq=seq, final_norm=final_norm)


def _rope_tables(seq, tm):
    half = RET_DK // 2
    pos = jnp.arange(seq, dtype=jnp.int32)
    row = (pos // GRID_W).astype(F32)
    col = (pos % GRID_W).astype(F32)
    inv = ROPE_BASE ** (-jnp.arange(0, half, 2, dtype=F32) / half)
    a_row, a_col = row[:, None] * inv[None, :], col[:, None] * inv[None, :]
    cos_h = jnp.concatenate([jnp.cos(a_row)] * 2 + [jnp.cos(a_col)] * 2, axis=-1)
    sin_h = jnp.concatenate([-jnp.sin(a_row), jnp.sin(a_row), -jnp.sin(a_col), jnp.sin(a_col)], -1)
    reps = 2 * RET_HEADS
    cos_t = jnp.concatenate([jnp.tile(cos_h, (1, reps)), jnp.ones((tm, reps * RET_DK), F32)], 0)
    sin_t = jnp.concatenate([jnp.tile(sin_h, (1, reps)), jnp.zeros((tm, reps * RET_DK), F32)], 0)
    return cos_t, sin_t


def _rope_partner_columns():
    quarter = RET_DK // 4
    idx = jnp.arange(2 * RET_HEADS * RET_DK, dtype=jnp.int32)
    within = idx % (2 * quarter)
    return jnp.where(within < quarter, idx + quarter, idx - quarter)


def kernel(x, c, ctx, c_ctx, ada_w, ada_b, norm1_w, norm2_w, ev_w_in, ev_short_w, ev_short_b, ev_filt_w1, ev_filt_b1, ev_filt_w2, ev_filt_b2, ev_filt_w3, ev_filt_b3, ev_filt_w4, ev_filt_freq, ev_hyena_bias, ev_w_out, od_w_in, od_gate_w1_f, od_gate_w2_f, od_gate_b_f, od_gate_w1_b, od_gate_w2_b, od_gate_b_b, od_norm_w, od_w_out, router_w, router_bias, exp_w_gate, exp_w_up, exp_w_down, sh_w_gate, sh_w_up, sh_w_down, final_norm_w):
    batch, seq, d = x.shape
    ctx_len = ctx.shape[1]
    depth = ada_w.shape[0]
    t_lat, t_ctx = batch * seq, batch * ctx_len
    t_all = t_lat + t_ctx
    tm = 512

    x_all = jnp.concatenate([x.reshape(t_lat, d), ctx.reshape(t_ctx, d)], axis=0)
    cond8 = jnp.concatenate([c_ctx[None, :], c, jnp.zeros((8 - 1 - batch, d), F32)], axis=0)
    mods_all = adaln_rows(cond8, ada_w, ada_b).reshape(depth, 8, 6, d)

    for i in range(depth):
        last = i == depth - 1
        j = i // 2
        mods = mods_all[i]
        nw1, nw2 = norm1_w[i][None, :], norm2_w[i][None, :]
        if i % 2 == 0:
            qk = 2 * RET_HEADS * RET_DK
            k_scale = jnp.concatenate([jnp.ones((qk // 2,), F32),
                                       jnp.full((qk // 2,), RET_DK ** -0.5, F32)])
            w_in = ev_w_in[j]
            w_qk = w_in[:, :qk] * k_scale
            w_ext = jnp.concatenate([w_qk, w_in[:, qk:], w_qk[:, _rope_partner_columns()]],
                                    axis=1).astype(BF16)
            cos_t, sin_t = _rope_tables(seq, tm)
            proj = inproj_even(x_all, mods, nw1, w_ext, cos_t, sin_t, t_lat=t_lat, seq=seq, tm=tm)
            log_g = [math.log1p(-2.0 ** (-5.0 - h)) for h in range(RET_HEADS)]
            mix_a = bidir_scan(proj, None, None, jnp.ones((1, RET_DV), F32),
                               batch=batch, seq=seq, ctx_len=ctx_len, heads=RET_HEADS, dk=RET_DK,
                               dv=RET_DV, chunk=RET_CHUNK, q_blk=0, k_blk=1, v_blk=1, g_blk=2,
                               log_decay_f=log_g, log_decay_b=log_g[::-1])
            hp = dict(short_w=ev_short_w[j], short_b=ev_short_b[j][None, :],
                      w1=jnp.pad(ev_filt_w1[j], ((0, LANES - HYENA_EMB), (0, 0))),
                      b1=ev_filt_b1[j][None, :], w2=ev_filt_w2[j], b2=ev_filt_b2[j][None, :],
                      w3=ev_filt_w3[j], b3=ev_filt_b3[j][None, :], w4=ev_filt_w4[j],
                      freq=ev_filt_freq[j][None, :], bias=ev_hyena_bias[j][None, :])
            hy_lat = hyena_long_conv(proj, hp, row0=0, batch=batch, seq_len=seq, z_blk=1)
            hy_ctx = hyena_long_conv(proj, hp, row0=t_lat, batch=batch, seq_len=ctx_len, z_blk=1)
            mix_b = jnp.concatenate([hy_lat, hy_ctx], axis=0)
            a_blk, b_blk = 0, 0
            w_out = ev_w_out[j].astype(BF16)
        else:
            kd = GLA_HEADS * GLA_DK
            pad_cols = LANES - 2 * GLA_RANK
            w_ext = jnp.concatenate([od_w_in[j], od_gate_w1_f[j], od_gate_w1_b[j],
                                     jnp.zeros((d, pad_cols), F32)], axis=1).astype(BF16)
            w2 = jnp.concatenate(
                [jnp.pad(od_gate_w2_f[j], ((0, LANES - GLA_RANK), (0, 0))),
                 jnp.pad(od_gate_w2_b[j], ((GLA_RANK, LANES - 2 * GLA_RANK), (0, 0)))], axis=1)
            w2_hi = w2.astype(BF16)
            w2_split = jnp.stack([w2_hi, (w2 - w2_hi.astype(F32)).astype(BF16)])
            b2 = jnp.concatenate([od_gate_b_f[j], od_gate_b_b[j]])[None, :]
            proj, la_f, la_b = inproj_odd(x_all, mods, nw1, w_ext, w2_split, b2,
                                          t_lat=t_lat, seq=seq, tm=tm)
            mix_a = bidir_scan(proj, la_f, la_b, od_norm_w[j][None, :],
                               batch=batch, seq=seq, ctx_len=ctx_len, heads=GLA_HEADS, dk=GLA_DK,
                               dv=GLA_DV, chunk=GLA_CHUNK, q_blk=0, k_blk=1, v_blk=1, g_blk=2)
            mix_b = mix_a
            a_blk, b_blk = 0, 1
            w_out = od_w_out[j].astype(BF16)

        n_rows = t_lat if last else t_all
        rw_t = router_w[i].T
        rw_hi = rw_t.astype(BF16)
        rw_split = jnp.stack([rw_hi, (rw_t - rw_hi.astype(F32)).astype(BF16)])
        x_mid, h2, logits = outproj(mix_a, a_blk, mix_b, b_blk, x_all, mods, nw2, w_out,
                                    rw_split, n_rows=n_rows, t_lat=t_lat, seq=seq, tm=tm)
        x_all = moe_block(h2, logits, router_bias[i], x_mid, mods, final_norm_w[None, :],
                          exp_w_gate, exp_w_up, exp_w_down,
                          sh_w_gate[i], sh_w_up[i], sh_w_down[i],
                          layer=i, t_lat=t_lat, seq=seq, final_norm=last)
    return x_all[:t_lat].reshape(batch, seq, d)
```

```python
import functools
import math

import jax
import jax.numpy as jnp
from jax import lax
from jax.experimental import pallas as pl
from jax.experimental.pallas import tpu as pltpu

F32 = jnp.float32
BF16 = jnp.bfloat16
HIGHEST = lax.Precision.HIGHEST

D_MODEL = 1024
GRID_W = 64
NORM_EPS = 1e-6
RET_HEADS, RET_DK, RET_DV, RET_CHUNK = 4, 64, 128, 128
ROPE_BASE = 10000.0
HYENA_CH, HYENA_EMB = 512, 33
HYENA_FAST_DECAY, HYENA_SLOW_DECAY, HYENA_TARGET = 0.3, 1.5, 1e-2
GLA_HEADS, GLA_DK, GLA_DV, GLA_RANK, GLA_TAU, GLA_CHUNK = 4, 128, 256, 16, 16.0, 64
N_EXPERTS, TOP_K, N_GROUPS, TOPK_GROUPS = 64, 8, 8, 4
GROUP_SIZE = N_EXPERTS // N_GROUPS
EXPERT_FF = 256
ROUTED_SCALE = 2.5

LANES = 128
SCAN_ROWS = 256
MOE_TILE = 256
ROW_UNIT = 16
EXPERT_ROWS = 1024
SLOT_CHUNK = 512
SUB_SLOTS = 256
SLOTS = -(-(MOE_TILE * TOP_K + N_EXPERTS * (ROW_UNIT - 1) + ROW_UNIT) // SLOT_CHUNK) * SLOT_CHUNK
N_UNITS = SLOTS // ROW_UNIT
PAD_UNITS = EXPERT_ROWS // ROW_UNIT
VMEM_LIMIT = 56 * 1024 * 1024


def _cparams(sem):
    return pltpu.CompilerParams(dimension_semantics=sem, vmem_limit_bytes=VMEM_LIMIT)


def _rms(x):
    return x * lax.rsqrt(jnp.mean(x * x, axis=-1, keepdims=True) + NORM_EPS)


def _silu(x):
    return x * jax.nn.sigmoid(x)


def _adaln_kernel(c_ref, w_ref, b_ref, o_ref):
    c = c_ref[...]
    o_ref[0] = jnp.dot(_silu(c), w_ref[0], preferred_element_type=F32,
                       precision=HIGHEST) + b_ref[0]


def adaln_rows(cond8, ada_w, ada_b):
    depth, d, n = ada_w.shape
    tn = 512
    return pl.pallas_call(
        _adaln_kernel,
        grid=(depth, n // tn),
        in_specs=[pl.BlockSpec((8, d), lambda l, j: (0, 0)),
                  pl.BlockSpec((1, d, tn), lambda l, j: (l, 0, j)),
                  pl.BlockSpec((1, 1, tn), lambda l, j: (l, 0, j))],
        out_specs=pl.BlockSpec((1, 8, tn), lambda l, j: (l, 0, j)),
        out_shape=jax.ShapeDtypeStruct((depth, 8, n), F32),
        compiler_params=_cparams(("parallel", "parallel")),
        name="adaln_rows",
    )(cond8, ada_w, ada_b.reshape(depth, 1, n))


def _modulated_norm(x, mod, nw, shift_row, scale_row):
    return _rms(x) * nw * (1.0 + mod[scale_row:scale_row + 1]) + mod[shift_row:shift_row + 1]


def _inproj_even_kernel(x_ref, mod_ref, nw_ref, w_ref, cos_ref, sin_ref, o_ref):
    h = _modulated_norm(x_ref[...], mod_ref[0], nw_ref[...], 0, 1).astype(BF16)
    n_main = o_ref.shape[1]
    qk = jnp.dot(h, w_ref[:, 0:512], preferred_element_type=F32)
    qk_sw = jnp.dot(h, w_ref[:, n_main:n_main + 512], preferred_element_type=F32)
    o_ref[:, 0:512] = (qk * cos_ref[...] + qk_sw * sin_ref[...]).astype(BF16)
    for c0 in range(512, n_main, 512):
        o_ref[:, c0:c0 + 512] = jnp.dot(h, w_ref[:, c0:c0 + 512],
                                        preferred_element_type=F32).astype(BF16)


def _inproj_odd_kernel(x_ref, mod_ref, nw_ref, w_ref, w2_ref, b2_ref, o_ref, laf_ref, lab_ref):
    h = _modulated_norm(x_ref[...], mod_ref[0], nw_ref[...], 0, 1).astype(BF16)
    n_main = o_ref.shape[1]
    kd = GLA_HEADS * GLA_DK
    q = jnp.dot(h, w_ref[:, 0:kd], preferred_element_type=F32)
    o_ref[:, 0:kd] = (q * (GLA_DK ** -0.5)).astype(BF16)
    for c0 in range(kd, n_main, 512):
        o_ref[:, c0:c0 + 512] = jnp.dot(h, w_ref[:, c0:c0 + 512],
                                        preferred_element_type=F32).astype(BF16)
    low = jnp.dot(h, w_ref[:, n_main:n_main + LANES], preferred_element_type=F32)

    low_hi = low.astype(BF16)
    low_lo = (low - low_hi.astype(F32)).astype(BF16)
    z = jnp.dot(jnp.concatenate([low_hi, low_lo, low_hi], axis=1), w2_ref[...],
                preferred_element_type=F32) + b2_ref[...]
    la = (jnp.minimum(z, 0.0) - jnp.log(1.0 + jnp.exp(-jnp.abs(z)))) * (1.0 / GLA_TAU)
    laf_ref[...] = la[:, 0:kd]
    lab_ref[...] = la[:, kd:2 * kd]


def _seg_of_tile(i, tm, t_lat, seq):
    return jnp.where(i < t_lat // tm, 1 + (i * tm) // seq, 0)


def inproj_even(x_all, mods, nw, w_ext, cos_t, sin_t, *, t_lat, seq, tm=512):
    t_all, d = x_all.shape
    n_ext = w_ext.shape[1]
    n_main = n_ext - 512
    n_lat_tiles, pos_tiles = t_lat // tm, seq // tm

    def pos_map(i):
        return (jnp.where(i < n_lat_tiles, i % pos_tiles, pos_tiles), 0)

    return pl.pallas_call(
        _inproj_even_kernel,
        grid=(t_all // tm,),
        in_specs=[pl.BlockSpec((tm, d), lambda i: (i, 0)),
                  pl.BlockSpec((1, 6, d), lambda i: (_seg_of_tile(i, tm, t_lat, seq), 0, 0)),
                  pl.BlockSpec((1, d), lambda i: (0, 0)),
                  pl.BlockSpec((d, n_ext), lambda i: (0, 0)),
                  pl.BlockSpec((tm, 512), pos_map),
                  pl.BlockSpec((tm, 512), pos_map)],
        out_specs=pl.BlockSpec((tm, n_main), lambda i: (i, 0)),
        out_shape=jax.ShapeDtypeStruct((t_all, n_main), BF16),
        compiler_params=_cparams(("parallel",)),
        name="inproj_even",
    )(x_all, mods, nw, w_ext, cos_t, sin_t)


def inproj_odd(x_all, mods, nw, w_ext, w2, b2, *, t_lat, seq, tm=512):
    t_all, d = x_all.shape
    n_main = w_ext.shape[1] - LANES
    kd = GLA_HEADS * GLA_DK
    full = lambda shape: pl.BlockSpec(shape, lambda i: (0,) * len(shape))
    return pl.pallas_call(
        _inproj_odd_kernel,
        grid=(t_all // tm,),
        in_specs=[pl.BlockSpec((tm, d), lambda i: (i, 0)),
                  pl.BlockSpec((1, 6, d), lambda i: (_seg_of_tile(i, tm, t_lat, seq), 0, 0)),
                  full((1, d)), full(w_ext.shape), full(w2.shape), full(b2.shape)],
        out_specs=[pl.BlockSpec((tm, n_main), lambda i: (i, 0)),
                   pl.BlockSpec((tm, kd), lambda i: (i, 0)),
                   pl.BlockSpec((tm, kd), lambda i: (i, 0))],
        out_shape=[jax.ShapeDtypeStruct((t_all, n_main), BF16),
                   jax.ShapeDtypeStruct((t_all, kd), F32),
                   jax.ShapeDtypeStruct((t_all, kd), F32)],
        compiler_params=_cparams(("parallel",)),
        name="inproj_odd",
    )(x_all, mods, nw, w_ext, w2, b2)


def _scan_kernel(*refs, heads, dk, dv, chunk, gated, reverse, log_decay):
    it = iter(refs)
    q_ref, k_ref, v_ref = next(it), next(it), next(it)
    la_ref = next(it) if gated else None
    if reverse:
        oprev_ref, gate_ref, nw_ref = next(it), next(it), next(it)
    out_ref = next(it)
    state_ref = next(it)
    dec_ref = None if gated else next(it)

    hpg = LANES // dk
    groups = heads // hpg
    rows = q_ref.shape[0]
    n_chunks = rows // chunk
    j = pl.program_id(1)

    row_i = lax.broadcasted_iota(jnp.int32, (rows, rows), 0)
    col_i = lax.broadcasted_iota(jnp.int32, (rows, rows), 1)
    same_chunk = (row_i // chunk) == (col_i // chunk)
    keep = jnp.logical_and(same_chunk, (col_i >= row_i) if reverse else (row_i >= col_i))

    @pl.when(j == 0)
    def _init():
        state_ref[...] = jnp.zeros_like(state_ref)
        if not gated:
            pos = lax.broadcasted_iota(jnp.int32, (rows, LANES), 0) % chunk
            steps = ((chunk - pos) if reverse else (pos + 1)).astype(F32)
            lane = lax.broadcasted_iota(jnp.int32, (rows, LANES), 1)
            for g in range(groups):
                lg = jnp.zeros((rows, LANES), F32)
                for a in range(hpg):
                    lg = jnp.where(lane // dk == a, log_decay[g * hpg + a], lg)
                logb = steps * lg
                b_end = float(chunk) * lg
                dec_ref[g, 0] = jnp.exp(logb)
                dec_ref[g, 1] = jnp.exp(-logb)
                dec_ref[g, 2] = jnp.exp(b_end - logb)
                dec_ref[g, 3] = jnp.exp(b_end)

    if gated:
        tri = jnp.where(keep, 1.0, 0.0).astype(BF16)
        la = la_ref[...]
        la_hi = la.astype(BF16)
        la_lo = (la - la_hi.astype(F32)).astype(BF16)
        logb_all = (jnp.dot(tri, la_hi, preferred_element_type=F32)
                    + jnp.dot(tri, la_lo, preferred_element_type=F32))

    lane1 = lax.broadcasted_iota(jnp.int32, (1, LANES), 1)
    order = range(n_chunks - 1, -1, -1) if reverse else range(n_chunks)
    end_row = lambda c: c * chunk if reverse else (c + 1) * chunk - 1
    for g in range(groups):
        ksl = slice(g * LANES, (g + 1) * LANES)
        qg = q_ref[:, ksl].astype(F32)
        kg = k_ref[:, ksl].astype(F32)
        if gated:
            logb = logb_all[:, ksl]
            ends = [logb[end_row(c):end_row(c) + 1] for c in range(n_chunks)]
            b_end = jnp.concatenate([jnp.broadcast_to(e, (chunk, LANES)) for e in ends], axis=0)
            e_q, e_k, e_s = jnp.exp(logb), jnp.exp(-logb), jnp.exp(b_end - logb)
            e_e = [jnp.exp(e) for e in ends]
        else:
            e_q, e_k, e_s = dec_ref[g, 0], dec_ref[g, 1], dec_ref[g, 2]
            e_e = [dec_ref[g, 3][0:1]] * n_chunks
        qd = qg * e_q
        kd_ = (kg * e_k).astype(BF16)
        ks = (kg * e_s).astype(BF16)
        for a in range(hpg):
            h = g * hpg + a
            qa = (jnp.where(lane1 // dk == a, qd, 0.0) if hpg > 1 else qd).astype(BF16)
            vh = v_ref[:, h * dv:(h + 1) * dv]
            s = lax.dot_general(qa, kd_, (((1,), (1,)), ((), ())), preferred_element_type=F32)
            o_intra = jnp.dot(jnp.where(keep, s, 0.0).astype(BF16), vh,
                              preferred_element_type=F32)
            st = state_ref[h]
            for c in order:
                rs = slice(c * chunk, (c + 1) * chunk)
                o = o_intra[rs] + lax.dot_general(qa[rs], st.astype(BF16), (((1,), (1,)), ((), ())),
                                                  preferred_element_type=F32)
                st = e_e[c] * st + lax.dot_general(vh[rs], ks[rs], (((0,), (0,)), ((), ())),
                                                   preferred_element_type=F32)
                osl = (rs, slice(h * dv, (h + 1) * dv))
                if reverse:
                    o = _rms(o + oprev_ref[osl]) * nw_ref[...]
                    out_ref[osl] = (o * _silu(gate_ref[osl].astype(F32))).astype(BF16)
                else:
                    out_ref[osl] = o
            state_ref[h] = st


def bidir_scan(proj, la_f, la_b, norm_w, *, batch, seq, ctx_len, heads, dk, dv, chunk,
               q_blk, k_blk, v_blk, g_blk, log_decay_f=None, log_decay_b=None):
    t_all = proj.shape[0]
    gated = la_f is not None
    hk, hv = heads * dk, heads * dv
    rb = SCAN_ROWS
    lat_blocks = seq // rb
    ctx_base = (batch * seq) // rb
    assert ctx_len == rb

    def rows_fwd(b, j):
        return jnp.where(j == 0, ctx_base + b, b * lat_blocks + j - 1)

    def rows_bwd(b, j):
        return jnp.where(j == 0, ctx_base + b, b * lat_blocks + lat_blocks - j)

    outs = None
    for reverse, rows_of, la, ld in ((False, rows_fwd, la_f, log_decay_f),
                                     (True, rows_bwd, la_b, log_decay_b)):
        spec = lambda w, cb: pl.BlockSpec((rb, w), lambda b, j, cb=cb: (rows_of(b, j), cb))
        in_specs = [spec(hk, q_blk), spec(hk, k_blk), spec(hv, v_blk)]
        args = [proj, proj, proj]
        if gated:
            in_specs.append(spec(hk, 0))
            args.append(la)
        if reverse:
            in_specs += [spec(hv, 0), spec(hv, g_blk), pl.BlockSpec((1, dv), lambda b, j: (0, 0))]
            args += [outs, proj, norm_w]
        scratch = [pltpu.VMEM((heads, dv, LANES), F32)]
        if not gated:
            scratch.append(pltpu.VMEM((hk // LANES, 4, rb, LANES), F32))
        kern = functools.partial(_scan_kernel, heads=heads, dk=dk, dv=dv, chunk=chunk,
                                 gated=gated, reverse=reverse, log_decay=ld)
        outs = pl.pallas_call(
            kern,
            grid=(batch, lat_blocks + 1),
            in_specs=in_specs,
            out_specs=spec(hv, 0),
            out_shape=jax.ShapeDtypeStruct((t_all, hv), BF16 if reverse else F32),
            scratch_shapes=scratch,
            compiler_params=_cparams(("parallel", "arbitrary")),
            name="scan_bwd" if reverse else "scan_fwd",
        )(*args)
    return outs


def _shortconv_kernel(z_ref, zp_ref, zn_ref, w_ref, b_ref, u_ref, x0_ref, *, tiles_per_seq):
    i = pl.program_id(0)
    tm = z_ref.shape[0]
    z = z_ref[...].astype(F32)
    first = (i % tiles_per_seq) == 0
    last = (i % tiles_per_seq) == tiles_per_seq - 1
    halo = zp_ref.shape[0]
    prev_row = jnp.where(first, 0.0, zp_ref[halo - 1:halo, :].astype(F32))
    next_row = jnp.where(last, 0.0, zn_ref[0:1, :].astype(F32))
    row = lax.broadcasted_iota(jnp.int32, z.shape, 0)
    z_prev = jnp.where(row == 0, prev_row, pltpu.roll(z, 1, 0))
    z_next = jnp.where(row == tm - 1, next_row, pltpu.roll(z, tm - 1, 0))
    y = w_ref[0:1] * z_prev + w_ref[1:2] * z + w_ref[2:3] * z_next + b_ref[...]
    c = HYENA_CH
    x0_ref[...] = y[:, 0:c].astype(BF16)
    u_ref[...] = (y[:, c:2 * c] * y[:, 2 * c:3 * c]).astype(BF16)


def hyena_shortconv(proj, short_w, short_b, *, row0, batch, seq_len, z_blk, tm=256):
    halo = 16
    tiles_per_seq = seq_len // tm
    n_tiles = batch * tiles_per_seq
    t0, h_per_tile = row0 // tm, tm // halo
    nz = 3 * HYENA_CH
    n_halo_blocks = proj.shape[0] // halo
    out_map = lambda i: (i % tiles_per_seq, i // tiles_per_seq)
    kern = functools.partial(_shortconv_kernel, tiles_per_seq=tiles_per_seq)
    return pl.pallas_call(
        kern,
        grid=(n_tiles,),
        in_specs=[pl.BlockSpec((tm, nz), lambda i: (t0 + i, z_blk)),
                  pl.BlockSpec((halo, nz),
                               lambda i: (jnp.maximum((t0 + i) * h_per_tile - 1, 0), z_blk)),
                  pl.BlockSpec((halo, nz),
                               lambda i: (jnp.minimum((t0 + i + 1) * h_per_tile,
                                                      n_halo_blocks - 1), z_blk)),
                  pl.BlockSpec((3, nz), lambda i: (0, 0)),
                  pl.BlockSpec((1, nz), lambda i: (0, 0))],
        out_specs=[pl.BlockSpec((tm, HYENA_CH), out_map), pl.BlockSpec((tm, HYENA_CH), out_map)],
        out_shape=[jax.ShapeDtypeStruct((seq_len, batch * HYENA_CH), BF16)] * 2,
        compiler_params=_cparams(("parallel",)),
        name="hyena_shortconv",
    )(proj, proj, proj, short_w, short_b)


def _filter_kernel(feat_ref, w1_ref, b1_ref, w2_ref, b2_ref, w3_ref, b3_ref, w4_ref, fq_ref,
                   hp_ref, hm_ref, *, seq_len):
    i = pl.program_id(0)
    tl = feat_ref.shape[0]
    fq = fq_ref[...]
    dot = lambda a, b: jnp.dot(a, b, preferred_element_type=F32, precision=HIGHEST)
    h = jnp.sin(fq * (dot(feat_ref[...], w1_ref[...]) + b1_ref[...]))
    h = jnp.sin(fq * (dot(h, w2_ref[...]) + b2_ref[...]))
    h = jnp.sin(fq * (dot(h, w3_ref[...]) + b3_ref[...]))
    h = dot(h, w4_ref[...])
    c = HYENA_CH
    max_decay = math.log(HYENA_TARGET) / HYENA_FAST_DECAY
    min_decay = math.log(HYENA_TARGET) / HYENA_SLOW_DECAY
    ch = lax.broadcasted_iota(jnp.int32, (tl, c), 1).astype(F32)
    deltas = min_decay + ch * ((max_decay - min_decay) / (c - 1))
    row = lax.broadcasted_iota(jnp.int32, (tl, c), 0) + i * tl
    t = row.astype(F32) * (1.0 / (seq_len - 1))
    window = jnp.exp(-t * jnp.abs(deltas))
    h_f = h[:, 0:c] * window
    h_b = jnp.where(row == 0, 0.0, h[:, c:2 * c] * window)
    hp_ref[...] = (h_f + h_b).astype(BF16)
    hm_ref[...] = (h_f - h_b).astype(BF16)


def hyena_filter_pair(feats, w1p, b1, w2, b2, w3, b3, w4, fq):
    seq_len = feats.shape[0]
    tl = min(512, seq_len)
    full = lambda a: pl.BlockSpec(a.shape, lambda i: (0,) * a.ndim)
    kern = functools.partial(_filter_kernel, seq_len=seq_len)
    return pl.pallas_call(
        kern,
        grid=(seq_len // tl,),
        in_specs=[pl.BlockSpec((tl, feats.shape[1]), lambda i: (i, 0)),
                  full(w1p), full(b1), full(w2), full(b2), full(w3), full(b3), full(w4), full(fq)],
        out_specs=[pl.BlockSpec((tl, HYENA_CH), lambda i: (i, 0))] * 2,
        out_shape=[jax.ShapeDtypeStruct((seq_len, HYENA_CH), BF16)] * 2,
        compiler_params=_cparams(("parallel",)),
        name="hyena_filter",
    )(feats, w1p, b1, w2, b2, w3, b3, w4, fq)


def _dft_fwd_kernel(c_ref, s_ref, u_ref, a_ref, b_ref):
    u = u_ref[...]
    a_ref[...] = jnp.dot(c_ref[...], u, preferred_element_type=F32)
    b_ref[...] = jnp.dot(s_ref[...], u, preferred_element_type=F32)


def dft_forward(cs, ss, u_ext):
    n, cols = u_ext.shape
    tf, tn = min(512, n), 1024
    return pl.pallas_call(
        _dft_fwd_kernel,
        grid=(cols // tn, n // tf),
        in_specs=[pl.BlockSpec((tf, n), lambda c, f: (f, 0)),
                  pl.BlockSpec((tf, n), lambda c, f: (f, 0)),
                  pl.BlockSpec((n, tn), lambda c, f: (0, c))],
        out_specs=[pl.BlockSpec((tf, tn), lambda c, f: (f, c))] * 2,
        out_shape=[jax.ShapeDtypeStruct((n, cols), F32)] * 2,
        compiler_params=_cparams(("parallel", "parallel")),
        name="dft_forward",
    )(cs, ss, u_ext)


def _spectral_mul_kernel(au_ref, bu_ref, ap_ref, bp_ref, am_ref, bm_ref, hc_ref, hs_ref,
                         yre_ref, yim_ref, *, scale):
    hc, hs = hc_ref[...], hs_ref[...]
    k_re = hc * ap_ref[...] + hs * bp_ref[...]
    k_im = hs * am_ref[...] - hc * bm_ref[...]
    a, b = au_ref[...], bu_ref[...]
    yre_ref[...] = ((a * k_re + b * k_im) * scale).astype(BF16)
    yim_ref[...] = ((a * k_im - b * k_re) * scale).astype(BF16)


def spectral_mul(a, b, half_cos, half_sin, *, batch):
    n = a.shape[0]
    c = HYENA_CH
    tf = min(512, n)
    kern = functools.partial(_spectral_mul_kernel, scale=1.0 / n)
    ucol = lambda f, bi: (f, bi)
    return pl.pallas_call(
        kern,
        grid=(n // tf, batch),
        in_specs=[pl.BlockSpec((tf, c), ucol), pl.BlockSpec((tf, c), ucol),
                  pl.BlockSpec((tf, c), lambda f, bi: (f, batch)),
                  pl.BlockSpec((tf, c), lambda f, bi: (f, batch)),
                  pl.BlockSpec((tf, c), lambda f, bi: (f, batch + 1)),
                  pl.BlockSpec((tf, c), lambda f, bi: (f, batch + 1)),
                  pl.BlockSpec((tf, 1), lambda f, bi: (f, 0)),
                  pl.BlockSpec((tf, 1), lambda f, bi: (f, 0))],
        out_specs=[pl.BlockSpec((tf, c), ucol)] * 2,
        out_shape=[jax.ShapeDtypeStruct((n, batch * c), BF16)] * 2,
        compiler_params=_cparams(("parallel", "parallel")),
        name="hyena_spectral_mul",
    )(a, b, a, b, a, b, half_cos, half_sin)


def _dft_inv_kernel(c_ref, s_ref, yre_ref, yim_ref, u_ref, x0_ref, bias_ref, o_ref):
    y = (jnp.dot(c_ref[...], yre_ref[...], preferred_element_type=F32)
         - jnp.dot(s_ref[...], yim_ref[...], preferred_element_type=F32))
    u = u_ref[...].astype(F32)
    o_ref[...] = (x0_ref[...].astype(F32) * (y + u * bias_ref[...])).astype(BF16)


def dft_inverse(cs, ss, yre, yim, u, x0, bias, *, batch):
    n = cs.shape[0]
    c = HYENA_CH
    tt = min(512, n)
    col = lambda bi, t: (0, bi)
    tile = lambda bi, t: (t, bi)
    return pl.pallas_call(
        _dft_inv_kernel,
        grid=(batch, n // tt),
        in_specs=[pl.BlockSpec((tt, n), lambda bi, t: (t, 0)),
                  pl.BlockSpec((tt, n), lambda bi, t: (t, 0)),
                  pl.BlockSpec((n, c), col), pl.BlockSpec((n, c), col),
                  pl.BlockSpec((tt, c), tile), pl.BlockSpec((tt, c), tile),
                  pl.BlockSpec((1, c), lambda bi, t: (0, 0))],
        out_specs=pl.BlockSpec((tt, c), lambda bi, t: (bi * (n // tt) + t, 0)),
        out_shape=jax.ShapeDtypeStruct((batch * n, c), BF16),
        compiler_params=_cparams(("parallel", "parallel")),
        name="dft_inverse",
    )(cs, ss, yre, yim, u, x0, bias)


def _dft_table_kernel(cg_ref, sg_ref, cd_ref, sd_ref, c_ref, s_ref):
    cg, sg, cd, sd = cg_ref[...], sg_ref[...], cd_ref[0], sd_ref[0]
    c_ref[...] = (cg * cd - sg * sd).astype(BF16)
    s_ref[...] = (sg * cd + cg * sd).astype(BF16)


def _shifted_dft_tables(n):
    tf = min(256, n)
    theta = 2.0 * math.pi / (8 * n)
    odd_s = 2 * jnp.arange(n, dtype=jnp.int32)[None, :] + 1
    odd_i = 2 * jnp.arange(tf, dtype=jnp.int32)[:, None] + 1
    gamma = ((odd_i * odd_s) % (8 * n)).astype(F32) * theta
    f0 = jnp.arange(n // tf, dtype=jnp.int32)[:, None] * tf
    delta = ((2 * f0 * odd_s) % (8 * n)).astype(F32)[:, None, :] * theta
    whole = pl.BlockSpec((tf, n), lambda i: (0, 0))
    row = pl.BlockSpec((1, 1, n), lambda i: (i, 0, 0))
    cs, ss = pl.pallas_call(
        _dft_table_kernel,
        grid=(n // tf,),
        in_specs=[whole, whole, row, row],
        out_specs=[pl.BlockSpec((tf, n), lambda i: (i, 0))] * 2,
        out_shape=[jax.ShapeDtypeStruct((n, n), BF16)] * 2,
        compiler_params=_cparams(("parallel",)),
        name="dft_tables",
    )(jnp.cos(gamma), jnp.sin(gamma), jnp.cos(delta), jnp.sin(delta))
    half = (2 * jnp.arange(n, dtype=jnp.int32) + 1).astype(F32)[:, None] * (math.pi / (4 * n))
    return cs, ss, jnp.cos(half), jnp.sin(half)


def _filter_features(seq_len):
    t = jnp.linspace(0.0, 1.0, seq_len, dtype=F32)[:, None]
    bands = (HYENA_EMB - 1) // 2
    ang = 2.0 * math.pi * jnp.arange(seq_len, dtype=F32)[:, None] / seq_len
    fr = jnp.linspace(1e-4, bands - 1, bands, dtype=F32)[None, :]
    feats = jnp.concatenate([t, jnp.cos(fr * ang), -jnp.sin(fr * ang)], axis=-1)
    return jnp.pad(feats, ((0, 0), (0, LANES - HYENA_EMB)))


def hyena_long_conv(proj, hp, *, row0, batch, seq_len, z_blk):
    u, x0 = hyena_shortconv(proj, hp["short_w"], hp["short_b"], row0=row0, batch=batch,
                            seq_len=seq_len, z_blk=z_blk)
    f_p, f_m = hyena_filter_pair(_filter_features(seq_len), hp["w1"], hp["b1"], hp["w2"], hp["b2"],
                                 hp["w3"], hp["b3"], hp["w4"], hp["freq"])
    cs, ss, half_cos, half_sin = _shifted_dft_tables(seq_len)
    a, b = dft_forward(cs, ss, jnp.concatenate([u, f_p, f_m], axis=1))
    yre, yim = spectral_mul(a, b, half_cos, half_sin, batch=batch)
    return dft_inverse(cs, ss, yre, yim, u, x0, hp["bias"], batch=batch)


def _outproj_kernel(ma_ref, mb_ref, x_ref, mod_ref, nw_ref, w_ref, rw_ref,
                    xo_ref, h_ref, lt_ref):
    half = ma_ref.shape[1]
    y = (jnp.dot(ma_ref[...], w_ref[0:half], preferred_element_type=F32)
         + jnp.dot(mb_ref[...], w_ref[half:2 * half], preferred_element_type=F32))
    mod = mod_ref[0]
    x = x_ref[...] + mod[2:3] * y
    xo_ref[...] = x
    h = _modulated_norm(x, mod, nw_ref[...], 3, 4)
    h_ref[...] = h.astype(BF16)
    h_hi = h.astype(BF16)
    h_lo = (h - h_hi.astype(F32)).astype(BF16)
    nt = lambda a, b: lax.dot_general(a, b, (((1,), (1,)), ((), ())), preferred_element_type=F32)
    lt_ref[...] = nt(rw_ref[0], h_hi) + (nt(rw_ref[0], h_lo) + nt(rw_ref[1], h_hi))


def outproj(mix_a, a_blk, mix_b, b_blk, x_all, mods, nw, w_out, router_wt, *,
            n_rows, t_lat, seq, tm=512):
    d = x_all.shape[1]
    half = d // 2
    return pl.pallas_call(
        _outproj_kernel,
        grid=(n_rows // tm,),
        in_specs=[pl.BlockSpec((tm, half), lambda i: (i, a_blk)),
                  pl.BlockSpec((tm, half), lambda i: (i, b_blk)),
                  pl.BlockSpec((tm, d), lambda i: (i, 0)),
                  pl.BlockSpec((1, 6, d), lambda i: (_seg_of_tile(i, tm, t_lat, seq), 0, 0)),
                  pl.BlockSpec((1, d), lambda i: (0, 0)),
                  pl.BlockSpec((d, d), lambda i: (0, 0)),
                  pl.BlockSpec((2, N_EXPERTS, d), lambda i: (0, 0, 0))],
        out_specs=[pl.BlockSpec((tm, d), lambda i: (i, 0)),
                   pl.BlockSpec((tm, d), lambda i: (i, 0)),
                   pl.BlockSpec((N_EXPERTS, tm), lambda i: (0, i))],
        out_shape=[jax.ShapeDtypeStruct((n_rows, d), F32),
                   jax.ShapeDtypeStruct((n_rows, d), BF16),
                   jax.ShapeDtypeStruct((N_EXPERTS, n_rows), F32)],
        compiler_params=_cparams(("parallel",)),
        name="outproj",
    )(mix_a, mix_b, x_all, mods, nw, w_out, router_wt)


def _first_max(x, idx, sentinel):
    m = jnp.max(x, axis=0, keepdims=True)
    first = jnp.min(jnp.where(x == m, idx, sentinel), axis=0, keepdims=True)
    return m, idx == first


def _router_kernel(lt_ref, bias_ref, eye_ref, before_ref, below_ref, lists_ref, tok_ref, pc_ref):
    tn = lt_ref.shape[1]
    scores = jax.nn.sigmoid(lt_ref[...])
    sel = scores + bias_ref[...]
    neg = -jnp.inf
    in_grp = lax.broadcasted_iota(jnp.int32, (GROUP_SIZE, tn), 0)
    gscore = []
    for g in range(N_GROUPS):
        x = sel[g * GROUP_SIZE:(g + 1) * GROUP_SIZE]
        m1, hit = _first_max(x, in_grp, GROUP_SIZE)
        gscore.append(m1 + jnp.max(jnp.where(hit, neg, x), axis=0, keepdims=True))
    rows = []
    for g in range(N_GROUPS):
        beaten = jnp.zeros((1, tn), jnp.int32)
        for o in range(N_GROUPS):
            if o != g:
                wins = (gscore[o] >= gscore[g]) if o < g else (gscore[o] > gscore[g])
                beaten = beaten + wins.astype(jnp.int32)
        keep = jnp.broadcast_to(beaten < TOPK_GROUPS, (GROUP_SIZE, tn))
        rows.append(jnp.where(keep, sel[g * GROUP_SIZE:(g + 1) * GROUP_SIZE], neg))
    cand = jnp.concatenate(rows, axis=0)
    eidx = lax.broadcasted_iota(jnp.int32, cand.shape, 0)
    chosen = jnp.zeros(cand.shape, jnp.bool_)
    hits = []
    for _ in range(TOP_K):
        _, hit = _first_max(cand, eidx, N_EXPERTS)
        hits.append(hit)
        chosen = jnp.logical_or(chosen, hit)
        cand = jnp.where(hit, neg, cand)
    w = jnp.where(chosen, scores, 0.0)
    gates_t = w / jnp.sum(w, axis=0, keepdims=True) * ROUTED_SCALE
    chosen_f = jnp.where(chosen, 1.0, 0.0)
    rank = jnp.dot(chosen_f.astype(BF16), before_ref[...], preferred_element_type=F32)
    count = jnp.sum(chosen_f, axis=1, keepdims=True)
    pc = jnp.floor((count + (ROW_UNIT - 1)) * (1.0 / ROW_UNIT)) * ROW_UNIT
    pc_lanes = jnp.broadcast_to(pc, (N_EXPERTS, LANES))
    start = jnp.dot(below_ref[...], pc_lanes, preferred_element_type=F32,
                    precision=HIGHEST)[:, 0:1]
    slot = start + rank
    pick = lambda hit, v: jnp.sum(jnp.where(hit, v, 0.0), axis=0, keepdims=True)
    lists = jnp.concatenate([pick(h, slot) for h in hits] + [pick(h, gates_t) for h in hits], axis=0)
    lists_ref[...] = lists
    pc_ref[0] = pc_lanes
    padded = jnp.concatenate([lists, jnp.zeros((LANES - 2 * TOP_K, tn), F32)], axis=0)
    tok_ref[...] = lax.dot_general(eye_ref[...], padded, (((1,), (1,)), ((), ())),
                                   preferred_element_type=F32, precision=HIGHEST)


def router(logits, router_bias):
    e, t = logits.shape
    tn = MOE_TILE
    n_tiles = t // tn
    eye = jnp.eye(tn, dtype=F32)
    tok_i = jnp.arange(tn, dtype=jnp.int32)
    before = (tok_i[:, None] < tok_i[None, :]).astype(BF16)
    exp_i = jnp.arange(e, dtype=jnp.int32)
    below = (exp_i[None, :] < exp_i[:, None]).astype(F32)
    const = lambda shape: pl.BlockSpec(shape, lambda i: (0,) * len(shape))
    return pl.pallas_call(
        _router_kernel,
        grid=(n_tiles,),
        in_specs=[pl.BlockSpec((e, tn), lambda i: (0, i)),
                  const((e, 1)), const((tn, tn)), const((tn, tn)), const((e, e))],
        out_specs=[pl.BlockSpec((2 * TOP_K, tn), lambda i: (0, i)),
                   pl.BlockSpec((tn, LANES), lambda i: (i, 0)),
                   pl.BlockSpec((1, e, LANES), lambda i: (i, 0, 0))],
        out_shape=[jax.ShapeDtypeStruct((2 * TOP_K, t), F32),
                   jax.ShapeDtypeStruct((t, LANES), F32),
                   jax.ShapeDtypeStruct((n_tiles, e, LANES), F32)],
        compiler_params=_cparams(("parallel",)),
        name="router",
    )(logits, router_bias.reshape(e, 1), eye, before, below)


def moe_plan(pc, n_tiles):
    e = pc.shape[1]
    assert n_tiles >= e
    cap_rows = _sorted_capacity(n_tiles)
    run_end = jnp.cumsum(pc, axis=1)
    run_start = run_end - pc
    total = jnp.sum(pc, axis=0)
    total_al = ((total + EXPERT_ROWS - 1) // EXPERT_ROWS) * EXPERT_ROWS
    range_end = jnp.cumsum(total_al)
    range_start = range_end - total_al
    run_row = range_start[None, :] + jnp.cumsum(pc, axis=0) - pc
    unit_row = jnp.arange(N_UNITS, dtype=jnp.int32) * ROW_UNIT
    owner = jnp.sum((run_end[:, None, :] <= unit_row[None, :, None]).astype(jnp.int32), axis=2)
    owner_hot = (owner[:, :, None] == jnp.arange(e, dtype=jnp.int32)).astype(jnp.int32)
    dst = jnp.sum(owner_hot * (run_row - run_start)[:, None, :], axis=2) + unit_row[None, :]
    used = owner < e
    dump = cap_rows + (jnp.arange(n_tiles, dtype=jnp.int32) % 2)[:, None] * DUMP_ROWS
    unit_dst = jnp.where(used, dst, dump + unit_row[None, :]).astype(jnp.int32)
    unit_src = jnp.where(used, dst, 0).astype(jnp.int32)
    pad_off = jnp.arange(PAD_UNITS, dtype=jnp.int32)[None, :] * ROW_UNIT
    pad_row = (range_start + total)[:, None] + pad_off
    pad_row = jnp.concatenate([jnp.where(pad_row < range_end[:, None], pad_row, -1),
                               jnp.full((n_tiles - e, PAD_UNITS), -1, jnp.int32)], axis=0)
    pad_dst = jnp.where(pad_row >= 0, pad_row, dump + SLOTS + pad_off).astype(jnp.int32)
    tile_end = range_end // EXPERT_ROWS
    tile_expert = jnp.sum((tile_end[None, :] <= jnp.arange(cap_rows // EXPERT_ROWS,
                                                           dtype=jnp.int32)[:, None])
                          .astype(jnp.int32), axis=1)
    tile_expert = jnp.minimum(tile_expert, e - 1).astype(jnp.int32)
    used_rows = run_end[:, -1].astype(jnp.int32)
    return unit_dst, unit_src, pad_dst, used_rows, tile_expert, tile_end[-1:].astype(jnp.int32)


def _sorted_capacity(n_tiles):
    worst_rows = n_tiles * (MOE_TILE * TOP_K + N_EXPERTS * (ROW_UNIT - 1)) + N_EXPERTS * (EXPERT_ROWS - 1)
    return -(-worst_rows // EXPERT_ROWS) * EXPERT_ROWS


DUMP_ROWS = SLOTS + PAD_UNITS * ROW_UNIT


UNITS_PER_CHUNK = SLOT_CHUNK // ROW_UNIT
MAIN_SLOTS = SLOTS - SLOT_CHUNK
assert MAIN_SLOTS >= MOE_TILE * TOP_K


def _dispatch_kernel(unit_ref, pad_ref, used_ref, h_ref, lists_ref, xs_ref, loc_ref, zero_ref, sems):
    i = pl.program_id(0)
    last = pl.num_programs(0) - 1
    slot = i % 2
    tn = h_ref.shape[0]

    @pl.when(i == 0)
    def _():
        zero_ref[...] = jnp.zeros_like(zero_ref)

    rel_rows = lax.broadcasted_iota(jnp.int32, (SUB_SLOTS, tn), 0).astype(F32).astype(BF16)

    def chunk(c0):
        parts = []
        for b0 in range(c0, c0 + SLOT_CHUNK, SUB_SLOTS):
            p = jnp.zeros((SUB_SLOTS, tn), BF16)
            for k in range(TOP_K):
                hit = rel_rows == (lists_ref[k:k + 1, :] - float(b0)).astype(BF16)
                p = jnp.where(hit, jnp.ones_like(p), p)
            parts.append(p)
        loc_ref[slot, c0:c0 + SLOT_CHUNK, :] = jnp.dot(
            jnp.concatenate(parts, axis=0), h_ref[...],
            preferred_element_type=F32).astype(BF16)
        for u in range(c0 // ROW_UNIT, c0 // ROW_UNIT + UNITS_PER_CHUNK):
            row = pl.multiple_of(unit_ref[i, u], ROW_UNIT)
            pltpu.make_async_copy(loc_ref.at[slot, pl.ds(u * ROW_UNIT, ROW_UNIT)],
                                  xs_ref.at[pl.ds(row, ROW_UNIT)], sems.at[slot]).start()

    for c0 in range(0, MAIN_SLOTS, SLOT_CHUNK):
        chunk(c0)

    @pl.when(used_ref[i] > MAIN_SLOTS)
    def _():
        chunk(MAIN_SLOTS)

    for p in range(PAD_UNITS):
        row = pl.multiple_of(pad_ref[i, p], ROW_UNIT)
        pltpu.make_async_copy(zero_ref, xs_ref.at[pl.ds(row, ROW_UNIT)], sems.at[slot]).start()

    def wait_all(s, tile):
        pltpu.make_async_copy(loc_ref.at[s, pl.ds(0, MAIN_SLOTS)],
                              xs_ref.at[pl.ds(0, MAIN_SLOTS)], sems.at[s]).wait()

        @pl.when(used_ref[tile] > MAIN_SLOTS)
        def _():
            pltpu.make_async_copy(loc_ref.at[s, pl.ds(MAIN_SLOTS, SLOT_CHUNK)],
                                  xs_ref.at[pl.ds(0, SLOT_CHUNK)], sems.at[s]).wait()
        pltpu.make_async_copy(loc_ref.at[s, pl.ds(0, PAD_UNITS * ROW_UNIT)],
                              xs_ref.at[pl.ds(0, PAD_UNITS * ROW_UNIT)], sems.at[s]).wait()

    @pl.when(i > 0)
    def _():
        wait_all(1 - slot, i - 1)

    @pl.when(i == last)
    def _():
        wait_all(slot, i)


def moe_dispatch(h, lists, unit_dst, pad_dst, used_rows, *, n_tiles):
    d = h.shape[1]
    cap_rows = _sorted_capacity(n_tiles) + 2 * DUMP_ROWS
    return pl.pallas_call(
        _dispatch_kernel,
        grid_spec=pltpu.PrefetchScalarGridSpec(
            num_scalar_prefetch=3,
            grid=(n_tiles,),
            in_specs=[pl.BlockSpec((MOE_TILE, d), lambda i, *_: (i, 0)),
                      pl.BlockSpec((2 * TOP_K, MOE_TILE), lambda i, *_: (0, i))],
            out_specs=pl.BlockSpec(memory_space=pl.ANY),
            scratch_shapes=[pltpu.VMEM((2, SLOTS, d), BF16), pltpu.VMEM((ROW_UNIT, d), BF16),
                            pltpu.SemaphoreType.DMA((2,))]),
        out_shape=jax.ShapeDtypeStruct((cap_rows, d), BF16),
        compiler_params=_cparams(("arbitrary",)),
        name="moe_dispatch",
    )(unit_dst, pad_dst, used_rows, h, lists)


X_RING = 3


def _expert_kernel(te_ref, nt_ref, xs_ref, wg_ref, wu_ref, wd_ref, y_ref, x_buf, wg_bf, wu_bf,
                   wd_bf, sems):
    t = pl.program_id(0)
    n_live = nt_ref[0]
    live = t < n_live
    tc = jnp.minimum(t, jnp.maximum(n_live - 1, 0))
    new_expert = jnp.logical_or(t == 0, te_ref[tc] != te_ref[jnp.maximum(tc - 1, 0)])
    rows = x_buf.shape[1]

    def x_copy(tile):
        s = tile % X_RING
        return pltpu.make_async_copy(xs_ref.at[pl.ds(pl.multiple_of(tile * rows, rows), rows)],
                                     x_buf.at[s], sems.at[s])

    for k in range(X_RING - 1):
        @pl.when(jnp.logical_and(t == 0, k < n_live))
        def _():
            x_copy(k).start()

    @pl.when(t + (X_RING - 1) < n_live)
    def _():
        x_copy(t + (X_RING - 1)).start()

    @pl.when(jnp.logical_and(live, new_expert))
    def _():
        wg_bf[...] = wg_ref[0, 0].astype(BF16)
        wu_bf[...] = wu_ref[0, 0].astype(BF16)
        wd_bf[...] = wd_ref[0, 0].astype(BF16)

    @pl.when(live)
    def _():
        x_copy(t).wait()
        x = x_buf[t % X_RING]
        act = (_silu(jnp.dot(x, wg_bf[...], preferred_element_type=F32))
               * jnp.dot(x, wu_bf[...], preferred_element_type=F32))
        y_ref[...] = jnp.dot(act.astype(BF16), wd_bf[...], preferred_element_type=F32).astype(BF16)


def moe_experts(xs, w_gate, w_up, w_down, layer, tile_expert, n_row_tiles):
    d = xs.shape[1]
    cap_rows = xs.shape[0] - 2 * DUMP_ROWS
    ff = w_gate.shape[3]
    last_live = lambda nt: jnp.maximum(nt[0] - 1, 0)
    row_map = lambda t, te, nt: (jnp.minimum(t, last_live(nt)), 0)
    w_map = lambda t, te, nt: (layer, te[jnp.minimum(t, last_live(nt))], 0, 0)
    return pl.pallas_call(
        _expert_kernel,
        grid_spec=pltpu.PrefetchScalarGridSpec(
            num_scalar_prefetch=2,
            grid=(cap_rows // EXPERT_ROWS,),
            in_specs=[pl.BlockSpec(memory_space=pl.ANY),
                      pl.BlockSpec((1, 1, d, ff), w_map), pl.BlockSpec((1, 1, d, ff), w_map),
                      pl.BlockSpec((1, 1, ff, d), w_map)],
            out_specs=pl.BlockSpec((EXPERT_ROWS, d), row_map),
            scratch_shapes=[pltpu.VMEM((X_RING, EXPERT_ROWS, d), BF16),
                            pltpu.VMEM((d, ff), BF16), pltpu.VMEM((d, ff), BF16),
                            pltpu.VMEM((ff, d), BF16), pltpu.SemaphoreType.DMA((X_RING,))]),
        out_shape=jax.ShapeDtypeStruct((cap_rows, d), BF16),
        compiler_params=_cparams(("arbitrary",)),
        name="moe_experts",
    )(tile_expert, n_row_tiles, xs, w_gate, w_up, w_down)


def _combine_kernel(unit_ref, used_ref, ys_ref, tok_ref, h_ref, x_ref, mod_ref, fnw_ref, sg_ref,
                    su_ref, sd_ref, o_ref, loc_ref, acc_ref, sems, *, final_norm):
    i = pl.program_id(0)
    last = pl.num_programs(0) - 1
    slot = i % 2
    tn = h_ref.shape[0]
    def fetch(tile, s):
        def units(u0, u1):
            for u in range(u0, u1):
                row = pl.multiple_of(unit_ref[tile, u], ROW_UNIT)
                pltpu.make_async_copy(ys_ref.at[pl.ds(row, ROW_UNIT)],
                                      loc_ref.at[s, pl.ds(u * ROW_UNIT, ROW_UNIT)], sems.at[s]).start()
        units(0, MAIN_SLOTS // ROW_UNIT)

        @pl.when(used_ref[tile] > MAIN_SLOTS)
        def _():
            units(MAIN_SLOTS // ROW_UNIT, N_UNITS)

    @pl.when(i == 0)
    def _():
        fetch(0, 0)

    @pl.when(i < last)
    def _():
        fetch(i + 1, 1 - slot)

    h = h_ref[...]
    act = (_silu(jnp.dot(h, sg_ref[...], preferred_element_type=F32))
           * jnp.dot(h, su_ref[...], preferred_element_type=F32))
    acc = jnp.dot(act.astype(BF16), sd_ref[...], preferred_element_type=F32)
    tail = used_ref[i] > MAIN_SLOTS
    pltpu.make_async_copy(ys_ref.at[pl.ds(0, MAIN_SLOTS)], loc_ref.at[slot, pl.ds(0, MAIN_SLOTS)],
                          sems.at[slot]).wait()

    @pl.when(tail)
    def _():
        pltpu.make_async_copy(ys_ref.at[pl.ds(0, SLOT_CHUNK)],
                              loc_ref.at[slot, pl.ds(MAIN_SLOTS, SLOT_CHUNK)], sems.at[slot]).wait()

    tok = tok_ref[...]

    rel_cols = lax.broadcasted_iota(jnp.int32, (tn, SUB_SLOTS), 1).astype(F32).astype(BF16)
    gate_w = [tok[:, TOP_K + k:TOP_K + k + 1].astype(BF16) for k in range(TOP_K)]

    def chunk(c0):
        parts = []
        for b0 in range(c0, c0 + SLOT_CHUNK, SUB_SLOTS):
            q = jnp.zeros((tn, SUB_SLOTS), BF16)
            for k in range(TOP_K):
                hit = rel_cols == (tok[:, k:k + 1] - float(b0)).astype(BF16)
                q = jnp.where(hit, gate_w[k], q)
            parts.append(q)
        return jnp.dot(jnp.concatenate(parts, axis=1), loc_ref[slot, c0:c0 + SLOT_CHUNK, :],
                       preferred_element_type=F32)

    for c0 in range(0, MAIN_SLOTS, SLOT_CHUNK):
        acc = acc + chunk(c0)
    acc_ref[...] = acc

    @pl.when(tail)
    def _():
        acc_ref[...] += chunk(MAIN_SLOTS)

    x = x_ref[...] + mod_ref[0][5:6] * acc_ref[...]
    if final_norm:
        x = _rms(x) * fnw_ref[...]
    o_ref[...] = x


def moe_combine(ys, tok, h, x_all, mods, fnw, sh_gate, sh_up, sh_down, unit_dst, used_rows, *,
                n_tiles, t_lat, seq, final_norm):
    d = h.shape[1]
    tn = MOE_TILE
    const = lambda a: pl.BlockSpec(a.shape, lambda i, *_: (0,) * a.ndim)
    kern = functools.partial(_combine_kernel, final_norm=final_norm)
    return pl.pallas_call(
        kern,
        grid_spec=pltpu.PrefetchScalarGridSpec(
            num_scalar_prefetch=2,
            grid=(n_tiles,),
            in_specs=[pl.BlockSpec(memory_space=pl.ANY),
                      pl.BlockSpec((tn, LANES), lambda i, *_: (i, 0)),
                      pl.BlockSpec((tn, d), lambda i, *_: (i, 0)),
                      pl.BlockSpec((tn, d), lambda i, *_: (i, 0)),
                      pl.BlockSpec((1, 6, d),
                                   lambda i, *_: (_seg_of_tile(i, tn, t_lat, seq), 0, 0)),
                      const(fnw), const(sh_gate), const(sh_up), const(sh_down)],
            out_specs=pl.BlockSpec((tn, d), lambda i, *_: (i, 0)),
            scratch_shapes=[pltpu.VMEM((2, SLOTS, d), BF16), pltpu.VMEM((tn, d), F32),
                            pltpu.SemaphoreType.DMA((2,))]),
        out_shape=jax.ShapeDtypeStruct((n_tiles * tn, d), F32),
        compiler_params=_cparams(("arbitrary",)),
        name="moe_combine",
    )(unit_dst, used_rows, ys, tok, h, x_all, mods, fnw, sh_gate, sh_up, sh_down)


def moe_block(h2, logits, router_bias, x_mid, mods, fnw, w_gate, w_up, w_down,
              sh_gate, sh_up, sh_down, *, layer, t_lat, seq, final_norm):
    n_tiles = h2.shape[0] // MOE_TILE
    lists, tok, pc = router(logits, router_bias)
    unit_dst, unit_src, pad_dst, used_rows, tile_expert, n_row_tiles = moe_plan(
        pc[:, :, 0].astype(jnp.int32), n_tiles)
    xs = moe_dispatch(h2, lists, unit_dst, pad_dst, used_rows, n_tiles=n_tiles)
    ys = moe_experts(xs, w_gate, w_up, w_down, layer, tile_expert, n_row_tiles)
    return moe_combine(ys, tok, h2, x_mid, mods, fnw, sh_gate.astype(BF16), sh_up.astype(BF16),
                       sh_down.astype(BF16), unit_src, used_rows, n_tiles=n_tiles, t_lat=t_lat,
                       seq=seq, final_norm=final_norm)


def _rope_tables(seq, tm):
    half = RET_DK // 2
    pos = jnp.arange(seq, dtype=jnp.int32)
    row = (pos // GRID_W).astype(F32)
    col = (pos % GRID_W).astype(F32)
    inv = ROPE_BASE ** (-jnp.arange(0, half, 2, dtype=F32) / half)
    a_row, a_col = row[:, None] * inv[None, :], col[:, None] * inv[None, :]
    cos_h = jnp.concatenate([jnp.cos(a_row)] * 2 + [jnp.cos(a_col)] * 2, axis=-1)
    sin_h = jnp.concatenate([-jnp.sin(a_row), jnp.sin(a_row), -jnp.sin(a_col), jnp.sin(a_col)], -1)
    reps = 2 * RET_HEADS
    cos_t = jnp.concatenate([jnp.tile(cos_h, (1, reps)), jnp.ones((tm, reps * RET_DK), F32)], 0)
    sin_t = jnp.concatenate([jnp.tile(sin_h, (1, reps)), jnp.zeros((tm, reps * RET_DK), F32)], 0)
    return cos_t, sin_t


def _rope_partner_columns():
    quarter = RET_DK // 4
    idx = jnp.arange(2 * RET_HEADS * RET_DK, dtype=jnp.int32)
    within = idx % (2 * quarter)
    return jnp.where(within < quarter, idx + quarter, idx - quarter)


def kernel(x, c, ctx, c_ctx, ada_w, ada_b, norm1_w, norm2_w, ev_w_in, ev_short_w, ev_short_b, ev_filt_w1, ev_filt_b1, ev_filt_w2, ev_filt_b2, ev_filt_w3, ev_filt_b3, ev_filt_w4, ev_filt_freq, ev_hyena_bias, ev_w_out, od_w_in, od_gate_w1_f, od_gate_w2_f, od_gate_b_f, od_gate_w1_b, od_gate_w2_b, od_gate_b_b, od_norm_w, od_w_out, router_w, router_bias, exp_w_gate, exp_w_up, exp_w_down, sh_w_gate, sh_w_up, sh_w_down, final_norm_w):
    batch, seq, d = x.shape
    ctx_len = ctx.shape[1]
    depth = ada_w.shape[0]
    t_lat, t_ctx = batch * seq, batch * ctx_len
    t_all = t_lat + t_ctx
    tm = 512

    x_all = jnp.concatenate([x.reshape(t_lat, d), ctx.reshape(t_ctx, d)], axis=0)
    cond8 = jnp.concatenate([c_ctx[None, :], c, jnp.zeros((8 - 1 - batch, d), F32)], axis=0)
    mods_all = adaln_rows(cond8, ada_w, ada_b).reshape(depth, 8, 6, d)

    for i in range(depth):
        last = i == depth - 1
        j = i // 2
        mods = mods_all[i]
        nw1, nw2 = norm1_w[i][None, :], norm2_w[i][None, :]
        if i % 2 == 0:
            qk = 2 * RET_HEADS * RET_DK
            k_scale = jnp.concatenate([jnp.ones((qk // 2,), F32),
                                       jnp.full((qk // 2,), RET_DK ** -0.5, F32)])
            w_in = ev_w_in[j]
            w_qk = w_in[:, :qk] * k_scale
            w_ext = jnp.concatenate([w_qk, w_in[:, qk:], w_qk[:, _rope_partner_columns()]],
                                    axis=1).astype(BF16)
            cos_t, sin_t = _rope_tables(seq, tm)
            proj = inproj_even(x_all, mods, nw1, w_ext, cos_t, sin_t, t_lat=t_lat, seq=seq, tm=tm)
            log_g = [math.log1p(-2.0 ** (-5.0 - h)) for h in range(RET_HEADS)]
            mix_a = bidir_scan(proj, None, None, jnp.ones((1, RET_DV), F32),
                               batch=batch, seq=seq, ctx_len=ctx_len, heads=RET_HEADS, dk=RET_DK,
                               dv=RET_DV, chunk=RET_CHUNK, q_blk=0, k_blk=1, v_blk=1, g_blk=2,
                               log_decay_f=log_g, log_decay_b=log_g[::-1])
            hp = dict(short_w=ev_short_w[j], short_b=ev_short_b[j][None, :],
                      w1=jnp.pad(ev_filt_w1[j], ((0, LANES - HYENA_EMB), (0, 0))),
                      b1=ev_filt_b1[j][None, :], w2=ev_filt_w2[j], b2=ev_filt_b2[j][None, :],
                      w3=ev_filt_w3[j], b3=ev_filt_b3[j][None, :], w4=ev_filt_w4[j],
                      freq=ev_filt_freq[j][None, :], bias=ev_hyena_bias[j][None, :])
            hy_lat = hyena_long_conv(proj, hp, row0=0, batch=batch, seq_len=seq, z_blk=1)
            hy_ctx = hyena_long_conv(proj, hp, row0=t_lat, batch=batch, seq_len=ctx_len, z_blk=1)
            mix_b = jnp.concatenate([hy_lat, hy_ctx], axis=0)
            a_blk, b_blk = 0, 0
            w_out = ev_w_out[j].astype(BF16)
        else:
            kd = GLA_HEADS * GLA_DK
            pad_cols = LANES - 2 * GLA_RANK
            w_ext = jnp.concatenate([od_w_in[j], od_gate_w1_f[j], od_gate_w1_b[j],
                                     jnp.zeros((d, pad_cols), F32)], axis=1).astype(BF16)
            w2 = jnp.concatenate(
                [jnp.pad(od_gate_w2_f[j], ((0, LANES - GLA_RANK), (0, 0))),
                 jnp.pad(od_gate_w2_b[j], ((GLA_RANK, LANES - 2 * GLA_RANK), (0, 0)))], axis=1)
            w2_hi = w2.astype(BF16)
            w2_split = jnp.concatenate([w2_hi, w2_hi, (w2 - w2_hi.astype(F32)).astype(BF16)], axis=0)
            b2 = jnp.concatenate([od_gate_b_f[j], od_gate_b_b[j]])[None, :]
            proj, la_f, la_b = inproj_odd(x_all, mods, nw1, w_ext, w2_split, b2,
                                          t_lat=t_lat, seq=seq, tm=tm)
            mix_a = bidir_scan(proj, la_f, la_b, od_norm_w[j][None, :],
                               batch=batch, seq=seq, ctx_len=ctx_len, heads=GLA_HEADS, dk=GLA_DK,
                               dv=GLA_DV, chunk=GLA_CHUNK, q_blk=0, k_blk=1, v_blk=1, g_blk=2)
            mix_b = mix_a
            a_blk, b_blk = 0, 1
            w_out = od_w_out[j].astype(BF16)

        n_rows = t_lat if last else t_all
        rw_t = router_w[i].T
        rw_hi = rw_t.astype(BF16)
        rw_split = jnp.stack([rw_hi, (rw_t - rw_hi.astype(F32)).astype(BF16)])
        x_mid, h2, logits = outproj(mix_a, a_blk, mix_b, b_blk, x_all, mods, nw2, w_out,
                                    rw_split, n_rows=n_rows, t_lat=t_lat, seq=seq, tm=tm)
        x_all = moe_block(h2, logits, router_bias[i], x_mid, mods, final_norm_w[None, :],
                          exp_w_gate, exp_w_up, exp_w_down,
                          sh_w_gate[i], sh_w_up[i], sh_w_down[i],
                          layer=i, t_lat=t_lat, seq=seq, final_norm=last)
    return x_all[:t_lat].reshape(batch, seq, d)
```

```python
import functools
import math

import jax
import jax.numpy as jnp
from jax import lax
from jax.experimental import pallas as pl
from jax.experimental.pallas import tpu as pltpu

F32 = jnp.float32
BF16 = jnp.bfloat16
HIGHEST = lax.Precision.HIGHEST

D_MODEL = 1024
GRID_W = 64
NORM_EPS = 1e-6
RET_HEADS, RET_DK, RET_DV, RET_CHUNK = 4, 64, 128, 128
ROPE_BASE = 10000.0
HYENA_CH, HYENA_EMB = 512, 33
HYENA_FAST_DECAY, HYENA_SLOW_DECAY, HYENA_TARGET = 0.3, 1.5, 1e-2
GLA_HEADS, GLA_DK, GLA_DV, GLA_RANK, GLA_TAU, GLA_CHUNK = 4, 128, 256, 16, 16.0, 64
N_EXPERTS, TOP_K, N_GROUPS, TOPK_GROUPS = 64, 8, 8, 4
GROUP_SIZE = N_EXPERTS // N_GROUPS
EXPERT_FF = 256
ROUTED_SCALE = 2.5

LANES = 128
SCAN_ROWS = 256
SCAN_LAT_ROWS = 1024
MOE_TILE = 256
ROW_UNIT = 16
EXPERT_ROWS = 1024
SLOT_CHUNK = 512
SUB_SLOTS = 256
SLOTS = -(-(MOE_TILE * TOP_K + N_EXPERTS * (ROW_UNIT - 1) + ROW_UNIT) // SLOT_CHUNK) * SLOT_CHUNK
N_UNITS = SLOTS // ROW_UNIT
PAD_UNITS = EXPERT_ROWS // ROW_UNIT
VMEM_LIMIT = 56 * 1024 * 1024


def _cparams(sem):
    return pltpu.CompilerParams(dimension_semantics=sem, vmem_limit_bytes=VMEM_LIMIT)


def _rms(x):
    return x * lax.rsqrt(jnp.mean(x * x, axis=-1, keepdims=True) + NORM_EPS)


def _silu(x):
    return x * jax.nn.sigmoid(x)


def _adaln_kernel(c_ref, w_ref, b_ref, o_ref):
    c = c_ref[...]
    o_ref[0] = jnp.dot(_silu(c), w_ref[0], preferred_element_type=F32,
                       precision=HIGHEST) + b_ref[0]


def adaln_rows(cond8, ada_w, ada_b):
    depth, d, n = ada_w.shape
    tn = 512
    return pl.pallas_call(
        _adaln_kernel,
        grid=(depth, n // tn),
        in_specs=[pl.BlockSpec((8, d), lambda l, j: (0, 0)),
                  pl.BlockSpec((1, d, tn), lambda l, j: (l, 0, j)),
                  pl.BlockSpec((1, 1, tn), lambda l, j: (l, 0, j))],
        out_specs=pl.BlockSpec((1, 8, tn), lambda l, j: (l, 0, j)),
        out_shape=jax.ShapeDtypeStruct((depth, 8, n), F32),
        compiler_params=_cparams(("parallel", "parallel")),
        name="adaln_rows",
    )(cond8, ada_w, ada_b.reshape(depth, 1, n))


def _modulated_norm(x, mod, nw, shift_row, scale_row):
    return _rms(x) * nw * (1.0 + mod[scale_row:scale_row + 1]) + mod[shift_row:shift_row + 1]


def _inproj_even_kernel(x_ref, mod_ref, nw_ref, w_ref, cos_ref, sin_ref, o_ref):
    h = _modulated_norm(x_ref[...], mod_ref[0], nw_ref[...], 0, 1).astype(BF16)
    n_main = o_ref.shape[1]
    qk = jnp.dot(h, w_ref[:, 0:512], preferred_element_type=F32)
    qk_sw = jnp.dot(h, w_ref[:, n_main:n_main + 512], preferred_element_type=F32)
    o_ref[:, 0:512] = (qk * cos_ref[...] + qk_sw * sin_ref[...]).astype(BF16)
    for c0 in range(512, n_main, 512):
        o_ref[:, c0:c0 + 512] = jnp.dot(h, w_ref[:, c0:c0 + 512],
                                        preferred_element_type=F32).astype(BF16)


def _inproj_odd_kernel(x_ref, mod_ref, nw_ref, w_ref, w2_ref, b2_ref, o_ref, laf_ref, lab_ref):
    h = _modulated_norm(x_ref[...], mod_ref[0], nw_ref[...], 0, 1).astype(BF16)
    n_main = o_ref.shape[1]
    kd = GLA_HEADS * GLA_DK
    q = jnp.dot(h, w_ref[:, 0:kd], preferred_element_type=F32)
    o_ref[:, 0:kd] = (q * (GLA_DK ** -0.5)).astype(BF16)
    for c0 in range(kd, n_main, 512):
        o_ref[:, c0:c0 + 512] = jnp.dot(h, w_ref[:, c0:c0 + 512],
                                        preferred_element_type=F32).astype(BF16)
    low = jnp.dot(h, w_ref[:, n_main:n_main + LANES], preferred_element_type=F32)

    low_hi = low.astype(BF16)
    low_lo = (low - low_hi.astype(F32)).astype(BF16)
    z = jnp.dot(jnp.concatenate([low_hi, low_lo, low_hi], axis=1), w2_ref[...],
                preferred_element_type=F32) + b2_ref[...]
    la = (jnp.minimum(z, 0.0) - jnp.log(1.0 + jnp.exp(-jnp.abs(z)))) * (1.0 / GLA_TAU)
    laf_ref[...] = la[:, 0:kd]
    lab_ref[...] = la[:, kd:2 * kd]


def _seg_of_tile(i, tm, t_lat, seq):
    return jnp.where(i < t_lat // tm, 1 + (i * tm) // seq, 0)


def inproj_even(x_all, mods, nw, w_ext, cos_t, sin_t, *, t_lat, seq, tm=512):
    t_all, d = x_all.shape
    n_ext = w_ext.shape[1]
    n_main = n_ext - 512
    n_lat_tiles, pos_tiles = t_lat // tm, seq // tm

    def pos_map(i):
        return (jnp.where(i < n_lat_tiles, i % pos_tiles, pos_tiles), 0)

    return pl.pallas_call(
        _inproj_even_kernel,
        grid=(t_all // tm,),
        in_specs=[pl.BlockSpec((tm, d), lambda i: (i, 0)),
                  pl.BlockSpec((1, 6, d), lambda i: (_seg_of_tile(i, tm, t_lat, seq), 0, 0)),
                  pl.BlockSpec((1, d), lambda i: (0, 0)),
                  pl.BlockSpec((d, n_ext), lambda i: (0, 0)),
                  pl.BlockSpec((tm, 512), pos_map),
                  pl.BlockSpec((tm, 512), pos_map)],
        out_specs=pl.BlockSpec((tm, n_main), lambda i: (i, 0)),
        out_shape=jax.ShapeDtypeStruct((t_all, n_main), BF16),
        compiler_params=_cparams(("parallel",)),
        name="inproj_even",
    )(x_all, mods, nw, w_ext, cos_t, sin_t)


def inproj_odd(x_all, mods, nw, w_ext, w2, b2, *, t_lat, seq, tm=512):
    t_all, d = x_all.shape
    n_main = w_ext.shape[1] - LANES
    kd = GLA_HEADS * GLA_DK
    full = lambda shape: pl.BlockSpec(shape, lambda i: (0,) * len(shape))
    return pl.pallas_call(
        _inproj_odd_kernel,
        grid=(t_all // tm,),
        in_specs=[pl.BlockSpec((tm, d), lambda i: (i, 0)),
                  pl.BlockSpec((1, 6, d), lambda i: (_seg_of_tile(i, tm, t_lat, seq), 0, 0)),
                  full((1, d)), full(w_ext.shape), full(w2.shape), full(b2.shape)],
        out_specs=[pl.BlockSpec((tm, n_main), lambda i: (i, 0)),
                   pl.BlockSpec((tm, kd), lambda i: (i, 0)),
                   pl.BlockSpec((tm, kd), lambda i: (i, 0))],
        out_shape=[jax.ShapeDtypeStruct((t_all, n_main), BF16),
                   jax.ShapeDtypeStruct((t_all, kd), F32),
                   jax.ShapeDtypeStruct((t_all, kd), F32)],
        compiler_params=_cparams(("parallel",)),
        name="inproj_odd",
    )(x_all, mods, nw, w_ext, w2, b2)


def _scan_kernel(*refs, heads, dk, dv, chunk, gated, reverse, log_decay, sub_blocks):
    n_in = 3 + int(gated) + (2 if reverse else 0)
    ctx_in, lat_in = refs[:n_in], refs[n_in:2 * n_in]
    rest = list(refs[2 * n_in:])
    nw_ref = rest.pop(0) if reverse else None
    out_lat, out_ctx, state_ref = rest[0], rest[1], rest[2]
    dec_ref = None if gated else rest[3]

    hpg = LANES // dk
    groups = heads // hpg
    rows = SCAN_ROWS
    n_chunks = rows // chunk
    j = pl.program_id(1)

    row_i = lax.broadcasted_iota(jnp.int32, (rows, rows), 0)
    col_i = lax.broadcasted_iota(jnp.int32, (rows, rows), 1)
    same_chunk = (row_i // chunk) == (col_i // chunk)
    keep = jnp.logical_and(same_chunk, (col_i >= row_i) if reverse else (row_i >= col_i))

    @pl.when(j == 0)
    def _init():
        state_ref[...] = jnp.zeros_like(state_ref)
        if not gated:
            pos = lax.broadcasted_iota(jnp.int32, (rows, LANES), 0) % chunk
            steps = ((chunk - pos) if reverse else (pos + 1)).astype(F32)
            lane = lax.broadcasted_iota(jnp.int32, (rows, LANES), 1)
            for g in range(groups):
                lg = jnp.zeros((rows, LANES), F32)
                for a in range(hpg):
                    lg = jnp.where(lane // dk == a, log_decay[g * hpg + a], lg)
                logb = steps * lg
                b_end = float(chunk) * lg
                dec_ref[g, 0] = jnp.exp(logb)
                dec_ref[g, 1] = jnp.exp(-logb)
                dec_ref[g, 2] = jnp.exp(b_end - logb)
                dec_ref[g, 3] = jnp.exp(b_end)

    lane1 = lax.broadcasted_iota(jnp.int32, (1, LANES), 1)
    order = range(n_chunks - 1, -1, -1) if reverse else range(n_chunks)
    end_row = lambda c: c * chunk if reverse else (c + 1) * chunk - 1

    def scan_rows(in_refs, out_ref, r0):
        q_ref, k_ref, v_ref = in_refs[:3]
        blk = slice(r0, r0 + rows)
        if reverse:
            oprev_ref, gate_ref = in_refs[-2:]
        if gated:
            tri = jnp.where(keep, 1.0, 0.0).astype(BF16)
            la = in_refs[3][blk, :]
            la_hi = la.astype(BF16)
            la_lo = (la - la_hi.astype(F32)).astype(BF16)
            logb_all = (jnp.dot(tri, la_hi, preferred_element_type=F32)
                        + jnp.dot(tri, la_lo, preferred_element_type=F32))
        for g in range(groups):
            ksl = slice(g * LANES, (g + 1) * LANES)
            qg = q_ref[blk, ksl].astype(F32)
            kg = k_ref[blk, ksl].astype(F32)
            if gated:
                logb = logb_all[:, ksl]
                ends = [logb[end_row(c):end_row(c) + 1] for c in range(n_chunks)]
                b_end = jnp.concatenate([jnp.broadcast_to(e, (chunk, LANES)) for e in ends], axis=0)
                e_q, e_k, e_s = jnp.exp(logb), jnp.exp(-logb), jnp.exp(b_end - logb)
                e_e = [jnp.exp(e) for e in ends]
            else:
                e_q, e_k, e_s = dec_ref[g, 0], dec_ref[g, 1], dec_ref[g, 2]
                e_e = [dec_ref[g, 3][0:1]] * n_chunks
            qd = qg * e_q
            kd_ = (kg * e_k).astype(BF16)
            ks = (kg * e_s).astype(BF16)
            for a in range(hpg):
                h = g * hpg + a
                qa = (jnp.where(lane1 // dk == a, qd, 0.0) if hpg > 1 else qd).astype(BF16)
                vh = v_ref[blk, h * dv:(h + 1) * dv]
                s = lax.dot_general(qa, kd_, (((1,), (1,)), ((), ())), preferred_element_type=F32)
                o_intra = jnp.dot(jnp.where(keep, s, 0.0).astype(BF16), vh,
                                  preferred_element_type=F32)
                st = state_ref[h]
                for c in order:
                    rs = slice(c * chunk, (c + 1) * chunk)
                    o = o_intra[rs] + lax.dot_general(qa[rs], st.astype(BF16),
                                                      (((1,), (1,)), ((), ())),
                                                      preferred_element_type=F32)
                    st = e_e[c] * st + lax.dot_general(vh[rs], ks[rs], (((0,), (0,)), ((), ())),
                                                       preferred_element_type=F32)
                    osl = (slice(r0 + c * chunk, r0 + (c + 1) * chunk), slice(h * dv, (h + 1) * dv))
                    if reverse:
                        o = _rms(o + oprev_ref[osl]) * nw_ref[...]
                        out_ref[osl] = (o * _silu(gate_ref[osl].astype(F32))).astype(BF16)
                    else:
                        out_ref[osl] = o
                state_ref[h] = st

    @pl.when(j == 0)
    def _():
        scan_rows(ctx_in, out_ctx, 0)

    @pl.when(j > 0)
    def _():
        for sb in (range(sub_blocks - 1, -1, -1) if reverse else range(sub_blocks)):
            scan_rows(lat_in, out_lat, sb * rows)


def bidir_scan(proj, la_f, la_b, norm_w, *, batch, seq, ctx_len, heads, dk, dv, chunk,
               q_blk, k_blk, v_blk, g_blk, log_decay_f=None, log_decay_b=None):
    gated = la_f is not None
    hk, hv = heads * dk, heads * dv
    rb, lb = SCAN_ROWS, SCAN_LAT_ROWS
    lat_blocks = seq // lb
    ctx_base = (batch * seq) // rb
    assert ctx_len == rb

    lat_fwd = lambda b, j: b * lat_blocks + jnp.maximum(j - 1, 0)
    lat_bwd = lambda b, j: b * lat_blocks + lat_blocks - 1 - jnp.maximum(j - 1, 0)

    prev = None
    for reverse, lat_of, la, ld in ((False, lat_fwd, la_f, log_decay_f),
                                    (True, lat_bwd, la_b, log_decay_b)):
        ctx_spec = lambda w, cb, base=ctx_base: pl.BlockSpec(
            (rb, w), lambda b, j, cb=cb: (base + b, cb))
        lat_spec = lambda w, cb: pl.BlockSpec((lb, w), lambda b, j, cb=cb: (lat_of(b, j), cb))
        in_specs, args = [], []
        for spec, is_ctx in ((ctx_spec, True), (lat_spec, False)):
            in_specs += [spec(hk, q_blk), spec(hk, k_blk), spec(hv, v_blk)]
            args += [proj, proj, proj]
            if gated:
                in_specs.append(spec(hk, 0))
                args.append(la)
            if reverse:
                in_specs += [ctx_spec(hv, 0, base=0) if is_ctx else spec(hv, 0), spec(hv, g_blk)]
                args += [prev[1] if is_ctx else prev[0], proj]
        if reverse:
            in_specs.append(pl.BlockSpec((1, dv), lambda b, j: (0, 0)))
            args.append(norm_w)
        scratch = [pltpu.VMEM((heads, dv, LANES), F32)]
        if not gated:
            scratch.append(pltpu.VMEM((hk // LANES, 4, rb, LANES), F32))
        kern = functools.partial(_scan_kernel, heads=heads, dk=dk, dv=dv, chunk=chunk,
                                 gated=gated, reverse=reverse, log_decay=ld, sub_blocks=lb // rb)
        out_dtype = BF16 if reverse else F32
        prev = pl.pallas_call(
            kern,
            grid=(batch, lat_blocks + 1),
            in_specs=in_specs,
            out_specs=[lat_spec(hv, 0), ctx_spec(hv, 0, base=0)],
            out_shape=[jax.ShapeDtypeStruct((batch * seq, hv), out_dtype),
                       jax.ShapeDtypeStruct((batch * ctx_len, hv), out_dtype)],
            scratch_shapes=scratch,
            compiler_params=_cparams(("parallel", "arbitrary")),
            name="scan_bwd" if reverse else "scan_fwd",
        )(*args)
    return jnp.concatenate(prev, axis=0)


def _shortconv_kernel(z_ref, zp_ref, zn_ref, w_ref, b_ref, u_ref, x0_ref, *, tiles_per_seq):
    i = pl.program_id(0)
    tm = z_ref.shape[0]
    z = z_ref[...].astype(F32)
    first = (i % tiles_per_seq) == 0
    last = (i % tiles_per_seq) == tiles_per_seq - 1
    halo = zp_ref.shape[0]
    prev_row = jnp.where(first, 0.0, zp_ref[halo - 1:halo, :].astype(F32))
    next_row = jnp.where(last, 0.0, zn_ref[0:1, :].astype(F32))
    row = lax.broadcasted_iota(jnp.int32, z.shape, 0)
    z_prev = jnp.where(row == 0, prev_row, pltpu.roll(z, 1, 0))
    z_next = jnp.where(row == tm - 1, next_row, pltpu.roll(z, tm - 1, 0))
    y = w_ref[0:1] * z_prev + w_ref[1:2] * z + w_ref[2:3] * z_next + b_ref[...]
    c = HYENA_CH
    x0_ref[...] = y[:, 0:c].astype(BF16)
    u_ref[...] = (y[:, c:2 * c] * y[:, 2 * c:3 * c]).astype(BF16)


def hyena_shortconv(proj, short_w, short_b, *, row0, batch, seq_len, z_blk, tm=256):
    halo = 16
    tiles_per_seq = seq_len // tm
    n_tiles = batch * tiles_per_seq
    t0, h_per_tile = row0 // tm, tm // halo
    nz = 3 * HYENA_CH
    n_halo_blocks = proj.shape[0] // halo
    out_map = lambda i: (i % tiles_per_seq, i // tiles_per_seq)
    kern = functools.partial(_shortconv_kernel, tiles_per_seq=tiles_per_seq)
    return pl.pallas_call(
        kern,
        grid=(n_tiles,),
        in_specs=[pl.BlockSpec((tm, nz), lambda i: (t0 + i, z_blk)),
                  pl.BlockSpec((halo, nz),
                               lambda i: (jnp.maximum((t0 + i) * h_per_tile - 1, 0), z_blk)),
                  pl.BlockSpec((halo, nz),
                               lambda i: (jnp.minimum((t0 + i + 1) * h_per_tile,
                                                      n_halo_blocks - 1), z_blk)),
                  pl.BlockSpec((3, nz), lambda i: (0, 0)),
                  pl.BlockSpec((1, nz), lambda i: (0, 0))],
        out_specs=[pl.BlockSpec((tm, HYENA_CH), out_map), pl.BlockSpec((tm, HYENA_CH), out_map)],
        out_shape=[jax.ShapeDtypeStruct((seq_len, batch * HYENA_CH), BF16)] * 2,
        compiler_params=_cparams(("parallel",)),
        name="hyena_shortconv",
    )(proj, proj, proj, short_w, short_b)


def _filter_kernel(feat_ref, w1_ref, b1_ref, w2_ref, b2_ref, w3_ref, b3_ref, w4_ref, fq_ref,
                   hp_ref, hm_ref, *, seq_len):
    i = pl.program_id(0)
    tl = feat_ref.shape[0]
    fq = fq_ref[...]
    dot = lambda a, b: jnp.dot(a, b, preferred_element_type=F32, precision=HIGHEST)
    h = jnp.sin(fq * (dot(feat_ref[...], w1_ref[...]) + b1_ref[...]))
    h = jnp.sin(fq * (dot(h, w2_ref[...]) + b2_ref[...]))
    h = jnp.sin(fq * (dot(h, w3_ref[...]) + b3_ref[...]))
    h = dot(h, w4_ref[...])
    c = HYENA_CH
    max_decay = math.log(HYENA_TARGET) / HYENA_FAST_DECAY
    min_decay = math.log(HYENA_TARGET) / HYENA_SLOW_DECAY
    ch = lax.broadcasted_iota(jnp.int32, (tl, c), 1).astype(F32)
    deltas = min_decay + ch * ((max_decay - min_decay) / (c - 1))
    row = lax.broadcasted_iota(jnp.int32, (tl, c), 0) + i * tl
    t = row.astype(F32) * (1.0 / (seq_len - 1))
    window = jnp.exp(-t * jnp.abs(deltas))
    h_f = h[:, 0:c] * window
    h_b = jnp.where(row == 0, 0.0, h[:, c:2 * c] * window)
    hp_ref[...] = (h_f + h_b).astype(BF16)
    hm_ref[...] = (h_f - h_b).astype(BF16)


def hyena_filter_pair(feats, w1p, b1, w2, b2, w3, b3, w4, fq):
    seq_len = feats.shape[0]
    tl = min(512, seq_len)
    full = lambda a: pl.BlockSpec(a.shape, lambda i: (0,) * a.ndim)
    kern = functools.partial(_filter_kernel, seq_len=seq_len)
    return pl.pallas_call(
        kern,
        grid=(seq_len // tl,),
        in_specs=[pl.BlockSpec((tl, feats.shape[1]), lambda i: (i, 0)),
                  full(w1p), full(b1), full(w2), full(b2), full(w3), full(b3), full(w4), full(fq)],
        out_specs=[pl.BlockSpec((tl, HYENA_CH), lambda i: (i, 0))] * 2,
        out_shape=[jax.ShapeDtypeStruct((seq_len, HYENA_CH), BF16)] * 2,
        compiler_params=_cparams(("parallel",)),
        name="hyena_filter",
    )(feats, w1p, b1, w2, b2, w3, b3, w4, fq)


def _dft_fwd_kernel(c_ref, s_ref, u_ref, a_ref, b_ref):
    u = u_ref[...]
    a_ref[...] = jnp.dot(c_ref[...], u, preferred_element_type=F32)
    b_ref[...] = jnp.dot(s_ref[...], u, preferred_element_type=F32)


def dft_forward(cs, ss, u_ext):
    n, cols = u_ext.shape
    tf, tn = min(512, n), 1024
    return pl.pallas_call(
        _dft_fwd_kernel,
        grid=(cols // tn, n // tf),
        in_specs=[pl.BlockSpec((tf, n), lambda c, f: (f, 0)),
                  pl.BlockSpec((tf, n), lambda c, f: (f, 0)),
                  pl.BlockSpec((n, tn), lambda c, f: (0, c))],
        out_specs=[pl.BlockSpec((tf, tn), lambda c, f: (f, c))] * 2,
        out_shape=[jax.ShapeDtypeStruct((n, cols), F32)] * 2,
        compiler_params=_cparams(("parallel", "parallel")),
        name="dft_forward",
    )(cs, ss, u_ext)


def _spectral_mul_kernel(au_ref, bu_ref, ap_ref, bp_ref, am_ref, bm_ref, hc_ref, hs_ref,
                         yre_ref, yim_ref, *, scale):
    hc, hs = hc_ref[...], hs_ref[...]
    k_re = hc * ap_ref[...] + hs * bp_ref[...]
    k_im = hs * am_ref[...] - hc * bm_ref[...]
    a, b = au_ref[...], bu_ref[...]
    yre_ref[...] = ((a * k_re + b * k_im) * scale).astype(BF16)
    yim_ref[...] = ((a * k_im - b * k_re) * scale).astype(BF16)


def spectral_mul(a, b, half_cos, half_sin, *, batch):
    n = a.shape[0]
    c = HYENA_CH
    tf = min(512, n)
    kern = functools.partial(_spectral_mul_kernel, scale=1.0 / n)
    ucol = lambda f, bi: (f, bi)
    return pl.pallas_call(
        kern,
        grid=(n // tf, batch),
        in_specs=[pl.BlockSpec((tf, c), ucol), pl.BlockSpec((tf, c), ucol),
                  pl.BlockSpec((tf, c), lambda f, bi: (f, batch)),
                  pl.BlockSpec((tf, c), lambda f, bi: (f, batch)),
                  pl.BlockSpec((tf, c), lambda f, bi: (f, batch + 1)),
                  pl.BlockSpec((tf, c), lambda f, bi: (f, batch + 1)),
                  pl.BlockSpec((tf, 1), lambda f, bi: (f, 0)),
                  pl.BlockSpec((tf, 1), lambda f, bi: (f, 0))],
        out_specs=[pl.BlockSpec((tf, c), ucol)] * 2,
        out_shape=[jax.ShapeDtypeStruct((n, batch * c), BF16)] * 2,
        compiler_params=_cparams(("parallel", "parallel")),
        name="hyena_spectral_mul",
    )(a, b, a, b, a, b, half_cos, half_sin)


def _dft_inv_kernel(c_ref, s_ref, yre_ref, yim_ref, u_ref, x0_ref, bias_ref, o_ref):
    y = (jnp.dot(c_ref[...], yre_ref[...], preferred_element_type=F32)
         - jnp.dot(s_ref[...], yim_ref[...], preferred_element_type=F32))
    u = u_ref[...].astype(F32)
    o_ref[...] = (x0_ref[...].astype(F32) * (y + u * bias_ref[...])).astype(BF16)


def dft_inverse(cs, ss, yre, yim, u, x0, bias, *, batch):
    n = cs.shape[0]
    c = HYENA_CH
    tt = min(512, n)
    col = lambda bi, t: (0, bi)
    tile = lambda bi, t: (t, bi)
    return pl.pallas_call(
        _dft_inv_kernel,
        grid=(batch, n // tt),
        in_specs=[pl.BlockSpec((tt, n), lambda bi, t: (t, 0)),
                  pl.BlockSpec((tt, n), lambda bi, t: (t, 0)),
                  pl.BlockSpec((n, c), col), pl.BlockSpec((n, c), col),
                  pl.BlockSpec((tt, c), tile), pl.BlockSpec((tt, c), tile),
                  pl.BlockSpec((1, c), lambda bi, t: (0, 0))],
        out_specs=pl.BlockSpec((tt, c), lambda bi, t: (bi * (n // tt) + t, 0)),
        out_shape=jax.ShapeDtypeStruct((batch * n, c), BF16),
        compiler_params=_cparams(("parallel", "parallel")),
        name="dft_inverse",
    )(cs, ss, yre, yim, u, x0, bias)


def _dft_table_kernel(cg_ref, sg_ref, cd_ref, sd_ref, c_ref, s_ref):
    cg, sg, cd, sd = cg_ref[...], sg_ref[...], cd_ref[0], sd_ref[0]
    c_ref[...] = (cg * cd - sg * sd).astype(BF16)
    s_ref[...] = (sg * cd + cg * sd).astype(BF16)


def _shifted_dft_tables(n):
    tf = min(256, n)
    theta = 2.0 * math.pi / (8 * n)
    odd_s = 2 * jnp.arange(n, dtype=jnp.int32)[None, :] + 1
    odd_i = 2 * jnp.arange(tf, dtype=jnp.int32)[:, None] + 1
    gamma = ((odd_i * odd_s) % (8 * n)).astype(F32) * theta
    f0 = jnp.arange(n // tf, dtype=jnp.int32)[:, None] * tf
    delta = ((2 * f0 * odd_s) % (8 * n)).astype(F32)[:, None, :] * theta
    whole = pl.BlockSpec((tf, n), lambda i: (0, 0))
    row = pl.BlockSpec((1, 1, n), lambda i: (i, 0, 0))
    cs, ss = pl.pallas_call(
        _dft_table_kernel,
        grid=(n // tf,),
        in_specs=[whole, whole, row, row],
        out_specs=[pl.BlockSpec((tf, n), lambda i: (i, 0))] * 2,
        out_shape=[jax.ShapeDtypeStruct((n, n), BF16)] * 2,
        compiler_params=_cparams(("parallel",)),
        name="dft_tables",
    )(jnp.cos(gamma), jnp.sin(gamma), jnp.cos(delta), jnp.sin(delta))
    half = (2 * jnp.arange(n, dtype=jnp.int32) + 1).astype(F32)[:, None] * (math.pi / (4 * n))
    return cs, ss, jnp.cos(half), jnp.sin(half)


def _filter_features(seq_len):
    t = jnp.linspace(0.0, 1.0, seq_len, dtype=F32)[:, None]
    bands = (HYENA_EMB - 1) // 2
    ang = 2.0 * math.pi * jnp.arange(seq_len, dtype=F32)[:, None] / seq_len
    fr = jnp.linspace(1e-4, bands - 1, bands, dtype=F32)[None, :]
    feats = jnp.concatenate([t, jnp.cos(fr * ang), -jnp.sin(fr * ang)], axis=-1)
    return jnp.pad(feats, ((0, 0), (0, LANES - HYENA_EMB)))


def hyena_long_conv(proj, hp, *, row0, batch, seq_len, z_blk):
    u, x0 = hyena_shortconv(proj, hp["short_w"], hp["short_b"], row0=row0, batch=batch,
                            seq_len=seq_len, z_blk=z_blk)
    f_p, f_m = hyena_filter_pair(_filter_features(seq_len), hp["w1"], hp["b1"], hp["w2"], hp["b2"],
                                 hp["w3"], hp["b3"], hp["w4"], hp["freq"])
    cs, ss, half_cos, half_sin = _shifted_dft_tables(seq_len)
    a, b = dft_forward(cs, ss, jnp.concatenate([u, f_p, f_m], axis=1))
    yre, yim = spectral_mul(a, b, half_cos, half_sin, batch=batch)
    return dft_inverse(cs, ss, yre, yim, u, x0, hp["bias"], batch=batch)


def _outproj_kernel(ma_ref, mb_ref, x_ref, mod_ref, nw_ref, w_ref, rw_ref,
                    xo_ref, h_ref, lt_ref):
    half = ma_ref.shape[1]
    y = (jnp.dot(ma_ref[...], w_ref[0:half], preferred_element_type=F32)
         + jnp.dot(mb_ref[...], w_ref[half:2 * half], preferred_element_type=F32))
    mod = mod_ref[0]
    x = x_ref[...] + mod[2:3] * y
    xo_ref[...] = x
    h = _modulated_norm(x, mod, nw_ref[...], 3, 4)
    h_ref[...] = h.astype(BF16)
    h_hi = h.astype(BF16)
    h_lo = (h - h_hi.astype(F32)).astype(BF16)
    nt = lambda a, b: lax.dot_general(a, b, (((1,), (1,)), ((), ())), preferred_element_type=F32)
    lt_ref[...] = nt(rw_ref[0], h_hi) + (nt(rw_ref[0], h_lo) + nt(rw_ref[1], h_hi))


def outproj(mix_a, a_blk, mix_b, b_blk, x_all, mods, nw, w_out, router_wt, *,
            n_rows, t_lat, seq, tm=512):
    d = x_all.shape[1]
    half = d // 2
    return pl.pallas_call(
        _outproj_kernel,
        grid=(n_rows // tm,),
        in_specs=[pl.BlockSpec((tm, half), lambda i: (i, a_blk)),
                  pl.BlockSpec((tm, half), lambda i: (i, b_blk)),
                  pl.BlockSpec((tm, d), lambda i: (i, 0)),
                  pl.BlockSpec((1, 6, d), lambda i: (_seg_of_tile(i, tm, t_lat, seq), 0, 0)),
                  pl.BlockSpec((1, d), lambda i: (0, 0)),
                  pl.BlockSpec((d, d), lambda i: (0, 0)),
                  pl.BlockSpec((2, N_EXPERTS, d), lambda i: (0, 0, 0))],
        out_specs=[pl.BlockSpec((tm, d), lambda i: (i, 0)),
                   pl.BlockSpec((tm, d), lambda i: (i, 0)),
                   pl.BlockSpec((N_EXPERTS, tm), lambda i: (0, i))],
        out_shape=[jax.ShapeDtypeStruct((n_rows, d), F32),
                   jax.ShapeDtypeStruct((n_rows, d), BF16),
                   jax.ShapeDtypeStruct((N_EXPERTS, n_rows), F32)],
        compiler_params=_cparams(("parallel",)),
        name="outproj",
    )(mix_a, mix_b, x_all, mods, nw, w_out, router_wt)


def _first_max(x, idx, sentinel):
    m = jnp.max(x, axis=0, keepdims=True)
    first = jnp.min(jnp.where(x == m, idx, sentinel), axis=0, keepdims=True)
    return m, idx == first


def _router_kernel(lt_ref, bias_ref, eye_ref, before_ref, below_ref, lists_ref, tok_ref, pc_ref):
    tn = lt_ref.shape[1]
    scores = jax.nn.sigmoid(lt_ref[...])
    sel = scores + bias_ref[...]
    neg = -jnp.inf
    in_grp = lax.broadcasted_iota(jnp.int32, (GROUP_SIZE, tn), 0)
    gscore = []
    for g in range(N_GROUPS):
        x = sel[g * GROUP_SIZE:(g + 1) * GROUP_SIZE]
        m1, hit = _first_max(x, in_grp, GROUP_SIZE)
        gscore.append(m1 + jnp.max(jnp.where(hit, neg, x), axis=0, keepdims=True))
    rows = []
    for g in range(N_GROUPS):
        beaten = jnp.zeros((1, tn), jnp.int32)
        for o in range(N_GROUPS):
            if o != g:
                wins = (gscore[o] >= gscore[g]) if o < g else (gscore[o] > gscore[g])
                beaten = beaten + wins.astype(jnp.int32)
        keep = jnp.broadcast_to(beaten < TOPK_GROUPS, (GROUP_SIZE, tn))
        rows.append(jnp.where(keep, sel[g * GROUP_SIZE:(g + 1) * GROUP_SIZE], neg))
    cand = jnp.concatenate(rows, axis=0)
    eidx = lax.broadcasted_iota(jnp.int32, cand.shape, 0)
    chosen = jnp.zeros(cand.shape, jnp.bool_)
    hits = []
    for _ in range(TOP_K):
        _, hit = _first_max(cand, eidx, N_EXPERTS)
        hits.append(hit)
        chosen = jnp.logical_or(chosen, hit)
        cand = jnp.where(hit, neg, cand)
    w = jnp.where(chosen, scores, 0.0)
    gates_t = w / jnp.sum(w, axis=0, keepdims=True) * ROUTED_SCALE
    chosen_f = jnp.where(chosen, 1.0, 0.0)
    rank = jnp.dot(chosen_f.astype(BF16), before_ref[...], preferred_element_type=F32)
    count = jnp.sum(chosen_f, axis=1, keepdims=True)
    pc = jnp.floor((count + (ROW_UNIT - 1)) * (1.0 / ROW_UNIT)) * ROW_UNIT
    pc_lanes = jnp.broadcast_to(pc, (N_EXPERTS, LANES))
    start = jnp.dot(below_ref[...], pc_lanes, preferred_element_type=F32,
                    precision=HIGHEST)[:, 0:1]
    slot = start + rank
    pick = lambda hit, v: jnp.sum(jnp.where(hit, v, 0.0), axis=0, keepdims=True)
    lists = jnp.concatenate([pick(h, slot) for h in hits] + [pick(h, gates_t) for h in hits], axis=0)
    lists_ref[...] = lists
    pc_ref[0] = pc_lanes
    padded = jnp.concatenate([lists, jnp.zeros((LANES - 2 * TOP_K, tn), F32)], axis=0)
    tok_ref[...] = lax.dot_general(eye_ref[...], padded, (((1,), (1,)), ((), ())),
                                   preferred_element_type=F32, precision=HIGHEST)


def router(logits, router_bias):
    e, t = logits.shape
    tn = MOE_TILE
    n_tiles = t // tn
    eye = jnp.eye(tn, dtype=F32)
    tok_i = jnp.arange(tn, dtype=jnp.int32)
    before = (tok_i[:, None] < tok_i[None, :]).astype(BF16)
    exp_i = jnp.arange(e, dtype=jnp.int32)
    below = (exp_i[None, :] < exp_i[:, None]).astype(F32)
    const = lambda shape: pl.BlockSpec(shape, lambda i: (0,) * len(shape))
    return pl.pallas_call(
        _router_kernel,
        grid=(n_tiles,),
        in_specs=[pl.BlockSpec((e, tn), lambda i: (0, i)),
                  const((e, 1)), const((tn, tn)), const((tn, tn)), const((e, e))],
        out_specs=[pl.BlockSpec((2 * TOP_K, tn), lambda i: (0, i)),
                   pl.BlockSpec((tn, LANES), lambda i: (i, 0)),
                   pl.BlockSpec((1, e, LANES), lambda i: (i, 0, 0))],
        out_shape=[jax.ShapeDtypeStruct((2 * TOP_K, t), F32),
                   jax.ShapeDtypeStruct((t, LANES), F32),
                   jax.ShapeDtypeStruct((n_tiles, e, LANES), F32)],
        compiler_params=_cparams(("parallel",)),
        name="router",
    )(logits, router_bias.reshape(e, 1), eye, before, below)


def moe_plan(pc, n_tiles):
    e = pc.shape[1]
    assert n_tiles >= e
    cap_rows = _sorted_capacity(n_tiles)
    run_end = jnp.cumsum(pc, axis=1)
    run_start = run_end - pc
    total = jnp.sum(pc, axis=0)
    total_al = ((total + EXPERT_ROWS - 1) // EXPERT_ROWS) * EXPERT_ROWS
    range_end = jnp.cumsum(total_al)
    range_start = range_end - total_al
    run_row = range_start[None, :] + jnp.cumsum(pc, axis=0) - pc
    unit_row = jnp.arange(N_UNITS, dtype=jnp.int32) * ROW_UNIT
    owner = jnp.sum((run_end[:, None, :] <= unit_row[None, :, None]).astype(jnp.int32), axis=2)
    owner_hot = (owner[:, :, None] == jnp.arange(e, dtype=jnp.int32)).astype(jnp.int32)
    dst = jnp.sum(owner_hot * (run_row - run_start)[:, None, :], axis=2) + unit_row[None, :]
    used = owner < e
    dump = cap_rows + (jnp.arange(n_tiles, dtype=jnp.int32) % 2)[:, None] * DUMP_ROWS
    unit_dst = jnp.where(used, dst, dump + unit_row[None, :]).astype(jnp.int32)
    unit_src = jnp.where(used, dst, 0).astype(jnp.int32)
    pad_off = jnp.arange(PAD_UNITS, dtype=jnp.int32)[None, :] * ROW_UNIT
    pad_row = (range_start + total)[:, None] + pad_off
    pad_row = jnp.concatenate([jnp.where(pad_row < range_end[:, None], pad_row, -1),
                               jnp.full((n_tiles - e, PAD_UNITS), -1, jnp.int32)], axis=0)
    pad_dst = jnp.where(pad_row >= 0, pad_row, dump + SLOTS + pad_off).astype(jnp.int32)
    tile_end = range_end // EXPERT_ROWS
    tile_expert = jnp.sum((tile_end[None, :] <= jnp.arange(cap_rows // EXPERT_ROWS,
                                                           dtype=jnp.int32)[:, None])
                          .astype(jnp.int32), axis=1)
    tile_expert = jnp.minimum(tile_expert, e - 1).astype(jnp.int32)
    used_rows = run_end[:, -1].astype(jnp.int32)
    return unit_dst, unit_src, pad_dst, used_rows, tile_expert, tile_end[-1:].astype(jnp.int32)


def _sorted_capacity(n_tiles):
    worst_rows = n_tiles * (MOE_TILE * TOP_K + N_EXPERTS * (ROW_UNIT - 1)) + N_EXPERTS * (EXPERT_ROWS - 1)
    return -(-worst_rows // EXPERT_ROWS) * EXPERT_ROWS


DUMP_ROWS = SLOTS + PAD_UNITS * ROW_UNIT


UNITS_PER_CHUNK = SLOT_CHUNK // ROW_UNIT
MAIN_SLOTS = SLOTS - SLOT_CHUNK
assert MAIN_SLOTS >= MOE_TILE * TOP_K


def _dispatch_kernel(unit_ref, pad_ref, used_ref, h_ref, lists_ref, xs_ref, loc_ref, zero_ref, sems):
    i = pl.program_id(0)
    last = pl.num_programs(0) - 1
    slot = i % 2
    tn = h_ref.shape[0]

    @pl.when(i == 0)
    def _():
        zero_ref[...] = jnp.zeros_like(zero_ref)

    rel_rows = lax.broadcasted_iota(jnp.int32, (SUB_SLOTS, tn), 0).astype(F32).astype(BF16)

    def chunk(c0):
        parts = []
        for b0 in range(c0, c0 + SLOT_CHUNK, SUB_SLOTS):
            p = jnp.zeros((SUB_SLOTS, tn), BF16)
            for k in range(TOP_K):
                hit = rel_rows == (lists_ref[k:k + 1, :] - float(b0)).astype(BF16)
                p = jnp.where(hit, jnp.ones_like(p), p)
            parts.append(p)
        loc_ref[slot, c0:c0 + SLOT_CHUNK, :] = jnp.dot(
            jnp.concatenate(parts, axis=0), h_ref[...],
            preferred_element_type=F32).astype(BF16)
        for u in range(c0 // ROW_UNIT, c0 // ROW_UNIT + UNITS_PER_CHUNK):
            row = pl.multiple_of(unit_ref[i, u], ROW_UNIT)
            pltpu.make_async_copy(loc_ref.at[slot, pl.ds(u * ROW_UNIT, ROW_UNIT)],
                                  xs_ref.at[pl.ds(row, ROW_UNIT)], sems.at[slot]).start()

    for c0 in range(0, MAIN_SLOTS, SLOT_CHUNK):
        chunk(c0)

    @pl.when(used_ref[i] > MAIN_SLOTS)
    def _():
        chunk(MAIN_SLOTS)

    for p in range(PAD_UNITS):
        row = pl.multiple_of(pad_ref[i, p], ROW_UNIT)
        pltpu.make_async_copy(zero_ref, xs_ref.at[pl.ds(row, ROW_UNIT)], sems.at[slot]).start()

    def wait_all(s, tile):
        pltpu.make_async_copy(loc_ref.at[s, pl.ds(0, MAIN_SLOTS)],
                              xs_ref.at[pl.ds(0, MAIN_SLOTS)], sems.at[s]).wait()

        @pl.when(used_ref[tile] > MAIN_SLOTS)
        def _():
            pltpu.make_async_copy(loc_ref.at[s, pl.ds(MAIN_SLOTS, SLOT_CHUNK)],
                                  xs_ref.at[pl.ds(0, SLOT_CHUNK)], sems.at[s]).wait()
        pltpu.make_async_copy(loc_ref.at[s, pl.ds(0, PAD_UNITS * ROW_UNIT)],
                              xs_ref.at[pl.ds(0, PAD_UNITS * ROW_UNIT)], sems.at[s]).wait()

    @pl.when(i > 0)
    def _():
        wait_all(1 - slot, i - 1)

    @pl.when(i == last)
    def _():
        wait_all(slot, i)


def moe_dispatch(h, lists, unit_dst, pad_dst, used_rows, *, n_tiles):
    d = h.shape[1]
    cap_rows = _sorted_capacity(n_tiles) + 2 * DUMP_ROWS
    return pl.pallas_call(
        _dispatch_kernel,
        grid_spec=pltpu.PrefetchScalarGridSpec(
            num_scalar_prefetch=3,
            grid=(n_tiles,),
            in_specs=[pl.BlockSpec((MOE_TILE, d), lambda i, *_: (i, 0)),
                      pl.BlockSpec((2 * TOP_K, MOE_TILE), lambda i, *_: (0, i))],
            out_specs=pl.BlockSpec(memory_space=pl.ANY),
            scratch_shapes=[pltpu.VMEM((2, SLOTS, d), BF16), pltpu.VMEM((ROW_UNIT, d), BF16),
                            pltpu.SemaphoreType.DMA((2,))]),
        out_shape=jax.ShapeDtypeStruct((cap_rows, d), BF16),
        compiler_params=_cparams(("arbitrary",)),
        name="moe_dispatch",
    )(unit_dst, pad_dst, used_rows, h, lists)


X_RING = 3


def _expert_kernel(te_ref, nt_ref, xs_ref, wg_ref, wu_ref, wd_ref, y_ref, x_buf, wg_bf, wu_bf,
                   wd_bf, sems):
    t = pl.program_id(0)
    n_live = nt_ref[0]
    live = t < n_live
    tc = jnp.minimum(t, jnp.maximum(n_live - 1, 0))
    new_expert = jnp.logical_or(t == 0, te_ref[tc] != te_ref[jnp.maximum(tc - 1, 0)])
    rows = x_buf.shape[1]

    def x_copy(tile):
        s = tile % X_RING
        return pltpu.make_async_copy(xs_ref.at[pl.ds(pl.multiple_of(tile * rows, rows), rows)],
                                     x_buf.at[s], sems.at[s])

    for k in range(X_RING - 1):
        @pl.when(jnp.logical_and(t == 0, k < n_live))
        def _():
            x_copy(k).start()

    @pl.when(t + (X_RING - 1) < n_live)
    def _():
        x_copy(t + (X_RING - 1)).start()

    @pl.when(jnp.logical_and(live, new_expert))
    def _():
        wg_bf[...] = wg_ref[0, 0].astype(BF16)
        wu_bf[...] = wu_ref[0, 0].astype(BF16)
        wd_bf[...] = wd_ref[0, 0].astype(BF16)

    @pl.when(live)
    def _():
        x_copy(t).wait()
        x = x_buf[t % X_RING]
        act = (_silu(jnp.dot(x, wg_bf[...], preferred_element_type=F32))
               * jnp.dot(x, wu_bf[...], preferred_element_type=F32))
        y_ref[...] = jnp.dot(act.astype(BF16), wd_bf[...], preferred_element_type=F32).astype(BF16)


def moe_experts(xs, w_gate, w_up, w_down, layer, tile_expert, n_row_tiles):
    d = xs.shape[1]
    cap_rows = xs.shape[0] - 2 * DUMP_ROWS
    ff = w_gate.shape[3]
    last_live = lambda nt: jnp.maximum(nt[0] - 1, 0)
    row_map = lambda t, te, nt: (jnp.minimum(t, last_live(nt)), 0)
    w_map = lambda t, te, nt: (layer, te[jnp.minimum(t, last_live(nt))], 0, 0)
    return pl.pallas_call(
        _expert_kernel,
        grid_spec=pltpu.PrefetchScalarGridSpec(
            num_scalar_prefetch=2,
            grid=(cap_rows // EXPERT_ROWS,),
            in_specs=[pl.BlockSpec(memory_space=pl.ANY),
                      pl.BlockSpec((1, 1, d, ff), w_map), pl.BlockSpec((1, 1, d, ff), w_map),
                      pl.BlockSpec((1, 1, ff, d), w_map)],
            out_specs=pl.BlockSpec((EXPERT_ROWS, d), row_map),
            scratch_shapes=[pltpu.VMEM((X_RING, EXPERT_ROWS, d), BF16),
                            pltpu.VMEM((d, ff), BF16), pltpu.VMEM((d, ff), BF16),
                            pltpu.VMEM((ff, d), BF16), pltpu.SemaphoreType.DMA((X_RING,))]),
        out_shape=jax.ShapeDtypeStruct((cap_rows, d), BF16),
        compiler_params=_cparams(("arbitrary",)),
        name="moe_experts",
    )(tile_expert, n_row_tiles, xs, w_gate, w_up, w_down)


def _combine_kernel(unit_ref, used_ref, ys_ref, tok_ref, h_ref, x_ref, mod_ref, fnw_ref, sg_ref,
                    su_ref, sd_ref, o_ref, loc_ref, acc_ref, sems, *, final_norm):
    i = pl.program_id(0)
    last = pl.num_programs(0) - 1
    slot = i % 2
    tn = h_ref.shape[0]
    def fetch(tile, s):
        def units(u0, u1):
            for u in range(u0, u1):
                row = pl.multiple_of(unit_ref[tile, u], ROW_UNIT)
                pltpu.make_async_copy(ys_ref.at[pl.ds(row, ROW_UNIT)],
                                      loc_ref.at[s, pl.ds(u * ROW_UNIT, ROW_UNIT)], sems.at[s]).start()
        units(0, MAIN_SLOTS // ROW_UNIT)

        @pl.when(used_ref[tile] > MAIN_SLOTS)
        def _():
            units(MAIN_SLOTS // ROW_UNIT, N_UNITS)

    @pl.when(i == 0)
    def _():
        fetch(0, 0)

    @pl.when(i < last)
    def _():
        fetch(i + 1, 1 - slot)

    h = h_ref[...]
    act = (_silu(jnp.dot(h, sg_ref[...], preferred_element_type=F32))
           * jnp.dot(h, su_ref[...], preferred_element_type=F32))
    acc = jnp.dot(act.astype(BF16), sd_ref[...], preferred_element_type=F32)
    tail = used_ref[i] > MAIN_SLOTS
    pltpu.make_async_copy(ys_ref.at[pl.ds(0, MAIN_SLOTS)], loc_ref.at[slot, pl.ds(0, MAIN_SLOTS)],
                          sems.at[slot]).wait()

    @pl.when(tail)
    def _():
        pltpu.make_async_copy(ys_ref.at[pl.ds(0, SLOT_CHUNK)],
                              loc_ref.at[slot, pl.ds(MAIN_SLOTS, SLOT_CHUNK)], sems.at[slot]).wait()

    tok = tok_ref[...]

    rel_cols = lax.broadcasted_iota(jnp.int32, (tn, SUB_SLOTS), 1).astype(F32).astype(BF16)
    gate_w = [tok[:, TOP_K + k:TOP_K + k + 1].astype(BF16) for k in range(TOP_K)]

    def chunk(c0):
        parts = []
        for b0 in range(c0, c0 + SLOT_CHUNK, SUB_SLOTS):
            q = jnp.zeros((tn, SUB_SLOTS), BF16)
            for k in range(TOP_K):
                hit = rel_cols == (tok[:, k:k + 1] - float(b0)).astype(BF16)
                q = jnp.where(hit, gate_w[k], q)
            parts.append(q)
        return jnp.dot(jnp.concatenate(parts, axis=1), loc_ref[slot, c0:c0 + SLOT_CHUNK, :],
                       preferred_element_type=F32)

    for c0 in range(0, MAIN_SLOTS, SLOT_CHUNK):
        acc = acc + chunk(c0)
    acc_ref[...] = acc

    @pl.when(tail)
    def _():
        acc_ref[...] += chunk(MAIN_SLOTS)

    x = x_ref[...] + mod_ref[0][5:6] * acc_ref[...]
    if final_norm:
        x = _rms(x) * fnw_ref[...]
    o_ref[...] = x


def moe_combine(ys, tok, h, x_all, mods, fnw, sh_gate, sh_up, sh_down, unit_dst, used_rows, *,
                n_tiles, t_lat, seq, final_norm):
    d = h.shape[1]
    tn = MOE_TILE
    const = lambda a: pl.BlockSpec(a.shape, lambda i, *_: (0,) * a.ndim)
    kern = functools.partial(_combine_kernel, final_norm=final_norm)
    return pl.pallas_call(
        kern,
        grid_spec=pltpu.PrefetchScalarGridSpec(
            num_scalar_prefetch=2,
            grid=(n_tiles,),
            in_specs=[pl.BlockSpec(memory_space=pl.ANY),
                      pl.BlockSpec((tn, LANES), lambda i, *_: (i, 0)),
                      pl.BlockSpec((tn, d), lambda i, *_: (i, 0)),
                      pl.BlockSpec((tn, d), lambda i, *_: (i, 0)),
                      pl.BlockSpec((1, 6, d),
                                   lambda i, *_: (_seg_of_tile(i, tn, t_lat, seq), 0, 0)),
                      const(fnw), const(sh_gate), const(sh_up), const(sh_down)],
            out_specs=pl.BlockSpec((tn, d), lambda i, *_: (i, 0)),
            scratch_shapes=[pltpu.VMEM((2, SLOTS, d), BF16), pltpu.VMEM((tn, d), F32),
                            pltpu.SemaphoreType.DMA((2,))]),
        out_shape=jax.ShapeDtypeStruct((n_tiles * tn, d), F32),
        compiler_params=_cparams(("arbitrary",)),
        name="moe_combine",
    )(unit_dst, used_rows, ys, tok, h, x_all, mods, fnw, sh_gate, sh_up, sh_down)


def moe_block(h2, logits, router_bias, x_mid, mods, fnw, w_gate, w_up, w_down,
              sh_gate, sh_up, sh_down, *, layer, t_lat, seq, final_norm):
    n_tiles = h2.shape[0] // MOE_TILE
    lists, tok, pc = router(logits, router_bias)
    unit_dst, unit_src, pad_dst, used_rows, tile_expert, n_row_tiles = moe_plan(
        pc[:, :, 0].astype(jnp.int32), n_tiles)
    xs = moe_dispatch(h2, lists, unit_dst, pad_dst, used_rows, n_tiles=n_tiles)
    ys = moe_experts(xs, w_gate, w_up, w_down, layer, tile_expert, n_row_tiles)
    return moe_combine(ys, tok, h2, x_mid, mods, fnw, sh_gate.astype(BF16), sh_up.astype(BF16),
                       sh_down.astype(BF16), unit_src, used_rows, n_tiles=n_tiles, t_lat=t_lat,
                       seq=seq, final_norm=final_norm)


def _rope_tables(seq, tm):
    half = RET_DK // 2
    pos = jnp.arange(seq, dtype=jnp.int32)
    row = (pos // GRID_W).astype(F32)
    col = (pos % GRID_W).astype(F32)
    inv = ROPE_BASE ** (-jnp.arange(0, half, 2, dtype=F32) / half)
    a_row, a_col = row[:, None] * inv[None, :], col[:, None] * inv[None, :]
    cos_h = jnp.concatenate([jnp.cos(a_row)] * 2 + [jnp.cos(a_col)] * 2, axis=-1)
    sin_h = jnp.concatenate([-jnp.sin(a_row), jnp.sin(a_row), -jnp.sin(a_col), jnp.sin(a_col)], -1)
    reps = 2 * RET_HEADS
    cos_t = jnp.concatenate([jnp.tile(cos_h, (1, reps)), jnp.ones((tm, reps * RET_DK), F32)], 0)
    sin_t = jnp.concatenate([jnp.tile(sin_h, (1, reps)), jnp.zeros((tm, reps * RET_DK), F32)], 0)
    return cos_t, sin_t


def _rope_partner_columns():
    quarter = RET_DK // 4
    idx = jnp.arange(2 * RET_HEADS * RET_DK, dtype=jnp.int32)
    within = idx % (2 * quarter)
    return jnp.where(within < quarter, idx + quarter, idx - quarter)


def kernel(x, c, ctx, c_ctx, ada_w, ada_b, norm1_w, norm2_w, ev_w_in, ev_short_w, ev_short_b, ev_filt_w1, ev_filt_b1, ev_filt_w2, ev_filt_b2, ev_filt_w3, ev_filt_b3, ev_filt_w4, ev_filt_freq, ev_hyena_bias, ev_w_out, od_w_in, od_gate_w1_f, od_gate_w2_f, od_gate_b_f, od_gate_w1_b, od_gate_w2_b, od_gate_b_b, od_norm_w, od_w_out, router_w, router_bias, exp_w_gate, exp_w_up, exp_w_down, sh_w_gate, sh_w_up, sh_w_down, final_norm_w):
    batch, seq, d = x.shape
    ctx_len = ctx.shape[1]
    depth = ada_w.shape[0]
    t_lat, t_ctx = batch * seq, batch * ctx_len
    t_all = t_lat + t_ctx
    tm = 1024

    x_all = jnp.concatenate([x.reshape(t_lat, d), ctx.reshape(t_ctx, d)], axis=0)
    cond8 = jnp.concatenate([c_ctx[None, :], c, jnp.zeros((8 - 1 - batch, d), F32)], axis=0)
    mods_all = adaln_rows(cond8, ada_w, ada_b).reshape(depth, 8, 6, d)

    for i in range(depth):
        last = i == depth - 1
        j = i // 2
        mods = mods_all[i]
        nw1, nw2 = norm1_w[i][None, :], norm2_w[i][None, :]
        if i % 2 == 0:
            qk = 2 * RET_HEADS * RET_DK
            k_scale = jnp.concatenate([jnp.ones((qk // 2,), F32),
                                       jnp.full((qk // 2,), RET_DK ** -0.5, F32)])
            w_in = ev_w_in[j]
            w_qk = w_in[:, :qk] * k_scale
            w_ext = jnp.concatenate([w_qk, w_in[:, qk:], w_qk[:, _rope_partner_columns()]],
                                    axis=1).astype(BF16)
            cos_t, sin_t = _rope_tables(seq, tm)
            proj = inproj_even(x_all, mods, nw1, w_ext, cos_t, sin_t, t_lat=t_lat, seq=seq, tm=tm)
            log_g = [math.log1p(-2.0 ** (-5.0 - h)) for h in range(RET_HEADS)]
            mix_a = bidir_scan(proj, None, None, jnp.ones((1, RET_DV), F32),
                               batch=batch, seq=seq, ctx_len=ctx_len, heads=RET_HEADS, dk=RET_DK,
                               dv=RET_DV, chunk=RET_CHUNK, q_blk=0, k_blk=1, v_blk=1, g_blk=2,
                               log_decay_f=log_g, log_decay_b=log_g[::-1])
            hp = dict(short_w=ev_short_w[j], short_b=ev_short_b[j][None, :],
                      w1=jnp.pad(ev_filt_w1[j], ((0, LANES - HYENA_EMB), (0, 0))),
                      b1=ev_filt_b1[j][None, :], w2=ev_filt_w2[j], b2=ev_filt_b2[j][None, :],
                      w3=ev_filt_w3[j], b3=ev_filt_b3[j][None, :], w4=ev_filt_w4[j],
                      freq=ev_filt_freq[j][None, :], bias=ev_hyena_bias[j][None, :])
            hy_lat = hyena_long_conv(proj, hp, row0=0, batch=batch, seq_len=seq, z_blk=1)
            hy_ctx = hyena_long_conv(proj, hp, row0=t_lat, batch=batch, seq_len=ctx_len, z_blk=1)
            mix_b = jnp.concatenate([hy_lat, hy_ctx], axis=0)
            a_blk, b_blk = 0, 0
            w_out = ev_w_out[j].astype(BF16)
        else:
            kd = GLA_HEADS * GLA_DK
            pad_cols = LANES - 2 * GLA_RANK
            w_ext = jnp.concatenate([od_w_in[j], od_gate_w1_f[j], od_gate_w1_b[j],
                                     jnp.zeros((d, pad_cols), F32)], axis=1).astype(BF16)
            w2 = jnp.concatenate(
                [jnp.pad(od_gate_w2_f[j], ((0, LANES - GLA_RANK), (0, 0))),
                 jnp.pad(od_gate_w2_b[j], ((GLA_RANK, LANES - 2 * GLA_RANK), (0, 0)))], axis=1)
            w2_hi = w2.astype(BF16)
            w2_split = jnp.concatenate([w2_hi, w2_hi, (w2 - w2_hi.astype(F32)).astype(BF16)], axis=0)
            b2 = jnp.concatenate([od_gate_b_f[j], od_gate_b_b[j]])[None, :]
            proj, la_f, la_b = inproj_odd(x_all, mods, nw1, w_ext, w2_split, b2,
                                          t_lat=t_lat, seq=seq, tm=tm)
            mix_a = bidir_scan(proj, la_f, la_b, od_norm_w[j][None, :],
                               batch=batch, seq=seq, ctx_len=ctx_len, heads=GLA_HEADS, dk=GLA_DK,
                               dv=GLA_DV, chunk=GLA_CHUNK, q_blk=0, k_blk=1, v_blk=1, g_blk=2)
            mix_b = mix_a
            a_blk, b_blk = 0, 1
            w_out = od_w_out[j].astype(BF16)

        n_rows = t_lat if last else t_all
        rw_t = router_w[i].T
        rw_hi = rw_t.astype(BF16)
        rw_split = jnp.stack([rw_hi, (rw_t - rw_hi.astype(F32)).astype(BF16)])
        x_mid, h2, logits = outproj(mix_a, a_blk, mix_b, b_blk, x_all, mods, nw2, w_out,
                                    rw_split, n_rows=n_rows, t_lat=t_lat, seq=seq, tm=tm)
        x_all = moe_block(h2, logits, router_bias[i], x_mid, mods, final_norm_w[None, :],
                          exp_w_gate, exp_w_up, exp_w_down,
                          sh_w_gate[i], sh_w_up[i], sh_w_down[i],
                          layer=i, t_lat=t_lat, seq=seq, final_norm=last)
    return x_all[:t_lat].reshape(batch, seq, d)
```

```python
import functools
import math

import jax
import jax.numpy as jnp
from jax import lax
from jax.experimental import pallas as pl
from jax.experimental.pallas import tpu as pltpu

F32 = jnp.float32
BF16 = jnp.bfloat16
HIGHEST = lax.Precision.HIGHEST

D_MODEL = 1024
GRID_W = 64
NORM_EPS = 1e-6
RET_HEADS, RET_DK, RET_DV, RET_CHUNK = 4, 64, 128, 128
ROPE_BASE = 10000.0
HYENA_CH, HYENA_EMB = 512, 33
HYENA_FAST_DECAY, HYENA_SLOW_DECAY, HYENA_TARGET = 0.3, 1.5, 1e-2
GLA_HEADS, GLA_DK, GLA_DV, GLA_RANK, GLA_TAU, GLA_CHUNK = 4, 128, 256, 16, 16.0, 64
N_EXPERTS, TOP_K, N_GROUPS, TOPK_GROUPS = 64, 8, 8, 4
GROUP_SIZE = N_EXPERTS // N_GROUPS
EXPERT_FF = 256
ROUTED_SCALE = 2.5

LANES = 128
SCAN_ROWS = 256
SCAN_LAT_ROWS = 1024
MOE_TILE = 256
ROW_UNIT = 16
EXPERT_ROWS = 1024
SLOT_CHUNK = 512
SUB_SLOTS = 256
SLOTS = -(-(MOE_TILE * TOP_K + N_EXPERTS * (ROW_UNIT - 1) + ROW_UNIT) // SLOT_CHUNK) * SLOT_CHUNK
N_UNITS = SLOTS // ROW_UNIT
PAD_UNITS = EXPERT_ROWS // ROW_UNIT
VMEM_LIMIT = 56 * 1024 * 1024


def _cparams(sem):
    return pltpu.CompilerParams(dimension_semantics=sem, vmem_limit_bytes=VMEM_LIMIT)


def _rms(x):
    return x * lax.rsqrt(jnp.mean(x * x, axis=-1, keepdims=True) + NORM_EPS)


def _silu(x):
    return x * jax.nn.sigmoid(x)


def _adaln_kernel(c_ref, w_ref, b_ref, o_ref):
    c = c_ref[...]
    o_ref[0] = jnp.dot(_silu(c), w_ref[0], preferred_element_type=F32,
                       precision=HIGHEST) + b_ref[0]


def adaln_rows(cond8, ada_w, ada_b):
    depth, d, n = ada_w.shape
    tn = 512
    return pl.pallas_call(
        _adaln_kernel,
        grid=(depth, n // tn),
        in_specs=[pl.BlockSpec((8, d), lambda l, j: (0, 0)),
                  pl.BlockSpec((1, d, tn), lambda l, j: (l, 0, j)),
                  pl.BlockSpec((1, 1, tn), lambda l, j: (l, 0, j))],
        out_specs=pl.BlockSpec((1, 8, tn), lambda l, j: (l, 0, j)),
        out_shape=jax.ShapeDtypeStruct((depth, 8, n), F32),
        compiler_params=_cparams(("parallel", "parallel")),
        name="adaln_rows",
    )(cond8, ada_w, ada_b.reshape(depth, 1, n))


def _modulated_norm(x, mod, nw, shift_row, scale_row):
    return _rms(x) * nw * (1.0 + mod[scale_row:scale_row + 1]) + mod[shift_row:shift_row + 1]


def _split_row_specs(tm, width, n_lat_tiles, col_blk=0, ctx_row0=0):
    ctx_blk0 = ctx_row0 // tm
    return [pl.BlockSpec((tm, width), lambda i: (jnp.minimum(i, n_lat_tiles - 1), col_blk)),
            pl.BlockSpec((tm, width),
                         lambda i: (ctx_blk0 + jnp.maximum(i - n_lat_tiles, 0), col_blk))]


def _pick_rows(lat_ref, ctx_ref, n_lat_tiles):
    return jnp.where(pl.program_id(0) < n_lat_tiles, lat_ref[...], ctx_ref[...])


def _inproj_even_kernel(xl_ref, xc_ref, mod_ref, nw_ref, w_ref, cos_ref, sin_ref, o_ref, *,
                        n_lat_tiles):
    x = _pick_rows(xl_ref, xc_ref, n_lat_tiles)
    h = _modulated_norm(x, mod_ref[0], nw_ref[...], 0, 1).astype(BF16)
    n_main = o_ref.shape[1]
    qk = jnp.dot(h, w_ref[:, 0:512], preferred_element_type=F32)
    qk_sw = jnp.dot(h, w_ref[:, n_main:n_main + 512], preferred_element_type=F32)
    o_ref[:, 0:512] = (qk * cos_ref[...] + qk_sw * sin_ref[...]).astype(BF16)
    for c0 in range(512, n_main, 512):
        o_ref[:, c0:c0 + 512] = jnp.dot(h, w_ref[:, c0:c0 + 512],
                                        preferred_element_type=F32).astype(BF16)


def _inproj_odd_kernel(x_ref, mod_ref, nw_ref, w_ref, w2_ref, b2_ref, o_ref, laf_ref, lab_ref):
    h = _modulated_norm(x_ref[...], mod_ref[0], nw_ref[...], 0, 1).astype(BF16)
    n_main = o_ref.shape[1]
    kd = GLA_HEADS * GLA_DK
    q = jnp.dot(h, w_ref[:, 0:kd], preferred_element_type=F32)
    o_ref[:, 0:kd] = (q * (GLA_DK ** -0.5)).astype(BF16)
    for c0 in range(kd, n_main, 512):
        o_ref[:, c0:c0 + 512] = jnp.dot(h, w_ref[:, c0:c0 + 512],
                                        preferred_element_type=F32).astype(BF16)
    low = jnp.dot(h, w_ref[:, n_main:n_main + LANES], preferred_element_type=F32)

    low_hi = low.astype(BF16)
    low_lo = (low - low_hi.astype(F32)).astype(BF16)
    z = jnp.dot(jnp.concatenate([low_hi, low_lo, low_hi], axis=1), w2_ref[...],
                preferred_element_type=F32) + b2_ref[...]
    la = (jnp.minimum(z, 0.0) - jnp.log(1.0 + jnp.exp(-jnp.abs(z)))) * (1.0 / GLA_TAU)
    laf_ref[...] = la[:, 0:kd]
    lab_ref[...] = la[:, kd:2 * kd]


def _seg_of_tile(i, tm, t_lat, seq):
    return jnp.where(i < t_lat // tm, 1 + (i * tm) // seq, 0)


def inproj_even(x_rows, mods, nw, w_ext, cos_t, sin_t, *, t_lat, t_all, seq, tm=512):
    x_lat, x_ctx, ctx_row0 = x_rows
    d = x_lat.shape[1]
    n_ext = w_ext.shape[1]
    n_main = n_ext - 512
    n_lat_tiles, pos_tiles = t_lat // tm, seq // tm

    def pos_map(i):
        return (jnp.where(i < n_lat_tiles, i % pos_tiles, pos_tiles), 0)

    return pl.pallas_call(
        functools.partial(_inproj_even_kernel, n_lat_tiles=n_lat_tiles),
        grid=(t_all // tm,),
        in_specs=_split_row_specs(tm, d, n_lat_tiles, ctx_row0=ctx_row0) + [
                  pl.BlockSpec((1, 6, d), lambda i: (_seg_of_tile(i, tm, t_lat, seq), 0, 0)),
                  pl.BlockSpec((1, d), lambda i: (0, 0)),
                  pl.BlockSpec((d, n_ext), lambda i: (0, 0)),
                  pl.BlockSpec((tm, 512), pos_map),
                  pl.BlockSpec((tm, 512), pos_map)],
        out_specs=pl.BlockSpec((tm, n_main), lambda i: (i, 0)),
        out_shape=jax.ShapeDtypeStruct((t_all, n_main), BF16),
        compiler_params=_cparams(("parallel",)),
        name="inproj_even",
    )(x_lat, x_ctx, mods, nw, w_ext, cos_t, sin_t)


def inproj_odd(x_all, mods, nw, w_ext, w2, b2, *, t_lat, seq, tm=512):
    t_all, d = x_all.shape
    n_main = w_ext.shape[1] - LANES
    kd = GLA_HEADS * GLA_DK
    full = lambda shape: pl.BlockSpec(shape, lambda i: (0,) * len(shape))
    return pl.pallas_call(
        _inproj_odd_kernel,
        grid=(t_all // tm,),
        in_specs=[pl.BlockSpec((tm, d), lambda i: (i, 0)),
                  pl.BlockSpec((1, 6, d), lambda i: (_seg_of_tile(i, tm, t_lat, seq), 0, 0)),
                  full((1, d)), full(w_ext.shape), full(w2.shape), full(b2.shape)],
        out_specs=[pl.BlockSpec((tm, n_main), lambda i: (i, 0)),
                   pl.BlockSpec((tm, kd), lambda i: (i, 0)),
                   pl.BlockSpec((tm, kd), lambda i: (i, 0))],
        out_shape=[jax.ShapeDtypeStruct((t_all, n_main), BF16),
                   jax.ShapeDtypeStruct((t_all, kd), F32),
                   jax.ShapeDtypeStruct((t_all, kd), F32)],
        compiler_params=_cparams(("parallel",)),
        name="inproj_odd",
    )(x_all, mods, nw, w_ext, w2, b2)


def _scan_kernel(*refs, heads, dk, dv, chunk, gated, reverse, log_decay, sub_blocks):
    n_in = 3 + int(gated) + (2 if reverse else 0)
    ctx_in, lat_in = refs[:n_in], refs[n_in:2 * n_in]
    rest = list(refs[2 * n_in:])
    nw_ref = rest.pop(0) if reverse else None
    out_lat, out_ctx, state_ref = rest[0], rest[1], rest[2]
    dec_ref = None if gated else rest[3]

    hpg = LANES // dk
    groups = heads // hpg
    rows = SCAN_ROWS
    n_chunks = rows // chunk
    j = pl.program_id(1)

    row_i = lax.broadcasted_iota(jnp.int32, (rows, rows), 0)
    col_i = lax.broadcasted_iota(jnp.int32, (rows, rows), 1)
    same_chunk = (row_i // chunk) == (col_i // chunk)
    keep = jnp.logical_and(same_chunk, (col_i >= row_i) if reverse else (row_i >= col_i))

    @pl.when(j == 0)
    def _init():
        state_ref[...] = jnp.zeros_like(state_ref)
        if not gated:
            pos = lax.broadcasted_iota(jnp.int32, (rows, LANES), 0) % chunk
            steps = ((chunk - pos) if reverse else (pos + 1)).astype(F32)
            lane = lax.broadcasted_iota(jnp.int32, (rows, LANES), 1)
            for g in range(groups):
                lg = jnp.zeros((rows, LANES), F32)
                for a in range(hpg):
                    lg = jnp.where(lane // dk == a, log_decay[g * hpg + a], lg)
                logb = steps * lg
                b_end = float(chunk) * lg
                dec_ref[g, 0] = jnp.exp(logb)
                dec_ref[g, 1] = jnp.exp(-logb)
                dec_ref[g, 2] = jnp.exp(b_end - logb)
                dec_ref[g, 3] = jnp.exp(b_end)

    lane1 = lax.broadcasted_iota(jnp.int32, (1, LANES), 1)
    order = range(n_chunks - 1, -1, -1) if reverse else range(n_chunks)
    end_row = lambda c: c * chunk if reverse else (c + 1) * chunk - 1

    def scan_rows(in_refs, out_ref, r0):
        q_ref, k_ref, v_ref = in_refs[:3]
        blk = slice(r0, r0 + rows)
        if reverse:
            oprev_ref, gate_ref = in_refs[-2:]
        if gated:
            tri = jnp.where(keep, 1.0, 0.0).astype(BF16)
            la = in_refs[3][blk, :]
            la_hi = la.astype(BF16)
            la_lo = (la - la_hi.astype(F32)).astype(BF16)
            logb_all = (jnp.dot(tri, la_hi, preferred_element_type=F32)
                        + jnp.dot(tri, la_lo, preferred_element_type=F32))
        for g in range(groups):
            ksl = slice(g * LANES, (g + 1) * LANES)
            qg = q_ref[blk, ksl].astype(F32)
            kg = k_ref[blk, ksl].astype(F32)
            if gated:
                logb = logb_all[:, ksl]
                ends = [logb[end_row(c):end_row(c) + 1] for c in range(n_chunks)]
                b_end = jnp.concatenate([jnp.broadcast_to(e, (chunk, LANES)) for e in ends], axis=0)
                e_q, e_k, e_s = jnp.exp(logb), jnp.exp(-logb), jnp.exp(b_end - logb)
                e_e = [jnp.exp(e) for e in ends]
            else:
                e_q, e_k, e_s = dec_ref[g, 0], dec_ref[g, 1], dec_ref[g, 2]
                e_e = [dec_ref[g, 3][0:1]] * n_chunks
            qd = qg * e_q
            kd_ = (kg * e_k).astype(BF16)
            ks = (kg * e_s).astype(BF16)
            for a in range(hpg):
                h = g * hpg + a
                qa = (jnp.where(lane1 // dk == a, qd, 0.0) if hpg > 1 else qd).astype(BF16)
                vh = v_ref[blk, h * dv:(h + 1) * dv]
                s = lax.dot_general(qa, kd_, (((1,), (1,)), ((), ())), preferred_element_type=F32)
                o_intra = jnp.dot(jnp.where(keep, s, 0.0).astype(BF16), vh,
                                  preferred_element_type=F32)
                st = state_ref[h]
                for c in order:
                    rs = slice(c * chunk, (c + 1) * chunk)
                    o = o_intra[rs] + lax.dot_general(qa[rs], st.astype(BF16),
                                                      (((1,), (1,)), ((), ())),
                                                      preferred_element_type=F32)
                    st = e_e[c] * st + lax.dot_general(vh[rs], ks[rs], (((0,), (0,)), ((), ())),
                                                       preferred_element_type=F32)
                    osl = (slice(r0 + c * chunk, r0 + (c + 1) * chunk), slice(h * dv, (h + 1) * dv))
                    if reverse:
                        o = _rms(o + oprev_ref[osl]) * nw_ref[...]
                        out_ref[osl] = (o * _silu(gate_ref[osl].astype(F32))).astype(BF16)
                    else:
                        out_ref[osl] = o
                state_ref[h] = st

    @pl.when(j == 0)
    def _():
        scan_rows(ctx_in, out_ctx, 0)

    @pl.when(j > 0)
    def _():
        for sb in (range(sub_blocks - 1, -1, -1) if reverse else range(sub_blocks)):
            scan_rows(lat_in, out_lat, sb * rows)


def bidir_scan(proj, la_f, la_b, norm_w, *, batch, seq, ctx_len, heads, dk, dv, chunk,
               q_blk, k_blk, v_blk, g_blk, log_decay_f=None, log_decay_b=None):
    gated = la_f is not None
    hk, hv = heads * dk, heads * dv
    rb, lb = SCAN_ROWS, SCAN_LAT_ROWS
    lat_blocks = seq // lb
    ctx_base = (batch * seq) // rb
    assert ctx_len == rb

    lat_fwd = lambda b, j: b * lat_blocks + jnp.maximum(j - 1, 0)
    lat_bwd = lambda b, j: b * lat_blocks + lat_blocks - 1 - jnp.maximum(j - 1, 0)

    prev = None
    for reverse, lat_of, la, ld in ((False, lat_fwd, la_f, log_decay_f),
                                    (True, lat_bwd, la_b, log_decay_b)):
        ctx_spec = lambda w, cb, base=ctx_base: pl.BlockSpec(
            (rb, w), lambda b, j, cb=cb: (base + b, cb))
        lat_spec = lambda w, cb: pl.BlockSpec((lb, w), lambda b, j, cb=cb: (lat_of(b, j), cb))
        in_specs, args = [], []
        for spec, is_ctx in ((ctx_spec, True), (lat_spec, False)):
            in_specs += [spec(hk, q_blk), spec(hk, k_blk), spec(hv, v_blk)]
            args += [proj, proj, proj]
            if gated:
                in_specs.append(spec(hk, 0))
                args.append(la)
            if reverse:
                in_specs += [ctx_spec(hv, 0, base=0) if is_ctx else spec(hv, 0), spec(hv, g_blk)]
                args += [prev[1] if is_ctx else prev[0], proj]
        if reverse:
            in_specs.append(pl.BlockSpec((1, dv), lambda b, j: (0, 0)))
            args.append(norm_w)
        scratch = [pltpu.VMEM((heads, dv, LANES), F32)]
        if not gated:
            scratch.append(pltpu.VMEM((hk // LANES, 4, rb, LANES), F32))
        kern = functools.partial(_scan_kernel, heads=heads, dk=dk, dv=dv, chunk=chunk,
                                 gated=gated, reverse=reverse, log_decay=ld, sub_blocks=lb // rb)
        out_dtype = BF16 if reverse else F32
        prev = pl.pallas_call(
            kern,
            grid=(batch, lat_blocks + 1),
            in_specs=in_specs,
            out_specs=[lat_spec(hv, 0), ctx_spec(hv, 0, base=0)],
            out_shape=[jax.ShapeDtypeStruct((batch * seq, hv), out_dtype),
                       jax.ShapeDtypeStruct((batch * ctx_len, hv), out_dtype)],
            scratch_shapes=scratch,
            compiler_params=_cparams(("parallel", "arbitrary")),
            name="scan_bwd" if reverse else "scan_fwd",
        )(*args)
    return tuple(prev)


def _shortconv_kernel(z_ref, zp_ref, zn_ref, w_ref, b_ref, u_ref, x0_ref, *, tiles_per_seq):
    i = pl.program_id(0)
    tm = z_ref.shape[0]
    z = z_ref[...].astype(F32)
    first = (i % tiles_per_seq) == 0
    last = (i % tiles_per_seq) == tiles_per_seq - 1
    halo = zp_ref.shape[0]
    prev_row = jnp.where(first, 0.0, zp_ref[halo - 1:halo, :].astype(F32))
    next_row = jnp.where(last, 0.0, zn_ref[0:1, :].astype(F32))
    row = lax.broadcasted_iota(jnp.int32, z.shape, 0)
    z_prev = jnp.where(row == 0, prev_row, pltpu.roll(z, 1, 0))
    z_next = jnp.where(row == tm - 1, next_row, pltpu.roll(z, tm - 1, 0))
    y = w_ref[0:1] * z_prev + w_ref[1:2] * z + w_ref[2:3] * z_next + b_ref[...]
    c = HYENA_CH
    x0_ref[...] = y[:, 0:c].astype(BF16)
    u_ref[...] = (y[:, c:2 * c] * y[:, 2 * c:3 * c]).astype(BF16)


def hyena_shortconv(proj, short_w, short_b, *, row0, batch, seq_len, z_blk, tm=256):
    halo = 16
    tiles_per_seq = seq_len // tm
    n_tiles = batch * tiles_per_seq
    t0, h_per_tile = row0 // tm, tm // halo
    nz = 3 * HYENA_CH
    n_halo_blocks = proj.shape[0] // halo
    out_map = lambda i: (i % tiles_per_seq, i // tiles_per_seq)
    kern = functools.partial(_shortconv_kernel, tiles_per_seq=tiles_per_seq)
    return pl.pallas_call(
        kern,
        grid=(n_tiles,),
        in_specs=[pl.BlockSpec((tm, nz), lambda i: (t0 + i, z_blk)),
                  pl.BlockSpec((halo, nz),
                               lambda i: (jnp.maximum((t0 + i) * h_per_tile - 1, 0), z_blk)),
                  pl.BlockSpec((halo, nz),
                               lambda i: (jnp.minimum((t0 + i + 1) * h_per_tile,
                                                      n_halo_blocks - 1), z_blk)),
                  pl.BlockSpec((3, nz), lambda i: (0, 0)),
                  pl.BlockSpec((1, nz), lambda i: (0, 0))],
        out_specs=[pl.BlockSpec((tm, HYENA_CH), out_map), pl.BlockSpec((tm, HYENA_CH), out_map)],
        out_shape=[jax.ShapeDtypeStruct((seq_len, batch * HYENA_CH), BF16)] * 2,
        compiler_params=_cparams(("parallel",)),
        name="hyena_shortconv",
    )(proj, proj, proj, short_w, short_b)


def _filter_kernel(feat_ref, w1_ref, b1_ref, w2_ref, b2_ref, w3_ref, b3_ref, w4_ref, fq_ref,
                   o_ref, *, seq_len):
    i = pl.program_id(0)
    tl = feat_ref.shape[0]
    fq = fq_ref[...]
    dot = lambda a, b: jnp.dot(a, b, preferred_element_type=F32, precision=HIGHEST)
    h = jnp.sin(fq * (dot(feat_ref[...], w1_ref[...]) + b1_ref[...]))
    h = jnp.sin(fq * (dot(h, w2_ref[...]) + b2_ref[...]))
    h = jnp.sin(fq * (dot(h, w3_ref[...]) + b3_ref[...]))
    h = dot(h, w4_ref[...])
    c = HYENA_CH
    max_decay = math.log(HYENA_TARGET) / HYENA_FAST_DECAY
    min_decay = math.log(HYENA_TARGET) / HYENA_SLOW_DECAY
    ch = lax.broadcasted_iota(jnp.int32, (tl, c), 1).astype(F32)
    deltas = min_decay + ch * ((max_decay - min_decay) / (c - 1))
    row = lax.broadcasted_iota(jnp.int32, (tl, c), 0) + i * tl
    t = row.astype(F32) * (1.0 / (seq_len - 1))
    window = jnp.exp(-t * jnp.abs(deltas))
    h_f = h[:, 0:c] * window
    h_b = jnp.where(row == 0, 0.0, h[:, c:2 * c] * window)
    o_ref[:, 0:c] = (h_f + h_b).astype(BF16)
    o_ref[:, c:2 * c] = (h_f - h_b).astype(BF16)


def hyena_filter_pair(feats, w1p, b1, w2, b2, w3, b3, w4, fq):
    seq_len = feats.shape[0]
    tl = min(512, seq_len)
    full = lambda a: pl.BlockSpec(a.shape, lambda i: (0,) * a.ndim)
    kern = functools.partial(_filter_kernel, seq_len=seq_len)
    return pl.pallas_call(
        kern,
        grid=(seq_len // tl,),
        in_specs=[pl.BlockSpec((tl, feats.shape[1]), lambda i: (i, 0)),
                  full(w1p), full(b1), full(w2), full(b2), full(w3), full(b3), full(w4), full(fq)],
        out_specs=pl.BlockSpec((tl, 2 * HYENA_CH), lambda i: (i, 0)),
        out_shape=jax.ShapeDtypeStruct((seq_len, 2 * HYENA_CH), BF16),
        compiler_params=_cparams(("parallel",)),
        name="hyena_filter",
    )(feats, w1p, b1, w2, b2, w3, b3, w4, fq)


def _filter_spectrum_kernel(c_ref, s_ref, f_ref, hc_ref, hs_ref, kre_ref, kim_ref, *, scale):
    ch = HYENA_CH
    f = f_ref[...]
    a = jnp.dot(c_ref[...], f, preferred_element_type=F32)
    b = jnp.dot(s_ref[...], f, preferred_element_type=F32)
    hc, hs = hc_ref[...], hs_ref[...]
    kre_ref[...] = (hc * a[:, 0:ch] + hs * b[:, 0:ch]) * scale
    kim_ref[...] = (hs * a[:, ch:2 * ch] - hc * b[:, ch:2 * ch]) * scale


def filter_spectrum(cs, ss, filt, half_cos, half_sin):
    n = cs.shape[0]
    c = HYENA_CH
    tf = min(512, n)
    kern = functools.partial(_filter_spectrum_kernel, scale=1.0 / n)
    return pl.pallas_call(
        kern,
        grid=(n // tf,),
        in_specs=[pl.BlockSpec((tf, n), lambda f: (f, 0)), pl.BlockSpec((tf, n), lambda f: (f, 0)),
                  pl.BlockSpec((n, 2 * c), lambda f: (0, 0)),
                  pl.BlockSpec((tf, 1), lambda f: (f, 0)), pl.BlockSpec((tf, 1), lambda f: (f, 0))],
        out_specs=[pl.BlockSpec((tf, c), lambda f: (f, 0))] * 2,
        out_shape=[jax.ShapeDtypeStruct((n, c), F32)] * 2,
        compiler_params=_cparams(("parallel",)),
        name="hyena_filter_spectrum",
    )(cs, ss, filt, half_cos, half_sin)


def _dft_fwd_kernel(c_ref, s_ref, u_ref, kre_ref, kim_ref, yre_ref, yim_ref):
    u = u_ref[...]
    a = jnp.dot(c_ref[...], u, preferred_element_type=F32)
    b = jnp.dot(s_ref[...], u, preferred_element_type=F32)
    k_re, k_im = kre_ref[...], kim_ref[...]
    yre_ref[...] = (a * k_re + b * k_im).astype(BF16)
    yim_ref[...] = (a * k_im - b * k_re).astype(BF16)


def dft_forward(cs, ss, u, k_re, k_im, *, batch):
    n = cs.shape[0]
    c = HYENA_CH
    tf = min(512, n)
    return pl.pallas_call(
        _dft_fwd_kernel,
        grid=(batch, n // tf),
        in_specs=[pl.BlockSpec((tf, n), lambda bi, f: (f, 0)),
                  pl.BlockSpec((tf, n), lambda bi, f: (f, 0)),
                  pl.BlockSpec((n, c), lambda bi, f: (0, bi)),
                  pl.BlockSpec((tf, c), lambda bi, f: (f, 0)),
                  pl.BlockSpec((tf, c), lambda bi, f: (f, 0))],
        out_specs=[pl.BlockSpec((tf, c), lambda bi, f: (f, bi))] * 2,
        out_shape=[jax.ShapeDtypeStruct((n, batch * c), BF16)] * 2,
        compiler_params=_cparams(("parallel", "parallel")),
        name="dft_forward",
    )(cs, ss, u, k_re, k_im)


def _dft_inv_kernel(c_ref, s_ref, yre_ref, yim_ref, u_ref, x0_ref, bias_ref, o_ref):
    y = (jnp.dot(c_ref[...], yre_ref[...], preferred_element_type=F32)
         - jnp.dot(s_ref[...], yim_ref[...], preferred_element_type=F32))
    u = u_ref[...].astype(F32)
    o_ref[...] = (x0_ref[...].astype(F32) * (y + u * bias_ref[...])).astype(BF16)


def dft_inverse(cs, ss, yre, yim, u, x0, bias, *, batch):
    n = cs.shape[0]
    c = HYENA_CH
    tt = min(512, n)
    col = lambda bi, t: (0, bi)
    tile = lambda bi, t: (t, bi)
    return pl.pallas_call(
        _dft_inv_kernel,
        grid=(batch, n // tt),
        in_specs=[pl.BlockSpec((tt, n), lambda bi, t: (t, 0)),
                  pl.BlockSpec((tt, n), lambda bi, t: (t, 0)),
                  pl.BlockSpec((n, c), col), pl.BlockSpec((n, c), col),
                  pl.BlockSpec((tt, c), tile), pl.BlockSpec((tt, c), tile),
                  pl.BlockSpec((1, c), lambda bi, t: (0, 0))],
        out_specs=pl.BlockSpec((tt, c), lambda bi, t: (bi * (n // tt) + t, 0)),
        out_shape=jax.ShapeDtypeStruct((batch * n, c), BF16),
        compiler_params=_cparams(("parallel", "parallel")),
        name="dft_inverse",
    )(cs, ss, yre, yim, u, x0, bias)


def _dft_table_kernel(cg_ref, sg_ref, cd_ref, sd_ref, c_ref, s_ref):
    cg, sg, cd, sd = cg_ref[...], sg_ref[...], cd_ref[0], sd_ref[0]
    c_ref[...] = (cg * cd - sg * sd).astype(BF16)
    s_ref[...] = (sg * cd + cg * sd).astype(BF16)


def _shifted_dft_tables(n):
    tf = min(256, n)
    theta = 2.0 * math.pi / (8 * n)
    odd_s = 2 * jnp.arange(n, dtype=jnp.int32)[None, :] + 1
    odd_i = 2 * jnp.arange(tf, dtype=jnp.int32)[:, None] + 1
    gamma = ((odd_i * odd_s) % (8 * n)).astype(F32) * theta
    f0 = jnp.arange(n // tf, dtype=jnp.int32)[:, None] * tf
    delta = ((2 * f0 * odd_s) % (8 * n)).astype(F32)[:, None, :] * theta
    whole = pl.BlockSpec((tf, n), lambda i: (0, 0))
    row = pl.BlockSpec((1, 1, n), lambda i: (i, 0, 0))
    cs, ss = pl.pallas_call(
        _dft_table_kernel,
        grid=(n // tf,),
        in_specs=[whole, whole, row, row],
        out_specs=[pl.BlockSpec((tf, n), lambda i: (i, 0))] * 2,
        out_shape=[jax.ShapeDtypeStruct((n, n), BF16)] * 2,
        compiler_params=_cparams(("parallel",)),
        name="dft_tables",
    )(jnp.cos(gamma), jnp.sin(gamma), jnp.cos(delta), jnp.sin(delta))
    half = (2 * jnp.arange(n, dtype=jnp.int32) + 1).astype(F32)[:, None] * (math.pi / (4 * n))
    return cs, ss, jnp.cos(half), jnp.sin(half)


def _filter_features(seq_len):
    t = jnp.linspace(0.0, 1.0, seq_len, dtype=F32)[:, None]
    bands = (HYENA_EMB - 1) // 2
    ang = 2.0 * math.pi * jnp.arange(seq_len, dtype=F32)[:, None] / seq_len
    fr = jnp.linspace(1e-4, bands - 1, bands, dtype=F32)[None, :]
    feats = jnp.concatenate([t, jnp.cos(fr * ang), -jnp.sin(fr * ang)], axis=-1)
    return jnp.pad(feats, ((0, 0), (0, LANES - HYENA_EMB)))


def hyena_long_conv(proj, hp, *, row0, batch, seq_len, z_blk):
    u, x0 = hyena_shortconv(proj, hp["short_w"], hp["short_b"], row0=row0, batch=batch,
                            seq_len=seq_len, z_blk=z_blk, tm=min(512, seq_len))
    filt = hyena_filter_pair(_filter_features(seq_len), hp["w1"], hp["b1"], hp["w2"], hp["b2"],
                             hp["w3"], hp["b3"], hp["w4"], hp["freq"])
    cs, ss, half_cos, half_sin = _shifted_dft_tables(seq_len)
    k_re, k_im = filter_spectrum(cs, ss, filt, half_cos, half_sin)
    yre, yim = dft_forward(cs, ss, u, k_re, k_im, batch=batch)
    return dft_inverse(cs, ss, yre, yim, u, x0, hp["bias"], batch=batch)


def _outproj_kernel(mal_ref, mac_ref, mbl_ref, mbc_ref, xl_ref, xc_ref, mod_ref, nw_ref, w_ref,
                    rw_ref, xo_ref, h_ref, lt_ref, *, n_lat_tiles):
    half = mal_ref.shape[1]
    pick = functools.partial(_pick_rows, n_lat_tiles=n_lat_tiles)
    y = (jnp.dot(pick(mal_ref, mac_ref), w_ref[0:half], preferred_element_type=F32)
         + jnp.dot(pick(mbl_ref, mbc_ref), w_ref[half:2 * half], preferred_element_type=F32))
    mod = mod_ref[0]
    x = pick(xl_ref, xc_ref) + mod[2:3] * y
    xo_ref[...] = x
    h = _modulated_norm(x, mod, nw_ref[...], 3, 4)
    h_ref[...] = h.astype(BF16)
    h_hi = h.astype(BF16)
    h_lo = (h - h_hi.astype(F32)).astype(BF16)
    nt = lambda a, b: lax.dot_general(a, b, (((1,), (1,)), ((), ())), preferred_element_type=F32)
    lt_ref[...] = nt(rw_ref[0], h_hi) + (nt(rw_ref[0], h_lo) + nt(rw_ref[1], h_hi))


def outproj(mix_a, a_blk, mix_b, b_blk, x_rows, mods, nw, w_out, router_wt, *,
            n_rows, t_lat, seq, tm=512):
    x_lat, x_ctx, ctx_row0 = x_rows
    d = x_lat.shape[1]
    half = d // 2
    n_lat_tiles = t_lat // tm
    return pl.pallas_call(
        functools.partial(_outproj_kernel, n_lat_tiles=n_lat_tiles),
        grid=(n_rows // tm,),
        in_specs=_split_row_specs(tm, half, n_lat_tiles, a_blk)
                 + _split_row_specs(tm, half, n_lat_tiles, b_blk)
                 + _split_row_specs(tm, d, n_lat_tiles, ctx_row0=ctx_row0) + [
                  pl.BlockSpec((1, 6, d), lambda i: (_seg_of_tile(i, tm, t_lat, seq), 0, 0)),
                  pl.BlockSpec((1, d), lambda i: (0, 0)),
                  pl.BlockSpec((d, d), lambda i: (0, 0)),
                  pl.BlockSpec((2, N_EXPERTS, d), lambda i: (0, 0, 0))],
        out_specs=[pl.BlockSpec((tm, d), lambda i: (i, 0)),
                   pl.BlockSpec((tm, d), lambda i: (i, 0)),
                   pl.BlockSpec((N_EXPERTS, tm), lambda i: (0, i))],
        out_shape=[jax.ShapeDtypeStruct((n_rows, d), F32),
                   jax.ShapeDtypeStruct((n_rows, d), BF16),
                   jax.ShapeDtypeStruct((N_EXPERTS, n_rows), F32)],
        compiler_params=_cparams(("parallel",)),
        name="outproj",
    )(*mix_a, *mix_b, x_lat, x_ctx, mods, nw, w_out, router_wt)


def _first_max(x, idx, sentinel):
    m = jnp.max(x, axis=0, keepdims=True)
    first = jnp.min(jnp.where(x == m, idx, sentinel), axis=0, keepdims=True)
    return m, idx == first


def _router_kernel(lt_ref, bias_ref, eye_ref, before_ref, below_ref, lists_ref, tok_ref, pc_ref):
    tn = lt_ref.shape[1]
    scores = jax.nn.sigmoid(lt_ref[...])
    sel = scores + bias_ref[...]
    neg = -jnp.inf
    in_grp = lax.broadcasted_iota(jnp.int32, (GROUP_SIZE, tn), 0)
    gscore = []
    for g in range(N_GROUPS):
        x = sel[g * GROUP_SIZE:(g + 1) * GROUP_SIZE]
        m1, hit = _first_max(x, in_grp, GROUP_SIZE)
        gscore.append(m1 + jnp.max(jnp.where(hit, neg, x), axis=0, keepdims=True))
    rows = []
    for g in range(N_GROUPS):
        beaten = jnp.zeros((1, tn), jnp.int32)
        for o in range(N_GROUPS):
            if o != g:
                wins = (gscore[o] >= gscore[g]) if o < g else (gscore[o] > gscore[g])
                beaten = beaten + wins.astype(jnp.int32)
        keep = jnp.broadcast_to(beaten < TOPK_GROUPS, (GROUP_SIZE, tn))
        rows.append(jnp.where(keep, sel[g * GROUP_SIZE:(g + 1) * GROUP_SIZE], neg))
    cand = jnp.concatenate(rows, axis=0)
    eidx = lax.broadcasted_iota(jnp.int32, cand.shape, 0)
    chosen = jnp.zeros(cand.shape, jnp.bool_)
    hits = []
    for _ in range(TOP_K):
        _, hit = _first_max(cand, eidx, N_EXPERTS)
        hits.append(hit)
        chosen = jnp.logical_or(chosen, hit)
        cand = jnp.where(hit, neg, cand)
    w = jnp.where(chosen, scores, 0.0)
    gates_t = w / jnp.sum(w, axis=0, keepdims=True) * ROUTED_SCALE
    chosen_f = jnp.where(chosen, 1.0, 0.0)
    rank = jnp.dot(chosen_f.astype(BF16), before_ref[...], preferred_element_type=F32)
    count = jnp.sum(chosen_f, axis=1, keepdims=True)
    pc = jnp.floor((count + (ROW_UNIT - 1)) * (1.0 / ROW_UNIT)) * ROW_UNIT
    pc_lanes = jnp.broadcast_to(pc, (N_EXPERTS, LANES))
    start = jnp.dot(below_ref[...], pc_lanes, preferred_element_type=F32,
                    precision=HIGHEST)[:, 0:1]
    slot = start + rank
    pick = lambda hit, v: jnp.sum(jnp.where(hit, v, 0.0), axis=0, keepdims=True)
    lists = jnp.concatenate([pick(h, slot) for h in hits] + [pick(h, gates_t) for h in hits], axis=0)
    lists_ref[...] = lists
    pc_ref[0] = pc_lanes
    padded = jnp.concatenate([lists, jnp.zeros((LANES - 2 * TOP_K, tn), F32)], axis=0)
    tok_ref[...] = lax.dot_general(eye_ref[...], padded, (((1,), (1,)), ((), ())),
                                   preferred_element_type=F32, precision=HIGHEST)


def router(logits, router_bias):
    e, t = logits.shape
    tn = MOE_TILE
    n_tiles = t // tn
    eye = jnp.eye(tn, dtype=F32)
    tok_i = jnp.arange(tn, dtype=jnp.int32)
    before = (tok_i[:, None] < tok_i[None, :]).astype(BF16)
    exp_i = jnp.arange(e, dtype=jnp.int32)
    below = (exp_i[None, :] < exp_i[:, None]).astype(F32)
    const = lambda shape: pl.BlockSpec(shape, lambda i: (0,) * len(shape))
    return pl.pallas_call(
        _router_kernel,
        grid=(n_tiles,),
        in_specs=[pl.BlockSpec((e, tn), lambda i: (0, i)),
                  const((e, 1)), const((tn, tn)), const((tn, tn)), const((e, e))],
        out_specs=[pl.BlockSpec((2 * TOP_K, tn), lambda i: (0, i)),
                   pl.BlockSpec((tn, LANES), lambda i: (i, 0)),
                   pl.BlockSpec((1, e, LANES), lambda i: (i, 0, 0))],
        out_shape=[jax.ShapeDtypeStruct((2 * TOP_K, t), F32),
                   jax.ShapeDtypeStruct((t, LANES), F32),
                   jax.ShapeDtypeStruct((n_tiles, e, LANES), F32)],
        compiler_params=_cparams(("parallel",)),
        name="router",
    )(logits, router_bias.reshape(e, 1), eye, before, below)


def moe_plan(pc, n_tiles):
    e = pc.shape[1]
    assert n_tiles >= e
    cap_rows = _sorted_capacity(n_tiles)
    run_end = jnp.cumsum(pc, axis=1)
    run_start = run_end - pc
    total = jnp.sum(pc, axis=0)
    total_al = ((total + EXPERT_ROWS - 1) // EXPERT_ROWS) * EXPERT_ROWS
    range_end = jnp.cumsum(total_al)
    range_start = range_end - total_al
    run_row = range_start[None, :] + jnp.cumsum(pc, axis=0) - pc
    unit_row = jnp.arange(N_UNITS, dtype=jnp.int32) * ROW_UNIT
    owner = jnp.sum((run_end[:, None, :] <= unit_row[None, :, None]).astype(jnp.int32), axis=2)
    owner_hot = (owner[:, :, None] == jnp.arange(e, dtype=jnp.int32)).astype(jnp.int32)
    dst = jnp.sum(owner_hot * (run_row - run_start)[:, None, :], axis=2) + unit_row[None, :]
    used = owner < e
    dump = cap_rows + (jnp.arange(n_tiles, dtype=jnp.int32) % 2)[:, None] * DUMP_ROWS
    unit_dst = jnp.where(used, dst, dump + unit_row[None, :]).astype(jnp.int32)
    unit_src = jnp.where(used, dst, 0).astype(jnp.int32)
    pad_off = jnp.arange(PAD_UNITS, dtype=jnp.int32)[None, :] * ROW_UNIT
    pad_row = (range_start + total)[:, None] + pad_off
    pad_row = jnp.concatenate([jnp.where(pad_row < range_end[:, None], pad_row, -1),
                               jnp.full((n_tiles - e, PAD_UNITS), -1, jnp.int32)], axis=0)
    pad_dst = jnp.where(pad_row >= 0, pad_row, dump + SLOTS + pad_off).astype(jnp.int32)
    tile_end = range_end // EXPERT_ROWS
    tile_expert = jnp.sum((tile_end[None, :] <= jnp.arange(cap_rows // EXPERT_ROWS,
                                                           dtype=jnp.int32)[:, None])
                          .astype(jnp.int32), axis=1)
    tile_expert = jnp.minimum(tile_expert, e - 1).astype(jnp.int32)
    used_rows = run_end[:, -1].astype(jnp.int32)
    return unit_dst, unit_src, pad_dst, used_rows, tile_expert, tile_end[-1:].astype(jnp.int32)


def _sorted_capacity(n_tiles):
    worst_rows = n_tiles * (MOE_TILE * TOP_K + N_EXPERTS * (ROW_UNIT - 1)) + N_EXPERTS * (EXPERT_ROWS - 1)
    return -(-worst_rows // EXPERT_ROWS) * EXPERT_ROWS


DUMP_ROWS = SLOTS + PAD_UNITS * ROW_UNIT


UNITS_PER_CHUNK = SLOT_CHUNK // ROW_UNIT
MAIN_SLOTS = SLOTS - SLOT_CHUNK
assert MAIN_SLOTS >= MOE_TILE * TOP_K


def _dispatch_kernel(unit_ref, pad_ref, used_ref, h_ref, lists_ref, xs_ref, loc_ref, zero_ref, sems):
    i = pl.program_id(0)
    last = pl.num_programs(0) - 1
    slot = i % 2
    tn = h_ref.shape[0]

    @pl.when(i == 0)
    def _():
        zero_ref[...] = jnp.zeros_like(zero_ref)

    rel_rows = lax.broadcasted_iota(jnp.int32, (SUB_SLOTS, tn), 0).astype(F32).astype(BF16)

    def chunk(c0):
        parts = []
        for b0 in range(c0, c0 + SLOT_CHUNK, SUB_SLOTS):
            p = jnp.zeros((SUB_SLOTS, tn), BF16)
            for k in range(TOP_K):
                hit = rel_rows == (lists_ref[k:k + 1, :] - float(b0)).astype(BF16)
                p = jnp.where(hit, jnp.ones_like(p), p)
            parts.append(p)
        loc_ref[slot, c0:c0 + SLOT_CHUNK, :] = jnp.dot(
            jnp.concatenate(parts, axis=0), h_ref[...],
            preferred_element_type=F32).astype(BF16)
        for u in range(c0 // ROW_UNIT, c0 // ROW_UNIT + UNITS_PER_CHUNK):
            row = pl.multiple_of(unit_ref[i, u], ROW_UNIT)
            pltpu.make_async_copy(loc_ref.at[slot, pl.ds(u * ROW_UNIT, ROW_UNIT)],
                                  xs_ref.at[pl.ds(row, ROW_UNIT)], sems.at[slot]).start()

    for c0 in range(0, MAIN_SLOTS, SLOT_CHUNK):
        chunk(c0)

    @pl.when(used_ref[i] > MAIN_SLOTS)
    def _():
        chunk(MAIN_SLOTS)

    for p in range(PAD_UNITS):
        row = pl.multiple_of(pad_ref[i, p], ROW_UNIT)
        pltpu.make_async_copy(zero_ref, xs_ref.at[pl.ds(row, ROW_UNIT)], sems.at[slot]).start()

    def wait_all(s, tile):
        pltpu.make_async_copy(loc_ref.at[s, pl.ds(0, MAIN_SLOTS)],
                              xs_ref.at[pl.ds(0, MAIN_SLOTS)], sems.at[s]).wait()

        @pl.when(used_ref[tile] > MAIN_SLOTS)
        def _():
            pltpu.make_async_copy(loc_ref.at[s, pl.ds(MAIN_SLOTS, SLOT_CHUNK)],
                                  xs_ref.at[pl.ds(0, SLOT_CHUNK)], sems.at[s]).wait()
        pltpu.make_async_copy(loc_ref.at[s, pl.ds(0, PAD_UNITS * ROW_UNIT)],
                              xs_ref.at[pl.ds(0, PAD_UNITS * ROW_UNIT)], sems.at[s]).wait()

    @pl.when(i > 0)
    def _():
        wait_all(1 - slot, i - 1)

    @pl.when(i == last)
    def _():
        wait_all(slot, i)


def moe_dispatch(h, lists, unit_dst, pad_dst, used_rows, *, n_tiles):
    d = h.shape[1]
    cap_rows = _sorted_capacity(n_tiles) + 2 * DUMP_ROWS
    return pl.pallas_call(
        _dispatch_kernel,
        grid_spec=pltpu.PrefetchScalarGridSpec(
            num_scalar_prefetch=3,
            grid=(n_tiles,),
            in_specs=[pl.BlockSpec((MOE_TILE, d), lambda i, *_: (i, 0)),
                      pl.BlockSpec((2 * TOP_K, MOE_TILE), lambda i, *_: (0, i))],
            out_specs=pl.BlockSpec(memory_space=pl.ANY),
            scratch_shapes=[pltpu.VMEM((2, SLOTS, d), BF16), pltpu.VMEM((ROW_UNIT, d), BF16),
                            pltpu.SemaphoreType.DMA((2,))]),
        out_shape=jax.ShapeDtypeStruct((cap_rows, d), BF16),
        compiler_params=_cparams(("arbitrary",)),
        name="moe_dispatch",
    )(unit_dst, pad_dst, used_rows, h, lists)


X_RING = 3


def _expert_kernel(te_ref, nt_ref, xs_ref, wg_ref, wu_ref, wd_ref, y_ref, x_buf, wg_bf, wu_bf,
                   wd_bf, sems):
    t = pl.program_id(0)
    n_live = nt_ref[0]
    live = t < n_live
    tc = jnp.minimum(t, jnp.maximum(n_live - 1, 0))
    new_expert = jnp.logical_or(t == 0, te_ref[tc] != te_ref[jnp.maximum(tc - 1, 0)])
    rows = x_buf.shape[1]

    def x_copy(tile):
        s = tile % X_RING
        return pltpu.make_async_copy(xs_ref.at[pl.ds(pl.multiple_of(tile * rows, rows), rows)],
                                     x_buf.at[s], sems.at[s])

    for k in range(X_RING - 1):
        @pl.when(jnp.logical_and(t == 0, k < n_live))
        def _():
            x_copy(k).start()

    @pl.when(t + (X_RING - 1) < n_live)
    def _():
        x_copy(t + (X_RING - 1)).start()

    @pl.when(jnp.logical_and(live, new_expert))
    def _():
        wg_bf[...] = wg_ref[0, 0].astype(BF16)
        wu_bf[...] = wu_ref[0, 0].astype(BF16)
        wd_bf[...] = wd_ref[0, 0].astype(BF16)

    @pl.when(live)
    def _():
        x_copy(t).wait()
        x = x_buf[t % X_RING]
        act = (_silu(jnp.dot(x, wg_bf[...], preferred_element_type=F32))
               * jnp.dot(x, wu_bf[...], preferred_element_type=F32))
        y_ref[...] = jnp.dot(act.astype(BF16), wd_bf[...], preferred_element_type=F32).astype(BF16)


def moe_experts(xs, w_gate, w_up, w_down, layer, tile_expert, n_row_tiles):
    d = xs.shape[1]
    cap_rows = xs.shape[0] - 2 * DUMP_ROWS
    ff = w_gate.shape[3]
    last_live = lambda nt: jnp.maximum(nt[0] - 1, 0)
    row_map = lambda t, te, nt: (jnp.minimum(t, last_live(nt)), 0)
    w_map = lambda t, te, nt: (layer, te[jnp.minimum(t, last_live(nt))], 0, 0)
    return pl.pallas_call(
        _expert_kernel,
        grid_spec=pltpu.PrefetchScalarGridSpec(
            num_scalar_prefetch=2,
            grid=(cap_rows // EXPERT_ROWS,),
            in_specs=[pl.BlockSpec(memory_space=pl.ANY),
                      pl.BlockSpec((1, 1, d, ff), w_map), pl.BlockSpec((1, 1, d, ff), w_map),
                      pl.BlockSpec((1, 1, ff, d), w_map)],
            out_specs=pl.BlockSpec((EXPERT_ROWS, d), row_map),
            scratch_shapes=[pltpu.VMEM((X_RING, EXPERT_ROWS, d), BF16),
                            pltpu.VMEM((d, ff), BF16), pltpu.VMEM((d, ff), BF16),
                            pltpu.VMEM((ff, d), BF16), pltpu.SemaphoreType.DMA((X_RING,))]),
        out_shape=jax.ShapeDtypeStruct((cap_rows, d), BF16),
        compiler_params=_cparams(("arbitrary",)),
        name="moe_experts",
    )(tile_expert, n_row_tiles, xs, w_gate, w_up, w_down)


def _combine_kernel(unit_ref, used_ref, ys_ref, tok_ref, h_ref, x_ref, mod_ref, fnw_ref, sg_ref,
                    su_ref, sd_ref, o_ref, loc_ref, acc_ref, sems, *, final_norm):
    i = pl.program_id(0)
    last = pl.num_programs(0) - 1
    slot = i % 2
    tn = h_ref.shape[0]
    def fetch(tile, s):
        def units(u0, u1):
            for u in range(u0, u1):
                row = pl.multiple_of(unit_ref[tile, u], ROW_UNIT)
                pltpu.make_async_copy(ys_ref.at[pl.ds(row, ROW_UNIT)],
                                      loc_ref.at[s, pl.ds(u * ROW_UNIT, ROW_UNIT)], sems.at[s]).start()
        units(0, MAIN_SLOTS // ROW_UNIT)

        @pl.when(used_ref[tile] > MAIN_SLOTS)
        def _():
            units(MAIN_SLOTS // ROW_UNIT, N_UNITS)

    @pl.when(i == 0)
    def _():
        fetch(0, 0)

    @pl.when(i < last)
    def _():
        fetch(i + 1, 1 - slot)

    h = h_ref[...]
    act = (_silu(jnp.dot(h, sg_ref[...], preferred_element_type=F32))
           * jnp.dot(h, su_ref[...], preferred_element_type=F32))
    acc = jnp.dot(act.astype(BF16), sd_ref[...], preferred_element_type=F32)
    tail = used_ref[i] > MAIN_SLOTS
    pltpu.make_async_copy(ys_ref.at[pl.ds(0, MAIN_SLOTS)], loc_ref.at[slot, pl.ds(0, MAIN_SLOTS)],
                          sems.at[slot]).wait()

    @pl.when(tail)
    def _():
        pltpu.make_async_copy(ys_ref.at[pl.ds(0, SLOT_CHUNK)],
                              loc_ref.at[slot, pl.ds(MAIN_SLOTS, SLOT_CHUNK)], sems.at[slot]).wait()

    tok = tok_ref[...]

    rel_cols = lax.broadcasted_iota(jnp.int32, (tn, SUB_SLOTS), 1).astype(F32).astype(BF16)
    gate_w = [tok[:, TOP_K + k:TOP_K + k + 1].astype(BF16) for k in range(TOP_K)]

    def chunk(c0):
        parts = []
        for b0 in range(c0, c0 + SLOT_CHUNK, SUB_SLOTS):
            q = jnp.zeros((tn, SUB_SLOTS), BF16)
            for k in range(TOP_K):
                hit = rel_cols == (tok[:, k:k + 1] - float(b0)).astype(BF16)
                q = jnp.where(hit, gate_w[k], q)
            parts.append(q)
        return jnp.dot(jnp.concatenate(parts, axis=1), loc_ref[slot, c0:c0 + SLOT_CHUNK, :],
                       preferred_element_type=F32)

    for c0 in range(0, MAIN_SLOTS, SLOT_CHUNK):
        acc = acc + chunk(c0)
    acc_ref[...] = acc

    @pl.when(tail)
    def _():
        acc_ref[...] += chunk(MAIN_SLOTS)

    x = x_ref[...] + mod_ref[0][5:6] * acc_ref[...]
    if final_norm:
        x = _rms(x) * fnw_ref[...]
    o_ref[...] = x


def moe_combine(ys, tok, h, x_all, mods, fnw, sh_gate, sh_up, sh_down, unit_dst, used_rows, *,
                n_tiles, t_lat, seq, final_norm):
    d = h.shape[1]
    tn = MOE_TILE
    const = lambda a: pl.BlockSpec(a.shape, lambda i, *_: (0,) * a.ndim)
    kern = functools.partial(_combine_kernel, final_norm=final_norm)
    return pl.pallas_call(
        kern,
        grid_spec=pltpu.PrefetchScalarGridSpec(
            num_scalar_prefetch=2,
            grid=(n_tiles,),
            in_specs=[pl.BlockSpec(memory_space=pl.ANY),
                      pl.BlockSpec((tn, LANES), lambda i, *_: (i, 0)),
                      pl.BlockSpec((tn, d), lambda i, *_: (i, 0)),
                      pl.BlockSpec((tn, d), lambda i, *_: (i, 0)),
                      pl.BlockSpec((1, 6, d),
                                   lambda i, *_: (_seg_of_tile(i, tn, t_lat, seq), 0, 0)),
                      const(fnw), const(sh_gate), const(sh_up), const(sh_down)],
            out_specs=pl.BlockSpec((tn, d), lambda i, *_: (i, 0)),
            scratch_shapes=[pltpu.VMEM((2, SLOTS, d), BF16), pltpu.VMEM((tn, d), F32),
                            pltpu.SemaphoreType.DMA((2,))]),
        out_shape=jax.ShapeDtypeStruct((n_tiles * tn, d), F32),
        compiler_params=_cparams(("arbitrary",)),
        name="moe_combine",
    )(unit_dst, used_rows, ys, tok, h, x_all, mods, fnw, sh_gate, sh_up, sh_down)


def moe_block(h2, logits, router_bias, x_mid, mods, fnw, w_gate, w_up, w_down,
              sh_gate, sh_up, sh_down, *, layer, t_lat, seq, final_norm):
    n_tiles = h2.shape[0] // MOE_TILE
    lists, tok, pc = router(logits, router_bias)
    unit_dst, unit_src, pad_dst, used_rows, tile_expert, n_row_tiles = moe_plan(
        pc[:, :, 0].astype(jnp.int32), n_tiles)
    xs = moe_dispatch(h2, lists, unit_dst, pad_dst, used_rows, n_tiles=n_tiles)
    ys = moe_experts(xs, w_gate, w_up, w_down, layer, tile_expert, n_row_tiles)
    return moe_combine(ys, tok, h2, x_mid, mods, fnw, sh_gate.astype(BF16), sh_up.astype(BF16),
                       sh_down.astype(BF16), unit_src, used_rows, n_tiles=n_tiles, t_lat=t_lat,
                       seq=seq, final_norm=final_norm)


def _rope_tables(seq, tm):
    half = RET_DK // 2
    pos = jnp.arange(seq, dtype=jnp.int32)
    row = (pos // GRID_W).astype(F32)
    col = (pos % GRID_W).astype(F32)
    inv = ROPE_BASE ** (-jnp.arange(0, half, 2, dtype=F32) / half)
    a_row, a_col = row[:, None] * inv[None, :], col[:, None] * inv[None, :]
    cos_h = jnp.concatenate([jnp.cos(a_row)] * 2 + [jnp.cos(a_col)] * 2, axis=-1)
    sin_h = jnp.concatenate([-jnp.sin(a_row), jnp.sin(a_row), -jnp.sin(a_col), jnp.sin(a_col)], -1)
    reps = 2 * RET_HEADS
    cos_t = jnp.concatenate([jnp.tile(cos_h, (1, reps)), jnp.ones((tm, reps * RET_DK), F32)], 0)
    sin_t = jnp.concatenate([jnp.tile(sin_h, (1, reps)), jnp.zeros((tm, reps * RET_DK), F32)], 0)
    return cos_t, sin_t


def _rope_partner_columns():
    quarter = RET_DK // 4
    idx = jnp.arange(2 * RET_HEADS * RET_DK, dtype=jnp.int32)
    within = idx % (2 * quarter)
    return jnp.where(within < quarter, idx + quarter, idx - quarter)


def kernel(x, c, ctx, c_ctx, ada_w, ada_b, norm1_w, norm2_w, ev_w_in, ev_short_w, ev_short_b, ev_filt_w1, ev_filt_b1, ev_filt_w2, ev_filt_b2, ev_filt_w3, ev_filt_b3, ev_filt_w4, ev_filt_freq, ev_hyena_bias, ev_w_out, od_w_in, od_gate_w1_f, od_gate_w2_f, od_gate_b_f, od_gate_w1_b, od_gate_w2_b, od_gate_b_b, od_norm_w, od_w_out, router_w, router_bias, exp_w_gate, exp_w_up, exp_w_down, sh_w_gate, sh_w_up, sh_w_down, final_norm_w):
    batch, seq, d = x.shape
    ctx_len = ctx.shape[1]
    depth = ada_w.shape[0]
    t_lat, t_ctx = batch * seq, batch * ctx_len
    t_all = t_lat + t_ctx
    tm = 1024

    x_all = None
    x_rows = (x.reshape(t_lat, d), ctx.reshape(t_ctx, d), 0)
    cond8 = jnp.concatenate([c_ctx[None, :], c, jnp.zeros((8 - 1 - batch, d), F32)], axis=0)
    mods_all = adaln_rows(cond8, ada_w, ada_b).reshape(depth, 8, 6, d)

    for i in range(depth):
        last = i == depth - 1
        j = i // 2
        mods = mods_all[i]
        nw1, nw2 = norm1_w[i][None, :], norm2_w[i][None, :]
        if i % 2 == 0:
            qk = 2 * RET_HEADS * RET_DK
            k_scale = jnp.concatenate([jnp.ones((qk // 2,), F32),
                                       jnp.full((qk // 2,), RET_DK ** -0.5, F32)])
            w_in = ev_w_in[j]
            w_qk = w_in[:, :qk] * k_scale
            w_ext = jnp.concatenate([w_qk, w_in[:, qk:], w_qk[:, _rope_partner_columns()]],
                                    axis=1).astype(BF16)
            cos_t, sin_t = _rope_tables(seq, tm)
            proj = inproj_even(x_rows, mods, nw1, w_ext, cos_t, sin_t, t_lat=t_lat, t_all=t_all,
                               seq=seq, tm=tm)
            log_g = [math.log1p(-2.0 ** (-5.0 - h)) for h in range(RET_HEADS)]
            mix_a = bidir_scan(proj, None, None, jnp.ones((1, RET_DV), F32),
                               batch=batch, seq=seq, ctx_len=ctx_len, heads=RET_HEADS, dk=RET_DK,
                               dv=RET_DV, chunk=RET_CHUNK, q_blk=0, k_blk=1, v_blk=1, g_blk=2,
                               log_decay_f=log_g, log_decay_b=log_g[::-1])
            hp = dict(short_w=ev_short_w[j], short_b=ev_short_b[j][None, :],
                      w1=jnp.pad(ev_filt_w1[j], ((0, LANES - HYENA_EMB), (0, 0))),
                      b1=ev_filt_b1[j][None, :], w2=ev_filt_w2[j], b2=ev_filt_b2[j][None, :],
                      w3=ev_filt_w3[j], b3=ev_filt_b3[j][None, :], w4=ev_filt_w4[j],
                      freq=ev_filt_freq[j][None, :], bias=ev_hyena_bias[j][None, :])
            hy_lat = hyena_long_conv(proj, hp, row0=0, batch=batch, seq_len=seq, z_blk=1)
            hy_ctx = hyena_long_conv(proj, hp, row0=t_lat, batch=batch, seq_len=ctx_len, z_blk=1)
            mix_b = (hy_lat, hy_ctx)
            a_blk, b_blk = 0, 0
            w_out = ev_w_out[j].astype(BF16)
        else:
            kd = GLA_HEADS * GLA_DK
            pad_cols = LANES - 2 * GLA_RANK
            w_ext = jnp.concatenate([od_w_in[j], od_gate_w1_f[j], od_gate_w1_b[j],
                                     jnp.zeros((d, pad_cols), F32)], axis=1).astype(BF16)
            w2 = jnp.concatenate(
                [jnp.pad(od_gate_w2_f[j], ((0, LANES - GLA_RANK), (0, 0))),
                 jnp.pad(od_gate_w2_b[j], ((GLA_RANK, LANES - 2 * GLA_RANK), (0, 0)))], axis=1)
            w2_hi = w2.astype(BF16)
            w2_split = jnp.concatenate([w2_hi, w2_hi, (w2 - w2_hi.astype(F32)).astype(BF16)], axis=0)
            b2 = jnp.concatenate([od_gate_b_f[j], od_gate_b_b[j]])[None, :]
            if x_all is None:
                x_all = jnp.concatenate(x_rows[:2], axis=0)
            proj, la_f, la_b = inproj_odd(x_all, mods, nw1, w_ext, w2_split, b2,
                                          t_lat=t_lat, seq=seq, tm=tm)
            mix_a = bidir_scan(proj, la_f, la_b, od_norm_w[j][None, :],
                               batch=batch, seq=seq, ctx_len=ctx_len, heads=GLA_HEADS, dk=GLA_DK,
                               dv=GLA_DV, chunk=GLA_CHUNK, q_blk=0, k_blk=1, v_blk=1, g_blk=2)
            mix_b = mix_a
            a_blk, b_blk = 0, 1
            w_out = od_w_out[j].astype(BF16)

        n_rows = t_lat if last else t_all
        rw_t = router_w[i].T
        rw_hi = rw_t.astype(BF16)
        rw_split = jnp.stack([rw_hi, (rw_t - rw_hi.astype(F32)).astype(BF16)])
        x_mid, h2, logits = outproj(mix_a, a_blk, mix_b, b_blk, x_rows, mods, nw2, w_out,
                                    rw_split, n_rows=n_rows, t_lat=t_lat, seq=seq, tm=tm)
        x_all = moe_block(h2, logits, router_bias[i], x_mid, mods, final_norm_w[None, :],
                          exp_w_gate, exp_w_up, exp_w_down,
                          sh_w_gate[i], sh_w_up[i], sh_w_down[i],
                          layer=i, t_lat=t_lat, seq=seq, final_norm=last)
        x_rows = (x_all, x_all, t_lat)
    return x_all[:t_lat].reshape(batch, seq, d)
```

```python
import functools
import math

import jax
import jax.numpy as jnp
from jax import lax
from jax.experimental import pallas as pl
from jax.experimental.pallas import tpu as pltpu

F32 = jnp.float32
BF16 = jnp.bfloat16
HIGHEST = lax.Precision.HIGHEST

D_MODEL = 1024
GRID_W = 64
NORM_EPS = 1e-6
RET_HEADS, RET_DK, RET_DV, RET_CHUNK = 4, 64, 128, 128
ROPE_BASE = 10000.0
HYENA_CH, HYENA_EMB = 512, 33
HYENA_FAST_DECAY, HYENA_SLOW_DECAY, HYENA_TARGET = 0.3, 1.5, 1e-2
GLA_HEADS, GLA_DK, GLA_DV, GLA_RANK, GLA_TAU, GLA_CHUNK = 4, 128, 256, 16, 16.0, 64
N_EXPERTS, TOP_K, N_GROUPS, TOPK_GROUPS = 64, 8, 8, 4
GROUP_SIZE = N_EXPERTS // N_GROUPS
EXPERT_FF = 256
ROUTED_SCALE = 2.5

LANES = 128
SCAN_ROWS = 256
SCAN_LAT_ROWS = 1024
MOE_TILE = 256
ROW_UNIT = 16
EXPERT_ROWS = 1024
SLOT_CHUNK = 512
SUB_SLOTS = 256
SLOTS = -(-(MOE_TILE * TOP_K + N_EXPERTS * (ROW_UNIT - 1) + ROW_UNIT) // SLOT_CHUNK) * SLOT_CHUNK
N_UNITS = SLOTS // ROW_UNIT
PAD_UNITS = EXPERT_ROWS // ROW_UNIT
VMEM_LIMIT = 56 * 1024 * 1024


def _cparams(sem):
    return pltpu.CompilerParams(dimension_semantics=sem, vmem_limit_bytes=VMEM_LIMIT)


def _rms(x):
    return x * lax.rsqrt(jnp.mean(x * x, axis=-1, keepdims=True) + NORM_EPS)


def _silu(x):
    return x * jax.nn.sigmoid(x)


def _adaln_kernel(c_ref, w_ref, b_ref, o_ref):
    s = _silu(c_ref[...])
    w = w_ref[0]
    s_hi, w_hi = s.astype(BF16), w.astype(BF16)
    s_lo, w_lo = (s - s_hi.astype(F32)).astype(BF16), (w - w_hi.astype(F32)).astype(BF16)
    dot = lambda a, b: jnp.dot(a, b, preferred_element_type=F32)
    o_ref[0] = dot(s_hi, w_hi) + (dot(s_lo, w_hi) + dot(s_hi, w_lo)) + b_ref[0]


def adaln_rows(cond8, ada_w, ada_b):
    depth, d, n = ada_w.shape
    tn = 512
    return pl.pallas_call(
        _adaln_kernel,
        grid=(depth, n // tn),
        in_specs=[pl.BlockSpec((8, d), lambda l, j: (0, 0)),
                  pl.BlockSpec((1, d, tn), lambda l, j: (l, 0, j)),
                  pl.BlockSpec((1, 1, tn), lambda l, j: (l, 0, j))],
        out_specs=pl.BlockSpec((1, 8, tn), lambda l, j: (l, 0, j)),
        out_shape=jax.ShapeDtypeStruct((depth, 8, n), F32),
        compiler_params=_cparams(("parallel", "parallel")),
        name="adaln_rows",
    )(cond8, ada_w, ada_b.reshape(depth, 1, n))


def _modulated_norm(x, mod, nw, shift_row, scale_row):
    return _rms(x) * nw * (1.0 + mod[scale_row:scale_row + 1]) + mod[shift_row:shift_row + 1]


def _split_row_specs(tm, width, n_lat_tiles, col_blk=0, ctx_row0=0):
    ctx_blk0 = ctx_row0 // tm
    return [pl.BlockSpec((tm, width), lambda i: (jnp.minimum(i, n_lat_tiles - 1), col_blk)),
            pl.BlockSpec((tm, width),
                         lambda i: (ctx_blk0 + jnp.maximum(i - n_lat_tiles, 0), col_blk))]


def _pick_rows(lat_ref, ctx_ref, n_lat_tiles):
    return jnp.where(pl.program_id(0) < n_lat_tiles, lat_ref[...], ctx_ref[...])


def _inproj_even_kernel(xl_ref, xc_ref, mod_ref, nw_ref, w_ref, cos_ref, sin_ref, o_ref, *,
                        n_lat_tiles):
    x = _pick_rows(xl_ref, xc_ref, n_lat_tiles)
    h = _modulated_norm(x, mod_ref[0], nw_ref[...], 0, 1).astype(BF16)
    n_main = o_ref.shape[1]
    qk = jnp.dot(h, w_ref[:, 0:512], preferred_element_type=F32)
    qk_sw = jnp.dot(h, w_ref[:, n_main:n_main + 512], preferred_element_type=F32)
    is_lat = pl.program_id(0) < n_lat_tiles
    cos = jnp.where(is_lat, cos_ref[...], 1.0)
    sin = jnp.where(is_lat, sin_ref[...], 0.0)
    for c0 in range(0, 512, LANES):
        o_ref[:, c0:c0 + LANES] = (qk[:, c0:c0 + LANES] * cos
                                   + qk_sw[:, c0:c0 + LANES] * sin).astype(BF16)
    for c0 in range(512, n_main, 512):
        o_ref[:, c0:c0 + 512] = jnp.dot(h, w_ref[:, c0:c0 + 512],
                                        preferred_element_type=F32).astype(BF16)


def _inproj_odd_kernel(x_ref, mod_ref, nw_ref, w_ref, w2_ref, b2_ref, o_ref, laf_ref, lab_ref):
    h = _modulated_norm(x_ref[...], mod_ref[0], nw_ref[...], 0, 1).astype(BF16)
    n_main = o_ref.shape[1]
    kd = GLA_HEADS * GLA_DK
    q = jnp.dot(h, w_ref[:, 0:kd], preferred_element_type=F32)
    o_ref[:, 0:kd] = (q * (GLA_DK ** -0.5)).astype(BF16)
    for c0 in range(kd, n_main, 512):
        o_ref[:, c0:c0 + 512] = jnp.dot(h, w_ref[:, c0:c0 + 512],
                                        preferred_element_type=F32).astype(BF16)
    low = jnp.dot(h, w_ref[:, n_main:n_main + LANES], preferred_element_type=F32)

    low_hi = low.astype(BF16)
    low_lo = (low - low_hi.astype(F32)).astype(BF16)
    z = jnp.dot(jnp.concatenate([low_hi, low_lo, low_hi], axis=1), w2_ref[...],
                preferred_element_type=F32) + b2_ref[...]
    la = (jnp.minimum(z, 0.0) - jnp.log(1.0 + jnp.exp(-jnp.abs(z)))) * (1.0 / GLA_TAU)
    laf_ref[...] = la[:, 0:kd]
    lab_ref[...] = la[:, kd:2 * kd]


def _seg_of_tile(i, tm, t_lat, seq):
    return jnp.where(i < t_lat // tm, 1 + (i * tm) // seq, 0)


def inproj_even(x_rows, mods, nw, w_ext, cos_t, sin_t, *, t_lat, t_all, seq, tm=512):
    x_lat, x_ctx, ctx_row0 = x_rows
    d = x_lat.shape[1]
    n_ext = w_ext.shape[1]
    n_main = n_ext - 512
    n_lat_tiles, pos_tiles = t_lat // tm, seq // tm

    def pos_map(i):
        return (jnp.where(i < n_lat_tiles, i % pos_tiles, 0), 0)

    return pl.pallas_call(
        functools.partial(_inproj_even_kernel, n_lat_tiles=n_lat_tiles),
        grid=(t_all // tm,),
        in_specs=_split_row_specs(tm, d, n_lat_tiles, ctx_row0=ctx_row0) + [
                  pl.BlockSpec((1, 6, d), lambda i: (_seg_of_tile(i, tm, t_lat, seq), 0, 0)),
                  pl.BlockSpec((1, d), lambda i: (0, 0)),
                  pl.BlockSpec((d, n_ext), lambda i: (0, 0)),
                  pl.BlockSpec((tm, LANES), pos_map),
                  pl.BlockSpec((tm, LANES), pos_map)],
        out_specs=pl.BlockSpec((tm, n_main), lambda i: (i, 0)),
        out_shape=jax.ShapeDtypeStruct((t_all, n_main), BF16),
        compiler_params=_cparams(("parallel",)),
        name="inproj_even",
    )(x_lat, x_ctx, mods, nw, w_ext, cos_t, sin_t)


def inproj_odd(x_all, mods, nw, w_ext, w2, b2, *, t_lat, seq, tm=512):
    t_all, d = x_all.shape
    n_main = w_ext.shape[1] - LANES
    kd = GLA_HEADS * GLA_DK
    full = lambda shape: pl.BlockSpec(shape, lambda i: (0,) * len(shape))
    return pl.pallas_call(
        _inproj_odd_kernel,
        grid=(t_all // tm,),
        in_specs=[pl.BlockSpec((tm, d), lambda i: (i, 0)),
                  pl.BlockSpec((1, 6, d), lambda i: (_seg_of_tile(i, tm, t_lat, seq), 0, 0)),
                  full((1, d)), full(w_ext.shape), full(w2.shape), full(b2.shape)],
        out_specs=[pl.BlockSpec((tm, n_main), lambda i: (i, 0)),
                   pl.BlockSpec((tm, kd), lambda i: (i, 0)),
                   pl.BlockSpec((tm, kd), lambda i: (i, 0))],
        out_shape=[jax.ShapeDtypeStruct((t_all, n_main), BF16),
                   jax.ShapeDtypeStruct((t_all, kd), F32),
                   jax.ShapeDtypeStruct((t_all, kd), F32)],
        compiler_params=_cparams(("parallel",)),
        name="inproj_odd",
    )(x_all, mods, nw, w_ext, w2, b2)


def _scan_kernel(*refs, heads, dk, dv, chunk, gated, reverse, log_decay, sub_blocks):
    n_in = 3 + int(gated) + (2 if reverse else 0)
    ctx_in, lat_in = refs[:n_in], refs[n_in:2 * n_in]
    rest = list(refs[2 * n_in:])
    nw_ref = rest.pop(0) if reverse else None
    out_lat, out_ctx, state_ref = rest[0], rest[1], rest[2]
    dec_ref = None if gated else rest[3]

    hpg = LANES // dk
    groups = heads // hpg
    rows = SCAN_ROWS
    n_chunks = rows // chunk
    j = pl.program_id(1)

    row_i = lax.broadcasted_iota(jnp.int32, (rows, rows), 0)
    col_i = lax.broadcasted_iota(jnp.int32, (rows, rows), 1)
    same_chunk = (row_i // chunk) == (col_i // chunk)
    keep = jnp.logical_and(same_chunk, (col_i >= row_i) if reverse else (row_i >= col_i))

    @pl.when(j == 0)
    def _init():
        state_ref[...] = jnp.zeros_like(state_ref)
        if not gated:
            pos = lax.broadcasted_iota(jnp.int32, (rows, LANES), 0) % chunk
            steps = ((chunk - pos) if reverse else (pos + 1)).astype(F32)
            lane = lax.broadcasted_iota(jnp.int32, (rows, LANES), 1)
            for g in range(groups):
                lg = jnp.zeros((rows, LANES), F32)
                for a in range(hpg):
                    lg = jnp.where(lane // dk == a, log_decay[g * hpg + a], lg)
                logb = steps * lg
                b_end = float(chunk) * lg
                dec_ref[g, 0] = jnp.exp(logb)
                dec_ref[g, 1] = jnp.exp(-logb)
                dec_ref[g, 2] = jnp.exp(b_end - logb)
                dec_ref[g, 3] = jnp.exp(b_end)

    lane1 = lax.broadcasted_iota(jnp.int32, (1, LANES), 1)
    order = range(n_chunks - 1, -1, -1) if reverse else range(n_chunks)
    end_row = lambda c: c * chunk if reverse else (c + 1) * chunk - 1

    def scan_rows(in_refs, out_ref, r0):
        q_ref, k_ref, v_ref = in_refs[:3]
        blk = slice(r0, r0 + rows)
        if reverse:
            oprev_ref, gate_ref = in_refs[-2:]
        if gated:
            tri = jnp.where(keep, 1.0, 0.0).astype(BF16)
            la = in_refs[3][blk, :]
            la_hi = la.astype(BF16)
            la_lo = (la - la_hi.astype(F32)).astype(BF16)
            logb_all = (jnp.dot(tri, la_hi, preferred_element_type=F32)
                        + jnp.dot(tri, la_lo, preferred_element_type=F32))
        for g in range(groups):
            ksl = slice(g * LANES, (g + 1) * LANES)
            qg = q_ref[blk, ksl].astype(F32)
            kg = k_ref[blk, ksl].astype(F32)
            if gated:
                logb = logb_all[:, ksl]
                ends = [logb[end_row(c):end_row(c) + 1] for c in range(n_chunks)]
                b_end = jnp.concatenate([jnp.broadcast_to(e, (chunk, LANES)) for e in ends], axis=0)
                e_q, e_k, e_s = jnp.exp(logb), jnp.exp(-logb), jnp.exp(b_end - logb)
                e_e = [jnp.exp(e) for e in ends]
            else:
                e_q, e_k, e_s = dec_ref[g, 0], dec_ref[g, 1], dec_ref[g, 2]
                e_e = [dec_ref[g, 3][0:1]] * n_chunks
            qd = qg * e_q
            kd_ = (kg * e_k).astype(BF16)
            ks = (kg * e_s).astype(BF16)
            for a in range(hpg):
                h = g * hpg + a
                qa = (jnp.where(lane1 // dk == a, qd, 0.0) if hpg > 1 else qd).astype(BF16)
                vh = v_ref[blk, h * dv:(h + 1) * dv]
                s = lax.dot_general(qa, kd_, (((1,), (1,)), ((), ())), preferred_element_type=F32)
                o_intra = jnp.dot(jnp.where(keep, s, 0.0).astype(BF16), vh,
                                  preferred_element_type=F32)
                st = state_ref[h]
                for c in order:
                    rs = slice(c * chunk, (c + 1) * chunk)
                    o = o_intra[rs] + lax.dot_general(qa[rs], st.astype(BF16),
                                                      (((1,), (1,)), ((), ())),
                                                      preferred_element_type=F32)
                    st = e_e[c] * st + lax.dot_general(vh[rs], ks[rs], (((0,), (0,)), ((), ())),
                                                       preferred_element_type=F32)
                    osl = (slice(r0 + c * chunk, r0 + (c + 1) * chunk), slice(h * dv, (h + 1) * dv))
                    if reverse:
                        o = _rms(o + oprev_ref[osl]) * nw_ref[...]
                        out_ref[osl] = (o * _silu(gate_ref[osl].astype(F32))).astype(BF16)
                    else:
                        out_ref[osl] = o
                state_ref[h] = st

    @pl.when(j == 0)
    def _():
        scan_rows(ctx_in, out_ctx, 0)

    @pl.when(j > 0)
    def _():
        for sb in (range(sub_blocks - 1, -1, -1) if reverse else range(sub_blocks)):
            scan_rows(lat_in, out_lat, sb * rows)


def bidir_scan(proj, la_f, la_b, norm_w, *, batch, seq, ctx_len, heads, dk, dv, chunk,
               q_blk, k_blk, v_blk, g_blk, log_decay_f=None, log_decay_b=None):
    gated = la_f is not None
    hk, hv = heads * dk, heads * dv
    rb, lb = SCAN_ROWS, SCAN_LAT_ROWS
    lat_blocks = seq // lb
    ctx_base = (batch * seq) // rb
    assert ctx_len == rb

    lat_fwd = lambda b, j: b * lat_blocks + jnp.maximum(j - 1, 0)
    lat_bwd = lambda b, j: b * lat_blocks + lat_blocks - 1 - jnp.maximum(j - 1, 0)

    prev = None
    for reverse, lat_of, la, ld in ((False, lat_fwd, la_f, log_decay_f),
                                    (True, lat_bwd, la_b, log_decay_b)):
        ctx_spec = lambda w, cb, base=ctx_base: pl.BlockSpec(
            (rb, w), lambda b, j, cb=cb: (base + b, cb))
        lat_spec = lambda w, cb: pl.BlockSpec((lb, w), lambda b, j, cb=cb: (lat_of(b, j), cb))
        in_specs, args = [], []
        for spec, is_ctx in ((ctx_spec, True), (lat_spec, False)):
            in_specs += [spec(hk, q_blk), spec(hk, k_blk), spec(hv, v_blk)]
            args += [proj, proj, proj]
            if gated:
                in_specs.append(spec(hk, 0))
                args.append(la)
            if reverse:
                in_specs += [ctx_spec(hv, 0, base=0) if is_ctx else spec(hv, 0), spec(hv, g_blk)]
                args += [prev[1] if is_ctx else prev[0], proj]
        if reverse:
            in_specs.append(pl.BlockSpec((1, dv), lambda b, j: (0, 0)))
            args.append(norm_w)
        scratch = [pltpu.VMEM((heads, dv, LANES), F32)]
        if not gated:
            scratch.append(pltpu.VMEM((hk // LANES, 4, rb, LANES), F32))
        kern = functools.partial(_scan_kernel, heads=heads, dk=dk, dv=dv, chunk=chunk,
                                 gated=gated, reverse=reverse, log_decay=ld, sub_blocks=lb // rb)
        out_dtype = BF16 if reverse else F32
        prev = pl.pallas_call(
            kern,
            grid=(batch, lat_blocks + 1),
            in_specs=in_specs,
            out_specs=[lat_spec(hv, 0), ctx_spec(hv, 0, base=0)],
            out_shape=[jax.ShapeDtypeStruct((batch * seq, hv), out_dtype),
                       jax.ShapeDtypeStruct((batch * ctx_len, hv), out_dtype)],
            scratch_shapes=scratch,
            compiler_params=_cparams(("parallel", "arbitrary")),
            name="scan_bwd" if reverse else "scan_fwd",
        )(*args)
    return tuple(prev)


def _shortconv_kernel(z_ref, zp_ref, zn_ref, w_ref, b_ref, u_ref, x0_ref, *, tiles_per_seq):
    i = pl.program_id(0)
    tm = z_ref.shape[0]
    z = z_ref[...].astype(F32)
    first = (i % tiles_per_seq) == 0
    last = (i % tiles_per_seq) == tiles_per_seq - 1
    halo = zp_ref.shape[0]
    prev_row = jnp.where(first, 0.0, zp_ref[halo - 1:halo, :].astype(F32))
    next_row = jnp.where(last, 0.0, zn_ref[0:1, :].astype(F32))
    row = lax.broadcasted_iota(jnp.int32, z.shape, 0)
    z_prev = jnp.where(row == 0, prev_row, pltpu.roll(z, 1, 0))
    z_next = jnp.where(row == tm - 1, next_row, pltpu.roll(z, tm - 1, 0))
    y = w_ref[0:1] * z_prev + w_ref[1:2] * z + w_ref[2:3] * z_next + b_ref[...]
    c = HYENA_CH
    x0_ref[...] = y[:, 0:c].astype(BF16)
    u_ref[...] = (y[:, c:2 * c] * y[:, 2 * c:3 * c]).astype(BF16)


def hyena_shortconv(proj, short_w, short_b, *, row0, batch, seq_len, z_blk, tm=256):
    halo = 16
    tiles_per_seq = seq_len // tm
    n_tiles = batch * tiles_per_seq
    t0, h_per_tile = row0 // tm, tm // halo
    nz = 3 * HYENA_CH
    n_halo_blocks = proj.shape[0] // halo
    out_map = lambda i: (i % tiles_per_seq, i // tiles_per_seq)
    kern = functools.partial(_shortconv_kernel, tiles_per_seq=tiles_per_seq)
    return pl.pallas_call(
        kern,
        grid=(n_tiles,),
        in_specs=[pl.BlockSpec((tm, nz), lambda i: (t0 + i, z_blk)),
                  pl.BlockSpec((halo, nz),
                               lambda i: (jnp.maximum((t0 + i) * h_per_tile - 1, 0), z_blk)),
                  pl.BlockSpec((halo, nz),
                               lambda i: (jnp.minimum((t0 + i + 1) * h_per_tile,
                                                      n_halo_blocks - 1), z_blk)),
                  pl.BlockSpec((3, nz), lambda i: (0, 0)),
                  pl.BlockSpec((1, nz), lambda i: (0, 0))],
        out_specs=[pl.BlockSpec((tm, HYENA_CH), out_map), pl.BlockSpec((tm, HYENA_CH), out_map)],
        out_shape=[jax.ShapeDtypeStruct((seq_len, batch * HYENA_CH), BF16)] * 2,
        compiler_params=_cparams(("parallel",)),
        name="hyena_shortconv",
    )(proj, proj, proj, short_w, short_b)


def _filter_kernel(feat_ref, w1_ref, b1_ref, w2_ref, b2_ref, w3_ref, b3_ref, w4_ref, fq_ref,
                   o_ref, *, seq_len):
    i = pl.program_id(0)
    tl = feat_ref.shape[0]
    fq = fq_ref[...]
    dot = lambda a, b: jnp.dot(a, b, preferred_element_type=F32, precision=HIGHEST)
    h = jnp.sin(fq * (dot(feat_ref[...], w1_ref[...]) + b1_ref[...]))
    h = jnp.sin(fq * (dot(h, w2_ref[...]) + b2_ref[...]))
    h = jnp.sin(fq * (dot(h, w3_ref[...]) + b3_ref[...]))
    h = dot(h, w4_ref[...])
    c = HYENA_CH
    max_decay = math.log(HYENA_TARGET) / HYENA_FAST_DECAY
    min_decay = math.log(HYENA_TARGET) / HYENA_SLOW_DECAY
    ch = lax.broadcasted_iota(jnp.int32, (tl, c), 1).astype(F32)
    deltas = min_decay + ch * ((max_decay - min_decay) / (c - 1))
    row = lax.broadcasted_iota(jnp.int32, (tl, c), 0) + i * tl
    t = row.astype(F32) * (1.0 / (seq_len - 1))
    window = jnp.exp(-t * jnp.abs(deltas))
    h_f = h[:, 0:c] * window
    h_b = jnp.where(row == 0, 0.0, h[:, c:2 * c] * window)
    o_ref[:, 0:c] = (h_f + h_b).astype(BF16)
    o_ref[:, c:2 * c] = (h_f - h_b).astype(BF16)


def hyena_filter_pair(feats, w1p, b1, w2, b2, w3, b3, w4, fq):
    seq_len = feats.shape[0]
    tl = min(2048, seq_len)
    full = lambda a: pl.BlockSpec(a.shape, lambda i: (0,) * a.ndim)
    kern = functools.partial(_filter_kernel, seq_len=seq_len)
    return pl.pallas_call(
        kern,
        grid=(seq_len // tl,),
        in_specs=[pl.BlockSpec((tl, feats.shape[1]), lambda i: (i, 0)),
                  full(w1p), full(b1), full(w2), full(b2), full(w3), full(b3), full(w4), full(fq)],
        out_specs=pl.BlockSpec((tl, 2 * HYENA_CH), lambda i: (i, 0)),
        out_shape=jax.ShapeDtypeStruct((seq_len, 2 * HYENA_CH), BF16),
        compiler_params=_cparams(("parallel",)),
        name="hyena_filter",
    )(feats, w1p, b1, w2, b2, w3, b3, w4, fq)


def _filter_spectrum_kernel(c_ref, s_ref, f_ref, hc_ref, hs_ref, kre_ref, kim_ref, *, scale):
    ch = HYENA_CH
    f = f_ref[...]
    a = jnp.dot(c_ref[...], f, preferred_element_type=F32)
    b = jnp.dot(s_ref[...], f, preferred_element_type=F32)
    hc, hs = hc_ref[...], hs_ref[...]
    kre_ref[...] = (hc * a[:, 0:ch] + hs * b[:, 0:ch]) * scale
    kim_ref[...] = (hs * a[:, ch:2 * ch] - hc * b[:, ch:2 * ch]) * scale


def filter_spectrum(cs, ss, filt, half_cos, half_sin):
    n = cs.shape[0]
    c = HYENA_CH
    tf = min(512, n)
    kern = functools.partial(_filter_spectrum_kernel, scale=1.0 / n)
    return pl.pallas_call(
        kern,
        grid=(n // tf,),
        in_specs=[pl.BlockSpec((tf, n), lambda f: (f, 0)), pl.BlockSpec((tf, n), lambda f: (f, 0)),
                  pl.BlockSpec((n, 2 * c), lambda f: (0, 0)),
                  pl.BlockSpec((tf, 1), lambda f: (f, 0)), pl.BlockSpec((tf, 1), lambda f: (f, 0))],
        out_specs=[pl.BlockSpec((tf, c), lambda f: (f, 0))] * 2,
        out_shape=[jax.ShapeDtypeStruct((n, c), F32)] * 2,
        compiler_params=_cparams(("parallel",)),
        name="hyena_filter_spectrum",
    )(cs, ss, filt, half_cos, half_sin)


def _dft_fwd_kernel(c_ref, s_ref, u_ref, kre_ref, kim_ref, yre_ref, yim_ref):
    u = u_ref[...]
    a = jnp.dot(c_ref[...], u, preferred_element_type=F32)
    b = jnp.dot(s_ref[...], u, preferred_element_type=F32)
    k_re, k_im = kre_ref[...], kim_ref[...]
    yre_ref[...] = (a * k_re + b * k_im).astype(BF16)
    yim_ref[...] = (a * k_im - b * k_re).astype(BF16)


def dft_forward(cs, ss, u, k_re, k_im, *, batch):
    n = cs.shape[0]
    c = HYENA_CH
    tf = min(512, n)
    return pl.pallas_call(
        _dft_fwd_kernel,
        grid=(batch, n // tf),
        in_specs=[pl.BlockSpec((tf, n), lambda bi, f: (f, 0)),
                  pl.BlockSpec((tf, n), lambda bi, f: (f, 0)),
                  pl.BlockSpec((n, c), lambda bi, f: (0, bi)),
                  pl.BlockSpec((tf, c), lambda bi, f: (f, 0)),
                  pl.BlockSpec((tf, c), lambda bi, f: (f, 0))],
        out_specs=[pl.BlockSpec((tf, c), lambda bi, f: (f, bi))] * 2,
        out_shape=[jax.ShapeDtypeStruct((n, batch * c), BF16)] * 2,
        compiler_params=_cparams(("parallel", "parallel")),
        name="dft_forward",
    )(cs, ss, u, k_re, k_im)


def _dft_inv_kernel(c_ref, s_ref, yre_ref, yim_ref, u_ref, x0_ref, bias_ref, o_ref):
    y = (jnp.dot(c_ref[...], yre_ref[...], preferred_element_type=F32)
         - jnp.dot(s_ref[...], yim_ref[...], preferred_element_type=F32))
    u = u_ref[...].astype(F32)
    o_ref[...] = (x0_ref[...].astype(F32) * (y + u * bias_ref[...])).astype(BF16)


def dft_inverse(cs, ss, yre, yim, u, x0, bias, *, batch):
    n = cs.shape[0]
    c = HYENA_CH
    tt = min(512, n)
    col = lambda bi, t: (0, bi)
    tile = lambda bi, t: (t, bi)
    return pl.pallas_call(
        _dft_inv_kernel,
        grid=(batch, n // tt),
        in_specs=[pl.BlockSpec((tt, n), lambda bi, t: (t, 0)),
                  pl.BlockSpec((tt, n), lambda bi, t: (t, 0)),
                  pl.BlockSpec((n, c), col), pl.BlockSpec((n, c), col),
                  pl.BlockSpec((tt, c), tile), pl.BlockSpec((tt, c), tile),
                  pl.BlockSpec((1, c), lambda bi, t: (0, 0))],
        out_specs=pl.BlockSpec((tt, c), lambda bi, t: (bi * (n // tt) + t, 0)),
        out_shape=jax.ShapeDtypeStruct((batch * n, c), BF16),
        compiler_params=_cparams(("parallel", "parallel")),
        name="dft_inverse",
    )(cs, ss, yre, yim, u, x0, bias)


def _dft_table_kernel(cg_ref, sg_ref, cd_ref, sd_ref, c_ref, s_ref):
    cg, sg, cd, sd = cg_ref[...], sg_ref[...], cd_ref[0], sd_ref[0]
    c_ref[...] = (cg * cd - sg * sd).astype(BF16)
    s_ref[...] = (sg * cd + cg * sd).astype(BF16)


def _shifted_dft_tables(n):
    tf = min(256, n)
    theta = 2.0 * math.pi / (8 * n)
    odd_s = 2 * jnp.arange(n, dtype=jnp.int32)[None, :] + 1
    odd_i = 2 * jnp.arange(tf, dtype=jnp.int32)[:, None] + 1
    gamma = ((odd_i * odd_s) % (8 * n)).astype(F32) * theta
    f0 = jnp.arange(n // tf, dtype=jnp.int32)[:, None] * tf
    delta = ((2 * f0 * odd_s) % (8 * n)).astype(F32)[:, None, :] * theta
    whole = pl.BlockSpec((tf, n), lambda i: (0, 0))
    row = pl.BlockSpec((1, 1, n), lambda i: (i, 0, 0))
    cs, ss = pl.pallas_call(
        _dft_table_kernel,
        grid=(n // tf,),
        in_specs=[whole, whole, row, row],
        out_specs=[pl.BlockSpec((tf, n), lambda i: (i, 0))] * 2,
        out_shape=[jax.ShapeDtypeStruct((n, n), BF16)] * 2,
        compiler_params=_cparams(("parallel",)),
        name="dft_tables",
    )(jnp.cos(gamma), jnp.sin(gamma), jnp.cos(delta), jnp.sin(delta))
    half = (2 * jnp.arange(n, dtype=jnp.int32) + 1).astype(F32)[:, None] * (math.pi / (4 * n))
    return cs, ss, jnp.cos(half), jnp.sin(half)


def _filter_features(seq_len):
    t = jnp.linspace(0.0, 1.0, seq_len, dtype=F32)[:, None]
    bands = (HYENA_EMB - 1) // 2
    ang = 2.0 * math.pi * jnp.arange(seq_len, dtype=F32)[:, None] / seq_len
    fr = jnp.linspace(1e-4, bands - 1, bands, dtype=F32)[None, :]
    feats = jnp.concatenate([t, jnp.cos(fr * ang), -jnp.sin(fr * ang)], axis=-1)
    return jnp.pad(feats, ((0, 0), (0, LANES - HYENA_EMB)))


def hyena_long_conv(proj, hp, *, row0, batch, seq_len, z_blk):
    u, x0 = hyena_shortconv(proj, hp["short_w"], hp["short_b"], row0=row0, batch=batch,
                            seq_len=seq_len, z_blk=z_blk, tm=min(512, seq_len))
    filt = hyena_filter_pair(_filter_features(seq_len), hp["w1"], hp["b1"], hp["w2"], hp["b2"],
                             hp["w3"], hp["b3"], hp["w4"], hp["freq"])
    cs, ss, half_cos, half_sin = _shifted_dft_tables(seq_len)
    k_re, k_im = filter_spectrum(cs, ss, filt, half_cos, half_sin)
    yre, yim = dft_forward(cs, ss, u, k_re, k_im, batch=batch)
    return dft_inverse(cs, ss, yre, yim, u, x0, hp["bias"], batch=batch)


def _outproj_kernel(mal_ref, mac_ref, mbl_ref, mbc_ref, xl_ref, xc_ref, mod_ref, nw_ref, w_ref,
                    rw_ref, xo_ref, h_ref, lt_ref, *, n_lat_tiles):
    half = mal_ref.shape[1]
    pick = functools.partial(_pick_rows, n_lat_tiles=n_lat_tiles)
    y = (jnp.dot(pick(mal_ref, mac_ref), w_ref[0:half], preferred_element_type=F32)
         + jnp.dot(pick(mbl_ref, mbc_ref), w_ref[half:2 * half], preferred_element_type=F32))
    mod = mod_ref[0]
    x = pick(xl_ref, xc_ref) + mod[2:3] * y
    xo_ref[...] = x
    h = _modulated_norm(x, mod, nw_ref[...], 3, 4)
    h_ref[...] = h.astype(BF16)
    h_hi = h.astype(BF16)
    h_lo = (h - h_hi.astype(F32)).astype(BF16)
    nt = lambda a, b: lax.dot_general(a, b, (((1,), (1,)), ((), ())), preferred_element_type=F32)
    lt_ref[...] = nt(rw_ref[0], h_hi) + (nt(rw_ref[0], h_lo) + nt(rw_ref[1], h_hi))


def outproj(mix_a, a_blk, mix_b, b_blk, x_rows, mods, nw, w_out, router_wt, *,
            n_rows, t_lat, seq, tm=512):
    x_lat, x_ctx, ctx_row0 = x_rows
    d = x_lat.shape[1]
    half = d // 2
    n_lat_tiles = t_lat // tm
    return pl.pallas_call(
        functools.partial(_outproj_kernel, n_lat_tiles=n_lat_tiles),
        grid=(n_rows // tm,),
        in_specs=_split_row_specs(tm, half, n_lat_tiles, a_blk)
                 + _split_row_specs(tm, half, n_lat_tiles, b_blk)
                 + _split_row_specs(tm, d, n_lat_tiles, ctx_row0=ctx_row0) + [
                  pl.BlockSpec((1, 6, d), lambda i: (_seg_of_tile(i, tm, t_lat, seq), 0, 0)),
                  pl.BlockSpec((1, d), lambda i: (0, 0)),
                  pl.BlockSpec((d, d), lambda i: (0, 0)),
                  pl.BlockSpec((2, N_EXPERTS, d), lambda i: (0, 0, 0))],
        out_specs=[pl.BlockSpec((tm, d), lambda i: (i, 0)),
                   pl.BlockSpec((tm, d), lambda i: (i, 0)),
                   pl.BlockSpec((N_EXPERTS, tm), lambda i: (0, i))],
        out_shape=[jax.ShapeDtypeStruct((n_rows, d), F32),
                   jax.ShapeDtypeStruct((n_rows, d), BF16),
                   jax.ShapeDtypeStruct((N_EXPERTS, n_rows), F32)],
        compiler_params=_cparams(("parallel",)),
        name="outproj",
    )(*mix_a, *mix_b, x_lat, x_ctx, mods, nw, w_out, router_wt)


def _first_max(x, idx, sentinel):
    m = jnp.max(x, axis=0, keepdims=True)
    first = jnp.min(jnp.where(x == m, idx, sentinel), axis=0, keepdims=True)
    return m, idx == first


def _router_kernel(lt_ref, bias_ref, before_ref, below_ref, lists_ref, pc_ref):
    tn = lt_ref.shape[1]
    scores = jax.nn.sigmoid(lt_ref[...])
    sel = scores + bias_ref[...]
    neg = -jnp.inf
    in_grp = lax.broadcasted_iota(jnp.int32, (GROUP_SIZE, tn), 0)
    gscore = []
    for g in range(N_GROUPS):
        x = sel[g * GROUP_SIZE:(g + 1) * GROUP_SIZE]
        m1, hit = _first_max(x, in_grp, GROUP_SIZE)
        gscore.append(m1 + jnp.max(jnp.where(hit, neg, x), axis=0, keepdims=True))
    rows = []
    for g in range(N_GROUPS):
        beaten = jnp.zeros((1, tn), jnp.int32)
        for o in range(N_GROUPS):
            if o != g:
                wins = (gscore[o] >= gscore[g]) if o < g else (gscore[o] > gscore[g])
                beaten = beaten + wins.astype(jnp.int32)
        keep = jnp.broadcast_to(beaten < TOPK_GROUPS, (GROUP_SIZE, tn))
        rows.append(jnp.where(keep, sel[g * GROUP_SIZE:(g + 1) * GROUP_SIZE], neg))
    cand = jnp.concatenate(rows, axis=0)
    eidx = lax.broadcasted_iota(jnp.int32, cand.shape, 0)
    chosen = jnp.zeros(cand.shape, jnp.bool_)
    hits = []
    for _ in range(TOP_K):
        _, hit = _first_max(cand, eidx, N_EXPERTS)
        hits.append(hit)
        chosen = jnp.logical_or(chosen, hit)
        cand = jnp.where(hit, neg, cand)
    w = jnp.where(chosen, scores, 0.0)
    gates_t = w / jnp.sum(w, axis=0, keepdims=True) * ROUTED_SCALE
    chosen_f = jnp.where(chosen, 1.0, 0.0)
    rank = jnp.dot(chosen_f.astype(BF16), before_ref[...], preferred_element_type=F32)
    count = jnp.sum(chosen_f, axis=1, keepdims=True)
    pc = jnp.floor((count + (ROW_UNIT - 1)) * (1.0 / ROW_UNIT)) * ROW_UNIT
    pc_lanes = jnp.broadcast_to(pc, (N_EXPERTS, LANES))
    start = jnp.dot(below_ref[...], pc_lanes, preferred_element_type=F32,
                    precision=HIGHEST)[:, 0:1]
    slot = start + rank
    pick = lambda hit, v: jnp.sum(jnp.where(hit, v, 0.0), axis=0, keepdims=True)
    lists = jnp.concatenate([pick(h, slot) for h in hits] + [pick(h, gates_t) for h in hits], axis=0)
    lists_ref[...] = lists
    pc_ref[0] = pc_lanes


def router(logits, router_bias):
    e, t = logits.shape
    tn = MOE_TILE
    n_tiles = t // tn
    tok_i = jnp.arange(tn, dtype=jnp.int32)
    before = (tok_i[:, None] < tok_i[None, :]).astype(BF16)
    exp_i = jnp.arange(e, dtype=jnp.int32)
    below = (exp_i[None, :] < exp_i[:, None]).astype(F32)
    const = lambda shape: pl.BlockSpec(shape, lambda i: (0,) * len(shape))
    return pl.pallas_call(
        _router_kernel,
        grid=(n_tiles,),
        in_specs=[pl.BlockSpec((e, tn), lambda i: (0, i)),
                  const((e, 1)), const((tn, tn)), const((e, e))],
        out_specs=[pl.BlockSpec((2 * TOP_K, tn), lambda i: (0, i)),
                   pl.BlockSpec((1, e, LANES), lambda i: (i, 0, 0))],
        out_shape=[jax.ShapeDtypeStruct((2 * TOP_K, t), F32),
                   jax.ShapeDtypeStruct((n_tiles, e, LANES), F32)],
        compiler_params=_cparams(("parallel",)),
        name="router",
    )(logits, router_bias.reshape(e, 1), before, below)


def moe_plan(pc, n_tiles):
    e = pc.shape[1]
    assert n_tiles >= e
    cap_rows = _sorted_capacity(n_tiles)
    run_end = jnp.cumsum(pc, axis=1)
    run_start = run_end - pc
    total = jnp.sum(pc, axis=0)
    total_al = ((total + EXPERT_ROWS - 1) // EXPERT_ROWS) * EXPERT_ROWS
    range_end = jnp.cumsum(total_al)
    range_start = range_end - total_al
    run_row = range_start[None, :] + jnp.cumsum(pc, axis=0) - pc
    unit_row = jnp.arange(N_UNITS, dtype=jnp.int32) * ROW_UNIT
    owner = jnp.sum((run_end[:, None, :] <= unit_row[None, :, None]).astype(jnp.int32), axis=2)
    owner_hot = (owner[:, :, None] == jnp.arange(e, dtype=jnp.int32)).astype(jnp.int32)
    dst = jnp.sum(owner_hot * (run_row - run_start)[:, None, :], axis=2) + unit_row[None, :]
    used = owner < e
    dump = cap_rows + (jnp.arange(n_tiles, dtype=jnp.int32) % 2)[:, None] * DUMP_ROWS
    unit_dst = jnp.where(used, dst, dump + unit_row[None, :]).astype(jnp.int32)
    unit_src = jnp.where(used, dst, 0).astype(jnp.int32)
    pad_off = jnp.arange(PAD_UNITS, dtype=jnp.int32)[None, :] * ROW_UNIT
    pad_row = (range_start + total)[:, None] + pad_off
    pad_row = jnp.concatenate([jnp.where(pad_row < range_end[:, None], pad_row, -1),
                               jnp.full((n_tiles - e, PAD_UNITS), -1, jnp.int32)], axis=0)
    pad_dst = jnp.where(pad_row >= 0, pad_row, dump + SLOTS + pad_off).astype(jnp.int32)
    tile_end = range_end // EXPERT_ROWS
    tile_expert = jnp.sum((tile_end[None, :] <= jnp.arange(cap_rows // EXPERT_ROWS,
                                                           dtype=jnp.int32)[:, None])
                          .astype(jnp.int32), axis=1)
    tile_expert = jnp.minimum(tile_expert, e - 1).astype(jnp.int32)
    used_rows = run_end[:, -1].astype(jnp.int32)
    return unit_dst, unit_src, pad_dst, used_rows, tile_expert, tile_end[-1:].astype(jnp.int32)


def _sorted_capacity(n_tiles):
    worst_rows = n_tiles * (MOE_TILE * TOP_K + N_EXPERTS * (ROW_UNIT - 1)) + N_EXPERTS * (EXPERT_ROWS - 1)
    return -(-worst_rows // EXPERT_ROWS) * EXPERT_ROWS


DUMP_ROWS = SLOTS + PAD_UNITS * ROW_UNIT


UNITS_PER_CHUNK = SLOT_CHUNK // ROW_UNIT
MAIN_SLOTS = SLOTS - SLOT_CHUNK
assert MAIN_SLOTS >= MOE_TILE * TOP_K


def _dispatch_kernel(unit_ref, pad_ref, used_ref, h_ref, lists_ref, xs_ref, loc_ref, zero_ref, sems):
    i = pl.program_id(0)
    last = pl.num_programs(0) - 1
    slot = i % 2
    tn = h_ref.shape[0]

    @pl.when(i == 0)
    def _():
        zero_ref[...] = jnp.zeros_like(zero_ref)

    rel_rows = lax.broadcasted_iota(jnp.int32, (SUB_SLOTS, tn), 0).astype(F32).astype(BF16)

    def chunk(c0):
        parts = []
        for b0 in range(c0, c0 + SLOT_CHUNK, SUB_SLOTS):
            p = jnp.zeros((SUB_SLOTS, tn), BF16)
            for k in range(TOP_K):
                hit = rel_rows == (lists_ref[k:k + 1, :] - float(b0)).astype(BF16)
                p = jnp.where(hit, jnp.ones_like(p), p)
            parts.append(p)
        loc_ref[slot, c0:c0 + SLOT_CHUNK, :] = jnp.dot(
            jnp.concatenate(parts, axis=0), h_ref[...],
            preferred_element_type=F32).astype(BF16)
        for u in range(c0 // ROW_UNIT, c0 // ROW_UNIT + UNITS_PER_CHUNK):
            row = pl.multiple_of(unit_ref[i, u], ROW_UNIT)
            pltpu.make_async_copy(loc_ref.at[slot, pl.ds(u * ROW_UNIT, ROW_UNIT)],
                                  xs_ref.at[pl.ds(row, ROW_UNIT)], sems.at[slot]).start()

    for c0 in range(0, MAIN_SLOTS, SLOT_CHUNK):
        chunk(c0)

    @pl.when(used_ref[i] > MAIN_SLOTS)
    def _():
        chunk(MAIN_SLOTS)

    for p in range(PAD_UNITS):
        row = pl.multiple_of(pad_ref[i, p], ROW_UNIT)
        pltpu.make_async_copy(zero_ref, xs_ref.at[pl.ds(row, ROW_UNIT)], sems.at[slot]).start()

    def wait_all(s, tile):
        pltpu.make_async_copy(loc_ref.at[s, pl.ds(0, MAIN_SLOTS)],
                              xs_ref.at[pl.ds(0, MAIN_SLOTS)], sems.at[s]).wait()

        @pl.when(used_ref[tile] > MAIN_SLOTS)
        def _():
            pltpu.make_async_copy(loc_ref.at[s, pl.ds(MAIN_SLOTS, SLOT_CHUNK)],
                                  xs_ref.at[pl.ds(0, SLOT_CHUNK)], sems.at[s]).wait()
        pltpu.make_async_copy(loc_ref.at[s, pl.ds(0, PAD_UNITS * ROW_UNIT)],
                              xs_ref.at[pl.ds(0, PAD_UNITS * ROW_UNIT)], sems.at[s]).wait()

    @pl.when(i > 0)
    def _():
        wait_all(1 - slot, i - 1)

    @pl.when(i == last)
    def _():
        wait_all(slot, i)


def moe_dispatch(h, lists, unit_dst, pad_dst, used_rows, *, n_tiles):
    d = h.shape[1]
    cap_rows = _sorted_capacity(n_tiles) + 2 * DUMP_ROWS
    return pl.pallas_call(
        _dispatch_kernel,
        grid_spec=pltpu.PrefetchScalarGridSpec(
            num_scalar_prefetch=3,
            grid=(n_tiles,),
            in_specs=[pl.BlockSpec((MOE_TILE, d), lambda i, *_: (i, 0)),
                      pl.BlockSpec((2 * TOP_K, MOE_TILE), lambda i, *_: (0, i))],
            out_specs=pl.BlockSpec(memory_space=pl.ANY),
            scratch_shapes=[pltpu.VMEM((2, SLOTS, d), BF16), pltpu.VMEM((ROW_UNIT, d), BF16),
                            pltpu.SemaphoreType.DMA((2,))]),
        out_shape=jax.ShapeDtypeStruct((cap_rows, d), BF16),
        compiler_params=_cparams(("arbitrary",)),
        name="moe_dispatch",
    )(unit_dst, pad_dst, used_rows, h, lists)


X_RING = 3


def _expert_kernel(te_ref, nt_ref, xs_ref, wg_ref, wu_ref, wd_ref, y_ref, x_buf, wg_bf, wu_bf,
                   wd_bf, sems):
    t = pl.program_id(0)
    n_live = nt_ref[0]
    live = t < n_live
    tc = jnp.minimum(t, jnp.maximum(n_live - 1, 0))
    new_expert = jnp.logical_or(t == 0, te_ref[tc] != te_ref[jnp.maximum(tc - 1, 0)])
    rows = x_buf.shape[1]

    def x_copy(tile):
        s = tile % X_RING
        return pltpu.make_async_copy(xs_ref.at[pl.ds(pl.multiple_of(tile * rows, rows), rows)],
                                     x_buf.at[s], sems.at[s])

    for k in range(X_RING - 1):
        @pl.when(jnp.logical_and(t == 0, k < n_live))
        def _():
            x_copy(k).start()

    @pl.when(t + (X_RING - 1) < n_live)
    def _():
        x_copy(t + (X_RING - 1)).start()

    @pl.when(jnp.logical_and(live, new_expert))
    def _():
        wg_bf[...] = wg_ref[0, 0].astype(BF16)
        wu_bf[...] = wu_ref[0, 0].astype(BF16)
        wd_bf[...] = wd_ref[0, 0].astype(BF16)

    @pl.when(live)
    def _():
        x_copy(t).wait()
        x = x_buf[t % X_RING]
        act = (_silu(jnp.dot(x, wg_bf[...], preferred_element_type=F32))
               * jnp.dot(x, wu_bf[...], preferred_element_type=F32))
        y_ref[...] = jnp.dot(act.astype(BF16), wd_bf[...], preferred_element_type=F32).astype(BF16)


def moe_experts(xs, w_gate, w_up, w_down, layer, tile_expert, n_row_tiles):
    d = xs.shape[1]
    cap_rows = xs.shape[0] - 2 * DUMP_ROWS
    ff = w_gate.shape[3]
    last_live = lambda nt: jnp.maximum(nt[0] - 1, 0)
    row_map = lambda t, te, nt: (jnp.minimum(t, last_live(nt)), 0)
    w_map = lambda t, te, nt: (layer, te[jnp.minimum(t, last_live(nt))], 0, 0)
    return pl.pallas_call(
        _expert_kernel,
        grid_spec=pltpu.PrefetchScalarGridSpec(
            num_scalar_prefetch=2,
            grid=(cap_rows // EXPERT_ROWS,),
            in_specs=[pl.BlockSpec(memory_space=pl.ANY),
                      pl.BlockSpec((1, 1, d, ff), w_map), pl.BlockSpec((1, 1, d, ff), w_map),
                      pl.BlockSpec((1, 1, ff, d), w_map)],
            out_specs=pl.BlockSpec((EXPERT_ROWS, d), row_map),
            scratch_shapes=[pltpu.VMEM((X_RING, EXPERT_ROWS, d), BF16),
                            pltpu.VMEM((d, ff), BF16), pltpu.VMEM((d, ff), BF16),
                            pltpu.VMEM((ff, d), BF16), pltpu.SemaphoreType.DMA((X_RING,))]),
        out_shape=jax.ShapeDtypeStruct((cap_rows, d), BF16),
        compiler_params=_cparams(("arbitrary",)),
        name="moe_experts",
    )(tile_expert, n_row_tiles, xs, w_gate, w_up, w_down)


def _combine_kernel(unit_ref, used_ref, ys_ref, lists_ref, h_ref, x_ref, mod_ref, fnw_ref, sg_ref,
                    su_ref, sd_ref, o_ref, loc_ref, acc_ref, sems, *, final_norm):
    i = pl.program_id(0)
    last = pl.num_programs(0) - 1
    slot = i % 2
    tn = h_ref.shape[0]
    def fetch(tile, s):
        def units(u0, u1):
            for u in range(u0, u1):
                row = pl.multiple_of(unit_ref[tile, u], ROW_UNIT)
                pltpu.make_async_copy(ys_ref.at[pl.ds(row, ROW_UNIT)],
                                      loc_ref.at[s, pl.ds(u * ROW_UNIT, ROW_UNIT)], sems.at[s]).start()
        units(0, MAIN_SLOTS // ROW_UNIT)

        @pl.when(used_ref[tile] > MAIN_SLOTS)
        def _():
            units(MAIN_SLOTS // ROW_UNIT, N_UNITS)

    @pl.when(i == 0)
    def _():
        fetch(0, 0)

    @pl.when(i < last)
    def _():
        fetch(i + 1, 1 - slot)

    h = h_ref[...]
    act = (_silu(jnp.dot(h, sg_ref[...], preferred_element_type=F32))
           * jnp.dot(h, su_ref[...], preferred_element_type=F32))
    acc = jnp.dot(act.astype(BF16), sd_ref[...], preferred_element_type=F32)
    tail = used_ref[i] > MAIN_SLOTS
    pltpu.make_async_copy(ys_ref.at[pl.ds(0, MAIN_SLOTS)], loc_ref.at[slot, pl.ds(0, MAIN_SLOTS)],
                          sems.at[slot]).wait()

    @pl.when(tail)
    def _():
        pltpu.make_async_copy(ys_ref.at[pl.ds(0, SLOT_CHUNK)],
                              loc_ref.at[slot, pl.ds(MAIN_SLOTS, SLOT_CHUNK)], sems.at[slot]).wait()

    rel_rows = lax.broadcasted_iota(jnp.int32, (SUB_SLOTS, tn), 0).astype(F32).astype(BF16)
    gate_w = [lists_ref[TOP_K + k:TOP_K + k + 1, :].astype(BF16) for k in range(TOP_K)]

    def chunk(c0):
        parts = []
        for b0 in range(c0, c0 + SLOT_CHUNK, SUB_SLOTS):
            q = jnp.zeros((SUB_SLOTS, tn), BF16)
            for k in range(TOP_K):
                hit = rel_rows == (lists_ref[k:k + 1, :] - float(b0)).astype(BF16)
                q = jnp.where(hit, gate_w[k], q)
            parts.append(q)
        return lax.dot_general(jnp.concatenate(parts, axis=0), loc_ref[slot, c0:c0 + SLOT_CHUNK, :],
                               (((0,), (0,)), ((), ())), preferred_element_type=F32)

    for c0 in range(0, MAIN_SLOTS, SLOT_CHUNK):
        acc = acc + chunk(c0)
    acc_ref[...] = acc

    @pl.when(tail)
    def _():
        acc_ref[...] += chunk(MAIN_SLOTS)

    x = x_ref[...] + mod_ref[0][5:6] * acc_ref[...]
    if final_norm:
        x = _rms(x) * fnw_ref[...]
    o_ref[...] = x


def moe_combine(ys, lists, h, x_all, mods, fnw, sh_gate, sh_up, sh_down, unit_dst, used_rows, *,
                n_tiles, t_lat, seq, final_norm):
    d = h.shape[1]
    tn = MOE_TILE
    const = lambda a: pl.BlockSpec(a.shape, lambda i, *_: (0,) * a.ndim)
    kern = functools.partial(_combine_kernel, final_norm=final_norm)
    return pl.pallas_call(
        kern,
        grid_spec=pltpu.PrefetchScalarGridSpec(
            num_scalar_prefetch=2,
            grid=(n_tiles,),
            in_specs=[pl.BlockSpec(memory_space=pl.ANY),
                      pl.BlockSpec((2 * TOP_K, tn), lambda i, *_: (0, i)),
                      pl.BlockSpec((tn, d), lambda i, *_: (i, 0)),
                      pl.BlockSpec((tn, d), lambda i, *_: (i, 0)),
                      pl.BlockSpec((1, 6, d),
                                   lambda i, *_: (_seg_of_tile(i, tn, t_lat, seq), 0, 0)),
                      const(fnw), const(sh_gate), const(sh_up), const(sh_down)],
            out_specs=pl.BlockSpec((tn, d), lambda i, *_: (i, 0)),
            scratch_shapes=[pltpu.VMEM((2, SLOTS, d), BF16), pltpu.VMEM((tn, d), F32),
                            pltpu.SemaphoreType.DMA((2,))]),
        out_shape=jax.ShapeDtypeStruct((n_tiles * tn, d), F32),
        compiler_params=_cparams(("arbitrary",)),
        name="moe_combine",
    )(unit_dst, used_rows, ys, lists, h, x_all, mods, fnw, sh_gate, sh_up, sh_down)


def moe_block(h2, logits, router_bias, x_mid, mods, fnw, w_gate, w_up, w_down,
              sh_gate, sh_up, sh_down, *, layer, t_lat, seq, final_norm):
    n_tiles = h2.shape[0] // MOE_TILE
    lists, pc = router(logits, router_bias)
    unit_dst, unit_src, pad_dst, used_rows, tile_expert, n_row_tiles = moe_plan(
        pc[:, :, 0].astype(jnp.int32), n_tiles)
    xs = moe_dispatch(h2, lists, unit_dst, pad_dst, used_rows, n_tiles=n_tiles)
    ys = moe_experts(xs, w_gate, w_up, w_down, layer, tile_expert, n_row_tiles)
    return moe_combine(ys, lists, h2, x_mid, mods, fnw, sh_gate.astype(BF16), sh_up.astype(BF16),
                       sh_down.astype(BF16), unit_src, used_rows, n_tiles=n_tiles, t_lat=t_lat,
                       seq=seq, final_norm=final_norm)


def _rope_tables(seq):
    half = RET_DK // 2
    pos = jnp.arange(seq, dtype=jnp.int32)
    row = (pos // GRID_W).astype(F32)
    col = (pos % GRID_W).astype(F32)
    inv = ROPE_BASE ** (-jnp.arange(0, half, 2, dtype=F32) / half)
    a_row, a_col = row[:, None] * inv[None, :], col[:, None] * inv[None, :]
    cos_h = jnp.concatenate([jnp.cos(a_row)] * 2 + [jnp.cos(a_col)] * 2, axis=-1)
    sin_h = jnp.concatenate([-jnp.sin(a_row), jnp.sin(a_row), -jnp.sin(a_col), jnp.sin(a_col)], -1)
    reps = LANES // RET_DK
    return jnp.tile(cos_h, (1, reps)), jnp.tile(sin_h, (1, reps))


def _rope_partner_columns():
    quarter = RET_DK // 4
    idx = jnp.arange(2 * RET_HEADS * RET_DK, dtype=jnp.int32)
    within = idx % (2 * quarter)
    return jnp.where(within < quarter, idx + quarter, idx - quarter)


def kernel(x, c, ctx, c_ctx, ada_w, ada_b, norm1_w, norm2_w, ev_w_in, ev_short_w, ev_short_b, ev_filt_w1, ev_filt_b1, ev_filt_w2, ev_filt_b2, ev_filt_w3, ev_filt_b3, ev_filt_w4, ev_filt_freq, ev_hyena_bias, ev_w_out, od_w_in, od_gate_w1_f, od_gate_w2_f, od_gate_b_f, od_gate_w1_b, od_gate_w2_b, od_gate_b_b, od_norm_w, od_w_out, router_w, router_bias, exp_w_gate, exp_w_up, exp_w_down, sh_w_gate, sh_w_up, sh_w_down, final_norm_w):
    batch, seq, d = x.shape
    ctx_len = ctx.shape[1]
    depth = ada_w.shape[0]
    t_lat, t_ctx = batch * seq, batch * ctx_len
    t_all = t_lat + t_ctx
    tm = 1024

    x_all = None
    x_rows = (x.reshape(t_lat, d), ctx.reshape(t_ctx, d), 0)
    cond8 = jnp.concatenate([c_ctx[None, :], c, jnp.zeros((8 - 1 - batch, d), F32)], axis=0)
    mods_all = adaln_rows(cond8, ada_w, ada_b).reshape(depth, 8, 6, d)

    for i in range(depth):
        last = i == depth - 1
        j = i // 2
        mods = mods_all[i]
        nw1, nw2 = norm1_w[i][None, :], norm2_w[i][None, :]
        if i % 2 == 0:
            qk = 2 * RET_HEADS * RET_DK
            k_scale = jnp.concatenate([jnp.ones((qk // 2,), F32),
                                       jnp.full((qk // 2,), RET_DK ** -0.5, F32)])
            w_in = ev_w_in[j]
            w_qk = w_in[:, :qk] * k_scale
            w_ext = jnp.concatenate([w_qk, w_in[:, qk:], w_qk[:, _rope_partner_columns()]],
                                    axis=1).astype(BF16)
            cos_t, sin_t = _rope_tables(seq)
            proj = inproj_even(x_rows, mods, nw1, w_ext, cos_t, sin_t, t_lat=t_lat, t_all=t_all,
                               seq=seq, tm=tm)
            log_g = [math.log1p(-2.0 ** (-5.0 - h)) for h in range(RET_HEADS)]
            mix_a = bidir_scan(proj, None, None, jnp.ones((1, RET_DV), F32),
                               batch=batch, seq=seq, ctx_len=ctx_len, heads=RET_HEADS, dk=RET_DK,
                               dv=RET_DV, chunk=RET_CHUNK, q_blk=0, k_blk=1, v_blk=1, g_blk=2,
                               log_decay_f=log_g, log_decay_b=log_g[::-1])
            hp = dict(short_w=ev_short_w[j], short_b=ev_short_b[j][None, :],
                      w1=jnp.pad(ev_filt_w1[j], ((0, LANES - HYENA_EMB), (0, 0))),
                      b1=ev_filt_b1[j][None, :], w2=ev_filt_w2[j], b2=ev_filt_b2[j][None, :],
                      w3=ev_filt_w3[j], b3=ev_filt_b3[j][None, :], w4=ev_filt_w4[j],
                      freq=ev_filt_freq[j][None, :], bias=ev_hyena_bias[j][None, :])
            hy_lat = hyena_long_conv(proj, hp, row0=0, batch=batch, seq_len=seq, z_blk=1)
            hy_ctx = hyena_long_conv(proj, hp, row0=t_lat, batch=batch, seq_len=ctx_len, z_blk=1)
            mix_b = (hy_lat, hy_ctx)
            a_blk, b_blk = 0, 0
            w_out = ev_w_out[j].astype(BF16)
        else:
            kd = GLA_HEADS * GLA_DK
            pad_cols = LANES - 2 * GLA_RANK
            w_ext = jnp.concatenate([od_w_in[j], od_gate_w1_f[j], od_gate_w1_b[j],
                                     jnp.zeros((d, pad_cols), F32)], axis=1).astype(BF16)
            w2 = jnp.concatenate(
                [jnp.pad(od_gate_w2_f[j], ((0, LANES - GLA_RANK), (0, 0))),
                 jnp.pad(od_gate_w2_b[j], ((GLA_RANK, LANES - 2 * GLA_RANK), (0, 0)))], axis=1)
            w2_hi = w2.astype(BF16)
            w2_split = jnp.concatenate([w2_hi, w2_hi, (w2 - w2_hi.astype(F32)).astype(BF16)], axis=0)
            b2 = jnp.concatenate([od_gate_b_f[j], od_gate_b_b[j]])[None, :]
            if x_all is None:
                x_all = jnp.concatenate(x_rows[:2], axis=0)
            proj, la_f, la_b = inproj_odd(x_all, mods, nw1, w_ext, w2_split, b2,
                                          t_lat=t_lat, seq=seq, tm=tm)
            mix_a = bidir_scan(proj, la_f, la_b, od_norm_w[j][None, :],
                               batch=batch, seq=seq, ctx_len=ctx_len, heads=GLA_HEADS, dk=GLA_DK,
                               dv=GLA_DV, chunk=GLA_CHUNK, q_blk=0, k_blk=1, v_blk=1, g_blk=2)
            mix_b = mix_a
            a_blk, b_blk = 0, 1
            w_out = od_w_out[j].astype(BF16)

        n_rows = t_lat if last else t_all
        rw_t = router_w[i].T
        rw_hi = rw_t.astype(BF16)
        rw_split = jnp.stack([rw_hi, (rw_t - rw_hi.astype(F32)).astype(BF16)])
        x_mid, h2, logits = outproj(mix_a, a_blk, mix_b, b_blk, x_rows, mods, nw2, w_out,
                                    rw_split, n_rows=n_rows, t_lat=t_lat, seq=seq, tm=tm)
        x_all = moe_block(h2, logits, router_bias[i], x_mid, mods, final_norm_w[None, :],
                          exp_w_gate, exp_w_up, exp_w_down,
                          sh_w_gate[i], sh_w_up[i], sh_w_down[i],
                          layer=i, t_lat=t_lat, seq=seq, final_norm=last)
        x_rows = (x_all, x_all, t_lat)
    return x_all[:t_lat].reshape(batch, seq, d)
```

```python
import functools
import math

import jax
import jax.numpy as jnp
from jax import lax
from jax.experimental import pallas as pl
from jax.experimental.pallas import tpu as pltpu

F32 = jnp.float32
BF16 = jnp.bfloat16
HIGHEST = lax.Precision.HIGHEST

D_MODEL = 1024
GRID_W = 64
NORM_EPS = 1e-6
RET_HEADS, RET_DK, RET_DV, RET_CHUNK = 4, 64, 128, 128
ROPE_BASE = 10000.0
HYENA_CH, HYENA_EMB = 512, 33
HYENA_FAST_DECAY, HYENA_SLOW_DECAY, HYENA_TARGET = 0.3, 1.5, 1e-2
GLA_HEADS, GLA_DK, GLA_DV, GLA_RANK, GLA_TAU, GLA_CHUNK = 4, 128, 256, 16, 16.0, 64
N_EXPERTS, TOP_K, N_GROUPS, TOPK_GROUPS = 64, 8, 8, 4
GROUP_SIZE = N_EXPERTS // N_GROUPS
EXPERT_FF = 256
ROUTED_SCALE = 2.5

LANES = 128
SCAN_ROWS = 256
SCAN_LAT_ROWS = 1024
MOE_TILE = 256
ROW_UNIT = 16
EXPERT_ROWS = 1024
SLOT_CHUNK = 512
SUB_SLOTS = 256
SLOTS = -(-(MOE_TILE * TOP_K + N_EXPERTS * (ROW_UNIT - 1) + ROW_UNIT) // SLOT_CHUNK) * SLOT_CHUNK
N_UNITS = SLOTS // ROW_UNIT
PAD_UNITS = EXPERT_ROWS // ROW_UNIT
VMEM_LIMIT = 56 * 1024 * 1024


def _cparams(sem):
    return pltpu.CompilerParams(dimension_semantics=sem, vmem_limit_bytes=VMEM_LIMIT)


def _rms(x):
    return x * lax.rsqrt(jnp.mean(x * x, axis=-1, keepdims=True) + NORM_EPS)


def _silu(x):
    return x * jax.nn.sigmoid(x)


def _adaln_kernel(c_ref, w_ref, b_ref, o_ref):
    s = _silu(c_ref[...])
    w = w_ref[0]
    s_hi, w_hi = s.astype(BF16), w.astype(BF16)
    s_lo, w_lo = (s - s_hi.astype(F32)).astype(BF16), (w - w_hi.astype(F32)).astype(BF16)
    dot = lambda a, b: jnp.dot(a, b, preferred_element_type=F32)
    o_ref[0] = dot(s_hi, w_hi) + (dot(s_lo, w_hi) + dot(s_hi, w_lo)) + b_ref[0]


def adaln_rows(cond8, ada_w, ada_b):
    depth, d, n = ada_w.shape
    tn = 512
    return pl.pallas_call(
        _adaln_kernel,
        grid=(depth, n // tn),
        in_specs=[pl.BlockSpec((8, d), lambda l, j: (0, 0)),
                  pl.BlockSpec((1, d, tn), lambda l, j: (l, 0, j)),
                  pl.BlockSpec((1, 1, tn), lambda l, j: (l, 0, j))],
        out_specs=pl.BlockSpec((1, 8, tn), lambda l, j: (l, 0, j)),
        out_shape=jax.ShapeDtypeStruct((depth, 8, n), F32),
        compiler_params=_cparams(("parallel", "parallel")),
        name="adaln_rows",
    )(cond8, ada_w, ada_b.reshape(depth, 1, n))


def _modulated_norm(x, mod, nw, shift_row, scale_row):
    return _rms(x) * nw * (1.0 + mod[scale_row:scale_row + 1]) + mod[shift_row:shift_row + 1]


def _split_row_specs(tm, width, n_lat_tiles, col_blk=0, ctx_row0=0):
    ctx_blk0 = ctx_row0 // tm
    return [pl.BlockSpec((tm, width), lambda i: (jnp.minimum(i, n_lat_tiles - 1), col_blk)),
            pl.BlockSpec((tm, width),
                         lambda i: (ctx_blk0 + jnp.maximum(i - n_lat_tiles, 0), col_blk))]


def _pick_rows(lat_ref, ctx_ref, n_lat_tiles):
    return jnp.where(pl.program_id(0) < n_lat_tiles, lat_ref[...], ctx_ref[...])


def _inproj_even_kernel(xl_ref, xc_ref, mod_ref, nw_ref, w_ref, cos_ref, sin_ref, o_ref, *,
                        n_lat_tiles):
    x = _pick_rows(xl_ref, xc_ref, n_lat_tiles)
    h = _modulated_norm(x, mod_ref[0], nw_ref[...], 0, 1).astype(BF16)
    n_main = o_ref.shape[1]
    qk = jnp.dot(h, w_ref[:, 0:512], preferred_element_type=F32)
    qk_sw = jnp.dot(h, w_ref[:, n_main:n_main + 512], preferred_element_type=F32)
    is_lat = pl.program_id(0) < n_lat_tiles
    cos = jnp.where(is_lat, cos_ref[...], 1.0)
    sin = jnp.where(is_lat, sin_ref[...], 0.0)
    for c0 in range(0, 512, LANES):
        o_ref[:, c0:c0 + LANES] = (qk[:, c0:c0 + LANES] * cos
                                   + qk_sw[:, c0:c0 + LANES] * sin).astype(BF16)
    for c0 in range(512, n_main, 512):
        o_ref[:, c0:c0 + 512] = jnp.dot(h, w_ref[:, c0:c0 + 512],
                                        preferred_element_type=F32).astype(BF16)


def _inproj_odd_kernel(x_ref, mod_ref, nw_ref, w_ref, w2_ref, b2_ref, o_ref, laf_ref, lab_ref):
    h = _modulated_norm(x_ref[...], mod_ref[0], nw_ref[...], 0, 1).astype(BF16)
    n_main = o_ref.shape[1]
    kd = GLA_HEADS * GLA_DK
    q = jnp.dot(h, w_ref[:, 0:kd], preferred_element_type=F32)
    o_ref[:, 0:kd] = (q * (GLA_DK ** -0.5)).astype(BF16)
    for c0 in range(kd, n_main, 512):
        o_ref[:, c0:c0 + 512] = jnp.dot(h, w_ref[:, c0:c0 + 512],
                                        preferred_element_type=F32).astype(BF16)
    low = jnp.dot(h, w_ref[:, n_main:n_main + LANES], preferred_element_type=F32)

    low_hi = low.astype(BF16)
    low_lo = (low - low_hi.astype(F32)).astype(BF16)
    z = jnp.dot(jnp.concatenate([low_hi, low_lo, low_hi], axis=1), w2_ref[...],
                preferred_element_type=F32) + b2_ref[...]
    la = (jnp.minimum(z, 0.0) - jnp.log(1.0 + jnp.exp(-jnp.abs(z)))) * (1.0 / GLA_TAU)
    laf_ref[...] = la[:, 0:kd]
    lab_ref[...] = la[:, kd:2 * kd]


def _seg_of_tile(i, tm, t_lat, seq):
    return jnp.where(i < t_lat // tm, 1 + (i * tm) // seq, 0)


def inproj_even(x_rows, mods, nw, w_ext, cos_t, sin_t, *, t_lat, t_all, seq, tm=512):
    x_lat, x_ctx, ctx_row0 = x_rows
    d = x_lat.shape[1]
    n_ext = w_ext.shape[1]
    n_main = n_ext - 512
    n_lat_tiles, pos_tiles = t_lat // tm, seq // tm

    def pos_map(i):
        return (jnp.where(i < n_lat_tiles, i % pos_tiles, 0), 0)

    return pl.pallas_call(
        functools.partial(_inproj_even_kernel, n_lat_tiles=n_lat_tiles),
        grid=(t_all // tm,),
        in_specs=_split_row_specs(tm, d, n_lat_tiles, ctx_row0=ctx_row0) + [
                  pl.BlockSpec((1, 6, d), lambda i: (_seg_of_tile(i, tm, t_lat, seq), 0, 0)),
                  pl.BlockSpec((1, d), lambda i: (0, 0)),
                  pl.BlockSpec((d, n_ext), lambda i: (0, 0)),
                  pl.BlockSpec((tm, LANES), pos_map),
                  pl.BlockSpec((tm, LANES), pos_map)],
        out_specs=pl.BlockSpec((tm, n_main), lambda i: (i, 0)),
        out_shape=jax.ShapeDtypeStruct((t_all, n_main), BF16),
        compiler_params=_cparams(("parallel",)),
        name="inproj_even",
    )(x_lat, x_ctx, mods, nw, w_ext, cos_t, sin_t)


def inproj_odd(x_all, mods, nw, w_ext, w2, b2, *, t_lat, seq, tm=512):
    t_all, d = x_all.shape
    n_main = w_ext.shape[1] - LANES
    kd = GLA_HEADS * GLA_DK
    full = lambda shape: pl.BlockSpec(shape, lambda i: (0,) * len(shape))
    return pl.pallas_call(
        _inproj_odd_kernel,
        grid=(t_all // tm,),
        in_specs=[pl.BlockSpec((tm, d), lambda i: (i, 0)),
                  pl.BlockSpec((1, 6, d), lambda i: (_seg_of_tile(i, tm, t_lat, seq), 0, 0)),
                  full((1, d)), full(w_ext.shape), full(w2.shape), full(b2.shape)],
        out_specs=[pl.BlockSpec((tm, n_main), lambda i: (i, 0)),
                   pl.BlockSpec((tm, kd), lambda i: (i, 0)),
                   pl.BlockSpec((tm, kd), lambda i: (i, 0))],
        out_shape=[jax.ShapeDtypeStruct((t_all, n_main), BF16),
                   jax.ShapeDtypeStruct((t_all, kd), F32),
                   jax.ShapeDtypeStruct((t_all, kd), F32)],
        compiler_params=_cparams(("parallel",)),
        name="inproj_odd",
    )(x_all, mods, nw, w_ext, w2, b2)


def _scan_kernel(*refs, heads, dk, dv, chunk, gated, reverse, log_decay, sub_blocks):
    n_in = 3 + int(gated) + (2 if reverse else 0)
    ctx_in, lat_in = refs[:n_in], refs[n_in:2 * n_in]
    rest = list(refs[2 * n_in:])
    nw_ref = rest.pop(0) if reverse else None
    out_lat, out_ctx, state_ref = rest[0], rest[1], rest[2]
    dec_ref = None if gated else rest[3]

    hpg = LANES // dk
    groups = heads // hpg
    rows = SCAN_ROWS
    n_chunks = rows // chunk
    j = pl.program_id(1)

    row_i = lax.broadcasted_iota(jnp.int32, (rows, rows), 0)
    col_i = lax.broadcasted_iota(jnp.int32, (rows, rows), 1)
    same_chunk = (row_i // chunk) == (col_i // chunk)
    keep = jnp.logical_and(same_chunk, (col_i >= row_i) if reverse else (row_i >= col_i))

    @pl.when(j == 0)
    def _init():
        state_ref[...] = jnp.zeros_like(state_ref)
        if not gated:
            pos = lax.broadcasted_iota(jnp.int32, (rows, LANES), 0) % chunk
            steps = ((chunk - pos) if reverse else (pos + 1)).astype(F32)
            lane = lax.broadcasted_iota(jnp.int32, (rows, LANES), 1)
            for g in range(groups):
                lg = jnp.zeros((rows, LANES), F32)
                for a in range(hpg):
                    lg = jnp.where(lane // dk == a, log_decay[g * hpg + a], lg)
                logb = steps * lg
                b_end = float(chunk) * lg
                dec_ref[g, 0] = jnp.exp(logb)
                dec_ref[g, 1] = jnp.exp(-logb)
                dec_ref[g, 2] = jnp.exp(b_end - logb)
                dec_ref[g, 3] = jnp.exp(b_end)

    lane1 = lax.broadcasted_iota(jnp.int32, (1, LANES), 1)
    order = range(n_chunks - 1, -1, -1) if reverse else range(n_chunks)
    end_row = lambda c: c * chunk if reverse else (c + 1) * chunk - 1

    def scan_rows(in_refs, out_ref, r0):
        q_ref, k_ref, v_ref = in_refs[:3]
        blk = slice(r0, r0 + rows)
        if reverse:
            oprev_ref, gate_ref = in_refs[-2:]
        if gated:
            tri = jnp.where(keep, 1.0, 0.0).astype(BF16)
            la = in_refs[3][blk, :]
            la_hi = la.astype(BF16)
            la_lo = (la - la_hi.astype(F32)).astype(BF16)
            logb_all = (jnp.dot(tri, la_hi, preferred_element_type=F32)
                        + jnp.dot(tri, la_lo, preferred_element_type=F32))
        for g in range(groups):
            ksl = slice(g * LANES, (g + 1) * LANES)
            qg = q_ref[blk, ksl].astype(F32)
            kg = k_ref[blk, ksl].astype(F32)
            if gated:
                logb = logb_all[:, ksl]
                ends = [logb[end_row(c):end_row(c) + 1] for c in range(n_chunks)]
                b_end = jnp.concatenate([jnp.broadcast_to(e, (chunk, LANES)) for e in ends], axis=0)
                e_q, e_k, e_s = jnp.exp(logb), jnp.exp(-logb), jnp.exp(b_end - logb)
                e_e = [jnp.exp(e) for e in ends]
            else:
                e_q, e_k, e_s = dec_ref[g, 0], dec_ref[g, 1], dec_ref[g, 2]
                e_e = [dec_ref[g, 3][0:1]] * n_chunks
            qd = qg * e_q
            kd_ = (kg * e_k).astype(BF16)
            ks = (kg * e_s).astype(BF16)
            for a in range(hpg):
                h = g * hpg + a
                qa = (jnp.where(lane1 // dk == a, qd, 0.0) if hpg > 1 else qd).astype(BF16)
                vh = v_ref[blk, h * dv:(h + 1) * dv]
                s = lax.dot_general(qa, kd_, (((1,), (1,)), ((), ())), preferred_element_type=F32)
                o_intra = jnp.dot(jnp.where(keep, s, 0.0).astype(BF16), vh,
                                  preferred_element_type=F32)
                st = state_ref[h]
                for c in order:
                    rs = slice(c * chunk, (c + 1) * chunk)
                    o = o_intra[rs] + lax.dot_general(qa[rs], st.astype(BF16),
                                                      (((1,), (1,)), ((), ())),
                                                      preferred_element_type=F32)
                    st = e_e[c] * st + lax.dot_general(vh[rs], ks[rs], (((0,), (0,)), ((), ())),
                                                       preferred_element_type=F32)
                    osl = (slice(r0 + c * chunk, r0 + (c + 1) * chunk), slice(h * dv, (h + 1) * dv))
                    if reverse:
                        o = _rms(o + oprev_ref[osl]) * nw_ref[...]
                        out_ref[osl] = (o * _silu(gate_ref[osl].astype(F32))).astype(BF16)
                    else:
                        out_ref[osl] = o
                state_ref[h] = st

    @pl.when(j == 0)
    def _():
        scan_rows(ctx_in, out_ctx, 0)

    @pl.when(j > 0)
    def _():
        for sb in (range(sub_blocks - 1, -1, -1) if reverse else range(sub_blocks)):
            scan_rows(lat_in, out_lat, sb * rows)


def bidir_scan(proj, la_f, la_b, norm_w, *, batch, seq, ctx_len, heads, dk, dv, chunk,
               q_blk, k_blk, v_blk, g_blk, log_decay_f=None, log_decay_b=None):
    gated = la_f is not None
    hk, hv = heads * dk, heads * dv
    rb, lb = SCAN_ROWS, SCAN_LAT_ROWS
    lat_blocks = seq // lb
    ctx_base = (batch * seq) // rb
    assert ctx_len == rb

    lat_fwd = lambda b, j: b * lat_blocks + jnp.maximum(j - 1, 0)
    lat_bwd = lambda b, j: b * lat_blocks + lat_blocks - 1 - jnp.maximum(j - 1, 0)

    prev = None
    for reverse, lat_of, la, ld in ((False, lat_fwd, la_f, log_decay_f),
                                    (True, lat_bwd, la_b, log_decay_b)):
        ctx_spec = lambda w, cb, base=ctx_base: pl.BlockSpec(
            (rb, w), lambda b, j, cb=cb: (base + b, cb))
        lat_spec = lambda w, cb: pl.BlockSpec((lb, w), lambda b, j, cb=cb: (lat_of(b, j), cb))
        in_specs, args = [], []
        for spec, is_ctx in ((ctx_spec, True), (lat_spec, False)):
            in_specs += [spec(hk, q_blk), spec(hk, k_blk), spec(hv, v_blk)]
            args += [proj, proj, proj]
            if gated:
                in_specs.append(spec(hk, 0))
                args.append(la)
            if reverse:
                in_specs += [ctx_spec(hv, 0, base=0) if is_ctx else spec(hv, 0), spec(hv, g_blk)]
                args += [prev[1] if is_ctx else prev[0], proj]
        if reverse:
            in_specs.append(pl.BlockSpec((1, dv), lambda b, j: (0, 0)))
            args.append(norm_w)
        scratch = [pltpu.VMEM((heads, dv, LANES), F32)]
        if not gated:
            scratch.append(pltpu.VMEM((hk // LANES, 4, rb, LANES), F32))
        kern = functools.partial(_scan_kernel, heads=heads, dk=dk, dv=dv, chunk=chunk,
                                 gated=gated, reverse=reverse, log_decay=ld, sub_blocks=lb // rb)
        out_dtype = BF16 if reverse else F32
        prev = pl.pallas_call(
            kern,
            grid=(batch, lat_blocks + 1),
            in_specs=in_specs,
            out_specs=[lat_spec(hv, 0), ctx_spec(hv, 0, base=0)],
            out_shape=[jax.ShapeDtypeStruct((batch * seq, hv), out_dtype),
                       jax.ShapeDtypeStruct((batch * ctx_len, hv), out_dtype)],
            scratch_shapes=scratch,
            compiler_params=_cparams(("parallel", "arbitrary")),
            name="scan_bwd" if reverse else "scan_fwd",
        )(*args)
    return tuple(prev)


def _shortconv_kernel(z_ref, zp_ref, zn_ref, w_ref, b_ref, u_ref, x0_ref, *, tiles_per_seq):
    i = pl.program_id(0)
    tm = z_ref.shape[0]
    z = z_ref[...].astype(F32)
    first = (i % tiles_per_seq) == 0
    last = (i % tiles_per_seq) == tiles_per_seq - 1
    halo = zp_ref.shape[0]
    prev_row = jnp.where(first, 0.0, zp_ref[halo - 1:halo, :].astype(F32))
    next_row = jnp.where(last, 0.0, zn_ref[0:1, :].astype(F32))
    row = lax.broadcasted_iota(jnp.int32, z.shape, 0)
    z_prev = jnp.where(row == 0, prev_row, pltpu.roll(z, 1, 0))
    z_next = jnp.where(row == tm - 1, next_row, pltpu.roll(z, tm - 1, 0))
    y = w_ref[0:1] * z_prev + w_ref[1:2] * z + w_ref[2:3] * z_next + b_ref[...]
    c = HYENA_CH
    x0_ref[...] = y[:, 0:c].astype(BF16)
    u_ref[...] = (y[:, c:2 * c] * y[:, 2 * c:3 * c]).astype(BF16)


def hyena_shortconv(proj, short_w, short_b, *, row0, batch, seq_len, z_blk, tm=256):
    halo = 16
    tiles_per_seq = seq_len // tm
    n_tiles = batch * tiles_per_seq
    t0, h_per_tile = row0 // tm, tm // halo
    nz = 3 * HYENA_CH
    n_halo_blocks = proj.shape[0] // halo
    out_map = lambda i: (i % tiles_per_seq, i // tiles_per_seq)
    kern = functools.partial(_shortconv_kernel, tiles_per_seq=tiles_per_seq)
    return pl.pallas_call(
        kern,
        grid=(n_tiles,),
        in_specs=[pl.BlockSpec((tm, nz), lambda i: (t0 + i, z_blk)),
                  pl.BlockSpec((halo, nz),
                               lambda i: (jnp.maximum((t0 + i) * h_per_tile - 1, 0), z_blk)),
                  pl.BlockSpec((halo, nz),
                               lambda i: (jnp.minimum((t0 + i + 1) * h_per_tile,
                                                      n_halo_blocks - 1), z_blk)),
                  pl.BlockSpec((3, nz), lambda i: (0, 0)),
                  pl.BlockSpec((1, nz), lambda i: (0, 0))],
        out_specs=[pl.BlockSpec((tm, HYENA_CH), out_map), pl.BlockSpec((tm, HYENA_CH), out_map)],
        out_shape=[jax.ShapeDtypeStruct((seq_len, batch * HYENA_CH), BF16)] * 2,
        compiler_params=_cparams(("parallel",)),
        name="hyena_shortconv",
    )(proj, proj, proj, short_w, short_b)


def _filter_kernel(feat_ref, w1_ref, b1_ref, w2_ref, b2_ref, w3_ref, b3_ref, w4_ref, fq_ref,
                   o_ref, *, seq_len):
    i = pl.program_id(0)
    tl = feat_ref.shape[0]
    fq = fq_ref[...]
    dot = lambda a, b: jnp.dot(a, b, preferred_element_type=F32, precision=HIGHEST)
    h = jnp.sin(fq * (dot(feat_ref[...], w1_ref[...]) + b1_ref[...]))
    h = jnp.sin(fq * (dot(h, w2_ref[...]) + b2_ref[...]))
    h = jnp.sin(fq * (dot(h, w3_ref[...]) + b3_ref[...]))
    h = dot(h, w4_ref[...])
    c = HYENA_CH
    max_decay = math.log(HYENA_TARGET) / HYENA_FAST_DECAY
    min_decay = math.log(HYENA_TARGET) / HYENA_SLOW_DECAY
    ch = lax.broadcasted_iota(jnp.int32, (tl, c), 1).astype(F32)
    deltas = min_decay + ch * ((max_decay - min_decay) / (c - 1))
    row = lax.broadcasted_iota(jnp.int32, (tl, c), 0) + i * tl
    t = row.astype(F32) * (1.0 / (seq_len - 1))
    window = jnp.exp(-t * jnp.abs(deltas))
    h_f = h[:, 0:c] * window
    h_b = jnp.where(row == 0, 0.0, h[:, c:2 * c] * window)
    o_ref[:, 0:c] = (h_f + h_b).astype(BF16)
    o_ref[:, c:2 * c] = (h_f - h_b).astype(BF16)


def hyena_filter_pair(feats, w1p, b1, w2, b2, w3, b3, w4, fq):
    seq_len = feats.shape[0]
    tl = min(2048, seq_len)
    full = lambda a: pl.BlockSpec(a.shape, lambda i: (0,) * a.ndim)
    kern = functools.partial(_filter_kernel, seq_len=seq_len)
    return pl.pallas_call(
        kern,
        grid=(seq_len // tl,),
        in_specs=[pl.BlockSpec((tl, feats.shape[1]), lambda i: (i, 0)),
                  full(w1p), full(b1), full(w2), full(b2), full(w3), full(b3), full(w4), full(fq)],
        out_specs=pl.BlockSpec((tl, 2 * HYENA_CH), lambda i: (i, 0)),
        out_shape=jax.ShapeDtypeStruct((seq_len, 2 * HYENA_CH), BF16),
        compiler_params=_cparams(("parallel",)),
        name="hyena_filter",
    )(feats, w1p, b1, w2, b2, w3, b3, w4, fq)


def _filter_spectrum_kernel(c_ref, s_ref, f_ref, hc_ref, hs_ref, kre_ref, kim_ref, *, scale):
    ch = HYENA_CH
    f = f_ref[...]
    a = jnp.dot(c_ref[...], f, preferred_element_type=F32)
    b = jnp.dot(s_ref[...], f, preferred_element_type=F32)
    hc, hs = hc_ref[...], hs_ref[...]
    kre_ref[...] = (hc * a[:, 0:ch] + hs * b[:, 0:ch]) * scale
    kim_ref[...] = (hs * a[:, ch:2 * ch] - hc * b[:, ch:2 * ch]) * scale


def filter_spectrum(cs, ss, filt, half_cos, half_sin):
    n = cs.shape[0]
    c = HYENA_CH
    tf = min(512, n)
    kern = functools.partial(_filter_spectrum_kernel, scale=1.0 / n)
    return pl.pallas_call(
        kern,
        grid=(n // tf,),
        in_specs=[pl.BlockSpec((tf, n), lambda f: (f, 0)), pl.BlockSpec((tf, n), lambda f: (f, 0)),
                  pl.BlockSpec((n, 2 * c), lambda f: (0, 0)),
                  pl.BlockSpec((tf, 1), lambda f: (f, 0)), pl.BlockSpec((tf, 1), lambda f: (f, 0))],
        out_specs=[pl.BlockSpec((tf, c), lambda f: (f, 0))] * 2,
        out_shape=[jax.ShapeDtypeStruct((n, c), F32)] * 2,
        compiler_params=_cparams(("parallel",)),
        name="hyena_filter_spectrum",
    )(cs, ss, filt, half_cos, half_sin)


def _dft_fwd_kernel(c_ref, s_ref, u_ref, kre_ref, kim_ref, yre_ref, yim_ref):
    u = u_ref[...]
    a = jnp.dot(c_ref[...], u, preferred_element_type=F32)
    b = jnp.dot(s_ref[...], u, preferred_element_type=F32)
    k_re, k_im = kre_ref[...], kim_ref[...]
    yre_ref[...] = (a * k_re + b * k_im).astype(BF16)
    yim_ref[...] = (a * k_im - b * k_re).astype(BF16)


def dft_forward(cs, ss, u, k_re, k_im, *, batch):
    n = cs.shape[0]
    c = HYENA_CH
    tf = min(512, n)
    return pl.pallas_call(
        _dft_fwd_kernel,
        grid=(batch, n // tf),
        in_specs=[pl.BlockSpec((tf, n), lambda bi, f: (f, 0)),
                  pl.BlockSpec((tf, n), lambda bi, f: (f, 0)),
                  pl.BlockSpec((n, c), lambda bi, f: (0, bi)),
                  pl.BlockSpec((tf, c), lambda bi, f: (f, 0)),
                  pl.BlockSpec((tf, c), lambda bi, f: (f, 0))],
        out_specs=[pl.BlockSpec((tf, c), lambda bi, f: (f, bi))] * 2,
        out_shape=[jax.ShapeDtypeStruct((n, batch * c), BF16)] * 2,
        compiler_params=_cparams(("parallel", "parallel")),
        name="dft_forward",
    )(cs, ss, u, k_re, k_im)


def _dft_inv_kernel(c_ref, s_ref, yre_ref, yim_ref, u_ref, x0_ref, bias_ref, o_ref):
    y = (jnp.dot(c_ref[...], yre_ref[...], preferred_element_type=F32)
         - jnp.dot(s_ref[...], yim_ref[...], preferred_element_type=F32))
    u = u_ref[...].astype(F32)
    o_ref[...] = (x0_ref[...].astype(F32) * (y + u * bias_ref[...])).astype(BF16)


def dft_inverse(cs, ss, yre, yim, u, x0, bias, *, batch):
    n = cs.shape[0]
    c = HYENA_CH
    tt = min(512, n)
    col = lambda bi, t: (0, bi)
    tile = lambda bi, t: (t, bi)
    return pl.pallas_call(
        _dft_inv_kernel,
        grid=(batch, n // tt),
        in_specs=[pl.BlockSpec((tt, n), lambda bi, t: (t, 0)),
                  pl.BlockSpec((tt, n), lambda bi, t: (t, 0)),
                  pl.BlockSpec((n, c), col), pl.BlockSpec((n, c), col),
                  pl.BlockSpec((tt, c), tile), pl.BlockSpec((tt, c), tile),
                  pl.BlockSpec((1, c), lambda bi, t: (0, 0))],
        out_specs=pl.BlockSpec((tt, c), lambda bi, t: (bi * (n // tt) + t, 0)),
        out_shape=jax.ShapeDtypeStruct((batch * n, c), BF16),
        compiler_params=_cparams(("parallel", "parallel")),
        name="dft_inverse",
    )(cs, ss, yre, yim, u, x0, bias)


def _dft_table_kernel(cg_ref, sg_ref, cd_ref, sd_ref, c_ref, s_ref):
    cg, sg, cd, sd = cg_ref[...], sg_ref[...], cd_ref[0], sd_ref[0]
    c_ref[...] = (cg * cd - sg * sd).astype(BF16)
    s_ref[...] = (sg * cd + cg * sd).astype(BF16)


def _shifted_dft_tables(n):
    tf = min(256, n)
    theta = 2.0 * math.pi / (8 * n)
    odd_s = 2 * jnp.arange(n, dtype=jnp.int32)[None, :] + 1
    odd_i = 2 * jnp.arange(tf, dtype=jnp.int32)[:, None] + 1
    gamma = ((odd_i * odd_s) % (8 * n)).astype(F32) * theta
    f0 = jnp.arange(n // tf, dtype=jnp.int32)[:, None] * tf
    delta = ((2 * f0 * odd_s) % (8 * n)).astype(F32)[:, None, :] * theta
    whole = pl.BlockSpec((tf, n), lambda i: (0, 0))
    row = pl.BlockSpec((1, 1, n), lambda i: (i, 0, 0))
    cs, ss = pl.pallas_call(
        _dft_table_kernel,
        grid=(n // tf,),
        in_specs=[whole, whole, row, row],
        out_specs=[pl.BlockSpec((tf, n), lambda i: (i, 0))] * 2,
        out_shape=[jax.ShapeDtypeStruct((n, n), BF16)] * 2,
        compiler_params=_cparams(("parallel",)),
        name="dft_tables",
    )(jnp.cos(gamma), jnp.sin(gamma), jnp.cos(delta), jnp.sin(delta))
    half = (2 * jnp.arange(n, dtype=jnp.int32) + 1).astype(F32)[:, None] * (math.pi / (4 * n))
    return cs, ss, jnp.cos(half), jnp.sin(half)


def _filter_features(seq_len):
    t = jnp.linspace(0.0, 1.0, seq_len, dtype=F32)[:, None]
    bands = (HYENA_EMB - 1) // 2
    ang = 2.0 * math.pi * jnp.arange(seq_len, dtype=F32)[:, None] / seq_len
    fr = jnp.linspace(1e-4, bands - 1, bands, dtype=F32)[None, :]
    feats = jnp.concatenate([t, jnp.cos(fr * ang), -jnp.sin(fr * ang)], axis=-1)
    return jnp.pad(feats, ((0, 0), (0, LANES - HYENA_EMB)))


def hyena_long_conv(proj, hp, *, row0, batch, seq_len, z_blk):
    u, x0 = hyena_shortconv(proj, hp["short_w"], hp["short_b"], row0=row0, batch=batch,
                            seq_len=seq_len, z_blk=z_blk, tm=min(512, seq_len))
    filt = hyena_filter_pair(_filter_features(seq_len), hp["w1"], hp["b1"], hp["w2"], hp["b2"],
                             hp["w3"], hp["b3"], hp["w4"], hp["freq"])
    cs, ss, half_cos, half_sin = _shifted_dft_tables(seq_len)
    k_re, k_im = filter_spectrum(cs, ss, filt, half_cos, half_sin)
    yre, yim = dft_forward(cs, ss, u, k_re, k_im, batch=batch)
    return dft_inverse(cs, ss, yre, yim, u, x0, hp["bias"], batch=batch)


def _outproj_kernel(mal_ref, mac_ref, mbl_ref, mbc_ref, xl_ref, xc_ref, mod_ref, nw_ref, w_ref,
                    rw_ref, xo_ref, h_ref, lt_ref, *, n_lat_tiles):
    half = mal_ref.shape[1]
    pick = functools.partial(_pick_rows, n_lat_tiles=n_lat_tiles)
    y = (jnp.dot(pick(mal_ref, mac_ref), w_ref[0:half], preferred_element_type=F32)
         + jnp.dot(pick(mbl_ref, mbc_ref), w_ref[half:2 * half], preferred_element_type=F32))
    mod = mod_ref[0]
    x = pick(xl_ref, xc_ref) + mod[2:3] * y
    xo_ref[...] = x
    h = _modulated_norm(x, mod, nw_ref[...], 3, 4)
    h_ref[...] = h.astype(BF16)
    h_hi = h.astype(BF16)
    h_lo = (h - h_hi.astype(F32)).astype(BF16)
    nt = lambda a, b: lax.dot_general(a, b, (((1,), (1,)), ((), ())), preferred_element_type=F32)
    lt_ref[...] = nt(rw_ref[0], h_hi) + (nt(rw_ref[0], h_lo) + nt(rw_ref[1], h_hi))


def outproj(mix_a, a_blk, mix_b, b_blk, x_rows, mods, nw, w_out, router_wt, *,
            n_rows, t_lat, seq, tm=512):
    x_lat, x_ctx, ctx_row0 = x_rows
    d = x_lat.shape[1]
    half = d // 2
    n_lat_tiles = t_lat // tm
    return pl.pallas_call(
        functools.partial(_outproj_kernel, n_lat_tiles=n_lat_tiles),
        grid=(n_rows // tm,),
        in_specs=_split_row_specs(tm, half, n_lat_tiles, a_blk)
                 + _split_row_specs(tm, half, n_lat_tiles, b_blk)
                 + _split_row_specs(tm, d, n_lat_tiles, ctx_row0=ctx_row0) + [
                  pl.BlockSpec((1, 6, d), lambda i: (_seg_of_tile(i, tm, t_lat, seq), 0, 0)),
                  pl.BlockSpec((1, d), lambda i: (0, 0)),
                  pl.BlockSpec((d, d), lambda i: (0, 0)),
                  pl.BlockSpec((2, N_EXPERTS, d), lambda i: (0, 0, 0))],
        out_specs=[pl.BlockSpec((tm, d), lambda i: (i, 0)),
                   pl.BlockSpec((tm, d), lambda i: (i, 0)),
                   pl.BlockSpec((N_EXPERTS, tm), lambda i: (0, i))],
        out_shape=[jax.ShapeDtypeStruct((n_rows, d), F32),
                   jax.ShapeDtypeStruct((n_rows, d), BF16),
                   jax.ShapeDtypeStruct((N_EXPERTS, n_rows), F32)],
        compiler_params=_cparams(("parallel",)),
        name="outproj",
    )(*mix_a, *mix_b, x_lat, x_ctx, mods, nw, w_out, router_wt)


def _first_max(x, idx, sentinel):
    m = jnp.max(x, axis=0, keepdims=True)
    first = jnp.min(jnp.where(x == m, idx, sentinel), axis=0, keepdims=True)
    return m, idx == first


def _router_kernel(lt_ref, bias_ref, before_ref, below_ref, lists_ref, pc_ref):
    tn = lt_ref.shape[1]
    scores = jax.nn.sigmoid(lt_ref[...])
    sel = scores + bias_ref[...]
    neg = -jnp.inf
    in_grp = lax.broadcasted_iota(jnp.int32, (GROUP_SIZE, tn), 0)
    gscore = []
    for g in range(N_GROUPS):
        x = sel[g * GROUP_SIZE:(g + 1) * GROUP_SIZE]
        m1, hit = _first_max(x, in_grp, GROUP_SIZE)
        gscore.append(m1 + jnp.max(jnp.where(hit, neg, x), axis=0, keepdims=True))
    rows = []
    for g in range(N_GROUPS):
        beaten = jnp.zeros((1, tn), jnp.int32)
        for o in range(N_GROUPS):
            if o != g:
                wins = (gscore[o] >= gscore[g]) if o < g else (gscore[o] > gscore[g])
                beaten = beaten + wins.astype(jnp.int32)
        keep = jnp.broadcast_to(beaten < TOPK_GROUPS, (GROUP_SIZE, tn))
        rows.append(jnp.where(keep, sel[g * GROUP_SIZE:(g + 1) * GROUP_SIZE], neg))
    cand = jnp.concatenate(rows, axis=0)
    eidx = lax.broadcasted_iota(jnp.int32, cand.shape, 0)
    chosen = jnp.zeros(cand.shape, jnp.bool_)
    hits = []
    for _ in range(TOP_K):
        _, hit = _first_max(cand, eidx, N_EXPERTS)
        hits.append(hit)
        chosen = jnp.logical_or(chosen, hit)
        cand = jnp.where(hit, neg, cand)
    w = jnp.where(chosen, scores, 0.0)
    gates_t = w / jnp.sum(w, axis=0, keepdims=True) * ROUTED_SCALE
    chosen_f = jnp.where(chosen, 1.0, 0.0)
    rank = jnp.dot(chosen_f.astype(BF16), before_ref[...], preferred_element_type=F32)
    count = jnp.sum(chosen_f, axis=1, keepdims=True)
    pc = jnp.floor((count + (ROW_UNIT - 1)) * (1.0 / ROW_UNIT)) * ROW_UNIT
    pc_lanes = jnp.broadcast_to(pc, (N_EXPERTS, LANES))
    start = jnp.dot(below_ref[...], pc_lanes, preferred_element_type=F32,
                    precision=HIGHEST)[:, 0:1]
    slot = start + rank
    pick = lambda hit, v: jnp.sum(jnp.where(hit, v, 0.0), axis=0, keepdims=True)
    lists = jnp.concatenate([pick(h, slot) for h in hits] + [pick(h, gates_t) for h in hits], axis=0)
    lists_ref[...] = lists
    pc_ref[0] = pc_lanes


def router(logits, router_bias):
    e, t = logits.shape
    tn = MOE_TILE
    n_tiles = t // tn
    tok_i = jnp.arange(tn, dtype=jnp.int32)
    before = (tok_i[:, None] < tok_i[None, :]).astype(BF16)
    exp_i = jnp.arange(e, dtype=jnp.int32)
    below = (exp_i[None, :] < exp_i[:, None]).astype(F32)
    const = lambda shape: pl.BlockSpec(shape, lambda i: (0,) * len(shape))
    return pl.pallas_call(
        _router_kernel,
        grid=(n_tiles,),
        in_specs=[pl.BlockSpec((e, tn), lambda i: (0, i)),
                  const((e, 1)), const((tn, tn)), const((e, e))],
        out_specs=[pl.BlockSpec((2 * TOP_K, tn), lambda i: (0, i)),
                   pl.BlockSpec((1, e, LANES), lambda i: (i, 0, 0))],
        out_shape=[jax.ShapeDtypeStruct((2 * TOP_K, t), F32),
                   jax.ShapeDtypeStruct((n_tiles, e, LANES), F32)],
        compiler_params=_cparams(("parallel",)),
        name="router",
    )(logits, router_bias.reshape(e, 1), before, below)


def moe_plan(pc, n_tiles):
    e = pc.shape[1]
    assert n_tiles >= e
    cap_rows = _sorted_capacity(n_tiles)
    run_end = jnp.cumsum(pc, axis=1)
    run_start = run_end - pc
    total = jnp.sum(pc, axis=0)
    total_al = ((total + EXPERT_ROWS - 1) // EXPERT_ROWS) * EXPERT_ROWS
    range_end = jnp.cumsum(total_al)
    range_start = range_end - total_al
    run_row = range_start[None, :] + jnp.cumsum(pc, axis=0) - pc
    unit_row = jnp.arange(N_UNITS, dtype=jnp.int32) * ROW_UNIT
    owner = jnp.sum((run_end[:, None, :] <= unit_row[None, :, None]).astype(jnp.int32), axis=2)
    owner_hot = (owner[:, :, None] == jnp.arange(e, dtype=jnp.int32)).astype(jnp.int32)
    dst = jnp.sum(owner_hot * (run_row - run_start)[:, None, :], axis=2) + unit_row[None, :]
    used = owner < e
    dump = cap_rows + (jnp.arange(n_tiles, dtype=jnp.int32) % 2)[:, None] * DUMP_ROWS
    unit_dst = jnp.where(used, dst, dump + unit_row[None, :]).astype(jnp.int32)
    unit_src = jnp.where(used, dst, 0).astype(jnp.int32)
    pad_off = jnp.arange(PAD_UNITS, dtype=jnp.int32)[None, :] * ROW_UNIT
    pad_row = (range_start + total)[:, None] + pad_off
    pad_row = jnp.concatenate([jnp.where(pad_row < range_end[:, None], pad_row, -1),
                               jnp.full((n_tiles - e, PAD_UNITS), -1, jnp.int32)], axis=0)
    pad_dst = jnp.where(pad_row >= 0, pad_row, dump + SLOTS + pad_off).astype(jnp.int32)
    tile_end = range_end // EXPERT_ROWS
    tile_expert = jnp.sum((tile_end[None, :] <= jnp.arange(cap_rows // EXPERT_ROWS,
                                                           dtype=jnp.int32)[:, None])
                          .astype(jnp.int32), axis=1)
    tile_expert = jnp.minimum(tile_expert, e - 1).astype(jnp.int32)
    used_rows = run_end[:, -1].astype(jnp.int32)
    tile_end = tile_end.astype(jnp.int32)
    return unit_dst, unit_src, pad_dst, used_rows, tile_expert, tile_end[-1:], tile_end


def _sorted_capacity(n_tiles):
    worst_rows = n_tiles * (MOE_TILE * TOP_K + N_EXPERTS * (ROW_UNIT - 1)) + N_EXPERTS * (EXPERT_ROWS - 1)
    return -(-worst_rows // EXPERT_ROWS) * EXPERT_ROWS


DUMP_ROWS = SLOTS + PAD_UNITS * ROW_UNIT


UNITS_PER_CHUNK = SLOT_CHUNK // ROW_UNIT
MAIN_SLOTS = SLOTS - SLOT_CHUNK
assert MAIN_SLOTS >= MOE_TILE * TOP_K


def _dispatch_kernel(unit_ref, pad_ref, used_ref, h_ref, lists_ref, xs_ref, loc_ref, zero_ref, sems):
    i = pl.program_id(0)
    last = pl.num_programs(0) - 1
    slot = i % 2
    tn = h_ref.shape[0]

    @pl.when(i == 0)
    def _():
        zero_ref[...] = jnp.zeros_like(zero_ref)

    rel_rows = lax.broadcasted_iota(jnp.int32, (SUB_SLOTS, tn), 0).astype(F32).astype(BF16)

    def chunk(c0):
        parts = []
        for b0 in range(c0, c0 + SLOT_CHUNK, SUB_SLOTS):
            p = jnp.zeros((SUB_SLOTS, tn), BF16)
            for k in range(TOP_K):
                hit = rel_rows == (lists_ref[k:k + 1, :] - float(b0)).astype(BF16)
                p = jnp.where(hit, jnp.ones_like(p), p)
            parts.append(p)
        loc_ref[slot, c0:c0 + SLOT_CHUNK, :] = jnp.dot(
            jnp.concatenate(parts, axis=0), h_ref[...],
            preferred_element_type=F32).astype(BF16)
        for u in range(c0 // ROW_UNIT, c0 // ROW_UNIT + UNITS_PER_CHUNK):
            row = pl.multiple_of(unit_ref[i, u], ROW_UNIT)
            pltpu.make_async_copy(loc_ref.at[slot, pl.ds(u * ROW_UNIT, ROW_UNIT)],
                                  xs_ref.at[pl.ds(row, ROW_UNIT)], sems.at[slot]).start()

    for c0 in range(0, MAIN_SLOTS, SLOT_CHUNK):
        chunk(c0)

    @pl.when(used_ref[i] > MAIN_SLOTS)
    def _():
        chunk(MAIN_SLOTS)

    for p in range(PAD_UNITS):
        row = pl.multiple_of(pad_ref[i, p], ROW_UNIT)
        pltpu.make_async_copy(zero_ref, xs_ref.at[pl.ds(row, ROW_UNIT)], sems.at[slot]).start()

    def wait_all(s, tile):
        pltpu.make_async_copy(loc_ref.at[s, pl.ds(0, MAIN_SLOTS)],
                              xs_ref.at[pl.ds(0, MAIN_SLOTS)], sems.at[s]).wait()

        @pl.when(used_ref[tile] > MAIN_SLOTS)
        def _():
            pltpu.make_async_copy(loc_ref.at[s, pl.ds(MAIN_SLOTS, SLOT_CHUNK)],
                                  xs_ref.at[pl.ds(0, SLOT_CHUNK)], sems.at[s]).wait()
        pltpu.make_async_copy(loc_ref.at[s, pl.ds(0, PAD_UNITS * ROW_UNIT)],
                              xs_ref.at[pl.ds(0, PAD_UNITS * ROW_UNIT)], sems.at[s]).wait()

    @pl.when(i > 0)
    def _():
        wait_all(1 - slot, i - 1)

    @pl.when(i == last)
    def _():
        wait_all(slot, i)


def moe_dispatch(h, lists, unit_dst, pad_dst, used_rows, *, n_tiles):
    d = h.shape[1]
    cap_rows = _sorted_capacity(n_tiles) + 2 * DUMP_ROWS
    return pl.pallas_call(
        _dispatch_kernel,
        grid_spec=pltpu.PrefetchScalarGridSpec(
            num_scalar_prefetch=3,
            grid=(n_tiles,),
            in_specs=[pl.BlockSpec((MOE_TILE, d), lambda i, *_: (i, 0)),
                      pl.BlockSpec((2 * TOP_K, MOE_TILE), lambda i, *_: (0, i))],
            out_specs=pl.BlockSpec(memory_space=pl.ANY),
            scratch_shapes=[pltpu.VMEM((2, SLOTS, d), BF16), pltpu.VMEM((ROW_UNIT, d), BF16),
                            pltpu.SemaphoreType.DMA((2,))]),
        out_shape=jax.ShapeDtypeStruct((cap_rows, d), BF16),
        compiler_params=_cparams(("arbitrary",)),
        name="moe_dispatch",
    )(unit_dst, pad_dst, used_rows, h, lists)


X_RING = 3


def _expert_kernel(te_ref, nt_ref, end_ref, xs_ref, wg_hbm, wu_hbm, wd_hbm, y_ref, x_buf,
                   wg_st, wu_st, wd_st, wg_bf, wu_bf, wd_bf, turn_ref, sems, wsems, *, layer):
    t = pl.program_id(0)
    n_live = nt_ref[0]
    live = t < n_live
    tc = jnp.minimum(t, jnp.maximum(n_live - 1, 0))
    new_expert = jnp.logical_or(t == 0, te_ref[tc] != te_ref[jnp.maximum(tc - 1, 0)])
    rows = x_buf.shape[1]

    def w_copies(e, s):
        return [pltpu.make_async_copy(hbm.at[layer, e], st.at[s], wsems.at[s])
                for hbm, st in ((wg_hbm, wg_st), (wu_hbm, wu_st), (wd_hbm, wd_st))]

    @pl.when(jnp.logical_and(t == 0, live))
    def _():
        turn_ref[0] = 0
        for cp in w_copies(te_ref[0], 0):
            cp.start()

    def x_copy(tile):
        s = tile % X_RING
        return pltpu.make_async_copy(xs_ref.at[pl.ds(pl.multiple_of(tile * rows, rows), rows)],
                                     x_buf.at[s], sems.at[s])

    for k in range(X_RING - 1):
        @pl.when(jnp.logical_and(t == 0, k < n_live))
        def _():
            x_copy(k).start()

    @pl.when(t + (X_RING - 1) < n_live)
    def _():
        x_copy(t + (X_RING - 1)).start()

    @pl.when(jnp.logical_and(live, new_expert))
    def _():
        e = te_ref[tc]
        s = turn_ref[0] % 2
        for cp in w_copies(e, s):
            cp.wait()
        wg_bf[...] = wg_st[s].astype(BF16)
        wu_bf[...] = wu_st[s].astype(BF16)
        wd_bf[...] = wd_st[s].astype(BF16)
        nxt = end_ref[e]

        @pl.when(nxt < n_live)
        def _():
            for cp in w_copies(te_ref[jnp.minimum(nxt, te_ref.shape[0] - 1)], 1 - s):
                cp.start()
        turn_ref[0] = turn_ref[0] + 1

    @pl.when(live)
    def _():
        x_copy(t).wait()
        x = x_buf[t % X_RING]
        act = (_silu(jnp.dot(x, wg_bf[...], preferred_element_type=F32))
               * jnp.dot(x, wu_bf[...], preferred_element_type=F32))
        y_ref[...] = jnp.dot(act.astype(BF16), wd_bf[...], preferred_element_type=F32).astype(BF16)


def moe_experts(xs, w_gate, w_up, w_down, layer, tile_expert, n_row_tiles, range_end_tile):
    d = xs.shape[1]
    cap_rows = xs.shape[0] - 2 * DUMP_ROWS
    ff = w_gate.shape[3]
    row_map = lambda t, te, nt, end: (jnp.minimum(t, jnp.maximum(nt[0] - 1, 0)), 0)
    hbm = pl.BlockSpec(memory_space=pl.ANY)
    return pl.pallas_call(
        functools.partial(_expert_kernel, layer=layer),
        grid_spec=pltpu.PrefetchScalarGridSpec(
            num_scalar_prefetch=3,
            grid=(cap_rows // EXPERT_ROWS,),
            in_specs=[hbm, hbm, hbm, hbm],
            out_specs=pl.BlockSpec((EXPERT_ROWS, d), row_map),
            scratch_shapes=[pltpu.VMEM((X_RING, EXPERT_ROWS, d), BF16),
                            pltpu.VMEM((2, d, ff), F32), pltpu.VMEM((2, d, ff), F32),
                            pltpu.VMEM((2, ff, d), F32),
                            pltpu.VMEM((d, ff), BF16), pltpu.VMEM((d, ff), BF16),
                            pltpu.VMEM((ff, d), BF16), pltpu.SMEM((1,), jnp.int32),
                            pltpu.SemaphoreType.DMA((X_RING,)), pltpu.SemaphoreType.DMA((2,))]),
        out_shape=jax.ShapeDtypeStruct((cap_rows, d), BF16),
        compiler_params=_cparams(("arbitrary",)),
        name="moe_experts",
    )(tile_expert, n_row_tiles, range_end_tile, xs, w_gate, w_up, w_down)


def _combine_kernel(unit_ref, used_ref, ys_ref, lists_ref, h_ref, x_ref, mod_ref, fnw_ref, sg_ref,
                    su_ref, sd_ref, o_ref, loc_ref, acc_ref, sems, *, final_norm):
    i = pl.program_id(0)
    last = pl.num_programs(0) - 1
    slot = i % 2
    tn = h_ref.shape[0]
    def fetch(tile, s):
        def units(u0, u1):
            for u in range(u0, u1):
                row = pl.multiple_of(unit_ref[tile, u], ROW_UNIT)
                pltpu.make_async_copy(ys_ref.at[pl.ds(row, ROW_UNIT)],
                                      loc_ref.at[s, pl.ds(u * ROW_UNIT, ROW_UNIT)], sems.at[s]).start()
        units(0, MAIN_SLOTS // ROW_UNIT)

        @pl.when(used_ref[tile] > MAIN_SLOTS)
        def _():
            units(MAIN_SLOTS // ROW_UNIT, N_UNITS)

    @pl.when(i == 0)
    def _():
        fetch(0, 0)

    @pl.when(i < last)
    def _():
        fetch(i + 1, 1 - slot)

    h = h_ref[...]
    act = (_silu(jnp.dot(h, sg_ref[...], preferred_element_type=F32))
           * jnp.dot(h, su_ref[...], preferred_element_type=F32))
    acc = jnp.dot(act.astype(BF16), sd_ref[...], preferred_element_type=F32)
    tail = used_ref[i] > MAIN_SLOTS
    pltpu.make_async_copy(ys_ref.at[pl.ds(0, MAIN_SLOTS)], loc_ref.at[slot, pl.ds(0, MAIN_SLOTS)],
                          sems.at[slot]).wait()

    @pl.when(tail)
    def _():
        pltpu.make_async_copy(ys_ref.at[pl.ds(0, SLOT_CHUNK)],
                              loc_ref.at[slot, pl.ds(MAIN_SLOTS, SLOT_CHUNK)], sems.at[slot]).wait()

    rel_rows = lax.broadcasted_iota(jnp.int32, (SUB_SLOTS, tn), 0).astype(F32).astype(BF16)
    gate_w = [lists_ref[TOP_K + k:TOP_K + k + 1, :].astype(BF16) for k in range(TOP_K)]

    def chunk(c0):
        parts = []
        for b0 in range(c0, c0 + SLOT_CHUNK, SUB_SLOTS):
            q = jnp.zeros((SUB_SLOTS, tn), BF16)
            for k in range(TOP_K):
                hit = rel_rows == (lists_ref[k:k + 1, :] - float(b0)).astype(BF16)
                q = jnp.where(hit, gate_w[k], q)
            parts.append(q)
        return lax.dot_general(jnp.concatenate(parts, axis=0), loc_ref[slot, c0:c0 + SLOT_CHUNK, :],
                               (((0,), (0,)), ((), ())), preferred_element_type=F32)

    for c0 in range(0, MAIN_SLOTS, SLOT_CHUNK):
        acc = acc + chunk(c0)
    acc_ref[...] = acc

    @pl.when(tail)
    def _():
        acc_ref[...] += chunk(MAIN_SLOTS)

    x = x_ref[...] + mod_ref[0][5:6] * acc_ref[...]
    if final_norm:
        x = _rms(x) * fnw_ref[...]
    o_ref[...] = x


def moe_combine(ys, lists, h, x_all, mods, fnw, sh_gate, sh_up, sh_down, unit_dst, used_rows, *,
                n_tiles, t_lat, seq, final_norm):
    d = h.shape[1]
    tn = MOE_TILE
    const = lambda a: pl.BlockSpec(a.shape, lambda i, *_: (0,) * a.ndim)
    kern = functools.partial(_combine_kernel, final_norm=final_norm)
    return pl.pallas_call(
        kern,
        grid_spec=pltpu.PrefetchScalarGridSpec(
            num_scalar_prefetch=2,
            grid=(n_tiles,),
            in_specs=[pl.BlockSpec(memory_space=pl.ANY),
                      pl.BlockSpec((2 * TOP_K, tn), lambda i, *_: (0, i)),
                      pl.BlockSpec((tn, d), lambda i, *_: (i, 0)),
                      pl.BlockSpec((tn, d), lambda i, *_: (i, 0)),
                      pl.BlockSpec((1, 6, d),
                                   lambda i, *_: (_seg_of_tile(i, tn, t_lat, seq), 0, 0)),
                      const(fnw), const(sh_gate), const(sh_up), const(sh_down)],
            out_specs=pl.BlockSpec((tn, d), lambda i, *_: (i, 0)),
            scratch_shapes=[pltpu.VMEM((2, SLOTS, d), BF16), pltpu.VMEM((tn, d), F32),
                            pltpu.SemaphoreType.DMA((2,))]),
        out_shape=jax.ShapeDtypeStruct((n_tiles * tn, d), F32),
        compiler_params=_cparams(("arbitrary",)),
        name="moe_combine",
    )(unit_dst, used_rows, ys, lists, h, x_all, mods, fnw, sh_gate, sh_up, sh_down)


def moe_block(h2, logits, router_bias, x_mid, mods, fnw, w_gate, w_up, w_down,
              sh_gate, sh_up, sh_down, *, layer, t_lat, seq, final_norm):
    n_tiles = h2.shape[0] // MOE_TILE
    lists, pc = router(logits, router_bias)
    unit_dst, unit_src, pad_dst, used_rows, tile_expert, n_row_tiles, range_end_tile = moe_plan(
        pc[:, :, 0].astype(jnp.int32), n_tiles)
    xs = moe_dispatch(h2, lists, unit_dst, pad_dst, used_rows, n_tiles=n_tiles)
    ys = moe_experts(xs, w_gate, w_up, w_down, layer, tile_expert, n_row_tiles, range_end_tile)
    return moe_combine(ys, lists, h2, x_mid, mods, fnw, sh_gate.astype(BF16), sh_up.astype(BF16),
                       sh_down.astype(BF16), unit_src, used_rows, n_tiles=n_tiles, t_lat=t_lat,
                       seq=seq, final_norm=final_norm)


def _rope_tables(seq):
    half = RET_DK // 2
    pos = jnp.arange(seq, dtype=jnp.int32)
    row = (pos // GRID_W).astype(F32)
    col = (pos % GRID_W).astype(F32)
    inv = ROPE_BASE ** (-jnp.arange(0, half, 2, dtype=F32) / half)
    a_row, a_col = row[:, None] * inv[None, :], col[:, None] * inv[None, :]
    cos_h = jnp.concatenate([jnp.cos(a_row)] * 2 + [jnp.cos(a_col)] * 2, axis=-1)
    sin_h = jnp.concatenate([-jnp.sin(a_row), jnp.sin(a_row), -jnp.sin(a_col), jnp.sin(a_col)], -1)
    reps = LANES // RET_DK
    return jnp.tile(cos_h, (1, reps)), jnp.tile(sin_h, (1, reps))


def _rope_partner_columns():
    quarter = RET_DK // 4
    idx = jnp.arange(2 * RET_HEADS * RET_DK, dtype=jnp.int32)
    within = idx % (2 * quarter)
    return jnp.where(within < quarter, idx + quarter, idx - quarter)


def kernel(x, c, ctx, c_ctx, ada_w, ada_b, norm1_w, norm2_w, ev_w_in, ev_short_w, ev_short_b, ev_filt_w1, ev_filt_b1, ev_filt_w2, ev_filt_b2, ev_filt_w3, ev_filt_b3, ev_filt_w4, ev_filt_freq, ev_hyena_bias, ev_w_out, od_w_in, od_gate_w1_f, od_gate_w2_f, od_gate_b_f, od_gate_w1_b, od_gate_w2_b, od_gate_b_b, od_norm_w, od_w_out, router_w, router_bias, exp_w_gate, exp_w_up, exp_w_down, sh_w_gate, sh_w_up, sh_w_down, final_norm_w):
    batch, seq, d = x.shape
    ctx_len = ctx.shape[1]
    depth = ada_w.shape[0]
    t_lat, t_ctx = batch * seq, batch * ctx_len
    t_all = t_lat + t_ctx
    tm = 1024

    x_all = None
    x_rows = (x.reshape(t_lat, d), ctx.reshape(t_ctx, d), 0)
    cond8 = jnp.concatenate([c_ctx[None, :], c, jnp.zeros((8 - 1 - batch, d), F32)], axis=0)
    mods_all = adaln_rows(cond8, ada_w, ada_b).reshape(depth, 8, 6, d)

    for i in range(depth):
        last = i == depth - 1
        j = i // 2
        mods = mods_all[i]
        nw1, nw2 = norm1_w[i][None, :], norm2_w[i][None, :]
        if i % 2 == 0:
            qk = 2 * RET_HEADS * RET_DK
            k_scale = jnp.concatenate([jnp.ones((qk // 2,), F32),
                                       jnp.full((qk // 2,), RET_DK ** -0.5, F32)])
            w_in = ev_w_in[j]
            w_qk = w_in[:, :qk] * k_scale
            w_ext = jnp.concatenate([w_qk, w_in[:, qk:], w_qk[:, _rope_partner_columns()]],
                                    axis=1).astype(BF16)
            cos_t, sin_t = _rope_tables(seq)
            proj = inproj_even(x_rows, mods, nw1, w_ext, cos_t, sin_t, t_lat=t_lat, t_all=t_all,
                               seq=seq, tm=tm)
            log_g = [math.log1p(-2.0 ** (-5.0 - h)) for h in range(RET_HEADS)]
            mix_a = bidir_scan(proj, None, None, jnp.ones((1, RET_DV), F32),
                               batch=batch, seq=seq, ctx_len=ctx_len, heads=RET_HEADS, dk=RET_DK,
                               dv=RET_DV, chunk=RET_CHUNK, q_blk=0, k_blk=1, v_blk=1, g_blk=2,
                               log_decay_f=log_g, log_decay_b=log_g[::-1])
            hp = dict(short_w=ev_short_w[j], short_b=ev_short_b[j][None, :],
                      w1=jnp.pad(ev_filt_w1[j], ((0, LANES - HYENA_EMB), (0, 0))),
                      b1=ev_filt_b1[j][None, :], w2=ev_filt_w2[j], b2=ev_filt_b2[j][None, :],
                      w3=ev_filt_w3[j], b3=ev_filt_b3[j][None, :], w4=ev_filt_w4[j],
                      freq=ev_filt_freq[j][None, :], bias=ev_hyena_bias[j][None, :])
            hy_lat = hyena_long_conv(proj, hp, row0=0, batch=batch, seq_len=seq, z_blk=1)
            hy_ctx = hyena_long_conv(proj, hp, row0=t_lat, batch=batch, seq_len=ctx_len, z_blk=1)
            mix_b = (hy_lat, hy_ctx)
            a_blk, b_blk = 0, 0
            w_out = ev_w_out[j].astype(BF16)
        else:
            kd = GLA_HEADS * GLA_DK
            pad_cols = LANES - 2 * GLA_RANK
            w_ext = jnp.concatenate([od_w_in[j], od_gate_w1_f[j], od_gate_w1_b[j],
                                     jnp.zeros((d, pad_cols), F32)], axis=1).astype(BF16)
            w2 = jnp.concatenate(
                [jnp.pad(od_gate_w2_f[j], ((0, LANES - GLA_RANK), (0, 0))),
                 jnp.pad(od_gate_w2_b[j], ((GLA_RANK, LANES - 2 * GLA_RANK), (0, 0)))], axis=1)
            w2_hi = w2.astype(BF16)
            w2_split = jnp.concatenate([w2_hi, w2_hi, (w2 - w2_hi.astype(F32)).astype(BF16)], axis=0)
            b2 = jnp.concatenate([od_gate_b_f[j], od_gate_b_b[j]])[None, :]
            if x_all is None:
                x_all = jnp.concatenate(x_rows[:2], axis=0)
            proj, la_f, la_b = inproj_odd(x_all, mods, nw1, w_ext, w2_split, b2,
                                          t_lat=t_lat, seq=seq, tm=tm)
            mix_a = bidir_scan(proj, la_f, la_b, od_norm_w[j][None, :],
                               batch=batch, seq=seq, ctx_len=ctx_len, heads=GLA_HEADS, dk=GLA_DK,
                               dv=GLA_DV, chunk=GLA_CHUNK, q_blk=0, k_blk=1, v_blk=1, g_blk=2)
            mix_b = mix_a
            a_blk, b_blk = 0, 1
            w_out = od_w_out[j].astype(BF16)

        n_rows = t_lat if last else t_all
        rw_t = router_w[i].T
        rw_hi = rw_t.astype(BF16)
        rw_split = jnp.stack([rw_hi, (rw_t - rw_hi.astype(F32)).astype(BF16)])
        x_mid, h2, logits = outproj(mix_a, a_blk, mix_b, b_blk, x_rows, mods, nw2, w_out,
                                    rw_split, n_rows=n_rows, t_lat=t_lat, seq=seq, tm=tm)
        x_all = moe_block(h2, logits, router_bias[i], x_mid, mods, final_norm_w[None, :],
                          exp_w_gate, exp_w_up, exp_w_down,
                          sh_w_gate[i], sh_w_up[i], sh_w_down[i],
                          layer=i, t_lat=t_lat, seq=seq, final_norm=last)
        x_rows = (x_all, x_all, t_lat)
    return x_all[:t_lat].reshape(batch, seq, d)
```

```python
import functools
import math

import jax
import jax.numpy as jnp
from jax import lax
from jax.experimental import pallas as pl
from jax.experimental.pallas import tpu as pltpu

F32 = jnp.float32
BF16 = jnp.bfloat16
HIGHEST = lax.Precision.HIGHEST

D_MODEL = 1024
GRID_W = 64
NORM_EPS = 1e-6
RET_HEADS, RET_DK, RET_DV, RET_CHUNK = 4, 64, 128, 128
ROPE_BASE = 10000.0
HYENA_CH, HYENA_EMB = 512, 33
HYENA_FAST_DECAY, HYENA_SLOW_DECAY, HYENA_TARGET = 0.3, 1.5, 1e-2
GLA_HEADS, GLA_DK, GLA_DV, GLA_RANK, GLA_TAU, GLA_CHUNK = 4, 128, 256, 16, 16.0, 64
N_EXPERTS, TOP_K, N_GROUPS, TOPK_GROUPS = 64, 8, 8, 4
GROUP_SIZE = N_EXPERTS // N_GROUPS
EXPERT_FF = 256
ROUTED_SCALE = 2.5

LANES = 128
SCAN_ROWS = 256
SCAN_LAT_ROWS = 1024
MOE_TILE = 256
ROW_UNIT = 16
EXPERT_ROWS = 1024
SLOT_CHUNK = 512
SUB_SLOTS = 256
SLOTS = -(-(MOE_TILE * TOP_K + N_EXPERTS * (ROW_UNIT - 1) + ROW_UNIT) // SLOT_CHUNK) * SLOT_CHUNK
N_UNITS = SLOTS // ROW_UNIT
PAD_UNITS = EXPERT_ROWS // ROW_UNIT
VMEM_LIMIT = 56 * 1024 * 1024


def _cparams(sem):
    return pltpu.CompilerParams(dimension_semantics=sem, vmem_limit_bytes=VMEM_LIMIT)


def _rms(x):
    return x * lax.rsqrt(jnp.mean(x * x, axis=-1, keepdims=True) + NORM_EPS)


def _silu(x):
    return x * jax.nn.sigmoid(x)


def _adaln_kernel(c_ref, w_ref, b_ref, o_ref):
    s = _silu(c_ref[...])
    w = w_ref[0]
    s_hi, w_hi = s.astype(BF16), w.astype(BF16)
    s_lo, w_lo = (s - s_hi.astype(F32)).astype(BF16), (w - w_hi.astype(F32)).astype(BF16)
    dot = lambda a, b: jnp.dot(a, b, preferred_element_type=F32)
    o_ref[0] = dot(s_hi, w_hi) + (dot(s_lo, w_hi) + dot(s_hi, w_lo)) + b_ref[0]


def adaln_rows(cond8, ada_w, ada_b):
    depth, d, n = ada_w.shape
    tn = 512
    return pl.pallas_call(
        _adaln_kernel,
        grid=(depth, n // tn),
        in_specs=[pl.BlockSpec((8, d), lambda l, j: (0, 0)),
                  pl.BlockSpec((1, d, tn), lambda l, j: (l, 0, j)),
                  pl.BlockSpec((1, 1, tn), lambda l, j: (l, 0, j))],
        out_specs=pl.BlockSpec((1, 8, tn), lambda l, j: (l, 0, j)),
        out_shape=jax.ShapeDtypeStruct((depth, 8, n), F32),
        compiler_params=_cparams(("parallel", "parallel")),
        name="adaln_rows",
    )(cond8, ada_w, ada_b.reshape(depth, 1, n))


def _modulated_norm(x, mod, nw, shift_row, scale_row):
    return _rms(x) * nw * (1.0 + mod[scale_row:scale_row + 1]) + mod[shift_row:shift_row + 1]


def _split_row_specs(tm, width, n_lat_tiles, col_blk=0, ctx_row0=0):
    ctx_blk0 = ctx_row0 // tm
    return [pl.BlockSpec((tm, width), lambda i: (jnp.minimum(i, n_lat_tiles - 1), col_blk)),
            pl.BlockSpec((tm, width),
                         lambda i: (ctx_blk0 + jnp.maximum(i - n_lat_tiles, 0), col_blk))]


def _pick_rows(lat_ref, ctx_ref, n_lat_tiles):
    return jnp.where(pl.program_id(0) < n_lat_tiles, lat_ref[...], ctx_ref[...])


def _inproj_even_kernel(xl_ref, xc_ref, mod_ref, nw_ref, w_ref, cos_ref, sin_ref, o_ref, *,
                        n_lat_tiles):
    x = _pick_rows(xl_ref, xc_ref, n_lat_tiles)
    h = _modulated_norm(x, mod_ref[0], nw_ref[...], 0, 1).astype(BF16)
    n_main = o_ref.shape[1]
    qk = jnp.dot(h, w_ref[:, 0:512], preferred_element_type=F32)
    qk_sw = jnp.dot(h, w_ref[:, n_main:n_main + 512], preferred_element_type=F32)
    is_lat = pl.program_id(0) < n_lat_tiles
    cos = jnp.where(is_lat, cos_ref[...], 1.0)
    sin = jnp.where(is_lat, sin_ref[...], 0.0)
    for c0 in range(0, 512, LANES):
        o_ref[:, c0:c0 + LANES] = (qk[:, c0:c0 + LANES] * cos
                                   + qk_sw[:, c0:c0 + LANES] * sin).astype(BF16)
    for c0 in range(512, n_main, 512):
        o_ref[:, c0:c0 + 512] = jnp.dot(h, w_ref[:, c0:c0 + 512],
                                        preferred_element_type=F32).astype(BF16)


def _inproj_odd_kernel(x_ref, mod_ref, nw_ref, w_ref, w2_ref, b2_ref, o_ref, laf_ref, lab_ref):
    h = _modulated_norm(x_ref[...], mod_ref[0], nw_ref[...], 0, 1).astype(BF16)
    n_main = o_ref.shape[1]
    kd = GLA_HEADS * GLA_DK
    q = jnp.dot(h, w_ref[:, 0:kd], preferred_element_type=F32)
    o_ref[:, 0:kd] = (q * (GLA_DK ** -0.5)).astype(BF16)
    for c0 in range(kd, n_main, 512):
        o_ref[:, c0:c0 + 512] = jnp.dot(h, w_ref[:, c0:c0 + 512],
                                        preferred_element_type=F32).astype(BF16)
    low = jnp.dot(h, w_ref[:, n_main:n_main + LANES], preferred_element_type=F32)

    low_hi = low.astype(BF16)
    low_lo = (low - low_hi.astype(F32)).astype(BF16)
    z = jnp.dot(jnp.concatenate([low_hi, low_lo, low_hi], axis=1), w2_ref[...],
                preferred_element_type=F32) + b2_ref[...]
    la = (jnp.minimum(z, 0.0) - jnp.log(1.0 + jnp.exp(-jnp.abs(z)))) * (1.0 / GLA_TAU)
    laf_ref[...] = la[:, 0:kd]
    lab_ref[...] = la[:, kd:2 * kd]


def _seg_of_tile(i, tm, t_lat, seq):
    return jnp.where(i < t_lat // tm, 1 + (i * tm) // seq, 0)


def inproj_even(x_rows, mods, nw, w_ext, cos_t, sin_t, *, t_lat, t_all, seq, tm=512):
    x_lat, x_ctx, ctx_row0 = x_rows
    d = x_lat.shape[1]
    n_ext = w_ext.shape[1]
    n_main = n_ext - 512
    n_lat_tiles, pos_tiles = t_lat // tm, seq // tm

    def pos_map(i):
        return (jnp.where(i < n_lat_tiles, i % pos_tiles, 0), 0)

    return pl.pallas_call(
        functools.partial(_inproj_even_kernel, n_lat_tiles=n_lat_tiles),
        grid=(t_all // tm,),
        in_specs=_split_row_specs(tm, d, n_lat_tiles, ctx_row0=ctx_row0) + [
                  pl.BlockSpec((1, 6, d), lambda i: (_seg_of_tile(i, tm, t_lat, seq), 0, 0)),
                  pl.BlockSpec((1, d), lambda i: (0, 0)),
                  pl.BlockSpec((d, n_ext), lambda i: (0, 0)),
                  pl.BlockSpec((tm, LANES), pos_map),
                  pl.BlockSpec((tm, LANES), pos_map)],
        out_specs=pl.BlockSpec((tm, n_main), lambda i: (i, 0)),
        out_shape=jax.ShapeDtypeStruct((t_all, n_main), BF16),
        compiler_params=_cparams(("parallel",)),
        name="inproj_even",
    )(x_lat, x_ctx, mods, nw, w_ext, cos_t, sin_t)


def inproj_odd(x_all, mods, nw, w_ext, w2, b2, *, t_lat, seq, tm=512):
    t_all, d = x_all.shape
    n_main = w_ext.shape[1] - LANES
    kd = GLA_HEADS * GLA_DK
    full = lambda shape: pl.BlockSpec(shape, lambda i: (0,) * len(shape))
    return pl.pallas_call(
        _inproj_odd_kernel,
        grid=(t_all // tm,),
        in_specs=[pl.BlockSpec((tm, d), lambda i: (i, 0)),
                  pl.BlockSpec((1, 6, d), lambda i: (_seg_of_tile(i, tm, t_lat, seq), 0, 0)),
                  full((1, d)), full(w_ext.shape), full(w2.shape), full(b2.shape)],
        out_specs=[pl.BlockSpec((tm, n_main), lambda i: (i, 0)),
                   pl.BlockSpec((tm, kd), lambda i: (i, 0)),
                   pl.BlockSpec((tm, kd), lambda i: (i, 0))],
        out_shape=[jax.ShapeDtypeStruct((t_all, n_main), BF16),
                   jax.ShapeDtypeStruct((t_all, kd), F32),
                   jax.ShapeDtypeStruct((t_all, kd), F32)],
        compiler_params=_cparams(("parallel",)),
        name="inproj_odd",
    )(x_all, mods, nw, w_ext, w2, b2)


def _scan_kernel(*refs, heads, dk, dv, chunk, gated, reverse, log_decay, sub_blocks):
    n_in = 3 + int(gated) + (2 if reverse else 0)
    ctx_in, lat_in = refs[:n_in], refs[n_in:2 * n_in]
    rest = list(refs[2 * n_in:])
    nw_ref = rest.pop(0) if reverse else None
    out_lat, out_ctx, state_ref = rest[0], rest[1], rest[2]
    dec_ref = None if gated else rest[3]

    hpg = LANES // dk
    groups = heads // hpg
    rows = SCAN_ROWS
    n_chunks = rows // chunk
    j = pl.program_id(1)

    row_i = lax.broadcasted_iota(jnp.int32, (rows, rows), 0)
    col_i = lax.broadcasted_iota(jnp.int32, (rows, rows), 1)
    same_chunk = (row_i // chunk) == (col_i // chunk)
    keep = jnp.logical_and(same_chunk, (col_i >= row_i) if reverse else (row_i >= col_i))

    @pl.when(j == 0)
    def _init():
        state_ref[...] = jnp.zeros_like(state_ref)
        if not gated:
            pos = lax.broadcasted_iota(jnp.int32, (rows, LANES), 0) % chunk
            steps = ((chunk - pos) if reverse else (pos + 1)).astype(F32)
            lane = lax.broadcasted_iota(jnp.int32, (rows, LANES), 1)
            for g in range(groups):
                lg = jnp.zeros((rows, LANES), F32)
                for a in range(hpg):
                    lg = jnp.where(lane // dk == a, log_decay[g * hpg + a], lg)
                logb = steps * lg
                b_end = float(chunk) * lg
                dec_ref[g, 0] = jnp.exp(logb)
                dec_ref[g, 1] = jnp.exp(-logb)
                dec_ref[g, 2] = jnp.exp(b_end - logb)
                dec_ref[g, 3] = jnp.exp(b_end)

    lane1 = lax.broadcasted_iota(jnp.int32, (1, LANES), 1)
    order = range(n_chunks - 1, -1, -1) if reverse else range(n_chunks)
    end_row = lambda c: c * chunk if reverse else (c + 1) * chunk - 1

    def scan_rows(in_refs, out_ref, r0):
        q_ref, k_ref, v_ref = in_refs[:3]
        blk = slice(r0, r0 + rows)
        if reverse:
            oprev_ref, gate_ref = in_refs[-2:]
        if gated:
            tri = jnp.where(keep, 1.0, 0.0).astype(BF16)
            la = in_refs[3][blk, :]
            la_hi = la.astype(BF16)
            la_lo = (la - la_hi.astype(F32)).astype(BF16)
            logb_all = (jnp.dot(tri, la_hi, preferred_element_type=F32)
                        + jnp.dot(tri, la_lo, preferred_element_type=F32))
        for g in range(groups):
            ksl = slice(g * LANES, (g + 1) * LANES)
            qg = q_ref[blk, ksl].astype(F32)
            kg = k_ref[blk, ksl].astype(F32)
            if gated:
                logb = logb_all[:, ksl]
                ends = [logb[end_row(c):end_row(c) + 1] for c in range(n_chunks)]
                b_end = jnp.concatenate([jnp.broadcast_to(e, (chunk, LANES)) for e in ends], axis=0)
                e_q, e_k, e_s = jnp.exp(logb), jnp.exp(-logb), jnp.exp(b_end - logb)
                e_e = [jnp.exp(e) for e in ends]
            else:
                e_q, e_k, e_s = dec_ref[g, 0], dec_ref[g, 1], dec_ref[g, 2]
                e_e = [dec_ref[g, 3][0:1]] * n_chunks
            qd = qg * e_q
            kd_ = (kg * e_k).astype(BF16)
            ks = (kg * e_s).astype(BF16)
            for a in range(hpg):
                h = g * hpg + a
                qa = (jnp.where(lane1 // dk == a, qd, 0.0) if hpg > 1 else qd).astype(BF16)
                vh = v_ref[blk, h * dv:(h + 1) * dv]
                s = lax.dot_general(qa, kd_, (((1,), (1,)), ((), ())), preferred_element_type=F32)
                o_intra = jnp.dot(jnp.where(keep, s, 0.0).astype(BF16), vh,
                                  preferred_element_type=F32)
                st = state_ref[h]
                for c in order:
                    rs = slice(c * chunk, (c + 1) * chunk)
                    o = o_intra[rs] + lax.dot_general(qa[rs], st.astype(BF16),
                                                      (((1,), (1,)), ((), ())),
                                                      preferred_element_type=F32)
                    st = e_e[c] * st + lax.dot_general(vh[rs], ks[rs], (((0,), (0,)), ((), ())),
                                                       preferred_element_type=F32)
                    osl = (slice(r0 + c * chunk, r0 + (c + 1) * chunk), slice(h * dv, (h + 1) * dv))
                    if reverse:
                        o = _rms(o + oprev_ref[osl]) * nw_ref[...]
                        out_ref[osl] = (o * _silu(gate_ref[osl].astype(F32))).astype(BF16)
                    else:
                        out_ref[osl] = o
                state_ref[h] = st

    @pl.when(j == 0)
    def _():
        scan_rows(ctx_in, out_ctx, 0)

    @pl.when(j > 0)
    def _():
        for sb in (range(sub_blocks - 1, -1, -1) if reverse else range(sub_blocks)):
            scan_rows(lat_in, out_lat, sb * rows)


def bidir_scan(proj, la_f, la_b, norm_w, *, batch, seq, ctx_len, heads, dk, dv, chunk,
               q_blk, k_blk, v_blk, g_blk, log_decay_f=None, log_decay_b=None):
    gated = la_f is not None
    hk, hv = heads * dk, heads * dv
    rb, lb = SCAN_ROWS, SCAN_LAT_ROWS
    lat_blocks = seq // lb
    ctx_base = (batch * seq) // rb
    assert ctx_len == rb

    lat_fwd = lambda b, j: b * lat_blocks + jnp.maximum(j - 1, 0)
    lat_bwd = lambda b, j: b * lat_blocks + lat_blocks - 1 - jnp.maximum(j - 1, 0)

    prev = None
    for reverse, lat_of, la, ld in ((False, lat_fwd, la_f, log_decay_f),
                                    (True, lat_bwd, la_b, log_decay_b)):
        ctx_spec = lambda w, cb, base=ctx_base: pl.BlockSpec(
            (rb, w), lambda b, j, cb=cb: (base + b, cb))
        lat_spec = lambda w, cb: pl.BlockSpec((lb, w), lambda b, j, cb=cb: (lat_of(b, j), cb))
        in_specs, args = [], []
        for spec, is_ctx in ((ctx_spec, True), (lat_spec, False)):
            in_specs += [spec(hk, q_blk), spec(hk, k_blk), spec(hv, v_blk)]
            args += [proj, proj, proj]
            if gated:
                in_specs.append(spec(hk, 0))
                args.append(la)
            if reverse:
                in_specs += [ctx_spec(hv, 0, base=0) if is_ctx else spec(hv, 0), spec(hv, g_blk)]
                args += [prev[1] if is_ctx else prev[0], proj]
        if reverse:
            in_specs.append(pl.BlockSpec((1, dv), lambda b, j: (0, 0)))
            args.append(norm_w)
        scratch = [pltpu.VMEM((heads, dv, LANES), F32)]
        if not gated:
            scratch.append(pltpu.VMEM((hk // LANES, 4, rb, LANES), F32))
        kern = functools.partial(_scan_kernel, heads=heads, dk=dk, dv=dv, chunk=chunk,
                                 gated=gated, reverse=reverse, log_decay=ld, sub_blocks=lb // rb)
        out_dtype = BF16 if reverse else F32
        prev = pl.pallas_call(
            kern,
            grid=(batch, lat_blocks + 1),
            in_specs=in_specs,
            out_specs=[lat_spec(hv, 0), ctx_spec(hv, 0, base=0)],
            out_shape=[jax.ShapeDtypeStruct((batch * seq, hv), out_dtype),
                       jax.ShapeDtypeStruct((batch * ctx_len, hv), out_dtype)],
            scratch_shapes=scratch,
            compiler_params=_cparams(("parallel", "arbitrary")),
            name="scan_bwd" if reverse else "scan_fwd",
        )(*args)
    return tuple(prev)


def _shortconv_kernel(z_ref, zp_ref, zn_ref, w_ref, b_ref, u_ref, x0_ref, *, tiles_per_seq):
    i = pl.program_id(0)
    tm = z_ref.shape[0]
    z = z_ref[...].astype(F32)
    first = (i % tiles_per_seq) == 0
    last = (i % tiles_per_seq) == tiles_per_seq - 1
    halo = zp_ref.shape[0]
    prev_row = jnp.where(first, 0.0, zp_ref[halo - 1:halo, :].astype(F32))
    next_row = jnp.where(last, 0.0, zn_ref[0:1, :].astype(F32))
    row = lax.broadcasted_iota(jnp.int32, z.shape, 0)
    z_prev = jnp.where(row == 0, prev_row, pltpu.roll(z, 1, 0))
    z_next = jnp.where(row == tm - 1, next_row, pltpu.roll(z, tm - 1, 0))
    y = w_ref[0:1] * z_prev + w_ref[1:2] * z + w_ref[2:3] * z_next + b_ref[...]
    c = HYENA_CH
    x0_ref[...] = y[:, 0:c].astype(BF16)
    u_ref[...] = (y[:, c:2 * c] * y[:, 2 * c:3 * c]).astype(BF16)


def hyena_shortconv(proj, short_w, short_b, *, row0, batch, seq_len, z_blk, tm=256):
    halo = 16
    tiles_per_seq = seq_len // tm
    n_tiles = batch * tiles_per_seq
    t0, h_per_tile = row0 // tm, tm // halo
    nz = 3 * HYENA_CH
    n_halo_blocks = proj.shape[0] // halo
    out_map = lambda i: (i % tiles_per_seq, i // tiles_per_seq)
    kern = functools.partial(_shortconv_kernel, tiles_per_seq=tiles_per_seq)
    return pl.pallas_call(
        kern,
        grid=(n_tiles,),
        in_specs=[pl.BlockSpec((tm, nz), lambda i: (t0 + i, z_blk)),
                  pl.BlockSpec((halo, nz),
                               lambda i: (jnp.maximum((t0 + i) * h_per_tile - 1, 0), z_blk)),
                  pl.BlockSpec((halo, nz),
                               lambda i: (jnp.minimum((t0 + i + 1) * h_per_tile,
                                                      n_halo_blocks - 1), z_blk)),
                  pl.BlockSpec((3, nz), lambda i: (0, 0)),
                  pl.BlockSpec((1, nz), lambda i: (0, 0))],
        out_specs=[pl.BlockSpec((tm, HYENA_CH), out_map), pl.BlockSpec((tm, HYENA_CH), out_map)],
        out_shape=[jax.ShapeDtypeStruct((seq_len, batch * HYENA_CH), BF16)] * 2,
        compiler_params=_cparams(("parallel",)),
        name="hyena_shortconv",
    )(proj, proj, proj, short_w, short_b)


def _filter_kernel(feat_ref, w1_ref, b1_ref, w2_ref, b2_ref, w3_ref, b3_ref, w4_ref, fq_ref,
                   o_ref, *, seq_len):
    i = pl.program_id(0)
    tl = feat_ref.shape[1]
    fq = fq_ref[...]
    dot = lambda a, b: jnp.dot(a, b, preferred_element_type=F32, precision=HIGHEST)
    h = jnp.sin(fq * (dot(w1_ref[...], feat_ref[...]) + b1_ref[...]))
    h = jnp.sin(fq * (dot(w2_ref[...], h) + b2_ref[...]))
    h = jnp.sin(fq * (dot(w3_ref[...], h) + b3_ref[...]))
    h = lax.dot_general(h, w4_ref[...], (((0,), (0,)), ((), ())),
                        preferred_element_type=F32, precision=HIGHEST)
    c = HYENA_CH
    max_decay = math.log(HYENA_TARGET) / HYENA_FAST_DECAY
    min_decay = math.log(HYENA_TARGET) / HYENA_SLOW_DECAY
    ch = lax.broadcasted_iota(jnp.int32, (tl, c), 1).astype(F32)
    deltas = min_decay + ch * ((max_decay - min_decay) / (c - 1))
    row = lax.broadcasted_iota(jnp.int32, (tl, c), 0) + i * tl
    t = row.astype(F32) * (1.0 / (seq_len - 1))
    window = jnp.exp(-t * jnp.abs(deltas))
    h_f = h[:, 0:c] * window
    h_b = jnp.where(row == 0, 0.0, h[:, c:2 * c] * window)
    o_ref[:, 0:c] = (h_f + h_b).astype(BF16)
    o_ref[:, c:2 * c] = (h_f - h_b).astype(BF16)


def hyena_filter_pair(feats, w1p, b1, w2, b2, w3, b3, w4, fq):
    seq_len = feats.shape[0]
    tl = min(2048, seq_len)
    full = lambda a: pl.BlockSpec(a.shape, lambda i: (0,) * a.ndim)
    kern = functools.partial(_filter_kernel, seq_len=seq_len)
    feats, w1p, w2, w3 = feats.T, w1p.T, w2.T, w3.T
    b1, b2, b3, fq = b1.T, b2.T, b3.T, fq.T
    return pl.pallas_call(
        kern,
        grid=(seq_len // tl,),
        in_specs=[pl.BlockSpec((feats.shape[0], tl), lambda i: (0, i)),
                  full(w1p), full(b1), full(w2), full(b2), full(w3), full(b3), full(w4), full(fq)],
        out_specs=pl.BlockSpec((tl, 2 * HYENA_CH), lambda i: (i, 0)),
        out_shape=jax.ShapeDtypeStruct((seq_len, 2 * HYENA_CH), BF16),
        compiler_params=_cparams(("parallel",)),
        name="hyena_filter",
    )(feats, w1p, b1, w2, b2, w3, b3, w4, fq)


def _filter_spectrum_kernel(c_ref, s_ref, f_ref, hc_ref, hs_ref, kre_ref, kim_ref, *, scale):
    ch = HYENA_CH
    f = f_ref[...]
    a = jnp.dot(c_ref[...], f, preferred_element_type=F32)
    b = jnp.dot(s_ref[...], f, preferred_element_type=F32)
    hc, hs = hc_ref[...], hs_ref[...]
    kre_ref[...] = (hc * a[:, 0:ch] + hs * b[:, 0:ch]) * scale
    kim_ref[...] = (hs * a[:, ch:2 * ch] - hc * b[:, ch:2 * ch]) * scale


def filter_spectrum(cs, ss, filt, half_cos, half_sin):
    n = cs.shape[0]
    c = HYENA_CH
    tf = min(512, n)
    kern = functools.partial(_filter_spectrum_kernel, scale=1.0 / n)
    return pl.pallas_call(
        kern,
        grid=(n // tf,),
        in_specs=[pl.BlockSpec((tf, n), lambda f: (f, 0)), pl.BlockSpec((tf, n), lambda f: (f, 0)),
                  pl.BlockSpec((n, 2 * c), lambda f: (0, 0)),
                  pl.BlockSpec((tf, 1), lambda f: (f, 0)), pl.BlockSpec((tf, 1), lambda f: (f, 0))],
        out_specs=[pl.BlockSpec((tf, c), lambda f: (f, 0))] * 2,
        out_shape=[jax.ShapeDtypeStruct((n, c), F32)] * 2,
        compiler_params=_cparams(("parallel",)),
        name="hyena_filter_spectrum",
    )(cs, ss, filt, half_cos, half_sin)


def _dft_fwd_kernel(c_ref, s_ref, u_ref, kre_ref, kim_ref, yre_ref, yim_ref):
    u = u_ref[...]
    a = jnp.dot(c_ref[...], u, preferred_element_type=F32)
    b = jnp.dot(s_ref[...], u, preferred_element_type=F32)
    k_re, k_im = kre_ref[...], kim_ref[...]
    yre_ref[...] = (a * k_re + b * k_im).astype(BF16)
    yim_ref[...] = (a * k_im - b * k_re).astype(BF16)


def dft_forward(cs, ss, u, k_re, k_im, *, batch):
    n = cs.shape[0]
    c = HYENA_CH
    tf = min(512, n)
    return pl.pallas_call(
        _dft_fwd_kernel,
        grid=(batch, n // tf),
        in_specs=[pl.BlockSpec((tf, n), lambda bi, f: (f, 0)),
                  pl.BlockSpec((tf, n), lambda bi, f: (f, 0)),
                  pl.BlockSpec((n, c), lambda bi, f: (0, bi)),
                  pl.BlockSpec((tf, c), lambda bi, f: (f, 0)),
                  pl.BlockSpec((tf, c), lambda bi, f: (f, 0))],
        out_specs=[pl.BlockSpec((tf, c), lambda bi, f: (f, bi))] * 2,
        out_shape=[jax.ShapeDtypeStruct((n, batch * c), BF16)] * 2,
        compiler_params=_cparams(("parallel", "parallel")),
        name="dft_forward",
    )(cs, ss, u, k_re, k_im)


def _dft_inv_kernel(c_ref, s_ref, yre_ref, yim_ref, u_ref, x0_ref, bias_ref, o_ref):
    y = (jnp.dot(c_ref[...], yre_ref[...], preferred_element_type=F32)
         - jnp.dot(s_ref[...], yim_ref[...], preferred_element_type=F32))
    u = u_ref[...].astype(F32)
    o_ref[...] = (x0_ref[...].astype(F32) * (y + u * bias_ref[...])).astype(BF16)


def dft_inverse(cs, ss, yre, yim, u, x0, bias, *, batch):
    n = cs.shape[0]
    c = HYENA_CH
    tt = min(512, n)
    col = lambda bi, t: (0, bi)
    tile = lambda bi, t: (t, bi)
    return pl.pallas_call(
        _dft_inv_kernel,
        grid=(batch, n // tt),
        in_specs=[pl.BlockSpec((tt, n), lambda bi, t: (t, 0)),
                  pl.BlockSpec((tt, n), lambda bi, t: (t, 0)),
                  pl.BlockSpec((n, c), col), pl.BlockSpec((n, c), col),
                  pl.BlockSpec((tt, c), tile), pl.BlockSpec((tt, c), tile),
                  pl.BlockSpec((1, c), lambda bi, t: (0, 0))],
        out_specs=pl.BlockSpec((tt, c), lambda bi, t: (bi * (n // tt) + t, 0)),
        out_shape=jax.ShapeDtypeStruct((batch * n, c), BF16),
        compiler_params=_cparams(("parallel", "parallel")),
        name="dft_inverse",
    )(cs, ss, yre, yim, u, x0, bias)


def _dft_table_kernel(cg_ref, sg_ref, cd_ref, sd_ref, c_ref, s_ref):
    cg, sg, cd, sd = cg_ref[...], sg_ref[...], cd_ref[0], sd_ref[0]
    c_ref[...] = (cg * cd - sg * sd).astype(BF16)
    s_ref[...] = (sg * cd + cg * sd).astype(BF16)


def _shifted_dft_tables(n):
    tf = min(256, n)
    theta = 2.0 * math.pi / (8 * n)
    odd_s = 2 * jnp.arange(n, dtype=jnp.int32)[None, :] + 1
    odd_i = 2 * jnp.arange(tf, dtype=jnp.int32)[:, None] + 1
    gamma = ((odd_i * odd_s) % (8 * n)).astype(F32) * theta
    f0 = jnp.arange(n // tf, dtype=jnp.int32)[:, None] * tf
    delta = ((2 * f0 * odd_s) % (8 * n)).astype(F32)[:, None, :] * theta
    whole = pl.BlockSpec((tf, n), lambda i: (0, 0))
    row = pl.BlockSpec((1, 1, n), lambda i: (i, 0, 0))
    cs, ss = pl.pallas_call(
        _dft_table_kernel,
        grid=(n // tf,),
        in_specs=[whole, whole, row, row],
        out_specs=[pl.BlockSpec((tf, n), lambda i: (i, 0))] * 2,
        out_shape=[jax.ShapeDtypeStruct((n, n), BF16)] * 2,
        compiler_params=_cparams(("parallel",)),
        name="dft_tables",
    )(jnp.cos(gamma), jnp.sin(gamma), jnp.cos(delta), jnp.sin(delta))
    half = (2 * jnp.arange(n, dtype=jnp.int32) + 1).astype(F32)[:, None] * (math.pi / (4 * n))
    return cs, ss, jnp.cos(half), jnp.sin(half)


def _filter_features(seq_len):
    t = jnp.linspace(0.0, 1.0, seq_len, dtype=F32)[:, None]
    bands = (HYENA_EMB - 1) // 2
    ang = 2.0 * math.pi * jnp.arange(seq_len, dtype=F32)[:, None] / seq_len
    fr = jnp.linspace(1e-4, bands - 1, bands, dtype=F32)[None, :]
    feats = jnp.concatenate([t, jnp.cos(fr * ang), -jnp.sin(fr * ang)], axis=-1)
    return jnp.pad(feats, ((0, 0), (0, LANES - HYENA_EMB)))


def hyena_long_conv(proj, hp, *, row0, batch, seq_len, z_blk):
    u, x0 = hyena_shortconv(proj, hp["short_w"], hp["short_b"], row0=row0, batch=batch,
                            seq_len=seq_len, z_blk=z_blk, tm=min(512, seq_len))
    filt = hyena_filter_pair(_filter_features(seq_len), hp["w1"], hp["b1"], hp["w2"], hp["b2"],
                             hp["w3"], hp["b3"], hp["w4"], hp["freq"])
    cs, ss, half_cos, half_sin = _shifted_dft_tables(seq_len)
    k_re, k_im = filter_spectrum(cs, ss, filt, half_cos, half_sin)
    yre, yim = dft_forward(cs, ss, u, k_re, k_im, batch=batch)
    return dft_inverse(cs, ss, yre, yim, u, x0, hp["bias"], batch=batch)


def _outproj_kernel(mal_ref, mac_ref, mbl_ref, mbc_ref, xl_ref, xc_ref, mod_ref, nw_ref, w_ref,
                    rw_ref, xo_ref, h_ref, lt_ref, *, n_lat_tiles):
    half = mal_ref.shape[1]
    pick = functools.partial(_pick_rows, n_lat_tiles=n_lat_tiles)
    y = (jnp.dot(pick(mal_ref, mac_ref), w_ref[0:half], preferred_element_type=F32)
         + jnp.dot(pick(mbl_ref, mbc_ref), w_ref[half:2 * half], preferred_element_type=F32))
    mod = mod_ref[0]
    x = pick(xl_ref, xc_ref) + mod[2:3] * y
    xo_ref[...] = x
    h = _modulated_norm(x, mod, nw_ref[...], 3, 4)
    h_ref[...] = h.astype(BF16)
    h_hi = h.astype(BF16)
    h_lo = (h - h_hi.astype(F32)).astype(BF16)
    nt = lambda a, b: lax.dot_general(a, b, (((1,), (1,)), ((), ())), preferred_element_type=F32)
    lt_ref[...] = nt(rw_ref[0], h_hi) + (nt(rw_ref[0], h_lo) + nt(rw_ref[1], h_hi))


def outproj(mix_a, a_blk, mix_b, b_blk, x_rows, mods, nw, w_out, router_wt, *,
            n_rows, t_lat, seq, tm=512):
    x_lat, x_ctx, ctx_row0 = x_rows
    d = x_lat.shape[1]
    half = d // 2
    n_lat_tiles = t_lat // tm
    return pl.pallas_call(
        functools.partial(_outproj_kernel, n_lat_tiles=n_lat_tiles),
        grid=(n_rows // tm,),
        in_specs=_split_row_specs(tm, half, n_lat_tiles, a_blk)
                 + _split_row_specs(tm, half, n_lat_tiles, b_blk)
                 + _split_row_specs(tm, d, n_lat_tiles, ctx_row0=ctx_row0) + [
                  pl.BlockSpec((1, 6, d), lambda i: (_seg_of_tile(i, tm, t_lat, seq), 0, 0)),
                  pl.BlockSpec((1, d), lambda i: (0, 0)),
                  pl.BlockSpec((d, d), lambda i: (0, 0)),
                  pl.BlockSpec((2, N_EXPERTS, d), lambda i: (0, 0, 0))],
        out_specs=[pl.BlockSpec((tm, d), lambda i: (i, 0)),
                   pl.BlockSpec((tm, d), lambda i: (i, 0)),
                   pl.BlockSpec((N_EXPERTS, tm), lambda i: (0, i))],
        out_shape=[jax.ShapeDtypeStruct((n_rows, d), F32),
                   jax.ShapeDtypeStruct((n_rows, d), BF16),
                   jax.ShapeDtypeStruct((N_EXPERTS, n_rows), F32)],
        compiler_params=_cparams(("parallel",)),
        name="outproj",
    )(*mix_a, *mix_b, x_lat, x_ctx, mods, nw, w_out, router_wt)


def _first_max(x, idx, sentinel):
    m = jnp.max(x, axis=0, keepdims=True)
    first = jnp.min(jnp.where(x == m, idx, sentinel), axis=0, keepdims=True)
    return m, idx == first


def _router_kernel(lt_ref, bias_ref, before_ref, below_ref, lists_ref, pc_ref):
    tn = lt_ref.shape[1]
    scores = jax.nn.sigmoid(lt_ref[...])
    sel = scores + bias_ref[...]
    neg = -jnp.inf
    in_grp = lax.broadcasted_iota(jnp.int32, (GROUP_SIZE, tn), 0)
    gscore = []
    for g in range(N_GROUPS):
        x = sel[g * GROUP_SIZE:(g + 1) * GROUP_SIZE]
        m1, hit = _first_max(x, in_grp, GROUP_SIZE)
        gscore.append(m1 + jnp.max(jnp.where(hit, neg, x), axis=0, keepdims=True))
    rows = []
    for g in range(N_GROUPS):
        beaten = jnp.zeros((1, tn), jnp.int32)
        for o in range(N_GROUPS):
            if o != g:
                wins = (gscore[o] >= gscore[g]) if o < g else (gscore[o] > gscore[g])
                beaten = beaten + wins.astype(jnp.int32)
        keep = jnp.broadcast_to(beaten < TOPK_GROUPS, (GROUP_SIZE, tn))
        rows.append(jnp.where(keep, sel[g * GROUP_SIZE:(g + 1) * GROUP_SIZE], neg))
    cand = jnp.concatenate(rows, axis=0)
    eidx = lax.broadcasted_iota(jnp.int32, cand.shape, 0)
    chosen = jnp.zeros(cand.shape, jnp.bool_)
    hits = []
    for _ in range(TOP_K):
        _, hit = _first_max(cand, eidx, N_EXPERTS)
        hits.append(hit)
        chosen = jnp.logical_or(chosen, hit)
        cand = jnp.where(hit, neg, cand)
    w = jnp.where(chosen, scores, 0.0)
    gates_t = w / jnp.sum(w, axis=0, keepdims=True) * ROUTED_SCALE
    chosen_f = jnp.where(chosen, 1.0, 0.0)
    rank = jnp.dot(chosen_f.astype(BF16), before_ref[...], preferred_element_type=F32)
    count = jnp.sum(chosen_f, axis=1, keepdims=True)
    pc = jnp.floor((count + (ROW_UNIT - 1)) * (1.0 / ROW_UNIT)) * ROW_UNIT
    pc_lanes = jnp.broadcast_to(pc, (N_EXPERTS, LANES))
    start = jnp.dot(below_ref[...], pc_lanes, preferred_element_type=F32,
                    precision=HIGHEST)[:, 0:1]
    slot = start + rank
    pick = lambda hit, v: jnp.sum(jnp.where(hit, v, 0.0), axis=0, keepdims=True)
    lists = jnp.concatenate([pick(h, slot) for h in hits] + [pick(h, gates_t) for h in hits], axis=0)
    lists_ref[...] = lists
    pc_ref[0] = pc_lanes


def router(logits, router_bias):
    e, t = logits.shape
    tn = MOE_TILE
    n_tiles = t // tn
    tok_i = jnp.arange(tn, dtype=jnp.int32)
    before = (tok_i[:, None] < tok_i[None, :]).astype(BF16)
    exp_i = jnp.arange(e, dtype=jnp.int32)
    below = (exp_i[None, :] < exp_i[:, None]).astype(F32)
    const = lambda shape: pl.BlockSpec(shape, lambda i: (0,) * len(shape))
    return pl.pallas_call(
        _router_kernel,
        grid=(n_tiles,),
        in_specs=[pl.BlockSpec((e, tn), lambda i: (0, i)),
                  const((e, 1)), const((tn, tn)), const((e, e))],
        out_specs=[pl.BlockSpec((2 * TOP_K, tn), lambda i: (0, i)),
                   pl.BlockSpec((1, e, LANES), lambda i: (i, 0, 0))],
        out_shape=[jax.ShapeDtypeStruct((2 * TOP_K, t), F32),
                   jax.ShapeDtypeStruct((n_tiles, e, LANES), F32)],
        compiler_params=_cparams(("parallel",)),
        name="router",
    )(logits, router_bias.reshape(e, 1), before, below)


def moe_plan(pc, n_tiles):
    e = pc.shape[1]
    assert n_tiles >= e
    cap_rows = _sorted_capacity(n_tiles)
    run_end = jnp.cumsum(pc, axis=1)
    run_start = run_end - pc
    total = jnp.sum(pc, axis=0)
    total_al = ((total + EXPERT_ROWS - 1) // EXPERT_ROWS) * EXPERT_ROWS
    range_end = jnp.cumsum(total_al)
    range_start = range_end - total_al
    run_row = range_start[None, :] + jnp.cumsum(pc, axis=0) - pc
    unit_row = jnp.arange(N_UNITS, dtype=jnp.int32) * ROW_UNIT
    owner = jnp.sum((run_end[:, None, :] <= unit_row[None, :, None]).astype(jnp.int32), axis=2)
    owner_hot = (owner[:, :, None] == jnp.arange(e, dtype=jnp.int32)).astype(jnp.int32)
    dst = jnp.sum(owner_hot * (run_row - run_start)[:, None, :], axis=2) + unit_row[None, :]
    used = owner < e
    dump = cap_rows + (jnp.arange(n_tiles, dtype=jnp.int32) % 2)[:, None] * DUMP_ROWS
    unit_dst = jnp.where(used, dst, dump + unit_row[None, :]).astype(jnp.int32)
    unit_src = jnp.where(used, dst, 0).astype(jnp.int32)
    pad_off = jnp.arange(PAD_UNITS, dtype=jnp.int32)[None, :] * ROW_UNIT
    pad_row = (range_start + total)[:, None] + pad_off
    pad_row = jnp.concatenate([jnp.where(pad_row < range_end[:, None], pad_row, -1),
                               jnp.full((n_tiles - e, PAD_UNITS), -1, jnp.int32)], axis=0)
    pad_dst = jnp.where(pad_row >= 0, pad_row, dump + SLOTS + pad_off).astype(jnp.int32)
    tile_end = range_end // EXPERT_ROWS
    tile_expert = jnp.sum((tile_end[None, :] <= jnp.arange(cap_rows // EXPERT_ROWS,
                                                           dtype=jnp.int32)[:, None])
                          .astype(jnp.int32), axis=1)
    tile_expert = jnp.minimum(tile_expert, e - 1).astype(jnp.int32)
    used_rows = run_end[:, -1].astype(jnp.int32)
    tile_end = tile_end.astype(jnp.int32)
    return unit_dst, unit_src, pad_dst, used_rows, tile_expert, tile_end[-1:], tile_end


def _sorted_capacity(n_tiles):
    worst_rows = n_tiles * (MOE_TILE * TOP_K + N_EXPERTS * (ROW_UNIT - 1)) + N_EXPERTS * (EXPERT_ROWS - 1)
    return -(-worst_rows // EXPERT_ROWS) * EXPERT_ROWS


DUMP_ROWS = SLOTS + PAD_UNITS * ROW_UNIT


UNITS_PER_CHUNK = SLOT_CHUNK // ROW_UNIT
MAIN_SLOTS = SLOTS - SLOT_CHUNK
assert MAIN_SLOTS >= MOE_TILE * TOP_K


def _dispatch_kernel(unit_ref, pad_ref, used_ref, h_ref, lists_ref, xs_ref, loc_ref, zero_ref, sems):
    i = pl.program_id(0)
    last = pl.num_programs(0) - 1
    slot = i % 2
    tn = h_ref.shape[0]

    @pl.when(i == 0)
    def _():
        zero_ref[...] = jnp.zeros_like(zero_ref)

    rel_rows = lax.broadcasted_iota(jnp.int32, (SUB_SLOTS, tn), 0).astype(F32).astype(BF16)

    def chunk(c0):
        parts = []
        for b0 in range(c0, c0 + SLOT_CHUNK, SUB_SLOTS):
            p = jnp.zeros((SUB_SLOTS, tn), BF16)
            for k in range(TOP_K):
                hit = rel_rows == (lists_ref[k:k + 1, :] - float(b0)).astype(BF16)
                p = jnp.where(hit, jnp.ones_like(p), p)
            parts.append(p)
        loc_ref[slot, c0:c0 + SLOT_CHUNK, :] = jnp.dot(
            jnp.concatenate(parts, axis=0), h_ref[...],
            preferred_element_type=F32).astype(BF16)
        for u in range(c0 // ROW_UNIT, c0 // ROW_UNIT + UNITS_PER_CHUNK):
            row = pl.multiple_of(unit_ref[i, u], ROW_UNIT)
            pltpu.make_async_copy(loc_ref.at[slot, pl.ds(u * ROW_UNIT, ROW_UNIT)],
                                  xs_ref.at[pl.ds(row, ROW_UNIT)], sems.at[slot]).start()

    for c0 in range(0, MAIN_SLOTS, SLOT_CHUNK):
        chunk(c0)

    @pl.when(used_ref[i] > MAIN_SLOTS)
    def _():
        chunk(MAIN_SLOTS)

    for p in range(PAD_UNITS):
        row = pl.multiple_of(pad_ref[i, p], ROW_UNIT)
        pltpu.make_async_copy(zero_ref, xs_ref.at[pl.ds(row, ROW_UNIT)], sems.at[slot]).start()

    def wait_all(s, tile):
        pltpu.make_async_copy(loc_ref.at[s, pl.ds(0, MAIN_SLOTS)],
                              xs_ref.at[pl.ds(0, MAIN_SLOTS)], sems.at[s]).wait()

        @pl.when(used_ref[tile] > MAIN_SLOTS)
        def _():
            pltpu.make_async_copy(loc_ref.at[s, pl.ds(MAIN_SLOTS, SLOT_CHUNK)],
                                  xs_ref.at[pl.ds(0, SLOT_CHUNK)], sems.at[s]).wait()
        pltpu.make_async_copy(loc_ref.at[s, pl.ds(0, PAD_UNITS * ROW_UNIT)],
                              xs_ref.at[pl.ds(0, PAD_UNITS * ROW_UNIT)], sems.at[s]).wait()

    @pl.when(i > 0)
    def _():
        wait_all(1 - slot, i - 1)

    @pl.when(i == last)
    def _():
        wait_all(slot, i)


def moe_dispatch(h, lists, unit_dst, pad_dst, used_rows, *, n_tiles):
    d = h.shape[1]
    cap_rows = _sorted_capacity(n_tiles) + 2 * DUMP_ROWS
    return pl.pallas_call(
        _dispatch_kernel,
        grid_spec=pltpu.PrefetchScalarGridSpec(
            num_scalar_prefetch=3,
            grid=(n_tiles,),
            in_specs=[pl.BlockSpec((MOE_TILE, d), lambda i, *_: (i, 0)),
                      pl.BlockSpec((2 * TOP_K, MOE_TILE), lambda i, *_: (0, i))],
            out_specs=pl.BlockSpec(memory_space=pl.ANY),
            scratch_shapes=[pltpu.VMEM((2, SLOTS, d), BF16), pltpu.VMEM((ROW_UNIT, d), BF16),
                            pltpu.SemaphoreType.DMA((2,))]),
        out_shape=jax.ShapeDtypeStruct((cap_rows, d), BF16),
        compiler_params=_cparams(("arbitrary",)),
        name="moe_dispatch",
    )(unit_dst, pad_dst, used_rows, h, lists)


X_RING = 3


def _expert_kernel(te_ref, nt_ref, end_ref, xs_ref, wg_hbm, wu_hbm, wd_hbm, y_ref, x_buf,
                   wg_st, wu_st, wd_st, wg_bf, wu_bf, wd_bf, turn_ref, sems, wsems, *, layer):
    t = pl.program_id(0)
    n_live = nt_ref[0]
    live = t < n_live
    tc = jnp.minimum(t, jnp.maximum(n_live - 1, 0))
    new_expert = jnp.logical_or(t == 0, te_ref[tc] != te_ref[jnp.maximum(tc - 1, 0)])
    rows = x_buf.shape[1]

    def w_copies(e, s):
        return [pltpu.make_async_copy(hbm.at[layer, e], st.at[s], wsems.at[s])
                for hbm, st in ((wg_hbm, wg_st), (wu_hbm, wu_st), (wd_hbm, wd_st))]

    @pl.when(jnp.logical_and(t == 0, live))
    def _():
        turn_ref[0] = 0
        for cp in w_copies(te_ref[0], 0):
            cp.start()

    def x_copy(tile):
        s = tile % X_RING
        return pltpu.make_async_copy(xs_ref.at[pl.ds(pl.multiple_of(tile * rows, rows), rows)],
                                     x_buf.at[s], sems.at[s])

    for k in range(X_RING - 1):
        @pl.when(jnp.logical_and(t == 0, k < n_live))
        def _():
            x_copy(k).start()

    @pl.when(t + (X_RING - 1) < n_live)
    def _():
        x_copy(t + (X_RING - 1)).start()

    @pl.when(jnp.logical_and(live, new_expert))
    def _():
        e = te_ref[tc]
        s = turn_ref[0] % 2
        for cp in w_copies(e, s):
            cp.wait()
        wg_bf[...] = wg_st[s].astype(BF16)
        wu_bf[...] = wu_st[s].astype(BF16)
        wd_bf[...] = wd_st[s].astype(BF16)
        nxt = end_ref[e]

        @pl.when(nxt < n_live)
        def _():
            for cp in w_copies(te_ref[jnp.minimum(nxt, te_ref.shape[0] - 1)], 1 - s):
                cp.start()
        turn_ref[0] = turn_ref[0] + 1

    @pl.when(live)
    def _():
        x_copy(t).wait()
        x = x_buf[t % X_RING]
        act = (_silu(jnp.dot(x, wg_bf[...], preferred_element_type=F32))
               * jnp.dot(x, wu_bf[...], preferred_element_type=F32))
        y_ref[...] = jnp.dot(act.astype(BF16), wd_bf[...], preferred_element_type=F32).astype(BF16)


def moe_experts(xs, w_gate, w_up, w_down, layer, tile_expert, n_row_tiles, range_end_tile):
    d = xs.shape[1]
    cap_rows = xs.shape[0] - 2 * DUMP_ROWS
    ff = w_gate.shape[3]
    row_map = lambda t, te, nt, end: (jnp.minimum(t, jnp.maximum(nt[0] - 1, 0)), 0)
    hbm = pl.BlockSpec(memory_space=pl.ANY)
    return pl.pallas_call(
        functools.partial(_expert_kernel, layer=layer),
        grid_spec=pltpu.PrefetchScalarGridSpec(
            num_scalar_prefetch=3,
            grid=(cap_rows // EXPERT_ROWS,),
            in_specs=[hbm, hbm, hbm, hbm],
            out_specs=pl.BlockSpec((EXPERT_ROWS, d), row_map),
            scratch_shapes=[pltpu.VMEM((X_RING, EXPERT_ROWS, d), BF16),
                            pltpu.VMEM((2, d, ff), F32), pltpu.VMEM((2, d, ff), F32),
                            pltpu.VMEM((2, ff, d), F32),
                            pltpu.VMEM((d, ff), BF16), pltpu.VMEM((d, ff), BF16),
                            pltpu.VMEM((ff, d), BF16), pltpu.SMEM((1,), jnp.int32),
                            pltpu.SemaphoreType.DMA((X_RING,)), pltpu.SemaphoreType.DMA((2,))]),
        out_shape=jax.ShapeDtypeStruct((cap_rows, d), BF16),
        compiler_params=_cparams(("arbitrary",)),
        name="moe_experts",
    )(tile_expert, n_row_tiles, range_end_tile, xs, w_gate, w_up, w_down)


def _combine_kernel(unit_ref, used_ref, ys_ref, lists_ref, h_ref, x_ref, mod_ref, fnw_ref, sg_ref,
                    su_ref, sd_ref, o_ref, loc_ref, acc_ref, sems, *, final_norm):
    i = pl.program_id(0)
    last = pl.num_programs(0) - 1
    slot = i % 2
    tn = h_ref.shape[0]
    def fetch(tile, s):
        def units(u0, u1):
            for u in range(u0, u1):
                row = pl.multiple_of(unit_ref[tile, u], ROW_UNIT)
                pltpu.make_async_copy(ys_ref.at[pl.ds(row, ROW_UNIT)],
                                      loc_ref.at[s, pl.ds(u * ROW_UNIT, ROW_UNIT)], sems.at[s]).start()
        units(0, MAIN_SLOTS // ROW_UNIT)

        @pl.when(used_ref[tile] > MAIN_SLOTS)
        def _():
            units(MAIN_SLOTS // ROW_UNIT, N_UNITS)

    @pl.when(i == 0)
    def _():
        fetch(0, 0)

    @pl.when(i < last)
    def _():
        fetch(i + 1, 1 - slot)

    h = h_ref[...]
    act = (_silu(jnp.dot(h, sg_ref[...], preferred_element_type=F32))
           * jnp.dot(h, su_ref[...], preferred_element_type=F32))
    acc = jnp.dot(act.astype(BF16), sd_ref[...], preferred_element_type=F32)
    tail = used_ref[i] > MAIN_SLOTS
    pltpu.make_async_copy(ys_ref.at[pl.ds(0, MAIN_SLOTS)], loc_ref.at[slot, pl.ds(0, MAIN_SLOTS)],
                          sems.at[slot]).wait()

    @pl.when(tail)
    def _():
        pltpu.make_async_copy(ys_ref.at[pl.ds(0, SLOT_CHUNK)],
                              loc_ref.at[slot, pl.ds(MAIN_SLOTS, SLOT_CHUNK)], sems.at[slot]).wait()

    rel_rows = lax.broadcasted_iota(jnp.int32, (SUB_SLOTS, tn), 0).astype(F32).astype(BF16)
    gate_w = [lists_ref[TOP_K + k:TOP_K + k + 1, :].astype(BF16) for k in range(TOP_K)]

    def chunk(c0):
        parts = []
        for b0 in range(c0, c0 + SLOT_CHUNK, SUB_SLOTS):
            q = jnp.zeros((SUB_SLOTS, tn), BF16)
            for k in range(TOP_K):
                hit = rel_rows == (lists_ref[k:k + 1, :] - float(b0)).astype(BF16)
                q = jnp.where(hit, gate_w[k], q)
            parts.append(q)
        return lax.dot_general(jnp.concatenate(parts, axis=0), loc_ref[slot, c0:c0 + SLOT_CHUNK, :],
                               (((0,), (0,)), ((), ())), preferred_element_type=F32)

    for c0 in range(0, MAIN_SLOTS, SLOT_CHUNK):
        acc = acc + chunk(c0)
    acc_ref[...] = acc

    @pl.when(tail)
    def _():
        acc_ref[...] += chunk(MAIN_SLOTS)

    x = x_ref[...] + mod_ref[0][5:6] * acc_ref[...]
    if final_norm:
        x = _rms(x) * fnw_ref[...]
    o_ref[...] = x


def moe_combine(ys, lists, h, x_all, mods, fnw, sh_gate, sh_up, sh_down, unit_dst, used_rows, *,
                n_tiles, t_lat, seq, final_norm):
    d = h.shape[1]
    tn = MOE_TILE
    const = lambda a: pl.BlockSpec(a.shape, lambda i, *_: (0,) * a.ndim)
    kern = functools.partial(_combine_kernel, final_norm=final_norm)
    return pl.pallas_call(
        kern,
        grid_spec=pltpu.PrefetchScalarGridSpec(
            num_scalar_prefetch=2,
            grid=(n_tiles,),
            in_specs=[pl.BlockSpec(memory_space=pl.ANY),
                      pl.BlockSpec((2 * TOP_K, tn), lambda i, *_: (0, i)),
                      pl.BlockSpec((tn, d), lambda i, *_: (i, 0)),
                      pl.BlockSpec((tn, d), lambda i, *_: (i, 0)),
                      pl.BlockSpec((1, 6, d),
                                   lambda i, *_: (_seg_of_tile(i, tn, t_lat, seq), 0, 0)),
                      const(fnw), const(sh_gate), const(sh_up), const(sh_down)],
            out_specs=pl.BlockSpec((tn, d), lambda i, *_: (i, 0)),
            scratch_shapes=[pltpu.VMEM((2, SLOTS, d), BF16), pltpu.VMEM((tn, d), F32),
                            pltpu.SemaphoreType.DMA((2,))]),
        out_shape=jax.ShapeDtypeStruct((n_tiles * tn, d), F32),
        compiler_params=_cparams(("arbitrary",)),
        name="moe_combine",
    )(unit_dst, used_rows, ys, lists, h, x_all, mods, fnw, sh_gate, sh_up, sh_down)


def moe_block(h2, logits, router_bias, x_mid, mods, fnw, w_gate, w_up, w_down,
              sh_gate, sh_up, sh_down, *, layer, t_lat, seq, final_norm):
    n_tiles = h2.shape[0] // MOE_TILE
    lists, pc = router(logits, router_bias)
    unit_dst, unit_src, pad_dst, used_rows, tile_expert, n_row_tiles, range_end_tile = moe_plan(
        pc[:, :, 0].astype(jnp.int32), n_tiles)
    xs = moe_dispatch(h2, lists, unit_dst, pad_dst, used_rows, n_tiles=n_tiles)
    ys = moe_experts(xs, w_gate, w_up, w_down, layer, tile_expert, n_row_tiles, range_end_tile)
    return moe_combine(ys, lists, h2, x_mid, mods, fnw, sh_gate.astype(BF16), sh_up.astype(BF16),
                       sh_down.astype(BF16), unit_src, used_rows, n_tiles=n_tiles, t_lat=t_lat,
                       seq=seq, final_norm=final_norm)


def _rope_tables(seq):
    half = RET_DK // 2
    pos = jnp.arange(seq, dtype=jnp.int32)
    row = (pos // GRID_W).astype(F32)
    col = (pos % GRID_W).astype(F32)
    inv = ROPE_BASE ** (-jnp.arange(0, half, 2, dtype=F32) / half)
    a_row, a_col = row[:, None] * inv[None, :], col[:, None] * inv[None, :]
    cos_h = jnp.concatenate([jnp.cos(a_row)] * 2 + [jnp.cos(a_col)] * 2, axis=-1)
    sin_h = jnp.concatenate([-jnp.sin(a_row), jnp.sin(a_row), -jnp.sin(a_col), jnp.sin(a_col)], -1)
    reps = LANES // RET_DK
    return jnp.tile(cos_h, (1, reps)), jnp.tile(sin_h, (1, reps))


def _rope_partner_columns():
    quarter = RET_DK // 4
    idx = jnp.arange(2 * RET_HEADS * RET_DK, dtype=jnp.int32)
    within = idx % (2 * quarter)
    return jnp.where(within < quarter, idx + quarter, idx - quarter)


def kernel(x, c, ctx, c_ctx, ada_w, ada_b, norm1_w, norm2_w, ev_w_in, ev_short_w, ev_short_b, ev_filt_w1, ev_filt_b1, ev_filt_w2, ev_filt_b2, ev_filt_w3, ev_filt_b3, ev_filt_w4, ev_filt_freq, ev_hyena_bias, ev_w_out, od_w_in, od_gate_w1_f, od_gate_w2_f, od_gate_b_f, od_gate_w1_b, od_gate_w2_b, od_gate_b_b, od_norm_w, od_w_out, router_w, router_bias, exp_w_gate, exp_w_up, exp_w_down, sh_w_gate, sh_w_up, sh_w_down, final_norm_w):
    batch, seq, d = x.shape
    ctx_len = ctx.shape[1]
    depth = ada_w.shape[0]
    t_lat, t_ctx = batch * seq, batch * ctx_len
    t_all = t_lat + t_ctx
    tm = 1024

    x_all = None
    x_rows = (x.reshape(t_lat, d), ctx.reshape(t_ctx, d), 0)
    cond8 = jnp.concatenate([c_ctx[None, :], c, jnp.zeros((8 - 1 - batch, d), F32)], axis=0)
    mods_all = adaln_rows(cond8, ada_w, ada_b).reshape(depth, 8, 6, d)

    for i in range(depth):
        last = i == depth - 1
        j = i // 2
        mods = mods_all[i]
        nw1, nw2 = norm1_w[i][None, :], norm2_w[i][None, :]
        if i % 2 == 0:
            qk = 2 * RET_HEADS * RET_DK
            k_scale = jnp.concatenate([jnp.ones((qk // 2,), F32),
                                       jnp.full((qk // 2,), RET_DK ** -0.5, F32)])
            w_in = ev_w_in[j]
            w_qk = w_in[:, :qk] * k_scale
            w_ext = jnp.concatenate([w_qk, w_in[:, qk:], w_qk[:, _rope_partner_columns()]],
                                    axis=1).astype(BF16)
            cos_t, sin_t = _rope_tables(seq)
            proj = inproj_even(x_rows, mods, nw1, w_ext, cos_t, sin_t, t_lat=t_lat, t_all=t_all,
                               seq=seq, tm=tm)
            log_g = [math.log1p(-2.0 ** (-5.0 - h)) for h in range(RET_HEADS)]
            mix_a = bidir_scan(proj, None, None, jnp.ones((1, RET_DV), F32),
                               batch=batch, seq=seq, ctx_len=ctx_len, heads=RET_HEADS, dk=RET_DK,
                               dv=RET_DV, chunk=RET_CHUNK, q_blk=0, k_blk=1, v_blk=1, g_blk=2,
                               log_decay_f=log_g, log_decay_b=log_g[::-1])
            hp = dict(short_w=ev_short_w[j], short_b=ev_short_b[j][None, :],
                      w1=jnp.pad(ev_filt_w1[j], ((0, LANES - HYENA_EMB), (0, 0))),
                      b1=ev_filt_b1[j][None, :], w2=ev_filt_w2[j], b2=ev_filt_b2[j][None, :],
                      w3=ev_filt_w3[j], b3=ev_filt_b3[j][None, :], w4=ev_filt_w4[j],
                      freq=ev_filt_freq[j][None, :], bias=ev_hyena_bias[j][None, :])
            hy_lat = hyena_long_conv(proj, hp, row0=0, batch=batch, seq_len=seq, z_blk=1)
            hy_ctx = hyena_long_conv(proj, hp, row0=t_lat, batch=batch, seq_len=ctx_len, z_blk=1)
            mix_b = (hy_lat, hy_ctx)
            a_blk, b_blk = 0, 0
            w_out = ev_w_out[j].astype(BF16)
        else:
            kd = GLA_HEADS * GLA_DK
            pad_cols = LANES - 2 * GLA_RANK
            w_ext = jnp.concatenate([od_w_in[j], od_gate_w1_f[j], od_gate_w1_b[j],
                                     jnp.zeros((d, pad_cols), F32)], axis=1).astype(BF16)
            w2 = jnp.concatenate(
                [jnp.pad(od_gate_w2_f[j], ((0, LANES - GLA_RANK), (0, 0))),
                 jnp.pad(od_gate_w2_b[j], ((GLA_RANK, LANES - 2 * GLA_RANK), (0, 0)))], axis=1)
            w2_hi = w2.astype(BF16)
            w2_split = jnp.concatenate([w2_hi, w2_hi, (w2 - w2_hi.astype(F32)).astype(BF16)], axis=0)
            b2 = jnp.concatenate([od_gate_b_f[j], od_gate_b_b[j]])[None, :]
            if x_all is None:
                x_all = jnp.concatenate(x_rows[:2], axis=0)
            proj, la_f, la_b = inproj_odd(x_all, mods, nw1, w_ext, w2_split, b2,
                                          t_lat=t_lat, seq=seq, tm=tm)
            mix_a = bidir_scan(proj, la_f, la_b, od_norm_w[j][None, :],
                               batch=batch, seq=seq, ctx_len=ctx_len, heads=GLA_HEADS, dk=GLA_DK,
                               dv=GLA_DV, chunk=GLA_CHUNK, q_blk=0, k_blk=1, v_blk=1, g_blk=2)
            mix_b = mix_a
            a_blk, b_blk = 0, 1
            w_out = od_w_out[j].astype(BF16)

        n_rows = t_lat if last else t_all
        rw_t = router_w[i].T
        rw_hi = rw_t.astype(BF16)
        rw_split = jnp.stack([rw_hi, (rw_t - rw_hi.astype(F32)).astype(BF16)])
        x_mid, h2, logits = outproj(mix_a, a_blk, mix_b, b_blk, x_rows, mods, nw2, w_out,
                                    rw_split, n_rows=n_rows, t_lat=t_lat, seq=seq, tm=tm)
        x_all = moe_block(h2, logits, router_bias[i], x_mid, mods, final_norm_w[None, :],
                          exp_w_gate, exp_w_up, exp_w_down,
                          sh_w_gate[i], sh_w_up[i], sh_w_down[i],
                          layer=i, t_lat=t_lat, seq=seq, final_norm=last)
        x_rows = (x_all, x_all, t_lat)
    return x_all[:t_lat].reshape(batch, seq, d)
```

```python
import functools
import math

import jax
import jax.numpy as jnp
from jax import lax
from jax.experimental import pallas as pl
from jax.experimental.pallas import tpu as pltpu

F32 = jnp.float32
BF16 = jnp.bfloat16
HIGHEST = lax.Precision.HIGHEST

D_MODEL = 1024
GRID_W = 64
NORM_EPS = 1e-6
RET_HEADS, RET_DK, RET_DV, RET_CHUNK = 4, 64, 128, 128
ROPE_BASE = 10000.0
HYENA_CH, HYENA_EMB = 512, 33
HYENA_FAST_DECAY, HYENA_SLOW_DECAY, HYENA_TARGET = 0.3, 1.5, 1e-2
GLA_HEADS, GLA_DK, GLA_DV, GLA_RANK, GLA_TAU, GLA_CHUNK = 4, 128, 256, 16, 16.0, 64
N_EXPERTS, TOP_K, N_GROUPS, TOPK_GROUPS = 64, 8, 8, 4
GROUP_SIZE = N_EXPERTS // N_GROUPS
EXPERT_FF = 256
ROUTED_SCALE = 2.5

LANES = 128
SCAN_ROWS = 256
SCAN_LAT_ROWS = 1024
MOE_TILE = 256
ROW_UNIT = 16
EXPERT_ROWS = 1024
SLOT_CHUNK = 512
SUB_SLOTS = 256
SLOTS = -(-(MOE_TILE * TOP_K + N_EXPERTS * (ROW_UNIT - 1) + ROW_UNIT) // SLOT_CHUNK) * SLOT_CHUNK
N_UNITS = SLOTS // ROW_UNIT
PAD_UNITS = EXPERT_ROWS // ROW_UNIT
PAD_BITS = (PAD_UNITS - 1).bit_length()
VMEM_LIMIT = 56 * 1024 * 1024


def _cparams(sem):
    return pltpu.CompilerParams(dimension_semantics=sem, vmem_limit_bytes=VMEM_LIMIT)


def _rms(x):
    return x * lax.rsqrt(jnp.mean(x * x, axis=-1, keepdims=True) + NORM_EPS)


def _silu(x):
    return x * jax.nn.sigmoid(x)


def _adaln_kernel(c_ref, w_ref, b_ref, o_ref):
    s = _silu(c_ref[...])
    w = w_ref[0]
    s_hi, w_hi = s.astype(BF16), w.astype(BF16)
    s_lo, w_lo = (s - s_hi.astype(F32)).astype(BF16), (w - w_hi.astype(F32)).astype(BF16)
    dot = lambda a, b: jnp.dot(a, b, preferred_element_type=F32)
    o_ref[0] = dot(s_hi, w_hi) + (dot(s_lo, w_hi) + dot(s_hi, w_lo)) + b_ref[0]


def adaln_rows(cond8, ada_w, ada_b):
    depth, d, n = ada_w.shape
    tn = 512
    return pl.pallas_call(
        _adaln_kernel,
        grid=(depth, n // tn),
        in_specs=[pl.BlockSpec((8, d), lambda l, j: (0, 0)),
                  pl.BlockSpec((1, d, tn), lambda l, j: (l, 0, j)),
                  pl.BlockSpec((1, 1, tn), lambda l, j: (l, 0, j))],
        out_specs=pl.BlockSpec((1, 8, tn), lambda l, j: (l, 0, j)),
        out_shape=jax.ShapeDtypeStruct((depth, 8, n), F32),
        compiler_params=_cparams(("parallel", "parallel")),
        name="adaln_rows",
    )(cond8, ada_w, ada_b.reshape(depth, 1, n))


def _modulated_norm(x, mod, nw, shift_row, scale_row):
    return _rms(x) * nw * (1.0 + mod[scale_row:scale_row + 1]) + mod[shift_row:shift_row + 1]


def _split_row_specs(tm, width, n_lat_tiles, col_blk=0, ctx_row0=0):
    ctx_blk0 = ctx_row0 // tm
    return [pl.BlockSpec((tm, width), lambda i: (jnp.minimum(i, n_lat_tiles - 1), col_blk)),
            pl.BlockSpec((tm, width),
                         lambda i: (ctx_blk0 + jnp.maximum(i - n_lat_tiles, 0), col_blk))]


def _pick_rows(lat_ref, ctx_ref, n_lat_tiles):
    return jnp.where(pl.program_id(0) < n_lat_tiles, lat_ref[...], ctx_ref[...])


def _inproj_even_kernel(xl_ref, xc_ref, mod_ref, nw_ref, w_ref, cos_ref, sin_ref, o_ref, *,
                        n_lat_tiles):
    x = _pick_rows(xl_ref, xc_ref, n_lat_tiles)
    h = _modulated_norm(x, mod_ref[0], nw_ref[...], 0, 1).astype(BF16)
    n_main = o_ref.shape[1]
    qk = jnp.dot(h, w_ref[:, 0:512], preferred_element_type=F32)
    qk_sw = jnp.dot(h, w_ref[:, n_main:n_main + 512], preferred_element_type=F32)
    is_lat = pl.program_id(0) < n_lat_tiles
    cos = jnp.where(is_lat, cos_ref[...], 1.0)
    sin = jnp.where(is_lat, sin_ref[...], 0.0)
    for c0 in range(0, 512, LANES):
        o_ref[:, c0:c0 + LANES] = (qk[:, c0:c0 + LANES] * cos
                                   + qk_sw[:, c0:c0 + LANES] * sin).astype(BF16)
    for c0 in range(512, n_main, 512):
        o_ref[:, c0:c0 + 512] = jnp.dot(h, w_ref[:, c0:c0 + 512],
                                        preferred_element_type=F32).astype(BF16)


def _inproj_odd_kernel(x_ref, mod_ref, nw_ref, w_ref, w2_ref, b2_ref, o_ref, laf_ref, lab_ref):
    h = _modulated_norm(x_ref[...], mod_ref[0], nw_ref[...], 0, 1).astype(BF16)
    n_main = o_ref.shape[1]
    kd = GLA_HEADS * GLA_DK
    q = jnp.dot(h, w_ref[:, 0:kd], preferred_element_type=F32)
    o_ref[:, 0:kd] = (q * (GLA_DK ** -0.5)).astype(BF16)
    for c0 in range(kd, n_main, 512):
        o_ref[:, c0:c0 + 512] = jnp.dot(h, w_ref[:, c0:c0 + 512],
                                        preferred_element_type=F32).astype(BF16)
    low = jnp.dot(h, w_ref[:, n_main:n_main + LANES], preferred_element_type=F32)

    low_hi = low.astype(BF16)
    low_lo = (low - low_hi.astype(F32)).astype(BF16)
    z = jnp.dot(jnp.concatenate([low_hi, low_lo, low_hi], axis=1), w2_ref[...],
                preferred_element_type=F32) + b2_ref[...]
    la = (jnp.minimum(z, 0.0) - jnp.log(1.0 + jnp.exp(-jnp.abs(z)))) * (1.0 / GLA_TAU)
    laf_ref[...] = la[:, 0:kd]
    lab_ref[...] = la[:, kd:2 * kd]


def _seg_of_tile(i, tm, t_lat, seq):
    return jnp.where(i < t_lat // tm, 1 + (i * tm) // seq, 0)


def inproj_even(x_rows, mods, nw, w_ext, cos_t, sin_t, *, t_lat, t_all, seq, tm=512):
    x_lat, x_ctx, ctx_row0 = x_rows
    d = x_lat.shape[1]
    n_ext = w_ext.shape[1]
    n_main = n_ext - 512
    n_lat_tiles, pos_tiles = t_lat // tm, seq // tm

    def pos_map(i):
        return (jnp.where(i < n_lat_tiles, i % pos_tiles, 0), 0)

    return pl.pallas_call(
        functools.partial(_inproj_even_kernel, n_lat_tiles=n_lat_tiles),
        grid=(t_all // tm,),
        in_specs=_split_row_specs(tm, d, n_lat_tiles, ctx_row0=ctx_row0) + [
                  pl.BlockSpec((1, 6, d), lambda i: (_seg_of_tile(i, tm, t_lat, seq), 0, 0)),
                  pl.BlockSpec((1, d), lambda i: (0, 0)),
                  pl.BlockSpec((d, n_ext), lambda i: (0, 0)),
                  pl.BlockSpec((tm, LANES), pos_map),
                  pl.BlockSpec((tm, LANES), pos_map)],
        out_specs=pl.BlockSpec((tm, n_main), lambda i: (i, 0)),
        out_shape=jax.ShapeDtypeStruct((t_all, n_main), BF16),
        compiler_params=_cparams(("parallel",)),
        name="inproj_even",
    )(x_lat, x_ctx, mods, nw, w_ext, cos_t, sin_t)


def inproj_odd(x_all, mods, nw, w_ext, w2, b2, *, t_lat, seq, tm=512):
    t_all, d = x_all.shape
    n_main = w_ext.shape[1] - LANES
    kd = GLA_HEADS * GLA_DK
    full = lambda shape: pl.BlockSpec(shape, lambda i: (0,) * len(shape))
    return pl.pallas_call(
        _inproj_odd_kernel,
        grid=(t_all // tm,),
        in_specs=[pl.BlockSpec((tm, d), lambda i: (i, 0)),
                  pl.BlockSpec((1, 6, d), lambda i: (_seg_of_tile(i, tm, t_lat, seq), 0, 0)),
                  full((1, d)), full(w_ext.shape), full(w2.shape), full(b2.shape)],
        out_specs=[pl.BlockSpec((tm, n_main), lambda i: (i, 0)),
                   pl.BlockSpec((tm, kd), lambda i: (i, 0)),
                   pl.BlockSpec((tm, kd), lambda i: (i, 0))],
        out_shape=[jax.ShapeDtypeStruct((t_all, n_main), BF16),
                   jax.ShapeDtypeStruct((t_all, kd), F32),
                   jax.ShapeDtypeStruct((t_all, kd), F32)],
        compiler_params=_cparams(("parallel",)),
        name="inproj_odd",
    )(x_all, mods, nw, w_ext, w2, b2)


def _scan_kernel(*refs, heads, dk, dv, chunk, gated, reverse, log_decay, sub_blocks):
    n_in = 3 + int(gated) + (2 if reverse else 0)
    ctx_in, lat_in = refs[:n_in], refs[n_in:2 * n_in]
    rest = list(refs[2 * n_in:])
    nw_ref = rest.pop(0) if reverse else None
    out_lat, out_ctx, state_ref = rest[0], rest[1], rest[2]
    dec_ref = None if gated else rest[3]

    hpg = LANES // dk
    groups = heads // hpg
    rows = SCAN_ROWS
    n_chunks = rows // chunk
    j = pl.program_id(1)

    row_i = lax.broadcasted_iota(jnp.int32, (rows, rows), 0)
    col_i = lax.broadcasted_iota(jnp.int32, (rows, rows), 1)
    same_chunk = (row_i // chunk) == (col_i // chunk)
    keep = jnp.logical_and(same_chunk, (col_i >= row_i) if reverse else (row_i >= col_i))

    @pl.when(j == 0)
    def _init():
        state_ref[...] = jnp.zeros_like(state_ref)
        if not gated:
            pos = lax.broadcasted_iota(jnp.int32, (rows, LANES), 0) % chunk
            steps = ((chunk - pos) if reverse else (pos + 1)).astype(F32)
            lane = lax.broadcasted_iota(jnp.int32, (rows, LANES), 1)
            for g in range(groups):
                lg = jnp.zeros((rows, LANES), F32)
                for a in range(hpg):
                    lg = jnp.where(lane // dk == a, log_decay[g * hpg + a], lg)
                logb = steps * lg
                b_end = float(chunk) * lg
                dec_ref[g, 0] = jnp.exp(logb)
                dec_ref[g, 1] = jnp.exp(-logb)
                dec_ref[g, 2] = jnp.exp(b_end - logb)
                dec_ref[g, 3] = jnp.exp(b_end)

    lane1 = lax.broadcasted_iota(jnp.int32, (1, LANES), 1)
    order = range(n_chunks - 1, -1, -1) if reverse else range(n_chunks)
    end_row = lambda c: c * chunk if reverse else (c + 1) * chunk - 1

    def scan_rows(in_refs, out_ref, r0):
        q_ref, k_ref, v_ref = in_refs[:3]
        blk = slice(r0, r0 + rows)
        if reverse:
            oprev_ref, gate_ref = in_refs[-2:]
        if gated:
            tri = jnp.where(keep, 1.0, 0.0).astype(BF16)
            la = in_refs[3][blk, :]
            la_hi = la.astype(BF16)
            la_lo = (la - la_hi.astype(F32)).astype(BF16)
            logb_all = (jnp.dot(tri, la_hi, preferred_element_type=F32)
                        + jnp.dot(tri, la_lo, preferred_element_type=F32))
        for g in range(groups):
            ksl = slice(g * LANES, (g + 1) * LANES)
            qg = q_ref[blk, ksl].astype(F32)
            kg = k_ref[blk, ksl].astype(F32)
            if gated:
                logb = logb_all[:, ksl]
                ends = [logb[end_row(c):end_row(c) + 1] for c in range(n_chunks)]
                b_end = jnp.concatenate([jnp.broadcast_to(e, (chunk, LANES)) for e in ends], axis=0)
                e_q, e_k, e_s = jnp.exp(logb), jnp.exp(-logb), jnp.exp(b_end - logb)
                e_e = [jnp.exp(e) for e in ends]
            else:
                e_q, e_k, e_s = dec_ref[g, 0], dec_ref[g, 1], dec_ref[g, 2]
                e_e = [dec_ref[g, 3][0:1]] * n_chunks
            qd = qg * e_q
            kd_ = (kg * e_k).astype(BF16)
            ks = (kg * e_s).astype(BF16)
            for a in range(hpg):
                h = g * hpg + a
                qa = (jnp.where(lane1 // dk == a, qd, 0.0) if hpg > 1 else qd).astype(BF16)
                vh = v_ref[blk, h * dv:(h + 1) * dv]
                s = lax.dot_general(qa, kd_, (((1,), (1,)), ((), ())), preferred_element_type=F32)
                o_intra = jnp.dot(jnp.where(keep, s, 0.0).astype(BF16), vh,
                                  preferred_element_type=F32)
                st = state_ref[h]
                for c in order:
                    rs = slice(c * chunk, (c + 1) * chunk)
                    o = o_intra[rs] + lax.dot_general(qa[rs], st.astype(BF16),
                                                      (((1,), (1,)), ((), ())),
                                                      preferred_element_type=F32)
                    st = e_e[c] * st + lax.dot_general(vh[rs], ks[rs], (((0,), (0,)), ((), ())),
                                                       preferred_element_type=F32)
                    osl = (slice(r0 + c * chunk, r0 + (c + 1) * chunk), slice(h * dv, (h + 1) * dv))
                    if reverse:
                        o = _rms(o + oprev_ref[osl]) * nw_ref[...]
                        out_ref[osl] = (o * _silu(gate_ref[osl].astype(F32))).astype(BF16)
                    else:
                        out_ref[osl] = o
                state_ref[h] = st

    @pl.when(j == 0)
    def _():
        scan_rows(ctx_in, out_ctx, 0)

    @pl.when(j > 0)
    def _():
        for sb in (range(sub_blocks - 1, -1, -1) if reverse else range(sub_blocks)):
            scan_rows(lat_in, out_lat, sb * rows)


def bidir_scan(proj, la_f, la_b, norm_w, *, batch, seq, ctx_len, heads, dk, dv, chunk,
               q_blk, k_blk, v_blk, g_blk, log_decay_f=None, log_decay_b=None):
    gated = la_f is not None
    hk, hv = heads * dk, heads * dv
    rb, lb = SCAN_ROWS, SCAN_LAT_ROWS
    lat_blocks = seq // lb
    ctx_base = (batch * seq) // rb
    assert ctx_len == rb

    lat_fwd = lambda b, j: b * lat_blocks + jnp.maximum(j - 1, 0)
    lat_bwd = lambda b, j: b * lat_blocks + lat_blocks - 1 - jnp.maximum(j - 1, 0)

    prev = None
    for reverse, lat_of, la, ld in ((False, lat_fwd, la_f, log_decay_f),
                                    (True, lat_bwd, la_b, log_decay_b)):
        ctx_spec = lambda w, cb, base=ctx_base: pl.BlockSpec(
            (rb, w), lambda b, j, cb=cb: (base + b, cb))
        lat_spec = lambda w, cb: pl.BlockSpec((lb, w), lambda b, j, cb=cb: (lat_of(b, j), cb))
        in_specs, args = [], []
        for spec, is_ctx in ((ctx_spec, True), (lat_spec, False)):
            in_specs += [spec(hk, q_blk), spec(hk, k_blk), spec(hv, v_blk)]
            args += [proj, proj, proj]
            if gated:
                in_specs.append(spec(hk, 0))
                args.append(la)
            if reverse:
                in_specs += [ctx_spec(hv, 0, base=0) if is_ctx else spec(hv, 0), spec(hv, g_blk)]
                args += [prev[1] if is_ctx else prev[0], proj]
        if reverse:
            in_specs.append(pl.BlockSpec((1, dv), lambda b, j: (0, 0)))
            args.append(norm_w)
        scratch = [pltpu.VMEM((heads, dv, LANES), F32)]
        if not gated:
            scratch.append(pltpu.VMEM((hk // LANES, 4, rb, LANES), F32))
        kern = functools.partial(_scan_kernel, heads=heads, dk=dk, dv=dv, chunk=chunk,
                                 gated=gated, reverse=reverse, log_decay=ld, sub_blocks=lb // rb)
        out_dtype = BF16 if reverse else F32
        prev = pl.pallas_call(
            kern,
            grid=(batch, lat_blocks + 1),
            in_specs=in_specs,
            out_specs=[lat_spec(hv, 0), ctx_spec(hv, 0, base=0)],
            out_shape=[jax.ShapeDtypeStruct((batch * seq, hv), out_dtype),
                       jax.ShapeDtypeStruct((batch * ctx_len, hv), out_dtype)],
            scratch_shapes=scratch,
            compiler_params=_cparams(("parallel", "arbitrary")),
            name="scan_bwd" if reverse else "scan_fwd",
        )(*args)
    return tuple(prev)


def _shortconv_kernel(z_ref, zp_ref, zn_ref, w_ref, b_ref, u_ref, x0_ref, *, tiles_per_seq):
    i = pl.program_id(0)
    tm = z_ref.shape[0]
    z = z_ref[...].astype(F32)
    first = (i % tiles_per_seq) == 0
    last = (i % tiles_per_seq) == tiles_per_seq - 1
    halo = zp_ref.shape[0]
    prev_row = jnp.where(first, 0.0, zp_ref[halo - 1:halo, :].astype(F32))
    next_row = jnp.where(last, 0.0, zn_ref[0:1, :].astype(F32))
    row = lax.broadcasted_iota(jnp.int32, z.shape, 0)
    z_prev = jnp.where(row == 0, prev_row, pltpu.roll(z, 1, 0))
    z_next = jnp.where(row == tm - 1, next_row, pltpu.roll(z, tm - 1, 0))
    y = w_ref[0:1] * z_prev + w_ref[1:2] * z + w_ref[2:3] * z_next + b_ref[...]
    c = HYENA_CH
    x0_ref[...] = y[:, 0:c].astype(BF16)
    u_ref[...] = (y[:, c:2 * c] * y[:, 2 * c:3 * c]).astype(BF16)


def hyena_shortconv(proj, short_w, short_b, *, row0, batch, seq_len, z_blk, tm=256):
    halo = 16
    tiles_per_seq = seq_len // tm
    n_tiles = batch * tiles_per_seq
    t0, h_per_tile = row0 // tm, tm // halo
    nz = 3 * HYENA_CH
    n_halo_blocks = proj.shape[0] // halo
    out_map = lambda i: (i % tiles_per_seq, i // tiles_per_seq)
    kern = functools.partial(_shortconv_kernel, tiles_per_seq=tiles_per_seq)
    return pl.pallas_call(
        kern,
        grid=(n_tiles,),
        in_specs=[pl.BlockSpec((tm, nz), lambda i: (t0 + i, z_blk)),
                  pl.BlockSpec((halo, nz),
                               lambda i: (jnp.maximum((t0 + i) * h_per_tile - 1, 0), z_blk)),
                  pl.BlockSpec((halo, nz),
                               lambda i: (jnp.minimum((t0 + i + 1) * h_per_tile,
                                                      n_halo_blocks - 1), z_blk)),
                  pl.BlockSpec((3, nz), lambda i: (0, 0)),
                  pl.BlockSpec((1, nz), lambda i: (0, 0))],
        out_specs=[pl.BlockSpec((tm, HYENA_CH), out_map), pl.BlockSpec((tm, HYENA_CH), out_map)],
        out_shape=[jax.ShapeDtypeStruct((seq_len, batch * HYENA_CH), BF16)] * 2,
        compiler_params=_cparams(("parallel",)),
        name="hyena_shortconv",
    )(proj, proj, proj, short_w, short_b)


def _filter_kernel(feat_ref, w1_ref, b1_ref, w2_ref, b2_ref, w3_ref, b3_ref, w4_ref, fq_ref,
                   o_ref, *, seq_len):
    i = pl.program_id(0)
    tl = feat_ref.shape[1]
    fq = fq_ref[...]
    dot = lambda a, b: jnp.dot(a, b, preferred_element_type=F32, precision=HIGHEST)
    h = jnp.sin(fq * (dot(w1_ref[...], feat_ref[...]) + b1_ref[...]))
    h = jnp.sin(fq * (dot(w2_ref[...], h) + b2_ref[...]))
    h = jnp.sin(fq * (dot(w3_ref[...], h) + b3_ref[...]))
    h = lax.dot_general(h, w4_ref[...], (((0,), (0,)), ((), ())),
                        preferred_element_type=F32, precision=HIGHEST)
    c = HYENA_CH
    max_decay = math.log(HYENA_TARGET) / HYENA_FAST_DECAY
    min_decay = math.log(HYENA_TARGET) / HYENA_SLOW_DECAY
    ch = lax.broadcasted_iota(jnp.int32, (tl, c), 1).astype(F32)
    deltas = min_decay + ch * ((max_decay - min_decay) / (c - 1))
    row = lax.broadcasted_iota(jnp.int32, (tl, c), 0) + i * tl
    t = row.astype(F32) * (1.0 / (seq_len - 1))
    window = jnp.exp(-t * jnp.abs(deltas))
    h_f = h[:, 0:c] * window
    h_b = jnp.where(row == 0, 0.0, h[:, c:2 * c] * window)
    o_ref[:, 0:c] = (h_f + h_b).astype(BF16)
    o_ref[:, c:2 * c] = (h_f - h_b).astype(BF16)


def hyena_filter_pair(feats, w1p, b1, w2, b2, w3, b3, w4, fq):
    seq_len = feats.shape[0]
    tl = min(2048, seq_len)
    full = lambda a: pl.BlockSpec(a.shape, lambda i: (0,) * a.ndim)
    kern = functools.partial(_filter_kernel, seq_len=seq_len)
    feats, w1p, w2, w3 = feats.T, w1p.T, w2.T, w3.T
    b1, b2, b3, fq = b1.T, b2.T, b3.T, fq.T
    return pl.pallas_call(
        kern,
        grid=(seq_len // tl,),
        in_specs=[pl.BlockSpec((feats.shape[0], tl), lambda i: (0, i)),
                  full(w1p), full(b1), full(w2), full(b2), full(w3), full(b3), full(w4), full(fq)],
        out_specs=pl.BlockSpec((tl, 2 * HYENA_CH), lambda i: (i, 0)),
        out_shape=jax.ShapeDtypeStruct((seq_len, 2 * HYENA_CH), BF16),
        compiler_params=_cparams(("parallel",)),
        name="hyena_filter",
    )(feats, w1p, b1, w2, b2, w3, b3, w4, fq)


def _filter_spectrum_kernel(c_ref, s_ref, f_ref, hc_ref, hs_ref, kre_ref, kim_ref, *, scale):
    ch = HYENA_CH
    f = f_ref[...]
    a = jnp.dot(c_ref[...], f, preferred_element_type=F32)
    b = jnp.dot(s_ref[...], f, preferred_element_type=F32)
    hc, hs = hc_ref[...], hs_ref[...]
    kre_ref[...] = (hc * a[:, 0:ch] + hs * b[:, 0:ch]) * scale
    kim_ref[...] = (hs * a[:, ch:2 * ch] - hc * b[:, ch:2 * ch]) * scale


def filter_spectrum(cs, ss, filt, half_cos, half_sin):
    n = cs.shape[0]
    c = HYENA_CH
    tf = min(512, n)
    kern = functools.partial(_filter_spectrum_kernel, scale=1.0 / n)
    return pl.pallas_call(
        kern,
        grid=(n // tf,),
        in_specs=[pl.BlockSpec((tf, n), lambda f: (f, 0)), pl.BlockSpec((tf, n), lambda f: (f, 0)),
                  pl.BlockSpec((n, 2 * c), lambda f: (0, 0)),
                  pl.BlockSpec((tf, 1), lambda f: (f, 0)), pl.BlockSpec((tf, 1), lambda f: (f, 0))],
        out_specs=[pl.BlockSpec((tf, c), lambda f: (f, 0))] * 2,
        out_shape=[jax.ShapeDtypeStruct((n, c), F32)] * 2,
        compiler_params=_cparams(("parallel",)),
        name="hyena_filter_spectrum",
    )(cs, ss, filt, half_cos, half_sin)


def _dft_fwd_kernel(c_ref, s_ref, u_ref, kre_ref, kim_ref, yre_ref, yim_ref):
    u = u_ref[...]
    a = jnp.dot(c_ref[...], u, preferred_element_type=F32)
    b = jnp.dot(s_ref[...], u, preferred_element_type=F32)
    k_re, k_im = kre_ref[...], kim_ref[...]
    yre_ref[...] = (a * k_re + b * k_im).astype(BF16)
    yim_ref[...] = (a * k_im - b * k_re).astype(BF16)


def dft_forward(cs, ss, u, k_re, k_im, *, batch):
    n = cs.shape[0]
    c = HYENA_CH
    tf = min(512, n)
    return pl.pallas_call(
        _dft_fwd_kernel,
        grid=(batch, n // tf),
        in_specs=[pl.BlockSpec((tf, n), lambda bi, f: (f, 0)),
                  pl.BlockSpec((tf, n), lambda bi, f: (f, 0)),
                  pl.BlockSpec((n, c), lambda bi, f: (0, bi)),
                  pl.BlockSpec((tf, c), lambda bi, f: (f, 0)),
                  pl.BlockSpec((tf, c), lambda bi, f: (f, 0))],
        out_specs=[pl.BlockSpec((tf, c), lambda bi, f: (f, bi))] * 2,
        out_shape=[jax.ShapeDtypeStruct((n, batch * c), BF16)] * 2,
        compiler_params=_cparams(("parallel", "parallel")),
        name="dft_forward",
    )(cs, ss, u, k_re, k_im)


def _dft_inv_kernel(c_ref, s_ref, yre_ref, yim_ref, u_ref, x0_ref, bias_ref, o_ref):
    y = (jnp.dot(c_ref[...], yre_ref[...], preferred_element_type=F32)
         - jnp.dot(s_ref[...], yim_ref[...], preferred_element_type=F32))
    u = u_ref[...].astype(F32)
    o_ref[...] = (x0_ref[...].astype(F32) * (y + u * bias_ref[...])).astype(BF16)


def dft_inverse(cs, ss, yre, yim, u, x0, bias, *, batch):
    n = cs.shape[0]
    c = HYENA_CH
    tt = min(512, n)
    col = lambda bi, t: (0, bi)
    tile = lambda bi, t: (t, bi)
    return pl.pallas_call(
        _dft_inv_kernel,
        grid=(batch, n // tt),
        in_specs=[pl.BlockSpec((tt, n), lambda bi, t: (t, 0)),
                  pl.BlockSpec((tt, n), lambda bi, t: (t, 0)),
                  pl.BlockSpec((n, c), col), pl.BlockSpec((n, c), col),
                  pl.BlockSpec((tt, c), tile), pl.BlockSpec((tt, c), tile),
                  pl.BlockSpec((1, c), lambda bi, t: (0, 0))],
        out_specs=pl.BlockSpec((tt, c), lambda bi, t: (bi * (n // tt) + t, 0)),
        out_shape=jax.ShapeDtypeStruct((batch * n, c), BF16),
        compiler_params=_cparams(("parallel", "parallel")),
        name="dft_inverse",
    )(cs, ss, yre, yim, u, x0, bias)


def _dft_table_kernel(cg_ref, sg_ref, cd_ref, sd_ref, c_ref, s_ref):
    cg, sg, cd, sd = cg_ref[...], sg_ref[...], cd_ref[0], sd_ref[0]
    c_ref[...] = (cg * cd - sg * sd).astype(BF16)
    s_ref[...] = (sg * cd + cg * sd).astype(BF16)


def _shifted_dft_tables(n):
    tf = min(256, n)
    theta = 2.0 * math.pi / (8 * n)
    odd_s = 2 * jnp.arange(n, dtype=jnp.int32)[None, :] + 1
    odd_i = 2 * jnp.arange(tf, dtype=jnp.int32)[:, None] + 1
    gamma = ((odd_i * odd_s) % (8 * n)).astype(F32) * theta
    f0 = jnp.arange(n // tf, dtype=jnp.int32)[:, None] * tf
    delta = ((2 * f0 * odd_s) % (8 * n)).astype(F32)[:, None, :] * theta
    whole = pl.BlockSpec((tf, n), lambda i: (0, 0))
    row = pl.BlockSpec((1, 1, n), lambda i: (i, 0, 0))
    cs, ss = pl.pallas_call(
        _dft_table_kernel,
        grid=(n // tf,),
        in_specs=[whole, whole, row, row],
        out_specs=[pl.BlockSpec((tf, n), lambda i: (i, 0))] * 2,
        out_shape=[jax.ShapeDtypeStruct((n, n), BF16)] * 2,
        compiler_params=_cparams(("parallel",)),
        name="dft_tables",
    )(jnp.cos(gamma), jnp.sin(gamma), jnp.cos(delta), jnp.sin(delta))
    half = (2 * jnp.arange(n, dtype=jnp.int32) + 1).astype(F32)[:, None] * (math.pi / (4 * n))
    return cs, ss, jnp.cos(half), jnp.sin(half)


def _filter_features(seq_len):
    t = jnp.linspace(0.0, 1.0, seq_len, dtype=F32)[:, None]
    bands = (HYENA_EMB - 1) // 2
    ang = 2.0 * math.pi * jnp.arange(seq_len, dtype=F32)[:, None] / seq_len
    fr = jnp.linspace(1e-4, bands - 1, bands, dtype=F32)[None, :]
    feats = jnp.concatenate([t, jnp.cos(fr * ang), -jnp.sin(fr * ang)], axis=-1)
    return jnp.pad(feats, ((0, 0), (0, LANES - HYENA_EMB)))


def hyena_long_conv(proj, hp, *, row0, batch, seq_len, z_blk):
    u, x0 = hyena_shortconv(proj, hp["short_w"], hp["short_b"], row0=row0, batch=batch,
                            seq_len=seq_len, z_blk=z_blk, tm=min(512, seq_len))
    filt = hyena_filter_pair(_filter_features(seq_len), hp["w1"], hp["b1"], hp["w2"], hp["b2"],
                             hp["w3"], hp["b3"], hp["w4"], hp["freq"])
    cs, ss, half_cos, half_sin = _shifted_dft_tables(seq_len)
    k_re, k_im = filter_spectrum(cs, ss, filt, half_cos, half_sin)
    yre, yim = dft_forward(cs, ss, u, k_re, k_im, batch=batch)
    return dft_inverse(cs, ss, yre, yim, u, x0, hp["bias"], batch=batch)


def _outproj_kernel(mal_ref, mac_ref, mbl_ref, mbc_ref, xl_ref, xc_ref, mod_ref, nw_ref, w_ref,
                    rw_ref, xo_ref, h_ref, lt_ref, *, n_lat_tiles):
    half = mal_ref.shape[1]
    pick = functools.partial(_pick_rows, n_lat_tiles=n_lat_tiles)
    y = (jnp.dot(pick(mal_ref, mac_ref), w_ref[0:half], preferred_element_type=F32)
         + jnp.dot(pick(mbl_ref, mbc_ref), w_ref[half:2 * half], preferred_element_type=F32))
    mod = mod_ref[0]
    x = pick(xl_ref, xc_ref) + mod[2:3] * y
    xo_ref[...] = x
    h = _modulated_norm(x, mod, nw_ref[...], 3, 4)
    h_ref[...] = h.astype(BF16)
    h_hi = h.astype(BF16)
    h_lo = (h - h_hi.astype(F32)).astype(BF16)
    nt = lambda a, b: lax.dot_general(a, b, (((1,), (1,)), ((), ())), preferred_element_type=F32)
    lt_ref[...] = nt(rw_ref[0], h_hi) + (nt(rw_ref[0], h_lo) + nt(rw_ref[1], h_hi))


def outproj(mix_a, a_blk, mix_b, b_blk, x_rows, mods, nw, w_out, router_wt, *,
            n_rows, t_lat, seq, tm=512):
    x_lat, x_ctx, ctx_row0 = x_rows
    d = x_lat.shape[1]
    half = d // 2
    n_lat_tiles = t_lat // tm
    return pl.pallas_call(
        functools.partial(_outproj_kernel, n_lat_tiles=n_lat_tiles),
        grid=(n_rows // tm,),
        in_specs=_split_row_specs(tm, half, n_lat_tiles, a_blk)
                 + _split_row_specs(tm, half, n_lat_tiles, b_blk)
                 + _split_row_specs(tm, d, n_lat_tiles, ctx_row0=ctx_row0) + [
                  pl.BlockSpec((1, 6, d), lambda i: (_seg_of_tile(i, tm, t_lat, seq), 0, 0)),
                  pl.BlockSpec((1, d), lambda i: (0, 0)),
                  pl.BlockSpec((d, d), lambda i: (0, 0)),
                  pl.BlockSpec((2, N_EXPERTS, d), lambda i: (0, 0, 0))],
        out_specs=[pl.BlockSpec((tm, d), lambda i: (i, 0)),
                   pl.BlockSpec((tm, d), lambda i: (i, 0)),
                   pl.BlockSpec((N_EXPERTS, tm), lambda i: (0, i))],
        out_shape=[jax.ShapeDtypeStruct((n_rows, d), F32),
                   jax.ShapeDtypeStruct((n_rows, d), BF16),
                   jax.ShapeDtypeStruct((N_EXPERTS, n_rows), F32)],
        compiler_params=_cparams(("parallel",)),
        name="outproj",
    )(*mix_a, *mix_b, x_lat, x_ctx, mods, nw, w_out, router_wt)


def _first_max(x, idx, sentinel):
    m = jnp.max(x, axis=0, keepdims=True)
    first = jnp.min(jnp.where(x == m, idx, sentinel), axis=0, keepdims=True)
    return m, idx == first


def _router_kernel(lt_ref, bias_ref, before_ref, below_ref, lists_ref, pc_ref):
    tn = lt_ref.shape[1]
    scores = jax.nn.sigmoid(lt_ref[...])
    sel = scores + bias_ref[...]
    neg = -jnp.inf
    in_grp = lax.broadcasted_iota(jnp.int32, (GROUP_SIZE, tn), 0)
    gscore = []
    for g in range(N_GROUPS):
        x = sel[g * GROUP_SIZE:(g + 1) * GROUP_SIZE]
        m1, hit = _first_max(x, in_grp, GROUP_SIZE)
        gscore.append(m1 + jnp.max(jnp.where(hit, neg, x), axis=0, keepdims=True))
    rows = []
    for g in range(N_GROUPS):
        beaten = jnp.zeros((1, tn), jnp.int32)
        for o in range(N_GROUPS):
            if o != g:
                wins = (gscore[o] >= gscore[g]) if o < g else (gscore[o] > gscore[g])
                beaten = beaten + wins.astype(jnp.int32)
        keep = jnp.broadcast_to(beaten < TOPK_GROUPS, (GROUP_SIZE, tn))
        rows.append(jnp.where(keep, sel[g * GROUP_SIZE:(g + 1) * GROUP_SIZE], neg))
    cand = jnp.concatenate(rows, axis=0)
    eidx = lax.broadcasted_iota(jnp.int32, cand.shape, 0)
    chosen = jnp.zeros(cand.shape, jnp.bool_)
    hits = []
    for _ in range(TOP_K):
        _, hit = _first_max(cand, eidx, N_EXPERTS)
        hits.append(hit)
        chosen = jnp.logical_or(chosen, hit)
        cand = jnp.where(hit, neg, cand)
    w = jnp.where(chosen, scores, 0.0)
    gates_t = w / jnp.sum(w, axis=0, keepdims=True) * ROUTED_SCALE
    chosen_f = jnp.where(chosen, 1.0, 0.0)
    rank = jnp.dot(chosen_f.astype(BF16), before_ref[...], preferred_element_type=F32)
    count = jnp.sum(chosen_f, axis=1, keepdims=True)
    pc = jnp.floor((count + (ROW_UNIT - 1)) * (1.0 / ROW_UNIT)) * ROW_UNIT
    pc_lanes = jnp.broadcast_to(pc, (N_EXPERTS, LANES))
    start = jnp.dot(below_ref[...], pc_lanes, preferred_element_type=F32,
                    precision=HIGHEST)[:, 0:1]
    slot = start + rank
    pick = lambda hit, v: jnp.sum(jnp.where(hit, v, 0.0), axis=0, keepdims=True)
    lists = jnp.concatenate([pick(h, slot) for h in hits] + [pick(h, gates_t) for h in hits], axis=0)
    lists_ref[...] = lists
    pc_ref[0] = pc_lanes


def router(logits, router_bias):
    e, t = logits.shape
    tn = MOE_TILE
    n_tiles = t // tn
    tok_i = jnp.arange(tn, dtype=jnp.int32)
    before = (tok_i[:, None] < tok_i[None, :]).astype(BF16)
    exp_i = jnp.arange(e, dtype=jnp.int32)
    below = (exp_i[None, :] < exp_i[:, None]).astype(F32)
    const = lambda shape: pl.BlockSpec(shape, lambda i: (0,) * len(shape))
    return pl.pallas_call(
        _router_kernel,
        grid=(n_tiles,),
        in_specs=[pl.BlockSpec((e, tn), lambda i: (0, i)),
                  const((e, 1)), const((tn, tn)), const((e, e))],
        out_specs=[pl.BlockSpec((2 * TOP_K, tn), lambda i: (0, i)),
                   pl.BlockSpec((1, e, LANES), lambda i: (i, 0, 0))],
        out_shape=[jax.ShapeDtypeStruct((2 * TOP_K, t), F32),
                   jax.ShapeDtypeStruct((n_tiles, e, LANES), F32)],
        compiler_params=_cparams(("parallel",)),
        name="router",
    )(logits, router_bias.reshape(e, 1), before, below)


def moe_plan(pc, n_tiles):
    e = pc.shape[1]
    assert n_tiles >= e
    cap_rows = _sorted_capacity(n_tiles)
    run_end = jnp.cumsum(pc, axis=1)
    run_start = run_end - pc
    total = jnp.sum(pc, axis=0)
    total_al = ((total + EXPERT_ROWS - 1) // EXPERT_ROWS) * EXPERT_ROWS
    range_end = jnp.cumsum(total_al)
    range_start = range_end - total_al
    run_row = range_start[None, :] + jnp.cumsum(pc, axis=0) - pc
    unit_row = jnp.arange(N_UNITS, dtype=jnp.int32) * ROW_UNIT
    owner = jnp.sum((run_end[:, None, :] <= unit_row[None, :, None]).astype(jnp.int32), axis=2)
    owner_hot = (owner[:, :, None] == jnp.arange(e, dtype=jnp.int32)).astype(jnp.int32)
    dst = jnp.sum(owner_hot * (run_row - run_start)[:, None, :], axis=2) + unit_row[None, :]
    used = owner < e
    dump = cap_rows + (jnp.arange(n_tiles, dtype=jnp.int32) % 2)[:, None] * DUMP_ROWS
    unit_dst = jnp.where(used, dst, dump + unit_row[None, :]).astype(jnp.int32)
    unit_src = jnp.where(used, dst, 0).astype(jnp.int32)
    pad_dst = jnp.stack([range_start + total, (total_al - total) // ROW_UNIT], axis=1)
    pad_dst = jnp.concatenate([pad_dst, jnp.zeros((n_tiles - e, 2), pad_dst.dtype)],
                              axis=0).astype(jnp.int32)
    tile_end = range_end // EXPERT_ROWS
    tile_expert = jnp.sum((tile_end[None, :] <= jnp.arange(cap_rows // EXPERT_ROWS,
                                                           dtype=jnp.int32)[:, None])
                          .astype(jnp.int32), axis=1)
    tile_expert = jnp.minimum(tile_expert, e - 1).astype(jnp.int32)
    used_rows = run_end[:, -1].astype(jnp.int32)
    tile_end = tile_end.astype(jnp.int32)
    return unit_dst, unit_src, pad_dst, used_rows, tile_expert, tile_end[-1:], tile_end


def _sorted_capacity(n_tiles):
    worst_rows = n_tiles * (MOE_TILE * TOP_K + N_EXPERTS * (ROW_UNIT - 1)) + N_EXPERTS * (EXPERT_ROWS - 1)
    return -(-worst_rows // EXPERT_ROWS) * EXPERT_ROWS


DUMP_ROWS = SLOTS + PAD_UNITS * ROW_UNIT


UNITS_PER_CHUNK = SLOT_CHUNK // ROW_UNIT
MAIN_SLOTS = SLOTS - SLOT_CHUNK
assert MAIN_SLOTS >= MOE_TILE * TOP_K


def _dispatch_kernel(unit_ref, pad_ref, used_ref, h_ref, lists_ref, xs_ref, loc_ref, zero_ref, sems):
    i = pl.program_id(0)
    last = pl.num_programs(0) - 1
    slot = i % 2
    tn = h_ref.shape[0]

    @pl.when(i == 0)
    def _():
        zero_ref[...] = jnp.zeros_like(zero_ref)

    rel_rows = lax.broadcasted_iota(jnp.int32, (SUB_SLOTS, tn), 0).astype(F32).astype(BF16)

    def chunk(c0):
        parts = []
        for b0 in range(c0, c0 + SLOT_CHUNK, SUB_SLOTS):
            p = jnp.zeros((SUB_SLOTS, tn), BF16)
            for k in range(TOP_K):
                hit = rel_rows == (lists_ref[k:k + 1, :] - float(b0)).astype(BF16)
                p = jnp.where(hit, jnp.ones_like(p), p)
            parts.append(p)
        loc_ref[slot, c0:c0 + SLOT_CHUNK, :] = jnp.dot(
            jnp.concatenate(parts, axis=0), h_ref[...],
            preferred_element_type=F32).astype(BF16)
        for u in range(c0 // ROW_UNIT, c0 // ROW_UNIT + UNITS_PER_CHUNK):
            row = pl.multiple_of(unit_ref[i, u], ROW_UNIT)
            pltpu.make_async_copy(loc_ref.at[slot, pl.ds(u * ROW_UNIT, ROW_UNIT)],
                                  xs_ref.at[pl.ds(row, ROW_UNIT)], sems.at[slot]).start()

    for c0 in range(0, MAIN_SLOTS, SLOT_CHUNK):
        chunk(c0)

    @pl.when(used_ref[i] > MAIN_SLOTS)
    def _():
        chunk(MAIN_SLOTS)

    def pad_copies(tile, s, act):
        row, units = pad_ref[tile, 0], pad_ref[tile, 1]
        for bit in range(PAD_BITS - 1, -1, -1):
            size = ROW_UNIT << bit
            has = (jnp.right_shift(units, bit) & 1) == 1

            @pl.when(has)
            def _(row=row):
                act(pltpu.make_async_copy(zero_ref.at[pl.ds(0, size)],
                                          xs_ref.at[pl.ds(pl.multiple_of(row, ROW_UNIT), size)],
                                          sems.at[s]))
            row = row + jnp.where(has, size, 0)

    pad_copies(i, slot, lambda cp: cp.start())

    def wait_all(s, tile):
        pltpu.make_async_copy(loc_ref.at[s, pl.ds(0, MAIN_SLOTS)],
                              xs_ref.at[pl.ds(0, MAIN_SLOTS)], sems.at[s]).wait()

        @pl.when(used_ref[tile] > MAIN_SLOTS)
        def _():
            pltpu.make_async_copy(loc_ref.at[s, pl.ds(MAIN_SLOTS, SLOT_CHUNK)],
                                  xs_ref.at[pl.ds(0, SLOT_CHUNK)], sems.at[s]).wait()
        pad_copies(tile, s, lambda cp: cp.wait())

    @pl.when(i > 0)
    def _():
        wait_all(1 - slot, i - 1)

    @pl.when(i == last)
    def _():
        wait_all(slot, i)


def moe_dispatch(h, lists, unit_dst, pad_dst, used_rows, *, n_tiles):
    d = h.shape[1]
    cap_rows = _sorted_capacity(n_tiles) + 2 * DUMP_ROWS
    return pl.pallas_call(
        _dispatch_kernel,
        grid_spec=pltpu.PrefetchScalarGridSpec(
            num_scalar_prefetch=3,
            grid=(n_tiles,),
            in_specs=[pl.BlockSpec((MOE_TILE, d), lambda i, *_: (i, 0)),
                      pl.BlockSpec((2 * TOP_K, MOE_TILE), lambda i, *_: (0, i))],
            out_specs=pl.BlockSpec(memory_space=pl.ANY),
            scratch_shapes=[pltpu.VMEM((2, SLOTS, d), BF16),
                            pltpu.VMEM((ROW_UNIT << (PAD_BITS - 1), d), BF16),
                            pltpu.SemaphoreType.DMA((2,))]),
        out_shape=jax.ShapeDtypeStruct((cap_rows, d), BF16),
        compiler_params=_cparams(("arbitrary",)),
        name="moe_dispatch",
    )(unit_dst, pad_dst, used_rows, h, lists)


X_RING = 3


def _expert_kernel(te_ref, nt_ref, end_ref, xs_ref, wg_hbm, wu_hbm, wd_hbm, y_ref, x_buf,
                   wg_st, wu_st, wd_st, wg_bf, wu_bf, wd_bf, turn_ref, sems, wsems, *, layer):
    t = pl.program_id(0)
    n_live = nt_ref[0]
    live = t < n_live
    tc = jnp.minimum(t, jnp.maximum(n_live - 1, 0))
    new_expert = jnp.logical_or(t == 0, te_ref[tc] != te_ref[jnp.maximum(tc - 1, 0)])
    rows = x_buf.shape[1]

    def w_copies(e, s):
        return [pltpu.make_async_copy(hbm.at[layer, e], st.at[s], wsems.at[s])
                for hbm, st in ((wg_hbm, wg_st), (wu_hbm, wu_st), (wd_hbm, wd_st))]

    @pl.when(jnp.logical_and(t == 0, live))
    def _():
        turn_ref[0] = 0
        for cp in w_copies(te_ref[0], 0):
            cp.start()

    def x_copy(tile):
        s = tile % X_RING
        return pltpu.make_async_copy(xs_ref.at[pl.ds(pl.multiple_of(tile * rows, rows), rows)],
                                     x_buf.at[s], sems.at[s])

    for k in range(X_RING - 1):
        @pl.when(jnp.logical_and(t == 0, k < n_live))
        def _():
            x_copy(k).start()

    @pl.when(t + (X_RING - 1) < n_live)
    def _():
        x_copy(t + (X_RING - 1)).start()

    @pl.when(jnp.logical_and(live, new_expert))
    def _():
        e = te_ref[tc]
        s = turn_ref[0] % 2
        for cp in w_copies(e, s):
            cp.wait()
        wg_bf[...] = wg_st[s].astype(BF16)
        wu_bf[...] = wu_st[s].astype(BF16)
        wd_bf[...] = wd_st[s].astype(BF16)
        nxt = end_ref[e]

        @pl.when(nxt < n_live)
        def _():
            for cp in w_copies(te_ref[jnp.minimum(nxt, te_ref.shape[0] - 1)], 1 - s):
                cp.start()
        turn_ref[0] = turn_ref[0] + 1

    @pl.when(live)
    def _():
        x_copy(t).wait()
        x = x_buf[t % X_RING]
        act = (_silu(jnp.dot(x, wg_bf[...], preferred_element_type=F32))
               * jnp.dot(x, wu_bf[...], preferred_element_type=F32))
        y_ref[...] = jnp.dot(act.astype(BF16), wd_bf[...], preferred_element_type=F32).astype(BF16)


def moe_experts(xs, w_gate, w_up, w_down, layer, tile_expert, n_row_tiles, range_end_tile):
    d = xs.shape[1]
    cap_rows = xs.shape[0] - 2 * DUMP_ROWS
    ff = w_gate.shape[3]
    row_map = lambda t, te, nt, end: (jnp.minimum(t, jnp.maximum(nt[0] - 1, 0)), 0)
    hbm = pl.BlockSpec(memory_space=pl.ANY)
    return pl.pallas_call(
        functools.partial(_expert_kernel, layer=layer),
        grid_spec=pltpu.PrefetchScalarGridSpec(
            num_scalar_prefetch=3,
            grid=(cap_rows // EXPERT_ROWS,),
            in_specs=[hbm, hbm, hbm, hbm],
            out_specs=pl.BlockSpec((EXPERT_ROWS, d), row_map),
            scratch_shapes=[pltpu.VMEM((X_RING, EXPERT_ROWS, d), BF16),
                            pltpu.VMEM((2, d, ff), F32), pltpu.VMEM((2, d, ff), F32),
                            pltpu.VMEM((2, ff, d), F32),
                            pltpu.VMEM((d, ff), BF16), pltpu.VMEM((d, ff), BF16),
                            pltpu.VMEM((ff, d), BF16), pltpu.SMEM((1,), jnp.int32),
                            pltpu.SemaphoreType.DMA((X_RING,)), pltpu.SemaphoreType.DMA((2,))]),
        out_shape=jax.ShapeDtypeStruct((cap_rows, d), BF16),
        compiler_params=_cparams(("arbitrary",)),
        name="moe_experts",
    )(tile_expert, n_row_tiles, range_end_tile, xs, w_gate, w_up, w_down)


def _combine_kernel(unit_ref, used_ref, ys_ref, lists_ref, h_ref, x_ref, mod_ref, fnw_ref, sg_ref,
                    su_ref, sd_ref, o_ref, loc_ref, acc_ref, sems, *, final_norm):
    i = pl.program_id(0)
    last = pl.num_programs(0) - 1
    slot = i % 2
    tn = h_ref.shape[0]
    def fetch(tile, s):
        def units(u0, u1):
            for u in range(u0, u1):
                row = pl.multiple_of(unit_ref[tile, u], ROW_UNIT)
                pltpu.make_async_copy(ys_ref.at[pl.ds(row, ROW_UNIT)],
                                      loc_ref.at[s, pl.ds(u * ROW_UNIT, ROW_UNIT)], sems.at[s]).start()
        units(0, MAIN_SLOTS // ROW_UNIT)

        @pl.when(used_ref[tile] > MAIN_SLOTS)
        def _():
            units(MAIN_SLOTS // ROW_UNIT, N_UNITS)

    @pl.when(i == 0)
    def _():
        fetch(0, 0)

    @pl.when(i < last)
    def _():
        fetch(i + 1, 1 - slot)

    h = h_ref[...]
    act = (_silu(jnp.dot(h, sg_ref[...], preferred_element_type=F32))
           * jnp.dot(h, su_ref[...], preferred_element_type=F32))
    acc = jnp.dot(act.astype(BF16), sd_ref[...], preferred_element_type=F32)
    tail = used_ref[i] > MAIN_SLOTS
    pltpu.make_async_copy(ys_ref.at[pl.ds(0, MAIN_SLOTS)], loc_ref.at[slot, pl.ds(0, MAIN_SLOTS)],
                          sems.at[slot]).wait()

    @pl.when(tail)
    def _():
        pltpu.make_async_copy(ys_ref.at[pl.ds(0, SLOT_CHUNK)],
                              loc_ref.at[slot, pl.ds(MAIN_SLOTS, SLOT_CHUNK)], sems.at[slot]).wait()

    rel_rows = lax.broadcasted_iota(jnp.int32, (SUB_SLOTS, tn), 0).astype(F32).astype(BF16)
    gate_w = [lists_ref[TOP_K + k:TOP_K + k + 1, :].astype(BF16) for k in range(TOP_K)]

    def chunk(c0):
        parts = []
        for b0 in range(c0, c0 + SLOT_CHUNK, SUB_SLOTS):
            q = jnp.zeros((SUB_SLOTS, tn), BF16)
            for k in range(TOP_K):
                hit = rel_rows == (lists_ref[k:k + 1, :] - float(b0)).astype(BF16)
                q = jnp.where(hit, gate_w[k], q)
            parts.append(q)
        return lax.dot_general(jnp.concatenate(parts, axis=0), loc_ref[slot, c0:c0 + SLOT_CHUNK, :],
                               (((0,), (0,)), ((), ())), preferred_element_type=F32)

    for c0 in range(0, MAIN_SLOTS, SLOT_CHUNK):
        acc = acc + chunk(c0)
    acc_ref[...] = acc

    @pl.when(tail)
    def _():
        acc_ref[...] += chunk(MAIN_SLOTS)

    x = x_ref[...] + mod_ref[0][5:6] * acc_ref[...]
    if final_norm:
        x = _rms(x) * fnw_ref[...]
    o_ref[...] = x


def moe_combine(ys, lists, h, x_all, mods, fnw, sh_gate, sh_up, sh_down, unit_dst, used_rows, *,
                n_tiles, t_lat, seq, final_norm):
    d = h.shape[1]
    tn = MOE_TILE
    const = lambda a: pl.BlockSpec(a.shape, lambda i, *_: (0,) * a.ndim)
    kern = functools.partial(_combine_kernel, final_norm=final_norm)
    return pl.pallas_call(
        kern,
        grid_spec=pltpu.PrefetchScalarGridSpec(
            num_scalar_prefetch=2,
            grid=(n_tiles,),
            in_specs=[pl.BlockSpec(memory_space=pl.ANY),
                      pl.BlockSpec((2 * TOP_K, tn), lambda i, *_: (0, i)),
                      pl.BlockSpec((tn, d), lambda i, *_: (i, 0)),
                      pl.BlockSpec((tn, d), lambda i, *_: (i, 0)),
                      pl.BlockSpec((1, 6, d),
                                   lambda i, *_: (_seg_of_tile(i, tn, t_lat, seq), 0, 0)),
                      const(fnw), const(sh_gate), const(sh_up), const(sh_down)],
            out_specs=pl.BlockSpec((tn, d), lambda i, *_: (i, 0)),
            scratch_shapes=[pltpu.VMEM((2, SLOTS, d), BF16), pltpu.VMEM((tn, d), F32),
                            pltpu.SemaphoreType.DMA((2,))]),
        out_shape=jax.ShapeDtypeStruct((n_tiles * tn, d), F32),
        compiler_params=_cparams(("arbitrary",)),
        name="moe_combine",
    )(unit_dst, used_rows, ys, lists, h, x_all, mods, fnw, sh_gate, sh_up, sh_down)


def moe_block(h2, logits, router_bias, x_mid, mods, fnw, w_gate, w_up, w_down,
              sh_gate, sh_up, sh_down, *, layer, t_lat, seq, final_norm):
    n_tiles = h2.shape[0] // MOE_TILE
    lists, pc = router(logits, router_bias)
    unit_dst, unit_src, pad_dst, used_rows, tile_expert, n_row_tiles, range_end_tile = moe_plan(
        pc[:, :, 0].astype(jnp.int32), n_tiles)
    xs = moe_dispatch(h2, lists, unit_dst, pad_dst, used_rows, n_tiles=n_tiles)
    ys = moe_experts(xs, w_gate, w_up, w_down, layer, tile_expert, n_row_tiles, range_end_tile)
    return moe_combine(ys, lists, h2, x_mid, mods, fnw, sh_gate.astype(BF16), sh_up.astype(BF16),
                       sh_down.astype(BF16), unit_src, used_rows, n_tiles=n_tiles, t_lat=t_lat,
                       seq=seq, final_norm=final_norm)


def _rope_tables(seq):
    half = RET_DK // 2
    pos = jnp.arange(seq, dtype=jnp.int32)
    row = (pos // GRID_W).astype(F32)
    col = (pos % GRID_W).astype(F32)
    inv = ROPE_BASE ** (-jnp.arange(0, half, 2, dtype=F32) / half)
    a_row, a_col = row[:, None] * inv[None, :], col[:, None] * inv[None, :]
    cos_h = jnp.concatenate([jnp.cos(a_row)] * 2 + [jnp.cos(a_col)] * 2, axis=-1)
    sin_h = jnp.concatenate([-jnp.sin(a_row), jnp.sin(a_row), -jnp.sin(a_col), jnp.sin(a_col)], -1)
    reps = LANES // RET_DK
    return jnp.tile(cos_h, (1, reps)), jnp.tile(sin_h, (1, reps))


def _rope_partner_columns():
    quarter = RET_DK // 4
    idx = jnp.arange(2 * RET_HEADS * RET_DK, dtype=jnp.int32)
    within = idx % (2 * quarter)
    return jnp.where(within < quarter, idx + quarter, idx - quarter)


def kernel(x, c, ctx, c_ctx, ada_w, ada_b, norm1_w, norm2_w, ev_w_in, ev_short_w, ev_short_b, ev_filt_w1, ev_filt_b1, ev_filt_w2, ev_filt_b2, ev_filt_w3, ev_filt_b3, ev_filt_w4, ev_filt_freq, ev_hyena_bias, ev_w_out, od_w_in, od_gate_w1_f, od_gate_w2_f, od_gate_b_f, od_gate_w1_b, od_gate_w2_b, od_gate_b_b, od_norm_w, od_w_out, router_w, router_bias, exp_w_gate, exp_w_up, exp_w_down, sh_w_gate, sh_w_up, sh_w_down, final_norm_w):
    batch, seq, d = x.shape
    ctx_len = ctx.shape[1]
    depth = ada_w.shape[0]
    t_lat, t_ctx = batch * seq, batch * ctx_len
    t_all = t_lat + t_ctx
    tm = 1024

    x_all = None
    x_rows = (x.reshape(t_lat, d), ctx.reshape(t_ctx, d), 0)
    cond8 = jnp.concatenate([c_ctx[None, :], c, jnp.zeros((8 - 1 - batch, d), F32)], axis=0)
    mods_all = adaln_rows(cond8, ada_w, ada_b).reshape(depth, 8, 6, d)

    for i in range(depth):
        last = i == depth - 1
        j = i // 2
        mods = mods_all[i]
        nw1, nw2 = norm1_w[i][None, :], norm2_w[i][None, :]
        if i % 2 == 0:
            qk = 2 * RET_HEADS * RET_DK
            k_scale = jnp.concatenate([jnp.ones((qk // 2,), F32),
                                       jnp.full((qk // 2,), RET_DK ** -0.5, F32)])
            w_in = ev_w_in[j]
            w_qk = w_in[:, :qk] * k_scale
            w_ext = jnp.concatenate([w_qk, w_in[:, qk:], w_qk[:, _rope_partner_columns()]],
                                    axis=1).astype(BF16)
            cos_t, sin_t = _rope_tables(seq)
            proj = inproj_even(x_rows, mods, nw1, w_ext, cos_t, sin_t, t_lat=t_lat, t_all=t_all,
                               seq=seq, tm=tm)
            log_g = [math.log1p(-2.0 ** (-5.0 - h)) for h in range(RET_HEADS)]
            mix_a = bidir_scan(proj, None, None, jnp.ones((1, RET_DV), F32),
                               batch=batch, seq=seq, ctx_len=ctx_len, heads=RET_HEADS, dk=RET_DK,
                               dv=RET_DV, chunk=RET_CHUNK, q_blk=0, k_blk=1, v_blk=1, g_blk=2,
                               log_decay_f=log_g, log_decay_b=log_g[::-1])
            hp = dict(short_w=ev_short_w[j], short_b=ev_short_b[j][None, :],
                      w1=jnp.pad(ev_filt_w1[j], ((0, LANES - HYENA_EMB), (0, 0))),
                      b1=ev_filt_b1[j][None, :], w2=ev_filt_w2[j], b2=ev_filt_b2[j][None, :],
                      w3=ev_filt_w3[j], b3=ev_filt_b3[j][None, :], w4=ev_filt_w4[j],
                      freq=ev_filt_freq[j][None, :], bias=ev_hyena_bias[j][None, :])
            hy_lat = hyena_long_conv(proj, hp, row0=0, batch=batch, seq_len=seq, z_blk=1)
            hy_ctx = hyena_long_conv(proj, hp, row0=t_lat, batch=batch, seq_len=ctx_len, z_blk=1)
            mix_b = (hy_lat, hy_ctx)
            a_blk, b_blk = 0, 0
            w_out = ev_w_out[j].astype(BF16)
        else:
            kd = GLA_HEADS * GLA_DK
            pad_cols = LANES - 2 * GLA_RANK
            w_ext = jnp.concatenate([od_w_in[j], od_gate_w1_f[j], od_gate_w1_b[j],
                                     jnp.zeros((d, pad_cols), F32)], axis=1).astype(BF16)
            w2 = jnp.concatenate(
                [jnp.pad(od_gate_w2_f[j], ((0, LANES - GLA_RANK), (0, 0))),
                 jnp.pad(od_gate_w2_b[j], ((GLA_RANK, LANES - 2 * GLA_RANK), (0, 0)))], axis=1)
            w2_hi = w2.astype(BF16)
            w2_split = jnp.concatenate([w2_hi, w2_hi, (w2 - w2_hi.astype(F32)).astype(BF16)], axis=0)
            b2 = jnp.concatenate([od_gate_b_f[j], od_gate_b_b[j]])[None, :]
            if x_all is None:
                x_all = jnp.concatenate(x_rows[:2], axis=0)
            proj, la_f, la_b = inproj_odd(x_all, mods, nw1, w_ext, w2_split, b2,
                                          t_lat=t_lat, seq=seq, tm=tm)
            mix_a = bidir_scan(proj, la_f, la_b, od_norm_w[j][None, :],
                               batch=batch, seq=seq, ctx_len=ctx_len, heads=GLA_HEADS, dk=GLA_DK,
                               dv=GLA_DV, chunk=GLA_CHUNK, q_blk=0, k_blk=1, v_blk=1, g_blk=2)
            mix_b = mix_a
            a_blk, b_blk = 0, 1
            w_out = od_w_out[j].astype(BF16)

        n_rows = t_lat if last else t_all
        rw_t = router_w[i].T
        rw_hi = rw_t.astype(BF16)
        rw_split = jnp.stack([rw_hi, (rw_t - rw_hi.astype(F32)).astype(BF16)])
        x_mid, h2, logits = outproj(mix_a, a_blk, mix_b, b_blk, x_rows, mods, nw2, w_out,
                                    rw_split, n_rows=n_rows, t_lat=t_lat, seq=seq, tm=tm)
        x_all = moe_block(h2, logits, router_bias[i], x_mid, mods, final_norm_w[None, :],
                          exp_w_gate, exp_w_up, exp_w_down,
                          sh_w_gate[i], sh_w_up[i], sh_w_down[i],
                          layer=i, t_lat=t_lat, seq=seq, final_norm=last)
        x_rows = (x_all, x_all, t_lat)
    return x_all[:t_lat].reshape(batch, seq, d)
```

```python
import functools
import math

import jax
import jax.numpy as jnp
from jax import lax
from jax.experimental import pallas as pl
from jax.experimental.pallas import tpu as pltpu

F32 = jnp.float32
BF16 = jnp.bfloat16
HIGHEST = lax.Precision.HIGHEST

D_MODEL = 1024
GRID_W = 64
NORM_EPS = 1e-6
RET_HEADS, RET_DK, RET_DV, RET_CHUNK = 4, 64, 128, 128
ROPE_BASE = 10000.0
HYENA_CH, HYENA_EMB = 512, 33
HYENA_FAST_DECAY, HYENA_SLOW_DECAY, HYENA_TARGET = 0.3, 1.5, 1e-2
GLA_HEADS, GLA_DK, GLA_DV, GLA_RANK, GLA_TAU, GLA_CHUNK = 4, 128, 256, 16, 16.0, 64
N_EXPERTS, TOP_K, N_GROUPS, TOPK_GROUPS = 64, 8, 8, 4
GROUP_SIZE = N_EXPERTS // N_GROUPS
EXPERT_FF = 256
ROUTED_SCALE = 2.5

LANES = 128
SCAN_ROWS = 256
SCAN_LAT_ROWS = 1024
MOE_TILE = 256
ROW_UNIT = 16
EXPERT_ROWS = 1024
SLOT_CHUNK = 512
SUB_SLOTS = 256
SLOTS = -(-(MOE_TILE * TOP_K + N_EXPERTS * (ROW_UNIT - 1) + ROW_UNIT) // SLOT_CHUNK) * SLOT_CHUNK
N_UNITS = SLOTS // ROW_UNIT
PAD_UNITS = EXPERT_ROWS // ROW_UNIT
PAD_BITS = (PAD_UNITS - 1).bit_length()
VMEM_LIMIT = 56 * 1024 * 1024


def _cparams(sem):
    return pltpu.CompilerParams(dimension_semantics=sem, vmem_limit_bytes=VMEM_LIMIT)


def _rms(x):
    return x * lax.rsqrt(jnp.mean(x * x, axis=-1, keepdims=True) + NORM_EPS)


def _silu(x):
    return x * jax.nn.sigmoid(x)


def _adaln_kernel(c_ref, w_ref, b_ref, o_ref):
    s = _silu(c_ref[...])
    w = w_ref[0]
    s_hi, w_hi = s.astype(BF16), w.astype(BF16)
    s_lo, w_lo = (s - s_hi.astype(F32)).astype(BF16), (w - w_hi.astype(F32)).astype(BF16)
    dot = lambda a, b: jnp.dot(a, b, preferred_element_type=F32)
    o_ref[0] = dot(s_hi, w_hi) + (dot(s_lo, w_hi) + dot(s_hi, w_lo)) + b_ref[0]


def adaln_rows(cond8, ada_w, ada_b):
    depth, d, n = ada_w.shape
    tn = 512
    return pl.pallas_call(
        _adaln_kernel,
        grid=(depth, n // tn),
        in_specs=[pl.BlockSpec((8, d), lambda l, j: (0, 0)),
                  pl.BlockSpec((1, d, tn), lambda l, j: (l, 0, j)),
                  pl.BlockSpec((1, 1, tn), lambda l, j: (l, 0, j))],
        out_specs=pl.BlockSpec((1, 8, tn), lambda l, j: (l, 0, j)),
        out_shape=jax.ShapeDtypeStruct((depth, 8, n), F32),
        compiler_params=_cparams(("parallel", "parallel")),
        name="adaln_rows",
    )(cond8, ada_w, ada_b.reshape(depth, 1, n))


def _modulated_norm(x, mod, nw, shift_row, scale_row):
    return _rms(x) * nw * (1.0 + mod[scale_row:scale_row + 1]) + mod[shift_row:shift_row + 1]


def _split_row_specs(tm, width, n_lat_tiles, col_blk=0, ctx_row0=0):
    ctx_blk0 = ctx_row0 // tm
    return [pl.BlockSpec((tm, width), lambda i: (jnp.minimum(i, n_lat_tiles - 1), col_blk)),
            pl.BlockSpec((tm, width),
                         lambda i: (ctx_blk0 + jnp.maximum(i - n_lat_tiles, 0), col_blk))]


def _pick_rows(lat_ref, ctx_ref, n_lat_tiles):
    return jnp.where(pl.program_id(0) < n_lat_tiles, lat_ref[...], ctx_ref[...])


def _inproj_even_kernel(xl_ref, xc_ref, mod_ref, nw_ref, w_ref, cos_ref, sin_ref, o_ref, *,
                        n_lat_tiles):
    x = _pick_rows(xl_ref, xc_ref, n_lat_tiles)
    h = _modulated_norm(x, mod_ref[0], nw_ref[...], 0, 1).astype(BF16)
    n_main = o_ref.shape[1]
    qk = jnp.dot(h, w_ref[:, 0:512], preferred_element_type=F32)
    qk_sw = jnp.dot(h, w_ref[:, n_main:n_main + 512], preferred_element_type=F32)
    is_lat = pl.program_id(0) < n_lat_tiles
    cos = jnp.where(is_lat, cos_ref[...], 1.0)
    sin = jnp.where(is_lat, sin_ref[...], 0.0)
    for c0 in range(0, 512, LANES):
        o_ref[:, c0:c0 + LANES] = (qk[:, c0:c0 + LANES] * cos
                                   + qk_sw[:, c0:c0 + LANES] * sin).astype(BF16)
    for c0 in range(512, n_main, 512):
        o_ref[:, c0:c0 + 512] = jnp.dot(h, w_ref[:, c0:c0 + 512],
                                        preferred_element_type=F32).astype(BF16)


def _inproj_odd_kernel(x_ref, mod_ref, nw_ref, w_ref, w2_ref, b2_ref, o_ref, laf_ref, lab_ref):
    h = _modulated_norm(x_ref[...], mod_ref[0], nw_ref[...], 0, 1).astype(BF16)
    n_main = o_ref.shape[1]
    kd = GLA_HEADS * GLA_DK
    q = jnp.dot(h, w_ref[:, 0:kd], preferred_element_type=F32)
    o_ref[:, 0:kd] = (q * (GLA_DK ** -0.5)).astype(BF16)
    for c0 in range(kd, n_main, 512):
        o_ref[:, c0:c0 + 512] = jnp.dot(h, w_ref[:, c0:c0 + 512],
                                        preferred_element_type=F32).astype(BF16)
    low = jnp.dot(h, w_ref[:, n_main:n_main + LANES], preferred_element_type=F32)

    low_hi = low.astype(BF16)
    low_lo = (low - low_hi.astype(F32)).astype(BF16)
    z = jnp.dot(jnp.concatenate([low_hi, low_lo, low_hi], axis=1), w2_ref[...],
                preferred_element_type=F32) + b2_ref[...]
    la = (jnp.minimum(z, 0.0) - jnp.log(1.0 + jnp.exp(-jnp.abs(z)))) * (1.0 / GLA_TAU)
    laf_ref[...] = la[:, 0:kd]
    lab_ref[...] = la[:, kd:2 * kd]


def _seg_of_tile(i, tm, t_lat, seq):
    return jnp.where(i < t_lat // tm, 1 + (i * tm) // seq, 0)


def inproj_even(x_rows, mods, nw, w_ext, cos_t, sin_t, *, t_lat, t_all, seq, tm=512):
    x_lat, x_ctx, ctx_row0 = x_rows
    d = x_lat.shape[1]
    n_ext = w_ext.shape[1]
    n_main = n_ext - 512
    n_lat_tiles, pos_tiles = t_lat // tm, seq // tm

    def pos_map(i):
        return (jnp.where(i < n_lat_tiles, i % pos_tiles, 0), 0)

    return pl.pallas_call(
        functools.partial(_inproj_even_kernel, n_lat_tiles=n_lat_tiles),
        grid=(t_all // tm,),
        in_specs=_split_row_specs(tm, d, n_lat_tiles, ctx_row0=ctx_row0) + [
                  pl.BlockSpec((1, 6, d), lambda i: (_seg_of_tile(i, tm, t_lat, seq), 0, 0)),
                  pl.BlockSpec((1, d), lambda i: (0, 0)),
                  pl.BlockSpec((d, n_ext), lambda i: (0, 0)),
                  pl.BlockSpec((tm, LANES), pos_map),
                  pl.BlockSpec((tm, LANES), pos_map)],
        out_specs=pl.BlockSpec((tm, n_main), lambda i: (i, 0)),
        out_shape=jax.ShapeDtypeStruct((t_all, n_main), BF16),
        compiler_params=_cparams(("parallel",)),
        name="inproj_even",
    )(x_lat, x_ctx, mods, nw, w_ext, cos_t, sin_t)


def inproj_odd(x_all, mods, nw, w_ext, w2, b2, *, t_lat, seq, tm=512):
    t_all, d = x_all.shape
    n_main = w_ext.shape[1] - LANES
    kd = GLA_HEADS * GLA_DK
    full = lambda shape: pl.BlockSpec(shape, lambda i: (0,) * len(shape))
    return pl.pallas_call(
        _inproj_odd_kernel,
        grid=(t_all // tm,),
        in_specs=[pl.BlockSpec((tm, d), lambda i: (i, 0)),
                  pl.BlockSpec((1, 6, d), lambda i: (_seg_of_tile(i, tm, t_lat, seq), 0, 0)),
                  full((1, d)), full(w_ext.shape), full(w2.shape), full(b2.shape)],
        out_specs=[pl.BlockSpec((tm, n_main), lambda i: (i, 0)),
                   pl.BlockSpec((tm, kd), lambda i: (i, 0)),
                   pl.BlockSpec((tm, kd), lambda i: (i, 0))],
        out_shape=[jax.ShapeDtypeStruct((t_all, n_main), BF16),
                   jax.ShapeDtypeStruct((t_all, kd), F32),
                   jax.ShapeDtypeStruct((t_all, kd), F32)],
        compiler_params=_cparams(("parallel",)),
        name="inproj_odd",
    )(x_all, mods, nw, w_ext, w2, b2)


def _scan_kernel(*refs, heads, dk, dv, chunk, gated, reverse, log_decay, sub_blocks):
    n_in = 3 + int(gated) + (2 if reverse else 0)
    ctx_in, lat_in = refs[:n_in], refs[n_in:2 * n_in]
    rest = list(refs[2 * n_in:])
    nw_ref = rest.pop(0) if reverse else None
    out_lat, out_ctx, state_ref = rest[0], rest[1], rest[2]
    dec_ref = None if gated else rest[3]

    hpg = LANES // dk
    groups = heads // hpg
    rows = SCAN_ROWS
    n_chunks = rows // chunk
    j = pl.program_id(1)

    row_i = lax.broadcasted_iota(jnp.int32, (rows, rows), 0)
    col_i = lax.broadcasted_iota(jnp.int32, (rows, rows), 1)
    same_chunk = (row_i // chunk) == (col_i // chunk)
    keep = jnp.logical_and(same_chunk, (col_i >= row_i) if reverse else (row_i >= col_i))

    @pl.when(j == 0)
    def _init():
        state_ref[...] = jnp.zeros_like(state_ref)
        if not gated:
            pos = lax.broadcasted_iota(jnp.int32, (rows, LANES), 0) % chunk
            steps = ((chunk - pos) if reverse else (pos + 1)).astype(F32)
            lane = lax.broadcasted_iota(jnp.int32, (rows, LANES), 1)
            for g in range(groups):
                lg = jnp.zeros((rows, LANES), F32)
                for a in range(hpg):
                    lg = jnp.where(lane // dk == a, log_decay[g * hpg + a], lg)
                logb = steps * lg
                b_end = float(chunk) * lg
                dec_ref[g, 0] = jnp.exp(logb)
                dec_ref[g, 1] = jnp.exp(-logb)
                dec_ref[g, 2] = jnp.exp(b_end - logb)
                dec_ref[g, 3] = jnp.exp(b_end)

    lane1 = lax.broadcasted_iota(jnp.int32, (1, LANES), 1)
    order = range(n_chunks - 1, -1, -1) if reverse else range(n_chunks)
    end_row = lambda c: c * chunk if reverse else (c + 1) * chunk - 1

    def scan_rows(in_refs, out_ref, r0):
        q_ref, k_ref, v_ref = in_refs[:3]
        blk = slice(r0, r0 + rows)
        if reverse:
            oprev_ref, gate_ref = in_refs[-2:]
        if gated:
            tri = jnp.where(keep, 1.0, 0.0).astype(BF16)
            la = in_refs[3][blk, :]
            la_hi = la.astype(BF16)
            la_lo = (la - la_hi.astype(F32)).astype(BF16)
            logb_all = (jnp.dot(tri, la_hi, preferred_element_type=F32)
                        + jnp.dot(tri, la_lo, preferred_element_type=F32))
        for g in range(groups):
            ksl = slice(g * LANES, (g + 1) * LANES)
            qg = q_ref[blk, ksl].astype(F32)
            kg = k_ref[blk, ksl].astype(F32)
            if gated:
                logb = logb_all[:, ksl]
                ends = [logb[end_row(c):end_row(c) + 1] for c in range(n_chunks)]
                b_end = jnp.concatenate([jnp.broadcast_to(e, (chunk, LANES)) for e in ends], axis=0)
                e_q, e_k, e_s = jnp.exp(logb), jnp.exp(-logb), jnp.exp(b_end - logb)
                e_e = [jnp.exp(e) for e in ends]
            else:
                e_q, e_k, e_s = dec_ref[g, 0], dec_ref[g, 1], dec_ref[g, 2]
                e_e = [dec_ref[g, 3][0:1]] * n_chunks
            qd = qg * e_q
            kd_ = (kg * e_k).astype(BF16)
            ks = (kg * e_s).astype(BF16)
            for a in range(hpg):
                h = g * hpg + a
                qa = (jnp.where(lane1 // dk == a, qd, 0.0) if hpg > 1 else qd).astype(BF16)
                vh = v_ref[blk, h * dv:(h + 1) * dv]
                s = lax.dot_general(qa, kd_, (((1,), (1,)), ((), ())), preferred_element_type=F32)
                o_intra = jnp.dot(jnp.where(keep, s, 0.0).astype(BF16), vh,
                                  preferred_element_type=F32)
                st = state_ref[h]
                for c in order:
                    rs = slice(c * chunk, (c + 1) * chunk)
                    o = o_intra[rs] + lax.dot_general(qa[rs], st.astype(BF16),
                                                      (((1,), (1,)), ((), ())),
                                                      preferred_element_type=F32)
                    st = e_e[c] * st + lax.dot_general(vh[rs], ks[rs], (((0,), (0,)), ((), ())),
                                                       preferred_element_type=F32)
                    osl = (slice(r0 + c * chunk, r0 + (c + 1) * chunk), slice(h * dv, (h + 1) * dv))
                    if reverse:
                        o = _rms(o + oprev_ref[osl]) * nw_ref[...]
                        out_ref[osl] = (o * _silu(gate_ref[osl].astype(F32))).astype(BF16)
                    else:
                        out_ref[osl] = o
                state_ref[h] = st

    @pl.when(j == 0)
    def _():
        scan_rows(ctx_in, out_ctx, 0)

    @pl.when(j > 0)
    def _():
        for sb in (range(sub_blocks - 1, -1, -1) if reverse else range(sub_blocks)):
            scan_rows(lat_in, out_lat, sb * rows)


def bidir_scan(proj, la_f, la_b, norm_w, *, batch, seq, ctx_len, heads, dk, dv, chunk,
               q_blk, k_blk, v_blk, g_blk, log_decay_f=None, log_decay_b=None):
    gated = la_f is not None
    hk, hv = heads * dk, heads * dv
    rb, lb = SCAN_ROWS, SCAN_LAT_ROWS
    lat_blocks = seq // lb
    ctx_base = (batch * seq) // rb
    assert ctx_len == rb

    lat_fwd = lambda b, j: b * lat_blocks + jnp.maximum(j - 1, 0)
    lat_bwd = lambda b, j: b * lat_blocks + lat_blocks - 1 - jnp.maximum(j - 1, 0)

    prev = None
    for reverse, lat_of, la, ld in ((False, lat_fwd, la_f, log_decay_f),
                                    (True, lat_bwd, la_b, log_decay_b)):
        ctx_spec = lambda w, cb, base=ctx_base: pl.BlockSpec(
            (rb, w), lambda b, j, cb=cb: (base + b, cb))
        lat_spec = lambda w, cb: pl.BlockSpec((lb, w), lambda b, j, cb=cb: (lat_of(b, j), cb))
        in_specs, args = [], []
        for spec, is_ctx in ((ctx_spec, True), (lat_spec, False)):
            in_specs += [spec(hk, q_blk), spec(hk, k_blk), spec(hv, v_blk)]
            args += [proj, proj, proj]
            if gated:
                in_specs.append(spec(hk, 0))
                args.append(la)
            if reverse:
                in_specs += [ctx_spec(hv, 0, base=0) if is_ctx else spec(hv, 0), spec(hv, g_blk)]
                args += [prev[1] if is_ctx else prev[0], proj]
        if reverse:
            in_specs.append(pl.BlockSpec((1, dv), lambda b, j: (0, 0)))
            args.append(norm_w)
        scratch = [pltpu.VMEM((heads, dv, LANES), F32)]
        if not gated:
            scratch.append(pltpu.VMEM((hk // LANES, 4, rb, LANES), F32))
        kern = functools.partial(_scan_kernel, heads=heads, dk=dk, dv=dv, chunk=chunk,
                                 gated=gated, reverse=reverse, log_decay=ld, sub_blocks=lb // rb)
        out_dtype = BF16 if reverse else F32
        prev = pl.pallas_call(
            kern,
            grid=(batch, lat_blocks + 1),
            in_specs=in_specs,
            out_specs=[lat_spec(hv, 0), ctx_spec(hv, 0, base=0)],
            out_shape=[jax.ShapeDtypeStruct((batch * seq, hv), out_dtype),
                       jax.ShapeDtypeStruct((batch * ctx_len, hv), out_dtype)],
            scratch_shapes=scratch,
            compiler_params=_cparams(("parallel", "arbitrary")),
            name="scan_bwd" if reverse else "scan_fwd",
        )(*args)
    return tuple(prev)


def _shortconv_kernel(z_ref, zp_ref, zn_ref, w_ref, b_ref, u_ref, x0_ref, *, tiles_per_seq):
    i = pl.program_id(0)
    tm = z_ref.shape[0]
    z = z_ref[...].astype(F32)
    first = (i % tiles_per_seq) == 0
    last = (i % tiles_per_seq) == tiles_per_seq - 1
    halo = zp_ref.shape[0]
    prev_row = jnp.where(first, 0.0, zp_ref[halo - 1:halo, :].astype(F32))
    next_row = jnp.where(last, 0.0, zn_ref[0:1, :].astype(F32))
    row = lax.broadcasted_iota(jnp.int32, z.shape, 0)
    z_prev = jnp.where(row == 0, prev_row, pltpu.roll(z, 1, 0))
    z_next = jnp.where(row == tm - 1, next_row, pltpu.roll(z, tm - 1, 0))
    y = w_ref[0:1] * z_prev + w_ref[1:2] * z + w_ref[2:3] * z_next + b_ref[...]
    c = HYENA_CH
    x0_ref[...] = y[:, 0:c].astype(BF16)
    u_ref[...] = (y[:, c:2 * c] * y[:, 2 * c:3 * c]).astype(BF16)


def hyena_shortconv(proj, short_w, short_b, *, row0, batch, seq_len, z_blk, tm=256):
    halo = 16
    tiles_per_seq = seq_len // tm
    n_tiles = batch * tiles_per_seq
    t0, h_per_tile = row0 // tm, tm // halo
    nz = 3 * HYENA_CH
    n_halo_blocks = proj.shape[0] // halo
    out_map = lambda i: (i % tiles_per_seq, i // tiles_per_seq)
    kern = functools.partial(_shortconv_kernel, tiles_per_seq=tiles_per_seq)
    return pl.pallas_call(
        kern,
        grid=(n_tiles,),
        in_specs=[pl.BlockSpec((tm, nz), lambda i: (t0 + i, z_blk)),
                  pl.BlockSpec((halo, nz),
                               lambda i: (jnp.maximum((t0 + i) * h_per_tile - 1, 0), z_blk)),
                  pl.BlockSpec((halo, nz),
                               lambda i: (jnp.minimum((t0 + i + 1) * h_per_tile,
                                                      n_halo_blocks - 1), z_blk)),
                  pl.BlockSpec((3, nz), lambda i: (0, 0)),
                  pl.BlockSpec((1, nz), lambda i: (0, 0))],
        out_specs=[pl.BlockSpec((tm, HYENA_CH), out_map), pl.BlockSpec((tm, HYENA_CH), out_map)],
        out_shape=[jax.ShapeDtypeStruct((seq_len, batch * HYENA_CH), BF16)] * 2,
        compiler_params=_cparams(("parallel",)),
        name="hyena_shortconv",
    )(proj, proj, proj, short_w, short_b)


def _filter_kernel(feat_ref, w1_ref, b1_ref, w2_ref, b2_ref, w3_ref, b3_ref, w4_ref, fq_ref,
                   o_ref, *, seq_len):
    i = pl.program_id(0)
    tl = feat_ref.shape[1]
    fq = fq_ref[...]
    dot = lambda a, b: jnp.dot(a, b, preferred_element_type=F32, precision=HIGHEST)
    h = jnp.sin(fq * (dot(w1_ref[...], feat_ref[...]) + b1_ref[...]))
    h = jnp.sin(fq * (dot(w2_ref[...], h) + b2_ref[...]))
    h = jnp.sin(fq * (dot(w3_ref[...], h) + b3_ref[...]))
    h = lax.dot_general(h, w4_ref[...], (((0,), (0,)), ((), ())),
                        preferred_element_type=F32, precision=HIGHEST)
    c = HYENA_CH
    max_decay = math.log(HYENA_TARGET) / HYENA_FAST_DECAY
    min_decay = math.log(HYENA_TARGET) / HYENA_SLOW_DECAY
    ch = lax.broadcasted_iota(jnp.int32, (tl, c), 1).astype(F32)
    deltas = min_decay + ch * ((max_decay - min_decay) / (c - 1))
    row = lax.broadcasted_iota(jnp.int32, (tl, c), 0) + i * tl
    t = row.astype(F32) * (1.0 / (seq_len - 1))
    window = jnp.exp(-t * jnp.abs(deltas))
    h_f = h[:, 0:c] * window
    h_b = jnp.where(row == 0, 0.0, h[:, c:2 * c] * window)
    o_ref[:, 0:c] = (h_f + h_b).astype(BF16)
    o_ref[:, c:2 * c] = (h_f - h_b).astype(BF16)


def hyena_filter_pair(feats, w1p, b1, w2, b2, w3, b3, w4, fq):
    seq_len = feats.shape[0]
    tl = min(2048, seq_len)
    full = lambda a: pl.BlockSpec(a.shape, lambda i: (0,) * a.ndim)
    kern = functools.partial(_filter_kernel, seq_len=seq_len)
    feats, w1p, w2, w3 = feats.T, w1p.T, w2.T, w3.T
    b1, b2, b3, fq = b1.T, b2.T, b3.T, fq.T
    return pl.pallas_call(
        kern,
        grid=(seq_len // tl,),
        in_specs=[pl.BlockSpec((feats.shape[0], tl), lambda i: (0, i)),
                  full(w1p), full(b1), full(w2), full(b2), full(w3), full(b3), full(w4), full(fq)],
        out_specs=pl.BlockSpec((tl, 2 * HYENA_CH), lambda i: (i, 0)),
        out_shape=jax.ShapeDtypeStruct((seq_len, 2 * HYENA_CH), BF16),
        compiler_params=_cparams(("parallel",)),
        name="hyena_filter",
    )(feats, w1p, b1, w2, b2, w3, b3, w4, fq)


def _filter_spectrum_kernel(c_ref, s_ref, f_ref, hc_ref, hs_ref, kre_ref, kim_ref, *, scale):
    ch = HYENA_CH
    f = f_ref[...]
    a = jnp.dot(c_ref[...], f, preferred_element_type=F32)
    b = jnp.dot(s_ref[...], f, preferred_element_type=F32)
    hc, hs = hc_ref[...], hs_ref[...]
    kre_ref[...] = (hc * a[:, 0:ch] + hs * b[:, 0:ch]) * scale
    kim_ref[...] = (hs * a[:, ch:2 * ch] - hc * b[:, ch:2 * ch]) * scale


def filter_spectrum(cs, ss, filt, half_cos, half_sin):
    n = cs.shape[0]
    c = HYENA_CH
    tf = min(512, n)
    kern = functools.partial(_filter_spectrum_kernel, scale=1.0 / n)
    return pl.pallas_call(
        kern,
        grid=(n // tf,),
        in_specs=[pl.BlockSpec((tf, n), lambda f: (f, 0)), pl.BlockSpec((tf, n), lambda f: (f, 0)),
                  pl.BlockSpec((n, 2 * c), lambda f: (0, 0)),
                  pl.BlockSpec((tf, 1), lambda f: (f, 0)), pl.BlockSpec((tf, 1), lambda f: (f, 0))],
        out_specs=[pl.BlockSpec((tf, c), lambda f: (f, 0))] * 2,
        out_shape=[jax.ShapeDtypeStruct((n, c), F32)] * 2,
        compiler_params=_cparams(("parallel",)),
        name="hyena_filter_spectrum",
    )(cs, ss, filt, half_cos, half_sin)


def _dft_fwd_kernel(c_ref, s_ref, u_ref, kre_ref, kim_ref, yre_ref, yim_ref):
    u = u_ref[...]
    a = jnp.dot(c_ref[...], u, preferred_element_type=F32)
    b = jnp.dot(s_ref[...], u, preferred_element_type=F32)
    k_re, k_im = kre_ref[...], kim_ref[...]
    yre_ref[...] = (a * k_re + b * k_im).astype(BF16)
    yim_ref[...] = (a * k_im - b * k_re).astype(BF16)


def dft_forward(cs, ss, u, k_re, k_im, *, batch):
    n = cs.shape[0]
    c = HYENA_CH
    tf = min(512, n)
    return pl.pallas_call(
        _dft_fwd_kernel,
        grid=(batch, n // tf),
        in_specs=[pl.BlockSpec((tf, n), lambda bi, f: (f, 0)),
                  pl.BlockSpec((tf, n), lambda bi, f: (f, 0)),
                  pl.BlockSpec((n, c), lambda bi, f: (0, bi)),
                  pl.BlockSpec((tf, c), lambda bi, f: (f, 0)),
                  pl.BlockSpec((tf, c), lambda bi, f: (f, 0))],
        out_specs=[pl.BlockSpec((tf, c), lambda bi, f: (f, bi))] * 2,
        out_shape=[jax.ShapeDtypeStruct((n, batch * c), BF16)] * 2,
        compiler_params=_cparams(("parallel", "parallel")),
        name="dft_forward",
    )(cs, ss, u, k_re, k_im)


def _dft_inv_kernel(c_ref, s_ref, yre_ref, yim_ref, u_ref, x0_ref, bias_ref, o_ref):
    y = (jnp.dot(c_ref[...], yre_ref[...], preferred_element_type=F32)
         - jnp.dot(s_ref[...], yim_ref[...], preferred_element_type=F32))
    u = u_ref[...].astype(F32)
    o_ref[...] = (x0_ref[...].astype(F32) * (y + u * bias_ref[...])).astype(BF16)


def dft_inverse(cs, ss, yre, yim, u, x0, bias, *, batch):
    n = cs.shape[0]
    c = HYENA_CH
    tt = min(512, n)
    col = lambda bi, t: (0, bi)
    tile = lambda bi, t: (t, bi)
    return pl.pallas_call(
        _dft_inv_kernel,
        grid=(batch, n // tt),
        in_specs=[pl.BlockSpec((tt, n), lambda bi, t: (t, 0)),
                  pl.BlockSpec((tt, n), lambda bi, t: (t, 0)),
                  pl.BlockSpec((n, c), col), pl.BlockSpec((n, c), col),
                  pl.BlockSpec((tt, c), tile), pl.BlockSpec((tt, c), tile),
                  pl.BlockSpec((1, c), lambda bi, t: (0, 0))],
        out_specs=pl.BlockSpec((tt, c), lambda bi, t: (bi * (n // tt) + t, 0)),
        out_shape=jax.ShapeDtypeStruct((batch * n, c), BF16),
        compiler_params=_cparams(("parallel", "parallel")),
        name="dft_inverse",
    )(cs, ss, yre, yim, u, x0, bias)


def _dft_table_kernel(cg_ref, sg_ref, cd_ref, sd_ref, c_ref, s_ref):
    cg, sg, cd, sd = cg_ref[...], sg_ref[...], cd_ref[0], sd_ref[0]
    c_ref[...] = (cg * cd - sg * sd).astype(BF16)
    s_ref[...] = (sg * cd + cg * sd).astype(BF16)


def _shifted_dft_tables(n):
    tf = min(256, n)
    theta = 2.0 * math.pi / (8 * n)
    odd_s = 2 * jnp.arange(n, dtype=jnp.int32)[None, :] + 1
    odd_i = 2 * jnp.arange(tf, dtype=jnp.int32)[:, None] + 1
    gamma = ((odd_i * odd_s) % (8 * n)).astype(F32) * theta
    f0 = jnp.arange(n // tf, dtype=jnp.int32)[:, None] * tf
    delta = ((2 * f0 * odd_s) % (8 * n)).astype(F32)[:, None, :] * theta
    whole = pl.BlockSpec((tf, n), lambda i: (0, 0))
    row = pl.BlockSpec((1, 1, n), lambda i: (i, 0, 0))
    cs, ss = pl.pallas_call(
        _dft_table_kernel,
        grid=(n // tf,),
        in_specs=[whole, whole, row, row],
        out_specs=[pl.BlockSpec((tf, n), lambda i: (i, 0))] * 2,
        out_shape=[jax.ShapeDtypeStruct((n, n), BF16)] * 2,
        compiler_params=_cparams(("parallel",)),
        name="dft_tables",
    )(jnp.cos(gamma), jnp.sin(gamma), jnp.cos(delta), jnp.sin(delta))
    half = (2 * jnp.arange(n, dtype=jnp.int32) + 1).astype(F32)[:, None] * (math.pi / (4 * n))
    return cs, ss, jnp.cos(half), jnp.sin(half)


def _filter_features(seq_len):
    t = jnp.linspace(0.0, 1.0, seq_len, dtype=F32)[:, None]
    bands = (HYENA_EMB - 1) // 2
    ang = 2.0 * math.pi * jnp.arange(seq_len, dtype=F32)[:, None] / seq_len
    fr = jnp.linspace(1e-4, bands - 1, bands, dtype=F32)[None, :]
    feats = jnp.concatenate([t, jnp.cos(fr * ang), -jnp.sin(fr * ang)], axis=-1)
    return jnp.pad(feats, ((0, 0), (0, LANES - HYENA_EMB)))


def hyena_long_conv(proj, hp, *, row0, batch, seq_len, z_blk):
    u, x0 = hyena_shortconv(proj, hp["short_w"], hp["short_b"], row0=row0, batch=batch,
                            seq_len=seq_len, z_blk=z_blk, tm=min(512, seq_len))
    filt = hyena_filter_pair(_filter_features(seq_len), hp["w1"], hp["b1"], hp["w2"], hp["b2"],
                             hp["w3"], hp["b3"], hp["w4"], hp["freq"])
    cs, ss, half_cos, half_sin = _shifted_dft_tables(seq_len)
    k_re, k_im = filter_spectrum(cs, ss, filt, half_cos, half_sin)
    yre, yim = dft_forward(cs, ss, u, k_re, k_im, batch=batch)
    return dft_inverse(cs, ss, yre, yim, u, x0, hp["bias"], batch=batch)


def _outproj_kernel(mal_ref, mac_ref, mbl_ref, mbc_ref, xl_ref, xc_ref, mod_ref, nw_ref, w_ref,
                    rw_ref, xo_ref, h_ref, lt_ref, *, n_lat_tiles):
    half = mal_ref.shape[1]
    pick = functools.partial(_pick_rows, n_lat_tiles=n_lat_tiles)
    y = (jnp.dot(pick(mal_ref, mac_ref), w_ref[0:half], preferred_element_type=F32)
         + jnp.dot(pick(mbl_ref, mbc_ref), w_ref[half:2 * half], preferred_element_type=F32))
    mod = mod_ref[0]
    x = pick(xl_ref, xc_ref) + mod[2:3] * y
    xo_ref[...] = x
    h = _modulated_norm(x, mod, nw_ref[...], 3, 4)
    h_ref[...] = h.astype(BF16)
    h_hi = h.astype(BF16)
    h_lo = (h - h_hi.astype(F32)).astype(BF16)
    nt = lambda a, b: lax.dot_general(a, b, (((1,), (1,)), ((), ())), preferred_element_type=F32)
    lt_ref[...] = nt(rw_ref[0], h_hi) + (nt(rw_ref[0], h_lo) + nt(rw_ref[1], h_hi))


def outproj(mix_a, a_blk, mix_b, b_blk, x_rows, mods, nw, w_out, router_wt, *,
            n_rows, t_lat, seq, tm=512):
    x_lat, x_ctx, ctx_row0 = x_rows
    d = x_lat.shape[1]
    half = d // 2
    n_lat_tiles = t_lat // tm
    return pl.pallas_call(
        functools.partial(_outproj_kernel, n_lat_tiles=n_lat_tiles),
        grid=(n_rows // tm,),
        in_specs=_split_row_specs(tm, half, n_lat_tiles, a_blk)
                 + _split_row_specs(tm, half, n_lat_tiles, b_blk)
                 + _split_row_specs(tm, d, n_lat_tiles, ctx_row0=ctx_row0) + [
                  pl.BlockSpec((1, 6, d), lambda i: (_seg_of_tile(i, tm, t_lat, seq), 0, 0)),
                  pl.BlockSpec((1, d), lambda i: (0, 0)),
                  pl.BlockSpec((d, d), lambda i: (0, 0)),
                  pl.BlockSpec((2, N_EXPERTS, d), lambda i: (0, 0, 0))],
        out_specs=[pl.BlockSpec((tm, d), lambda i: (i, 0)),
                   pl.BlockSpec((tm, d), lambda i: (i, 0)),
                   pl.BlockSpec((N_EXPERTS, tm), lambda i: (0, i))],
        out_shape=[jax.ShapeDtypeStruct((n_rows, d), F32),
                   jax.ShapeDtypeStruct((n_rows, d), BF16),
                   jax.ShapeDtypeStruct((N_EXPERTS, n_rows), F32)],
        compiler_params=_cparams(("parallel",)),
        name="outproj",
    )(*mix_a, *mix_b, x_lat, x_ctx, mods, nw, w_out, router_wt)


def _first_max(x, idx, sentinel):
    m = jnp.max(x, axis=0, keepdims=True)
    first = jnp.min(jnp.where(x == m, idx, sentinel), axis=0, keepdims=True)
    return m, idx == first


def _router_kernel(lt_ref, bias_ref, before_ref, below_ref, lists_ref, pc_ref):
    tn = lt_ref.shape[1]
    scores = jax.nn.sigmoid(lt_ref[...])
    sel = scores + bias_ref[...]
    neg = -jnp.inf
    in_grp = lax.broadcasted_iota(jnp.int32, (GROUP_SIZE, tn), 0)
    gscore = []
    for g in range(N_GROUPS):
        x = sel[g * GROUP_SIZE:(g + 1) * GROUP_SIZE]
        m1, hit = _first_max(x, in_grp, GROUP_SIZE)
        gscore.append(m1 + jnp.max(jnp.where(hit, neg, x), axis=0, keepdims=True))
    rows = []
    for g in range(N_GROUPS):
        beaten = jnp.zeros((1, tn), jnp.int32)
        for o in range(N_GROUPS):
            if o != g:
                wins = (gscore[o] >= gscore[g]) if o < g else (gscore[o] > gscore[g])
                beaten = beaten + wins.astype(jnp.int32)
        keep = jnp.broadcast_to(beaten < TOPK_GROUPS, (GROUP_SIZE, tn))
        rows.append(jnp.where(keep, sel[g * GROUP_SIZE:(g + 1) * GROUP_SIZE], neg))
    cand = jnp.concatenate(rows, axis=0)
    eidx = lax.broadcasted_iota(jnp.int32, cand.shape, 0)
    chosen = jnp.zeros(cand.shape, jnp.bool_)
    hits = []
    for _ in range(TOP_K):
        _, hit = _first_max(cand, eidx, N_EXPERTS)
        hits.append(hit)
        chosen = jnp.logical_or(chosen, hit)
        cand = jnp.where(hit, neg, cand)
    w = jnp.where(chosen, scores, 0.0)
    gates_t = w / jnp.sum(w, axis=0, keepdims=True) * ROUTED_SCALE
    chosen_f = jnp.where(chosen, 1.0, 0.0)
    rank = jnp.dot(chosen_f.astype(BF16), before_ref[...], preferred_element_type=F32)
    count = jnp.sum(chosen_f, axis=1, keepdims=True)
    pc = jnp.floor((count + (ROW_UNIT - 1)) * (1.0 / ROW_UNIT)) * ROW_UNIT
    pc_lanes = jnp.broadcast_to(pc, (N_EXPERTS, LANES))
    start = jnp.dot(below_ref[...], pc_lanes, preferred_element_type=F32,
                    precision=HIGHEST)[:, 0:1]
    slot = start + rank
    pick = lambda hit, v: jnp.sum(jnp.where(hit, v, 0.0), axis=0, keepdims=True)
    lists = jnp.concatenate([pick(h, slot) for h in hits] + [pick(h, gates_t) for h in hits], axis=0)
    lists_ref[...] = lists
    pc_ref[0] = pc_lanes


def router(logits, router_bias):
    e, t = logits.shape
    tn = MOE_TILE
    n_tiles = t // tn
    tok_i = jnp.arange(tn, dtype=jnp.int32)
    before = (tok_i[:, None] < tok_i[None, :]).astype(BF16)
    exp_i = jnp.arange(e, dtype=jnp.int32)
    below = (exp_i[None, :] < exp_i[:, None]).astype(F32)
    const = lambda shape: pl.BlockSpec(shape, lambda i: (0,) * len(shape))
    return pl.pallas_call(
        _router_kernel,
        grid=(n_tiles,),
        in_specs=[pl.BlockSpec((e, tn), lambda i: (0, i)),
                  const((e, 1)), const((tn, tn)), const((e, e))],
        out_specs=[pl.BlockSpec((2 * TOP_K, tn), lambda i: (0, i)),
                   pl.BlockSpec((1, e, LANES), lambda i: (i, 0, 0))],
        out_shape=[jax.ShapeDtypeStruct((2 * TOP_K, t), F32),
                   jax.ShapeDtypeStruct((n_tiles, e, LANES), F32)],
        compiler_params=_cparams(("parallel",)),
        name="router",
    )(logits, router_bias.reshape(e, 1), before, below)


def moe_plan(pc, n_tiles):
    e = pc.shape[1]
    assert n_tiles >= e
    cap_rows = _sorted_capacity(n_tiles)
    run_end = jnp.cumsum(pc, axis=1)
    run_start = run_end - pc
    total = jnp.sum(pc, axis=0)
    total_al = ((total + EXPERT_ROWS - 1) // EXPERT_ROWS) * EXPERT_ROWS
    range_end = jnp.cumsum(total_al)
    range_start = range_end - total_al
    run_row = range_start[None, :] + jnp.cumsum(pc, axis=0) - pc
    unit_row = jnp.arange(N_UNITS, dtype=jnp.int32) * ROW_UNIT
    owner = jnp.sum((run_end[:, None, :] <= unit_row[None, :, None]).astype(jnp.int32), axis=2)
    owner_hot = (owner[:, :, None] == jnp.arange(e, dtype=jnp.int32)).astype(jnp.int32)
    dst = jnp.sum(owner_hot * (run_row - run_start)[:, None, :], axis=2) + unit_row[None, :]
    used = owner < e
    dump = cap_rows + (jnp.arange(n_tiles, dtype=jnp.int32) % 2)[:, None] * DUMP_ROWS
    unit_dst = jnp.where(used, dst, dump + unit_row[None, :]).astype(jnp.int32)
    unit_src = jnp.where(used, dst, 0).astype(jnp.int32)
    pad_dst = jnp.stack([range_start + total, (total_al - total) // ROW_UNIT], axis=1)
    pad_dst = jnp.concatenate([pad_dst, jnp.zeros((n_tiles - e, 2), pad_dst.dtype)],
                              axis=0).astype(jnp.int32)
    tile_end = range_end // EXPERT_ROWS
    tile_expert = jnp.sum((tile_end[None, :] <= jnp.arange(cap_rows // EXPERT_ROWS,
                                                           dtype=jnp.int32)[:, None])
                          .astype(jnp.int32), axis=1)
    tile_expert = jnp.minimum(tile_expert, e - 1).astype(jnp.int32)
    used_rows = run_end[:, -1].astype(jnp.int32)
    tile_end = tile_end.astype(jnp.int32)
    return unit_dst, unit_src, pad_dst, used_rows, tile_expert, tile_end[-1:], tile_end


def _sorted_capacity(n_tiles):
    worst_rows = n_tiles * (MOE_TILE * TOP_K + N_EXPERTS * (ROW_UNIT - 1)) + N_EXPERTS * (EXPERT_ROWS - 1)
    return -(-worst_rows // EXPERT_ROWS) * EXPERT_ROWS


DUMP_ROWS = SLOTS + PAD_UNITS * ROW_UNIT


UNITS_PER_CHUNK = SLOT_CHUNK // ROW_UNIT
MAIN_SLOTS = SLOTS - SLOT_CHUNK
assert MAIN_SLOTS >= MOE_TILE * TOP_K


def _dispatch_kernel(unit_ref, pad_ref, used_ref, h_ref, lists_ref, xs_ref, loc_ref, zero_ref, sems):
    i = pl.program_id(0)
    last = pl.num_programs(0) - 1
    slot = i % 2
    tn = h_ref.shape[0]

    @pl.when(i == 0)
    def _():
        zero_ref[...] = jnp.zeros_like(zero_ref)

    rel_rows = lax.broadcasted_iota(jnp.int32, (SUB_SLOTS, tn), 0).astype(F32).astype(BF16)

    def chunk(c0):
        parts = []
        for b0 in range(c0, c0 + SLOT_CHUNK, SUB_SLOTS):
            p = jnp.zeros((SUB_SLOTS, tn), BF16)
            for k in range(TOP_K):
                hit = rel_rows == (lists_ref[k:k + 1, :] - float(b0)).astype(BF16)
                p = jnp.where(hit, jnp.ones_like(p), p)
            parts.append(p)
        loc_ref[slot, c0:c0 + SLOT_CHUNK, :] = jnp.dot(
            jnp.concatenate(parts, axis=0), h_ref[...],
            preferred_element_type=F32).astype(BF16)
        for u in range(c0 // ROW_UNIT, c0 // ROW_UNIT + UNITS_PER_CHUNK):
            row = pl.multiple_of(unit_ref[i, u], ROW_UNIT)
            pltpu.make_async_copy(loc_ref.at[slot, pl.ds(u * ROW_UNIT, ROW_UNIT)],
                                  xs_ref.at[pl.ds(row, ROW_UNIT)], sems.at[slot]).start()

    for c0 in range(0, MAIN_SLOTS, SLOT_CHUNK):
        chunk(c0)

    @pl.when(used_ref[i] > MAIN_SLOTS)
    def _():
        chunk(MAIN_SLOTS)

    def pad_copies(tile, s, act):
        row, units = pad_ref[tile, 0], pad_ref[tile, 1]
        for bit in range(PAD_BITS - 1, -1, -1):
            size = ROW_UNIT << bit
            has = (jnp.right_shift(units, bit) & 1) == 1

            @pl.when(has)
            def _(row=row):
                act(pltpu.make_async_copy(zero_ref.at[pl.ds(0, size)],
                                          xs_ref.at[pl.ds(pl.multiple_of(row, ROW_UNIT), size)],
                                          sems.at[s]))
            row = row + jnp.where(has, size, 0)

    pad_copies(i, slot, lambda cp: cp.start())

    def wait_all(s, tile):
        pltpu.make_async_copy(loc_ref.at[s, pl.ds(0, MAIN_SLOTS)],
                              xs_ref.at[pl.ds(0, MAIN_SLOTS)], sems.at[s]).wait()

        @pl.when(used_ref[tile] > MAIN_SLOTS)
        def _():
            pltpu.make_async_copy(loc_ref.at[s, pl.ds(MAIN_SLOTS, SLOT_CHUNK)],
                                  xs_ref.at[pl.ds(0, SLOT_CHUNK)], sems.at[s]).wait()
        pad_copies(tile, s, lambda cp: cp.wait())

    @pl.when(i > 0)
    def _():
        wait_all(1 - slot, i - 1)

    @pl.when(i == last)
    def _():
        wait_all(slot, i)


def moe_dispatch(h, lists, unit_dst, pad_dst, used_rows, *, n_tiles):
    d = h.shape[1]
    cap_rows = _sorted_capacity(n_tiles) + 2 * DUMP_ROWS
    return pl.pallas_call(
        _dispatch_kernel,
        grid_spec=pltpu.PrefetchScalarGridSpec(
            num_scalar_prefetch=3,
            grid=(n_tiles,),
            in_specs=[pl.BlockSpec((MOE_TILE, d), lambda i, *_: (i, 0)),
                      pl.BlockSpec((2 * TOP_K, MOE_TILE), lambda i, *_: (0, i))],
            out_specs=pl.BlockSpec(memory_space=pl.ANY),
            scratch_shapes=[pltpu.VMEM((2, SLOTS, d), BF16),
                            pltpu.VMEM((ROW_UNIT << (PAD_BITS - 1), d), BF16),
                            pltpu.SemaphoreType.DMA((2,))]),
        out_shape=jax.ShapeDtypeStruct((cap_rows, d), BF16),
        compiler_params=_cparams(("arbitrary",)),
        name="moe_dispatch",
    )(unit_dst, pad_dst, used_rows, h, lists)


X_RING = 3


def _expert_kernel(te_ref, nt_ref, end_ref, xs_ref, wg_hbm, wu_hbm, wd_hbm, y_ref, x_buf,
                   wg_st, wu_st, wd_st, wg_bf, wu_bf, wd_bf, turn_ref, sems, wsems, *, layer):
    t = pl.program_id(0)
    n_live = nt_ref[0]
    live = t < n_live
    tc = jnp.minimum(t, jnp.maximum(n_live - 1, 0))
    new_expert = jnp.logical_or(t == 0, te_ref[tc] != te_ref[jnp.maximum(tc - 1, 0)])
    rows = x_buf.shape[1]

    def w_copies(e, s):
        return [pltpu.make_async_copy(hbm.at[layer, e], st.at[s], wsems.at[s])
                for hbm, st in ((wg_hbm, wg_st), (wu_hbm, wu_st), (wd_hbm, wd_st))]

    @pl.when(jnp.logical_and(t == 0, live))
    def _():
        turn_ref[0] = 0
        for cp in w_copies(te_ref[0], 0):
            cp.start()

    def x_copy(tile):
        s = tile % X_RING
        return pltpu.make_async_copy(xs_ref.at[pl.ds(pl.multiple_of(tile * rows, rows), rows)],
                                     x_buf.at[s], sems.at[s])

    for k in range(X_RING - 1):
        @pl.when(jnp.logical_and(t == 0, k < n_live))
        def _():
            x_copy(k).start()

    @pl.when(t + (X_RING - 1) < n_live)
    def _():
        x_copy(t + (X_RING - 1)).start()

    @pl.when(jnp.logical_and(live, new_expert))
    def _():
        e = te_ref[tc]
        s = turn_ref[0] % 2
        for cp in w_copies(e, s):
            cp.wait()
        wg_bf[...] = wg_st[s].astype(BF16)
        wu_bf[...] = wu_st[s].astype(BF16)
        wd_bf[...] = wd_st[s].astype(BF16)
        nxt = end_ref[e]

        @pl.when(nxt < n_live)
        def _():
            for cp in w_copies(te_ref[jnp.minimum(nxt, te_ref.shape[0] - 1)], 1 - s):
                cp.start()
        turn_ref[0] = turn_ref[0] + 1

    @pl.when(live)
    def _():
        x_copy(t).wait()
        x = x_buf[t % X_RING]
        act = (_silu(jnp.dot(x, wg_bf[...], preferred_element_type=F32))
               * jnp.dot(x, wu_bf[...], preferred_element_type=F32))
        y_ref[...] = jnp.dot(act.astype(BF16), wd_bf[...], preferred_element_type=F32).astype(BF16)


def moe_experts(xs, w_gate, w_up, w_down, layer, tile_expert, n_row_tiles, range_end_tile):
    d = xs.shape[1]
    cap_rows = xs.shape[0] - 2 * DUMP_ROWS
    ff = w_gate.shape[3]
    row_map = lambda t, te, nt, end: (jnp.minimum(t, jnp.maximum(nt[0] - 1, 0)), 0)
    hbm = pl.BlockSpec(memory_space=pl.ANY)
    return pl.pallas_call(
        functools.partial(_expert_kernel, layer=layer),
        grid_spec=pltpu.PrefetchScalarGridSpec(
            num_scalar_prefetch=3,
            grid=(n_row_tiles[0],),
            in_specs=[hbm, hbm, hbm, hbm],
            out_specs=pl.BlockSpec((EXPERT_ROWS, d), row_map),
            scratch_shapes=[pltpu.VMEM((X_RING, EXPERT_ROWS, d), BF16),
                            pltpu.VMEM((2, d, ff), F32), pltpu.VMEM((2, d, ff), F32),
                            pltpu.VMEM((2, ff, d), F32),
                            pltpu.VMEM((d, ff), BF16), pltpu.VMEM((d, ff), BF16),
                            pltpu.VMEM((ff, d), BF16), pltpu.SMEM((1,), jnp.int32),
                            pltpu.SemaphoreType.DMA((X_RING,)), pltpu.SemaphoreType.DMA((2,))]),
        out_shape=jax.ShapeDtypeStruct((cap_rows, d), BF16),
        compiler_params=_cparams(("arbitrary",)),
        name="moe_experts",
    )(tile_expert, n_row_tiles, range_end_tile, xs, w_gate, w_up, w_down)


def _combine_kernel(unit_ref, used_ref, ys_ref, lists_ref, h_ref, x_ref, mod_ref, fnw_ref, sg_ref,
                    su_ref, sd_ref, o_ref, loc_ref, acc_ref, sems, *, final_norm):
    i = pl.program_id(0)
    last = pl.num_programs(0) - 1
    slot = i % 2
    tn = h_ref.shape[0]
    def fetch(tile, s):
        def units(u0, u1):
            for u in range(u0, u1):
                row = pl.multiple_of(unit_ref[tile, u], ROW_UNIT)
                pltpu.make_async_copy(ys_ref.at[pl.ds(row, ROW_UNIT)],
                                      loc_ref.at[s, pl.ds(u * ROW_UNIT, ROW_UNIT)], sems.at[s]).start()
        units(0, MAIN_SLOTS // ROW_UNIT)

        @pl.when(used_ref[tile] > MAIN_SLOTS)
        def _():
            units(MAIN_SLOTS // ROW_UNIT, N_UNITS)

    @pl.when(i == 0)
    def _():
        fetch(0, 0)

    @pl.when(i < last)
    def _():
        fetch(i + 1, 1 - slot)

    h = h_ref[...]
    act = (_silu(jnp.dot(h, sg_ref[...], preferred_element_type=F32))
           * jnp.dot(h, su_ref[...], preferred_element_type=F32))
    acc = jnp.dot(act.astype(BF16), sd_ref[...], preferred_element_type=F32)
    tail = used_ref[i] > MAIN_SLOTS
    pltpu.make_async_copy(ys_ref.at[pl.ds(0, MAIN_SLOTS)], loc_ref.at[slot, pl.ds(0, MAIN_SLOTS)],
                          sems.at[slot]).wait()

    @pl.when(tail)
    def _():
        pltpu.make_async_copy(ys_ref.at[pl.ds(0, SLOT_CHUNK)],
                              loc_ref.at[slot, pl.ds(MAIN_SLOTS, SLOT_CHUNK)], sems.at[slot]).wait()

    rel_rows = lax.broadcasted_iota(jnp.int32, (SUB_SLOTS, tn), 0).astype(F32).astype(BF16)
    gate_w = [lists_ref[TOP_K + k:TOP_K + k + 1, :].astype(BF16) for k in range(TOP_K)]

    def chunk(c0):
        parts = []
        for b0 in range(c0, c0 + SLOT_CHUNK, SUB_SLOTS):
            q = jnp.zeros((SUB_SLOTS, tn), BF16)
            for k in range(TOP_K):
                hit = rel_rows == (lists_ref[k:k + 1, :] - float(b0)).astype(BF16)
                q = jnp.where(hit, gate_w[k], q)
            parts.append(q)
        return lax.dot_general(jnp.concatenate(parts, axis=0), loc_ref[slot, c0:c0 + SLOT_CHUNK, :],
                               (((0,), (0,)), ((), ())), preferred_element_type=F32)

    for c0 in range(0, MAIN_SLOTS, SLOT_CHUNK):
        acc = acc + chunk(c0)
    acc_ref[...] = acc

    @pl.when(tail)
    def _():
        acc_ref[...] += chunk(MAIN_SLOTS)

    x = x_ref[...] + mod_ref[0][5:6] * acc_ref[...]
    if final_norm:
        x = _rms(x) * fnw_ref[...]
    o_ref[...] = x


def moe_combine(ys, lists, h, x_all, mods, fnw, sh_gate, sh_up, sh_down, unit_dst, used_rows, *,
                n_tiles, t_lat, seq, final_norm):
    d = h.shape[1]
    tn = MOE_TILE
    const = lambda a: pl.BlockSpec(a.shape, lambda i, *_: (0,) * a.ndim)
    kern = functools.partial(_combine_kernel, final_norm=final_norm)
    return pl.pallas_call(
        kern,
        grid_spec=pltpu.PrefetchScalarGridSpec(
            num_scalar_prefetch=2,
            grid=(n_tiles,),
            in_specs=[pl.BlockSpec(memory_space=pl.ANY),
                      pl.BlockSpec((2 * TOP_K, tn), lambda i, *_: (0, i)),
                      pl.BlockSpec((tn, d), lambda i, *_: (i, 0)),
                      pl.BlockSpec((tn, d), lambda i, *_: (i, 0)),
                      pl.BlockSpec((1, 6, d),
                                   lambda i, *_: (_seg_of_tile(i, tn, t_lat, seq), 0, 0)),
                      const(fnw), const(sh_gate), const(sh_up), const(sh_down)],
            out_specs=pl.BlockSpec((tn, d), lambda i, *_: (i, 0)),
            scratch_shapes=[pltpu.VMEM((2, SLOTS, d), BF16), pltpu.VMEM((tn, d), F32),
                            pltpu.SemaphoreType.DMA((2,))]),
        out_shape=jax.ShapeDtypeStruct((n_tiles * tn, d), F32),
        compiler_params=_cparams(("arbitrary",)),
        name="moe_combine",
    )(unit_dst, used_rows, ys, lists, h, x_all, mods, fnw, sh_gate, sh_up, sh_down)


def moe_block(h2, logits, router_bias, x_mid, mods, fnw, w_gate, w_up, w_down,
              sh_gate, sh_up, sh_down, *, layer, t_lat, seq, final_norm):
    n_tiles = h2.shape[0] // MOE_TILE
    lists, pc = router(logits, router_bias)
    unit_dst, unit_src, pad_dst, used_rows, tile_expert, n_row_tiles, range_end_tile = moe_plan(
        pc[:, :, 0].astype(jnp.int32), n_tiles)
    xs = moe_dispatch(h2, lists, unit_dst, pad_dst, used_rows, n_tiles=n_tiles)
    ys = moe_experts(xs, w_gate, w_up, w_down, layer, tile_expert, n_row_tiles, range_end_tile)
    return moe_combine(ys, lists, h2, x_mid, mods, fnw, sh_gate.astype(BF16), sh_up.astype(BF16),
                       sh_down.astype(BF16), unit_src, used_rows, n_tiles=n_tiles, t_lat=t_lat,
                       seq=seq, final_norm=final_norm)


def _rope_tables(seq):
    half = RET_DK // 2
    pos = jnp.arange(seq, dtype=jnp.int32)
    row = (pos // GRID_W).astype(F32)
    col = (pos % GRID_W).astype(F32)
    inv = ROPE_BASE ** (-jnp.arange(0, half, 2, dtype=F32) / half)
    a_row, a_col = row[:, None] * inv[None, :], col[:, None] * inv[None, :]
    cos_h = jnp.concatenate([jnp.cos(a_row)] * 2 + [jnp.cos(a_col)] * 2, axis=-1)
    sin_h = jnp.concatenate([-jnp.sin(a_row), jnp.sin(a_row), -jnp.sin(a_col), jnp.sin(a_col)], -1)
    reps = LANES // RET_DK
    return jnp.tile(cos_h, (1, reps)), jnp.tile(sin_h, (1, reps))


def _rope_partner_columns():
    quarter = RET_DK // 4
    idx = jnp.arange(2 * RET_HEADS * RET_DK, dtype=jnp.int32)
    within = idx % (2 * quarter)
    return jnp.where(within < quarter, idx + quarter, idx - quarter)


def kernel(x, c, ctx, c_ctx, ada_w, ada_b, norm1_w, norm2_w, ev_w_in, ev_short_w, ev_short_b, ev_filt_w1, ev_filt_b1, ev_filt_w2, ev_filt_b2, ev_filt_w3, ev_filt_b3, ev_filt_w4, ev_filt_freq, ev_hyena_bias, ev_w_out, od_w_in, od_gate_w1_f, od_gate_w2_f, od_gate_b_f, od_gate_w1_b, od_gate_w2_b, od_gate_b_b, od_norm_w, od_w_out, router_w, router_bias, exp_w_gate, exp_w_up, exp_w_down, sh_w_gate, sh_w_up, sh_w_down, final_norm_w):
    batch, seq, d = x.shape
    ctx_len = ctx.shape[1]
    depth = ada_w.shape[0]
    t_lat, t_ctx = batch * seq, batch * ctx_len
    t_all = t_lat + t_ctx
    tm = 1024

    x_all = None
    x_rows = (x.reshape(t_lat, d), ctx.reshape(t_ctx, d), 0)
    cond8 = jnp.concatenate([c_ctx[None, :], c, jnp.zeros((8 - 1 - batch, d), F32)], axis=0)
    mods_all = adaln_rows(cond8, ada_w, ada_b).reshape(depth, 8, 6, d)

    for i in range(depth):
        last = i == depth - 1
        j = i // 2
        mods = mods_all[i]
        nw1, nw2 = norm1_w[i][None, :], norm2_w[i][None, :]
        if i % 2 == 0:
            qk = 2 * RET_HEADS * RET_DK
            k_scale = jnp.concatenate([jnp.ones((qk // 2,), F32),
                                       jnp.full((qk // 2,), RET_DK ** -0.5, F32)])
            w_in = ev_w_in[j]
            w_qk = w_in[:, :qk] * k_scale
            w_ext = jnp.concatenate([w_qk, w_in[:, qk:], w_qk[:, _rope_partner_columns()]],
                                    axis=1).astype(BF16)
            cos_t, sin_t = _rope_tables(seq)
            proj = inproj_even(x_rows, mods, nw1, w_ext, cos_t, sin_t, t_lat=t_lat, t_all=t_all,
                               seq=seq, tm=tm)
            log_g = [math.log1p(-2.0 ** (-5.0 - h)) for h in range(RET_HEADS)]
            mix_a = bidir_scan(proj, None, None, jnp.ones((1, RET_DV), F32),
                               batch=batch, seq=seq, ctx_len=ctx_len, heads=RET_HEADS, dk=RET_DK,
                               dv=RET_DV, chunk=RET_CHUNK, q_blk=0, k_blk=1, v_blk=1, g_blk=2,
                               log_decay_f=log_g, log_decay_b=log_g[::-1])
            hp = dict(short_w=ev_short_w[j], short_b=ev_short_b[j][None, :],
                      w1=jnp.pad(ev_filt_w1[j], ((0, LANES - HYENA_EMB), (0, 0))),
                      b1=ev_filt_b1[j][None, :], w2=ev_filt_w2[j], b2=ev_filt_b2[j][None, :],
                      w3=ev_filt_w3[j], b3=ev_filt_b3[j][None, :], w4=ev_filt_w4[j],
                      freq=ev_filt_freq[j][None, :], bias=ev_hyena_bias[j][None, :])
            hy_lat = hyena_long_conv(proj, hp, row0=0, batch=batch, seq_len=seq, z_blk=1)
            hy_ctx = hyena_long_conv(proj, hp, row0=t_lat, batch=batch, seq_len=ctx_len, z_blk=1)
            mix_b = (hy_lat, hy_ctx)
            a_blk, b_blk = 0, 0
            w_out = ev_w_out[j].astype(BF16)
        else:
            kd = GLA_HEADS * GLA_DK
            pad_cols = LANES - 2 * GLA_RANK
            w_ext = jnp.concatenate([od_w_in[j], od_gate_w1_f[j], od_gate_w1_b[j],
                                     jnp.zeros((d, pad_cols), F32)], axis=1).astype(BF16)
            w2 = jnp.concatenate(
                [jnp.pad(od_gate_w2_f[j], ((0, LANES - GLA_RANK), (0, 0))),
                 jnp.pad(od_gate_w2_b[j], ((GLA_RANK, LANES - 2 * GLA_RANK), (0, 0)))], axis=1)
            w2_hi = w2.astype(BF16)
            w2_split = jnp.concatenate([w2_hi, w2_hi, (w2 - w2_hi.astype(F32)).astype(BF16)], axis=0)
            b2 = jnp.concatenate([od_gate_b_f[j], od_gate_b_b[j]])[None, :]
            if x_all is None:
                x_all = jnp.concatenate(x_rows[:2], axis=0)
            proj, la_f, la_b = inproj_odd(x_all, mods, nw1, w_ext, w2_split, b2,
                                          t_lat=t_lat, seq=seq, tm=tm)
            mix_a = bidir_scan(proj, la_f, la_b, od_norm_w[j][None, :],
                               batch=batch, seq=seq, ctx_len=ctx_len, heads=GLA_HEADS, dk=GLA_DK,
                               dv=GLA_DV, chunk=GLA_CHUNK, q_blk=0, k_blk=1, v_blk=1, g_blk=2)
            mix_b = mix_a
            a_blk, b_blk = 0, 1
            w_out = od_w_out[j].astype(BF16)

        n_rows = t_lat if last else t_all
        rw_t = router_w[i].T
        rw_hi = rw_t.astype(BF16)
        rw_split = jnp.stack([rw_hi, (rw_t - rw_hi.astype(F32)).astype(BF16)])
        x_mid, h2, logits = outproj(mix_a, a_blk, mix_b, b_blk, x_rows, mods, nw2, w_out,
                                    rw_split, n_rows=n_rows, t_lat=t_lat, seq=seq, tm=tm)
        x_all = moe_block(h2, logits, router_bias[i], x_mid, mods, final_norm_w[None, :],
                          exp_w_gate, exp_w_up, exp_w_down,
                          sh_w_gate[i], sh_w_up[i], sh_w_down[i],
                          layer=i, t_lat=t_lat, seq=seq, final_norm=last)
        x_rows = (x_all, x_all, t_lat)
    return x_all[:t_lat].reshape(batch, seq, d)
```
